```python
import math
import jax, jax.numpy as jnp
from jax import lax
import numpy as np

D_MODEL = 1024
BATCH = 8
SEQ = 2048
DEPTH = 1
DEC_BATCH = 128
DEC_SEQ = 4
PAST_LEN = 16384
PAGE_SIZE = 128

EPS = 1e-6
CONV_W = 4
RG_WIDTH = D_MODEL
RG_BLOCKS = 8
RG_BLOCK_W = RG_WIDTH // RG_BLOCKS
RG_C = 8.0
SSD_INNER = 2 * D_MODEL
SSD_HEADDIM = 64
SSD_HEADS = SSD_INNER // SSD_HEADDIM
SSD_GROUPS = 4
SSD_STATE = 128
SSD_GN = SSD_GROUPS * SSD_STATE
SSD_CONV_DIM = SSD_INNER + 2 * SSD_GN
SSD_CHUNK = 128
PEER_HEADS = 8
PEER_NKEYS = 128
PEER_EXPERTS = PEER_NKEYS * PEER_NKEYS
PEER_KEY_DIM = 256
PEER_HALF = PEER_KEY_DIM // 2
PEER_TOPK = 16
PEER_BLOCK = 128
IN_SIZES = (RG_WIDTH, RG_WIDTH, SSD_INNER, SSD_CONV_DIM, SSD_HEADS, 2 * D_MODEL)
IN_WIDTH = sum(IN_SIZES)
IN_SPLITS = [int(v) for v in np.cumsum(IN_SIZES)[:-1]]

kernel_name = 'hawk_ssd_peer_hybrid_step'


def rmsnorm(x, g):
    xf = x.astype(jnp.float32)
    y = xf * lax.rsqrt(jnp.mean(xf * xf, axis=-1, keepdims=True) + EPS) * g.astype(jnp.float32)
    return y.astype(x.dtype)


def causal_conv(u, buf, w, b):
    L = u.shape[1]
    padded = jnp.concatenate([buf.astype(u.dtype), u], axis=1)
    out = b.astype(u.dtype) + sum(padded[:, k:k + L] * w[k].astype(u.dtype) for k in range(CONV_W))
    return out, padded[:, L:]


def rg_lru(xc, h0, wa, ba, wx, bx, lam):
    B, L, _ = xc.shape
    xf = xc.astype(jnp.float32)
    xb = xf.reshape(B, L, RG_BLOCKS, RG_BLOCK_W)
    r = jax.nn.sigmoid(jnp.einsum('blhi,hij->blhj', xb, wa.astype(jnp.float32)).reshape(B, L, RG_WIDTH) + ba)
    i = jax.nn.sigmoid(jnp.einsum('blhi,hij->blhj', xb, wx.astype(jnp.float32)).reshape(B, L, RG_WIDTH) + bx)
    log_a = -RG_C * r * jax.nn.softplus(-lam.astype(jnp.float32))
    a = jnp.exp(log_a)
    bterm = jnp.sqrt(-jnp.expm1(2.0 * log_a)) * (i * xf)
    bterm = bterm.at[:, 0].add(a[:, 0] * h0.astype(jnp.float32))

    def combine(p, q):
        return p[0] * q[0], q[0] * p[1] + q[1]

    _, h = lax.associative_scan(combine, (a, bterm), axis=1)
    return h.astype(xc.dtype), h[:, -1]


def ssd_chunked(x, dt, A, Bm, Cm, h0):
    Bsz, L = x.shape[0], x.shape[1]
    Q = math.gcd(L, SSD_CHUNK)
    nc = L // Q
    G, R, P, N = SSD_GROUPS, SSD_HEADS // SSD_GROUPS, SSD_HEADDIM, SSD_STATE
    xs = x.astype(jnp.float32).reshape(Bsz, nc, Q, G, R, P)
    dts = dt.reshape(Bsz, nc, Q, G, R)
    Bs = Bm.astype(jnp.float32).reshape(Bsz, nc, Q, G, N)
    Cs = Cm.astype(jnp.float32).reshape(Bsz, nc, Q, G, N)
    acum = jnp.cumsum(dts * A.reshape(G, R), axis=2)
    seg = acum[:, :, :, None] - acum[:, :, None, :]
    mask = jnp.tril(jnp.ones((Q, Q), dtype=bool))[:, :, None, None]
    decay = jnp.exp(jnp.where(mask, seg, -jnp.inf))
    cb = jnp.einsum('bcign,bcjgn->bcijg', Cs, Bs)
    wgt = cb[..., None] * decay * dts[:, :, None]
    y_intra = jnp.einsum('bcijgr,bcjgrp->bcigrp', wgt, xs)
    last = acum[:, :, -1]
    to_end = jnp.exp(last[:, :, None] - acum) * dts
    S = jnp.einsum('bcjgrp,bcjgn->bcgrpn', to_end[..., None] * xs, Bs)
    chunk_decay = jnp.exp(last)

    def step(h, inp):
        s_c, d_c = inp
        return d_c[..., None, None] * h + s_c, h

    hT, h_starts = lax.scan(step, h0.astype(jnp.float32).reshape(Bsz, G, R, P, N),
                            (jnp.moveaxis(S, 1, 0), jnp.moveaxis(chunk_decay, 1, 0)))
    h_starts = jnp.moveaxis(h_starts, 0, 1)
    y_inter = jnp.einsum('bcign,bcgrpn->bcigrp', Cs, h_starts) * jnp.exp(acum)[..., None]
    y = (y_intra + y_inter).reshape(Bsz, L, SSD_HEADS, P)
    return y, hT.reshape(Bsz, SSD_HEADS, P, N)


def peer(h, w_q, sub_keys, u_emb, v_emb):
    shape = h.shape
    t = h.reshape(-1, D_MODEL)
    T = t.shape[0]
    blk = min(PEER_BLOCK, T)
    nb = -(-T // blk)
    t = jnp.pad(t, ((0, nb * blk - T), (0, 0))).reshape(nb, blk, D_MODEL)

    def block(tb):
        q = (tb @ w_q).reshape(blk, PEER_HEADS, 2, PEER_HALF)
        s = jnp.einsum('thsk,hsnk->thsn', q, sub_keys).astype(jnp.float32)
        sv, si = lax.top_k(s, PEER_TOPK)
        cand = sv[:, :, 0, :, None] + sv[:, :, 1, None, :]
        cand_idx = si[:, :, 0, :, None] * PEER_NKEYS + si[:, :, 1, None, :]
        fv, fi = lax.top_k(cand.reshape(blk, PEER_HEADS, PEER_TOPK * PEER_TOPK), PEER_TOPK)
        idx = jnp.take_along_axis(cand_idx.reshape(blk, PEER_HEADS, PEER_TOPK * PEER_TOPK), fi, axis=-1)
        g = jax.nn.softmax(fv, axis=-1)
        act = jax.nn.gelu(jnp.einsum('thkd,td->thk', u_emb[idx], tb).astype(jnp.float32))
        return jnp.einsum('thk,thkd->td', (g * act).astype(tb.dtype), v_emb[idx])

    out = lax.map(block, t).reshape(nb * blk, D_MODEL)[:T]
    return out.reshape(shape)


def trunk_layer(x, rg_h0, rg_buf, ssd_h0, ssd_buf,
                norm_mix, w_in, rg_conv_w, rg_conv_b, rg_wa, rg_ba, rg_wx, rg_bx, rg_lam, w_rg_out,
                ssd_conv_w, ssd_conv_b, ssd_dt_bias, ssd_a_log, ssd_d, ssd_norm, w_ssd_out, w_out,
                norm_ffn, peer_wq, peer_keys, peer_u, peer_v):
    Bsz, L = x.shape[0], x.shape[1]
    h = rmsnorm(x, norm_mix)
    proj = h @ w_in
    rg_x, rg_gate, ssd_z, ssd_xbc, ssd_dt, merge = jnp.split(proj, IN_SPLITS, axis=-1)
    rg_xc, new_rg_buf = causal_conv(rg_x, rg_buf, rg_conv_w, rg_conv_b)
    rg_y, new_rg_h = rg_lru(rg_xc, rg_h0, rg_wa, rg_ba, rg_wx, rg_bx, rg_lam)
    y_a = (rg_y * jax.nn.gelu(rg_gate)) @ w_rg_out
    xbc, new_ssd_buf = causal_conv(ssd_xbc, ssd_buf, ssd_conv_w, ssd_conv_b)
    xbc = jax.nn.silu(xbc)
    xs, bs, cs = jnp.split(xbc, [SSD_INNER, SSD_INNER + SSD_GN], axis=-1)
    xs = xs.reshape(Bsz, L, SSD_HEADS, SSD_HEADDIM)
    bs = bs.reshape(Bsz, L, SSD_GROUPS, SSD_STATE)
    cs = cs.reshape(Bsz, L, SSD_GROUPS, SSD_STATE)
    dt = jax.nn.softplus(ssd_dt.astype(jnp.float32) + ssd_dt_bias.astype(jnp.float32))
    A = -jnp.exp(ssd_a_log.astype(jnp.float32))
    y, new_ssd_h = ssd_chunked(xs, dt, A, bs, cs, ssd_h0)
    y = y + ssd_d.astype(jnp.float32)[:, None] * xs.astype(jnp.float32)
    y = y.reshape(Bsz, L, SSD_INNER) * jax.nn.silu(ssd_z.astype(jnp.float32))
    y_b = rmsnorm(y, ssd_norm).astype(x.dtype) @ w_ssd_out
    g_a, g_b = jnp.split(merge, 2, axis=-1)
    x = x + (jax.nn.sigmoid(g_a) * y_a + jax.nn.sigmoid(g_b) * y_b) @ w_out
    x = x + peer(rmsnorm(x, norm_ffn), peer_wq, peer_keys, peer_u, peer_v)
    return x, new_rg_h.astype(x.dtype), new_rg_buf, new_ssd_h.astype(x.dtype), new_ssd_buf


def setup_inputs(seed: int = 0) -> dict:
    key = jax.random.key(seed)
    ks = jax.random.split(key, 32)
    f32 = jnp.float32

    def nrm(k, shape, s):
        return jax.random.normal(k, shape, f32) * s

    Lz = DEPTH
    a8 = jax.random.uniform(ks[14], (Lz, RG_WIDTH), f32, minval=0.9, maxval=0.999)
    a = a8 ** (1.0 / RG_C)
    dt0 = jnp.exp(jax.random.uniform(ks[18], (Lz, SSD_HEADS), f32, minval=math.log(1e-3), maxval=math.log(1e-1)))
    return {
        'x_prompt': nrm(ks[0], (BATCH, SEQ, D_MODEL), 1.0),
        'x_sample': nrm(ks[1], (DEC_BATCH, DEC_SEQ, D_MODEL), 1.0),
        'state_rg_h': nrm(ks[2], (Lz, DEC_BATCH, RG_WIDTH), 0.5),
        'state_rg_conv': nrm(ks[3], (Lz, DEC_BATCH, CONV_W - 1, RG_WIDTH), 1.0),
        'state_ssd_h': nrm(ks[4], (Lz, DEC_BATCH, SSD_HEADS, SSD_HEADDIM, SSD_STATE), 0.1),
        'state_ssd_conv': nrm(ks[5], (Lz, DEC_BATCH, CONV_W - 1, SSD_CONV_DIM), 1.0),
        'norm_mix': 1.0 + nrm(ks[6], (Lz, D_MODEL), 0.02),
        'w_in': nrm(ks[7], (Lz, D_MODEL, IN_WIDTH), D_MODEL ** -0.5),
        'rg_conv_w': nrm(ks[8], (Lz, CONV_W, RG_WIDTH), CONV_W ** -0.5),
        'rg_conv_b': nrm(ks[9], (Lz, RG_WIDTH), 0.02),
        'rg_wa': nrm(ks[10], (Lz, RG_BLOCKS, RG_BLOCK_W, RG_BLOCK_W), RG_BLOCK_W ** -0.5),
        'rg_ba': nrm(ks[11], (Lz, RG_WIDTH), 0.02),
        'rg_wx': nrm(ks[12], (Lz, RG_BLOCKS, RG_BLOCK_W, RG_BLOCK_W), RG_BLOCK_W ** -0.5),
        'rg_bx': nrm(ks[13], (Lz, RG_WIDTH), 0.02),
        'rg_lam': jnp.log(a) - jnp.log1p(-a),
        'w_rg_out': nrm(ks[15], (Lz, RG_WIDTH, D_MODEL), RG_WIDTH ** -0.5),
        'ssd_conv_w': nrm(ks[16], (Lz, CONV_W, SSD_CONV_DIM), CONV_W ** -0.5),
        'ssd_conv_b': nrm(ks[17], (Lz, SSD_CONV_DIM), 0.02),
        'ssd_dt_bias': dt0 + jnp.log(-jnp.expm1(-dt0)),
        'ssd_a_log': jnp.log(jax.random.uniform(ks[19], (Lz, SSD_HEADS), f32, minval=1.0, maxval=16.0)),
        'ssd_d': 1.0 + nrm(ks[20], (Lz, SSD_HEADS), 0.02),
        'ssd_norm': 1.0 + nrm(ks[21], (Lz, SSD_INNER), 0.02),
        'w_ssd_out': nrm(ks[22], (Lz, SSD_INNER, D_MODEL), SSD_INNER ** -0.5),
        'w_out': nrm(ks[23], (Lz, D_MODEL, D_MODEL), D_MODEL ** -0.5),
        'norm_ffn': 1.0 + nrm(ks[24], (Lz, D_MODEL), 0.02),
        'peer_wq': nrm(ks[25], (Lz, D_MODEL, PEER_HEADS * PEER_KEY_DIM), D_MODEL ** -0.5),
        'peer_keys': nrm(ks[26], (Lz, PEER_HEADS, 2, PEER_NKEYS, PEER_HALF), PEER_HALF ** -0.5),
        'peer_u': nrm(ks[27], (Lz, PEER_EXPERTS, D_MODEL), D_MODEL ** -0.5),
        'peer_v': nrm(ks[28], (Lz, PEER_EXPERTS, D_MODEL), 0.3),
        'norm_final': 1.0 + nrm(ks[29], (D_MODEL,), 0.02),
    }


def reference(x_prompt, x_sample, state_rg_h, state_rg_conv, state_ssd_h, state_ssd_conv,
              norm_mix, w_in, rg_conv_w, rg_conv_b, rg_wa, rg_ba, rg_wx, rg_bx, rg_lam, w_rg_out,
              ssd_conv_w, ssd_conv_b, ssd_dt_bias, ssd_a_log, ssd_d, ssd_norm, w_ssd_out, w_out,
              norm_ffn, peer_wq, peer_keys, peer_u, peer_v, norm_final):
    yp, ys = x_prompt, x_sample
    dt_p = x_prompt.dtype
    p_rg_h, p_rg_c, p_ssd_h, p_ssd_c = [], [], [], []
    s_rg_h, s_rg_c, s_ssd_h, s_ssd_c = [], [], [], []
    for l in range(DEPTH):
        wl = (norm_mix[l], w_in[l], rg_conv_w[l], rg_conv_b[l], rg_wa[l], rg_ba[l], rg_wx[l], rg_bx[l],
              rg_lam[l], w_rg_out[l], ssd_conv_w[l], ssd_conv_b[l], ssd_dt_bias[l], ssd_a_log[l], ssd_d[l],
              ssd_norm[l], w_ssd_out[l], w_out[l], norm_ffn[l], peer_wq[l], peer_keys[l], peer_u[l], peer_v[l])
        yp, a1, a2, a3, a4 = trunk_layer(
            yp,
            jnp.zeros((BATCH, RG_WIDTH), dt_p),
            jnp.zeros((BATCH, CONV_W - 1, RG_WIDTH), dt_p),
            jnp.zeros((BATCH, SSD_HEADS, SSD_HEADDIM, SSD_STATE), dt_p),
            jnp.zeros((BATCH, CONV_W - 1, SSD_CONV_DIM), dt_p),
            *wl)
        ys, b1, b2, b3, b4 = trunk_layer(ys, state_rg_h[l], state_rg_conv[l], state_ssd_h[l], state_ssd_conv[l], *wl)
        p_rg_h.append(a1); p_rg_c.append(a2); p_ssd_h.append(a3); p_ssd_c.append(a4)
        s_rg_h.append(b1); s_rg_c.append(b2); s_ssd_h.append(b3); s_ssd_c.append(b4)
    y_prompt = rmsnorm(yp, norm_final)
    y_sample = rmsnorm(ys, norm_final)
    prompt_rg_h = jnp.stack(p_rg_h)
    prompt_rg_conv = jnp.stack(p_rg_c)
    prompt_ssd_h = jnp.stack(p_ssd_h)
    prompt_ssd_conv = jnp.stack(p_ssd_c)
    sample_rg_h = jnp.stack(s_rg_h)
    sample_rg_conv = jnp.stack(s_rg_c)
    sample_ssd_h = jnp.stack(s_ssd_h)
    sample_ssd_conv = jnp.stack(s_ssd_c)
    return (y_prompt, y_sample, prompt_rg_h, prompt_rg_conv, prompt_ssd_h, prompt_ssd_conv,
            sample_rg_h, sample_rg_conv, sample_ssd_h, sample_ssd_conv)
```

```python
import functools

import jax
import jax.numpy as jnp
from jax import lax
from jax.experimental import pallas as pl
from jax.experimental.pallas import tpu as pltpu

F32 = jnp.float32
BF16 = jnp.bfloat16
EPS = 1e-6
RG_C = 8.0
PEER_TOPK = 16
SSD_CHUNK = 128
LANES = 128
SUBLANES = 8
SAMPLE_ROWS = 8
NEG_INF = float("-inf")
HIGHEST = lax.Precision.HIGHEST
NT_DIMS = (((1,), (1,)), ((), ()))
VMEM_LIMIT = 56 * 1024 * 1024


def _cparams(sem):
    return pltpu.CompilerParams(dimension_semantics=sem, vmem_limit_bytes=VMEM_LIMIT)


def _pick(n, pref):
    t = min(n, pref)
    while n % t:
        t -= SUBLANES
    return t


def _sigmoid(x):
    return 1.0 / (1.0 + jnp.exp(-x))


def _silu(x):
    return x * _sigmoid(x)


def _softplus(x):
    return jnp.maximum(x, 0.0) + jnp.log1p(jnp.exp(-jnp.abs(x)))


def _gelu(x):
    return 0.5 * x * (1.0 + jnp.tanh(0.7978845608028654 * (x + 0.044715 * (x * x * x))))


def _rms(x, g):
    return x * lax.rsqrt(jnp.mean(x * x, axis=-1, keepdims=True) + EPS) * g


def _dot(a, b):
    return jnp.dot(a, b, preferred_element_type=F32)


def _dot_nt(a, b):
    return lax.dot_general(a, b, NT_DIMS, preferred_element_type=F32)


def _inproj_body(x_ref, g_ref, w_ref, wdt_ref, wdtT_ref, o_ref, odt_ref, odtT_ref, xn_ref):
    @pl.when(pl.program_id(1) == 0)
    def _():
        xn = _rms(x_ref[...], g_ref[...])
        xn_ref[...] = xn.astype(BF16)
        odt_ref[...] = jnp.dot(xn, wdt_ref[...], precision=HIGHEST, preferred_element_type=F32)
        odtT_ref[...] = lax.dot_general(wdtT_ref[...], xn, NT_DIMS, precision=HIGHEST,
                                        preferred_element_type=F32)

    o_ref[...] = _dot(xn_ref[...], w_ref[...])


def _inproj(x2d, g, w_main, wdt, wdtT):
    T, D = x2d.shape
    N = w_main.shape[1]
    H = wdtT.shape[0]
    TM = _pick(T, 512)
    TN = 1024
    return pl.pallas_call(
        _inproj_body,
        grid=(T // TM, N // TN),
        in_specs=[
            pl.BlockSpec((TM, D), lambda i, j: (i, 0)),
            pl.BlockSpec((1, D), lambda i, j: (0, 0)),
            pl.BlockSpec((D, TN), lambda i, j: (0, j)),
            pl.BlockSpec((D, LANES), lambda i, j: (0, 0)),
            pl.BlockSpec((H, D), lambda i, j: (0, 0)),
        ],
        out_specs=[
            pl.BlockSpec((TM, TN), lambda i, j: (i, j)),
            pl.BlockSpec((TM, LANES), lambda i, j: (i, 0)),
            pl.BlockSpec((H, TM), lambda i, j: (0, i)),
        ],
        out_shape=[
            jax.ShapeDtypeStruct((T, N), F32),
            jax.ShapeDtypeStruct((T, LANES), F32),
            jax.ShapeDtypeStruct((H, T), F32),
        ],
        scratch_shapes=[pltpu.VMEM((TM, D), BF16)],
        compiler_params=_cparams(("parallel", "arbitrary")),
        name="inproj",
    )(x2d, g, w_main, wdt, wdtT)


def _rg_gates(xc, wa_ref, wx_ref, ba, bx, lam):
    xb = xc.astype(BF16)
    nb, bw = wa_ref.shape[0], wa_ref.shape[1]
    rs, gs = [], []
    for k in range(nb):
        xk = xb[:, k * bw:(k + 1) * bw]
        rs.append(_dot(xk, wa_ref[k]))
        gs.append(_dot(xk, wx_ref[k]))
    r = _sigmoid(jnp.concatenate(rs, axis=1) + ba)
    i = _sigmoid(jnp.concatenate(gs, axis=1) + bx)
    log_a = -RG_C * r * _softplus(-lam)
    a = jnp.exp(log_a)
    em1 = jnp.tanh(log_a) * (a * a + 1.0)
    return a, jnp.sqrt(-em1) * (i * xc)


def _rg_prompt_body(x_ref, gate_ref, cw_ref, cb_ref, wa_ref, wx_ref, ba_ref, bx_ref, lam_ref, wo_ref,
                    ya_ref, hfin_ref, pad_s, a_s, b_s, h_s):
    Lc, W = x_ref.shape

    @pl.when(pl.program_id(1) == 0)
    def _():
        pad_s[0:SUBLANES, :] = jnp.zeros((SUBLANES, W), F32)
        h_s[...] = jnp.zeros(h_s.shape, F32)

    pad_s[SUBLANES:SUBLANES + Lc, :] = x_ref[...]
    xc = (cb_ref[...]
          + cw_ref[3:4, :] * pad_s[8:8 + Lc, :]
          + cw_ref[2:3, :] * pad_s[7:7 + Lc, :]
          + cw_ref[1:2, :] * pad_s[6:6 + Lc, :]
          + cw_ref[0:1, :] * pad_s[5:5 + Lc, :])
    pad_s[0:SUBLANES, :] = pad_s[Lc:Lc + SUBLANES, :]
    a, bt = _rg_gates(xc, wa_ref, wx_ref, ba_ref[...], bx_ref[...], lam_ref[...])
    a_s[...] = a
    b_s[...] = bt

    def step(t, h):
        h = a_s[pl.ds(t, 1), :] * h + b_s[pl.ds(t, 1), :]
        b_s[pl.ds(t, 1), :] = h
        return h

    h = lax.fori_loop(0, Lc, step, h_s[0:1, :], unroll=8)
    h_s[0:1, :] = h
    y = b_s[...] * _gelu(gate_ref[...])
    ya_ref[...] = _dot(y.astype(BF16), wo_ref[...])
    hfin_ref[0] = h


def _rg_prompt(proj, B, L, col_x, col_g, cw, cb, wa, wx, ba, bx, lam, wo):
    W = cw.shape[1]
    Lc = _pick(L, 256)
    nC = L // Lc
    full = lambda *s: pl.BlockSpec(s, lambda b, c: (0,) * len(s))
    return pl.pallas_call(
        _rg_prompt_body,
        grid=(B, nC),
        in_specs=[
            pl.BlockSpec((Lc, W), lambda b, c: (b * nC + c, col_x)),
            pl.BlockSpec((Lc, W), lambda b, c: (b * nC + c, col_g)),
            full(4, W), full(1, W), full(*wa.shape), full(*wx.shape), full(1, W), full(1, W), full(1, W),
            full(*wo.shape),
        ],
        out_specs=[
            pl.BlockSpec((Lc, wo.shape[1]), lambda b, c: (b * nC + c, 0)),
            pl.BlockSpec((1, 1, W), lambda b, c: (b, 0, 0)),
        ],
        out_shape=[
            jax.ShapeDtypeStruct((B * L, wo.shape[1]), F32),
            jax.ShapeDtypeStruct((B, 1, W), F32),
        ],
        scratch_shapes=[
            pltpu.VMEM((Lc + SUBLANES, W), F32),
            pltpu.VMEM((Lc, W), F32),
            pltpu.VMEM((Lc, W), F32),
            pltpu.VMEM((SUBLANES, W), F32),
        ],
        compiler_params=_cparams(("parallel", "arbitrary")),
        name="rg_prompt",
    )(proj, proj, cw, cb, wa, wx, ba, bx, lam, wo)


def _conv_rolled(u, cw_ref, cb_ref):
    return (cb_ref[...]
            + cw_ref[3:4, :] * u
            + cw_ref[2:3, :] * pltpu.roll(u, 1, axis=0)
            + cw_ref[1:2, :] * pltpu.roll(u, 2, axis=0)
            + cw_ref[0:1, :] * pltpu.roll(u, 3, axis=0))


def _rg_sample_body(x_ref, gate_ref, cpad_ref, hpad_ref, cw_ref, cb_ref, wa_ref, wx_ref, ba_ref, bx_ref,
                    lam_ref, wo_ref, ya_ref, h_ref):
    xc = _conv_rolled(x_ref[...] + cpad_ref[...], cw_ref, cb_ref)
    a, bt = _rg_gates(xc, wa_ref, wx_ref, ba_ref[...], bx_ref[...], lam_ref[...])
    row = lax.broadcasted_iota(jnp.int32, a.shape, 0) & (SAMPLE_ROWS - 1)
    h = hpad_ref[...]
    for k in range(SAMPLE_ROWS // 2, SAMPLE_ROWS):
        h = jnp.where(row == k, a * pltpu.roll(h, 1, axis=0) + bt, h)
    h_ref[...] = h
    ya_ref[...] = _dot((h * _gelu(gate_ref[...])).astype(BF16), wo_ref[...])


def _rg_sample(proj, col_x, col_g, cpad, hpad, cw, cb, wa, wx, ba, bx, lam, wo):
    T = proj.shape[0]
    W = cw.shape[1]
    TM = _pick(T, 256)
    full = lambda *s: pl.BlockSpec(s, lambda i: (0,) * len(s))
    return pl.pallas_call(
        _rg_sample_body,
        grid=(T // TM,),
        in_specs=[
            pl.BlockSpec((TM, W), lambda i: (i, col_x)),
            pl.BlockSpec((TM, W), lambda i: (i, col_g)),
            pl.BlockSpec((TM, W), lambda i: (i, 0)),
            pl.BlockSpec((TM, W), lambda i: (i, 0)),
            full(4, W), full(1, W), full(*wa.shape), full(*wx.shape), full(1, W), full(1, W), full(1, W),
            full(*wo.shape),
        ],
        out_specs=[
            pl.BlockSpec((TM, wo.shape[1]), lambda i: (i, 0)),
            pl.BlockSpec((TM, W), lambda i: (i, 0)),
        ],
        out_shape=[
            jax.ShapeDtypeStruct((T, wo.shape[1]), F32),
            jax.ShapeDtypeStruct((T, W), F32),
        ],
        compiler_params=_cparams(("parallel",)),
        name="rg_sample",
    )(proj, proj, cpad, hpad, cw, cb, wa, wx, ba, bx, lam, wo)


def _ssd_chunk(act_s, dt, dtT, alog_ref, alogT_ref, dexp_ref, seq_shift, G, R, P, N,
               yint_fn, y_s, xw_s):
    Q = act_s.shape[0]
    inner = G * R * P
    a = dt * (-jnp.exp(alog_ref[...]))
    aT = dtT * (-jnp.exp(alogT_ref[...]))
    ri = lax.broadcasted_iota(jnp.int32, (Q, Q), 0)
    ci = lax.broadcasted_iota(jnp.int32, (Q, Q), 1)
    same = (ri >> seq_shift) == (ci >> seq_shift)
    causal = same & (ci <= ri)
    causal_f = jnp.where(causal, 1.0, 0.0).astype(F32)
    same_f = jnp.where(same, 1.0, 0.0).astype(F32)
    acum = jnp.dot(causal_f, a, precision=HIGHEST, preferred_element_type=F32)
    tot = jnp.dot(same_f, a, precision=HIGHEST, preferred_element_type=F32)
    acumT = lax.dot_general(aT, causal_f, NT_DIMS, precision=HIGHEST,
                            preferred_element_type=F32)
    eacum = jnp.exp(acum)
    toend = jnp.exp(tot - acum) * dt
    for g in range(G):
        Bg = act_s[:, inner + g * N:inner + (g + 1) * N].astype(BF16)
        Cg = act_s[:, inner + (G + g) * N:inner + (G + g + 1) * N].astype(BF16)
        cb = _dot_nt(Cg, Bg)
        yint = yint_fn(g, Cg)
        for r in range(R):
            h = g * R + r
            sl = slice(h * P, (h + 1) * P)
            seg = acum[:, h:h + 1] - acumT[h:h + 1, :]
            decay = jnp.exp(jnp.where(causal, seg, NEG_INF))
            wgt = (cb * decay * dtT[h:h + 1, :]).astype(BF16)
            xh = act_s[:, sl]
            yh = _dot(wgt, xh.astype(BF16)) + dexp_ref[:, sl] * xh
            if yint is not None:
                yh = yh + yint[:, r * P:(r + 1) * P] * eacum[:, h:h + 1]
            y_s[:, sl] = yh
            xw_s[:, sl] = xh * toend[:, h:h + 1]
    return tot, eacum


def _ssd_finish(y, z, nw_ref, wo_ref):
    y = y * _silu(z)
    return _dot(_rms(y, nw_ref[...]).astype(BF16), wo_ref[...])


def _ssd_prompt_body(G, R, P, N,
                     xbc_ref, z_ref, dt_ref, dtT_ref, cw_ref, cb_ref, dtb_ref, dtbT_ref, alog_ref, alogT_ref,
                     dexp_ref, nw_ref, wo_ref, yb_ref, st_ref, pad_s, act_s, y_s, xw_s):
    Q = xbc_ref.shape[0]

    @pl.when(pl.program_id(1) == 0)
    def _():
        pad_s[0:SUBLANES, :] = jnp.zeros((SUBLANES, pad_s.shape[1]), F32)
        st_ref[...] = jnp.zeros(st_ref.shape, F32)

    pad_s[SUBLANES:SUBLANES + Q, :] = xbc_ref[...]
    conv = (cb_ref[...]
            + cw_ref[3:4, :] * pad_s[8:8 + Q, :]
            + cw_ref[2:3, :] * pad_s[7:7 + Q, :]
            + cw_ref[1:2, :] * pad_s[6:6 + Q, :]
            + cw_ref[0:1, :] * pad_s[5:5 + Q, :])
    pad_s[0:SUBLANES, :] = pad_s[Q:Q + SUBLANES, :]
    act_s[...] = _silu(conv)
    dt = _softplus(dt_ref[...] + dtb_ref[...])
    dtT = _softplus(dtT_ref[...] + dtbT_ref[...])

    def yint_fn(g, Cg):
        stg = st_ref[0, g * R:(g + 1) * R].reshape(R * P, N)
        return _dot_nt(Cg, stg.astype(BF16))

    seq_shift = Q.bit_length() - 1
    tot, _ = _ssd_chunk(act_s, dt, dtT, alog_ref, alogT_ref, dexp_ref, seq_shift, G, R, P, N,
                        yint_fn, y_s, xw_s)
    dec = jnp.exp(tot[0:1, :])
    inner = G * R * P
    for g in range(G):
        Bg = act_s[:, inner + g * N:inner + (g + 1) * N].astype(BF16)
        xwT = jnp.transpose(xw_s[:, g * R * P:(g + 1) * R * P])
        S = _dot(xwT.astype(BF16), Bg)
        for r in range(R):
            h = g * R + r
            st_ref[0, h] = dec[:, h:h + 1] * st_ref[0, h] + S[r * P:(r + 1) * P, :]
    yb_ref[...] = _ssd_finish(y_s[...], z_ref[...], nw_ref, wo_ref)


def _ssd_prompt(proj, dt, dtT, B, L, col_xbc, col_z, dims, cw, cb, dtb, dtbT, alog, alogT, dexp, nw, wo):
    G, R, P, N = dims
    H = G * R
    inner = H * P
    CD = cw.shape[1]
    Q = _pick(L, SSD_CHUNK)
    nC = L // Q
    full = lambda *s: pl.BlockSpec(s, lambda b, c: (0,) * len(s))
    return pl.pallas_call(
        functools.partial(_ssd_prompt_body, G, R, P, N),
        grid=(B, nC),
        in_specs=[
            pl.BlockSpec((Q, CD), lambda b, c: (b * nC + c, col_xbc)),
            pl.BlockSpec((Q, inner), lambda b, c: (b * nC + c, col_z)),
            pl.BlockSpec((Q, LANES), lambda b, c: (b * nC + c, 0)),
            pl.BlockSpec((H, Q), lambda b, c: (0, b * nC + c)),
            full(4, CD), full(1, CD), full(1, LANES), full(H, 1), full(1, LANES), full(H, 1),
            full(1, inner), full(1, inner), full(*wo.shape),
        ],
        out_specs=[
            pl.BlockSpec((Q, wo.shape[1]), lambda b, c: (b * nC + c, 0)),
            pl.BlockSpec((1, H, P, N), lambda b, c: (b, 0, 0, 0)),
        ],
        out_shape=[
            jax.ShapeDtypeStruct((B * L, wo.shape[1]), F32),
            jax.ShapeDtypeStruct((B, H, P, N), F32),
        ],
        scratch_shapes=[
            pltpu.VMEM((Q + SUBLANES, CD), F32),
            pltpu.VMEM((Q, CD), F32),
            pltpu.VMEM((Q, inner), F32),
            pltpu.VMEM((Q, inner), F32),
        ],
        compiler_params=_cparams(("parallel", "arbitrary")),
        name="ssd_prompt",
    )(proj, proj, dt, dtT, cw, cb, dtb, dtbT, alog, alogT, dexp, nw, wo)


def _ssd_sample_body(G, R, P, N,
                     xbc_ref, cpad_ref, z_ref, dt_ref, dtT_ref, h0_ref, cw_ref, cb_ref, dtb_ref, dtbT_ref,
                     alog_ref, alogT_ref, dexp_ref, nw_ref, wo_ref, yb_ref, st_ref,
                     act_s, y_s, xw_s, yint_s, xwT_s, tot_s, eacum_s):
    Q = xbc_ref.shape[0]
    inner = G * R * P
    bi = pl.program_id(1)
    half = SAMPLE_ROWS // 2

    @pl.when(bi == 0)
    def _():
        act_s[...] = _silu(_conv_rolled(xbc_ref[...] + cpad_ref[...], cw_ref, cb_ref))
        rows = lax.broadcasted_iota(jnp.int32, (Q, LANES), 0) & (SAMPLE_ROWS - 1)
        cols = lax.broadcasted_iota(jnp.int32, dtT_ref.shape, 1) & (SAMPLE_ROWS - 1)
        dt = jnp.where(rows >= half, _softplus(dt_ref[...] + dtb_ref[...]), 0.0)
        dtT = jnp.where(cols >= half, _softplus(dtT_ref[...] + dtbT_ref[...]), 0.0)
        seq_shift = SAMPLE_ROWS.bit_length() - 1
        tot, eacum = _ssd_chunk(act_s, dt, dtT, alog_ref, alogT_ref, dexp_ref, seq_shift, G, R, P, N,
                                lambda g, Cg: None, y_s, xw_s)
        tot_s[...] = tot
        eacum_s[...] = eacum
        for g in range(G):
            xwT_s[g] = jnp.transpose(xw_s[:, g * R * P:(g + 1) * R * P]).astype(BF16)

    r0 = pl.multiple_of(bi * SAMPLE_ROWS, SAMPLE_ROWS)
    rid = lax.broadcasted_iota(jnp.int32, (Q, N), 0)
    mine = (rid >= r0) & (rid < r0 + SAMPLE_ROWS)
    dec = jnp.exp(tot_s[pl.ds(r0, 1), :])
    for g in range(G):
        Cb = act_s[pl.ds(r0, SAMPLE_ROWS), inner + (G + g) * N:inner + (G + g + 1) * N]
        stg = h0_ref[0, g * R:(g + 1) * R].reshape(R * P, N)
        yint_s[pl.ds(r0, SAMPLE_ROWS), g * R * P:(g + 1) * R * P] = _dot_nt(Cb, stg)
        Bg = jnp.where(mine, act_s[:, inner + g * N:inner + (g + 1) * N], 0.0).astype(BF16)
        S = _dot(xwT_s[g], Bg)
        for r in range(R):
            h = g * R + r
            st_ref[0, h] = dec[:, h:h + 1] * h0_ref[0, h] + S[r * P:(r + 1) * P, :]

    @pl.when(bi == pl.num_programs(1) - 1)
    def _():
        for h in range(G * R):
            sl = slice(h * P, (h + 1) * P)
            y_s[:, sl] = y_s[:, sl] + yint_s[:, sl] * eacum_s[:, h:h + 1]
        yb_ref[...] = _ssd_finish(y_s[...], z_ref[...], nw_ref, wo_ref)


def _ssd_sample(proj, dt, dtT, cpad, h0, col_xbc, col_z, dims, cw, cb, dtb, dtbT, alog, alogT, dexp, nw, wo):
    G, R, P, N = dims
    H = G * R
    inner = H * P
    CD = cw.shape[1]
    T = proj.shape[0]
    Bs = h0.shape[0]
    Q = _pick(T, SSD_CHUNK)
    nb = Q // SAMPLE_ROWS
    full = lambda *s: pl.BlockSpec(s, lambda i, j: (0,) * len(s))
    return pl.pallas_call(
        functools.partial(_ssd_sample_body, G, R, P, N),
        grid=(T // Q, nb),
        in_specs=[
            pl.BlockSpec((Q, CD), lambda i, j: (i, col_xbc)),
            pl.BlockSpec((Q, CD), lambda i, j: (i, 0)),
            pl.BlockSpec((Q, inner), lambda i, j: (i, col_z)),
            pl.BlockSpec((Q, LANES), lambda i, j: (i, 0)),
            pl.BlockSpec((H, Q), lambda i, j: (0, i)),
            pl.BlockSpec((1, H, P, N), lambda i, j: (i * nb + j, 0, 0, 0)),
            full(4, CD), full(1, CD), full(1, LANES), full(H, 1), full(1, LANES), full(H, 1),
            full(1, inner), full(1, inner), full(*wo.shape),
        ],
        out_specs=[
            pl.BlockSpec((Q, wo.shape[1]), lambda i, j: (i, 0)),
            pl.BlockSpec((1, H, P, N), lambda i, j: (i * nb + j, 0, 0, 0)),
        ],
        out_shape=[
            jax.ShapeDtypeStruct((T, wo.shape[1]), F32),
            jax.ShapeDtypeStruct((Bs, H, P, N), F32),
        ],
        scratch_shapes=[
            pltpu.VMEM((Q, CD), F32),
            pltpu.VMEM((Q, inner), F32),
            pltpu.VMEM((Q, inner), F32),
            pltpu.VMEM((Q, inner), F32),
            pltpu.VMEM((G, R * P, Q), BF16),
            pltpu.VMEM((Q, LANES), F32),
            pltpu.VMEM((Q, LANES), F32),
        ],
        compiler_params=_cparams(("parallel", "arbitrary")),
        name="ssd_sample",
    )(proj, cpad, proj, dt, dtT, h0, cw, cb, dtb, dtbT, alog, alogT, dexp, nw, wo)


def _merge_body(x_ref, ya_ref, yb_ref, ga_ref, gb_ref, wout_ref, nf_ref, x1_ref, h2_ref):
    m = _sigmoid(ga_ref[...]) * ya_ref[...] + _sigmoid(gb_ref[...]) * yb_ref[...]
    x1 = x_ref[...] + _dot(m.astype(BF16), wout_ref[...])
    x1_ref[...] = x1
    h2_ref[...] = _rms(x1, nf_ref[...]).astype(BF16)


def _merge(x2d, ya, yb, proj, col_ga, col_gb, wout, nf):
    T, D = x2d.shape
    TM = _pick(T, 512)
    row = lambda c: pl.BlockSpec((TM, D), lambda i: (i, c))
    return pl.pallas_call(
        _merge_body,
        grid=(T // TM,),
        in_specs=[row(0), row(0), row(0), row(col_ga), row(col_gb),
                  pl.BlockSpec(wout.shape, lambda i: (0, 0)), pl.BlockSpec((1, D), lambda i: (0, 0))],
        out_specs=[row(0), row(0)],
        out_shape=[jax.ShapeDtypeStruct((T, D), F32), jax.ShapeDtypeStruct((T, D), BF16)],
        compiler_params=_cparams(("parallel",)),
        name="merge",
    )(x2d, ya, yb, proj, proj, wout, nf)


def _staircase(k):
    return [(ka, k // (ka + 1)) for ka in range(k)]


def _kth_largest(cur, k):
    m = None
    for _ in range(k):
        m = jnp.max(cur, axis=0, keepdims=True)
        cur = jnp.where(cur == m, NEG_INF, cur)
    return m


def _route_body(h2_ref, wq_ref, keys_ref, sa_ref, ea_ref, sb_ref, eb_ref, tau_ref, sv_s, cand_s):
    NH = keys_ref.shape[0]
    NK, KH = keys_ref.shape[2], keys_ref.shape[3]
    K = PEER_TOPK
    q = _dot(h2_ref[...], wq_ref[...]).astype(BF16)
    s_refs = (sa_ref, sb_ref)
    e_refs = (ea_ref, eb_ref)
    cand_s[...] = jnp.full(cand_s.shape, NEG_INF, F32)
    for h in range(NH):
        for s in range(2):
            off = (h * 2 + s) * KH
            sT = _dot_nt(keys_ref[h, s], q[:, off:off + KH])
            cur = sT
            for k in range(K):
                m = jnp.max(cur, axis=0, keepdims=True)
                sv_s[s, k:k + 1, :] = m
                cur = jnp.where(cur == m, NEG_INF, cur)
            sM = jnp.where(sT >= sv_s[s, K - 1:K, :], sT, NEG_INF)
            s_refs[s][h] = sM
            e_refs[s][h] = jnp.exp(sM - sv_s[s, 0:1, :])
        off = 0
        for ka, nb in _staircase(K):
            cand_s[off:off + nb, :] = sv_s[0, ka:ka + 1, :] + sv_s[1, 0:nb, :]
            off += nb
        cand = cand_s[...]
        tau = _kth_largest(cand, K)
        top = sv_s[0, 0:1, :] + sv_s[1, 0:1, :]
        z = jnp.sum(jnp.where(cand >= tau, jnp.exp(cand - top), 0.0), axis=0, keepdims=True)
        ea_ref[h] = ea_ref[h] / z
        tau_ref[h] = tau


def _route(h2, wq, keys):
    T, D = h2.shape
    NH, _, NK, KH = keys.shape
    TB = _pick(T, 512)
    ncand = sum(nb for _, nb in _staircase(PEER_TOPK))
    ncand_pad = -(-ncand // SUBLANES) * SUBLANES
    tok = lambda r: pl.BlockSpec((NH, r, TB), lambda i: (0, 0, i))
    shp = lambda r: jax.ShapeDtypeStruct((NH, r, T), F32)
    return pl.pallas_call(
        _route_body,
        grid=(T // TB,),
        in_specs=[pl.BlockSpec((TB, D), lambda i: (i, 0)),
                  pl.BlockSpec(wq.shape, lambda i: (0, 0)),
                  pl.BlockSpec(keys.shape, lambda i: (0, 0, 0, 0))],
        out_specs=[tok(NK), tok(NK), tok(NK), tok(NK), tok(1)],
        out_shape=[shp(NK), shp(NK), shp(NK), shp(NK), shp(1)],
        scratch_shapes=[pltpu.VMEM((2, PEER_TOPK, TB), F32), pltpu.VMEM((ncand_pad, TB), F32)],
        compiler_params=_cparams(("parallel",)),
        name="peer_route",
    )(h2, wq, keys)


def _peer_body(h2_ref, x1_ref, u_ref, vt_ref, sa_ref, ea_ref, sb_ref, eb_ref, tau_ref, nfin_ref, y_ref,
               s_s, a_s, acc_s):
    ec = pl.program_id(1)
    NH, NK, TB = sb_ref.shape
    ni = u_ref.shape[0] // NK

    @pl.when(ec == 0)
    def _():
        acc_s[...] = jnp.zeros(acc_s.shape, F32)

    s_s[...] = _dot_nt(u_ref[...], h2_ref[...])

    for tl in range(TB // LANES):
        ln = slice(tl * LANES, (tl + 1) * LANES)
        rows = [pl.ds(pl.multiple_of(h * NK + ec * ni, SUBLANES), ni) for h in range(NH)]
        sa = [sa_ref[rows[h], ln] for h in range(NH)]
        ea = [ea_ref[rows[h], ln] for h in range(NH)]
        for il in range(ni):
            w = jnp.zeros((NK, LANES), F32)
            for h in range(NH):
                hit = (sa[h][il:il + 1, :] + sb_ref[h, :, ln]) >= tau_ref[h, :, ln]
                w = w + ea[h][il:il + 1, :] * jnp.where(hit, eb_ref[h, :, ln], 0.0)
            e = slice(il * NK, (il + 1) * NK)
            a_s[e, ln] = (_gelu(s_s[e, ln]) * w).astype(BF16)
    acc_s[...] += _dot(vt_ref[...], a_s[...])

    @pl.when(ec == pl.num_programs(1) - 1)
    def _():
        x2 = x1_ref[...] + jnp.transpose(acc_s[...])
        y_ref[...] = _rms(x2, nfin_ref[...])


def _peer(h2, x1, u, vt, sa, ea, sb, eb, tau, nfin):
    T, D = h2.shape
    E = u.shape[0]
    NH, NK, _ = sa.shape
    TB = _pick(T, 512)
    EC = SUBLANES * NK
    assert TB % LANES == 0 and E % EC == 0
    tok = lambda r: pl.BlockSpec((NH, r, TB), lambda i, j: (0, 0, i))
    flat = pl.BlockSpec((NH * NK, TB), lambda i, j: (0, i))
    return pl.pallas_call(
        _peer_body,
        grid=(T // TB, E // EC),
        in_specs=[pl.BlockSpec((TB, D), lambda i, j: (i, 0)),
                  pl.BlockSpec((TB, D), lambda i, j: (i, 0)),
                  pl.BlockSpec((EC, D), lambda i, j: (j, 0)),
                  pl.BlockSpec((D, EC), lambda i, j: (0, j)),
                  flat, flat, tok(NK), tok(NK), tok(1),
                  pl.BlockSpec((1, D), lambda i, j: (0, 0))],
        out_specs=pl.BlockSpec((TB, D), lambda i, j: (i, 0)),
        out_shape=jax.ShapeDtypeStruct((T, D), F32),
        scratch_shapes=[pltpu.VMEM((EC, TB), F32), pltpu.VMEM((EC, TB), BF16), pltpu.VMEM((D, TB), F32)],
        compiler_params=_cparams(("parallel", "arbitrary")),
        name="peer_mix",
    )(h2, x1, u, vt, sa.reshape(NH * NK, T), ea.reshape(NH * NK, T), sb, eb, tau, nfin)


def _pad_lanes(row):
    return jnp.pad(row, ((0, 0), (0, LANES - row.shape[1])))


def kernel(x_prompt, x_sample, state_rg_h, state_rg_conv, state_ssd_h, state_ssd_conv, norm_mix, w_in, rg_conv_w, rg_conv_b, rg_wa, rg_ba, rg_wx, rg_bx, rg_lam, w_rg_out, ssd_conv_w, ssd_conv_b, ssd_dt_bias, ssd_a_log, ssd_d, ssd_norm, w_ssd_out, w_out, norm_ffn, peer_wq, peer_keys, peer_u, peer_v, norm_final):
    depth = w_in.shape[0]
    B, L, D = x_prompt.shape
    Bs, Ls, _ = x_sample.shape
    assert depth == 1 and Ls == SAMPLE_ROWS // 2 and rg_conv_w.shape[1] == 4 and L >= 3
    W = rg_conv_w.shape[2]
    H = ssd_a_log.shape[1]
    inner = ssd_norm.shape[1]
    CD = ssd_conv_w.shape[2]
    P = inner // H
    N = state_ssd_h.shape[-1]
    G = (CD - inner) // (2 * N)
    R = H // G
    dims = (G, R, P, N)
    assert W == D and inner == 2 * D and CD == 3 * D and H % SUBLANES == 0 and H <= LANES
    o_x, o_g, o_z, o_xbc, o_dt, o_m = 0, W, 2 * W, 2 * W + inner, 2 * W + inner + CD, 2 * W + inner + CD + H
    col_xbc, col_x, col_z, col_g, col_ga, col_gb = 0, CD // W, (CD + W) // inner, (CD + W + inner) // W, \
        (CD + 2 * W + inner) // D, (CD + 2 * W + inner + D) // D

    yp = x_prompt.reshape(B * L, D)
    ys = x_sample.reshape(Bs * Ls, D)
    outs = {k: [] for k in ("prh", "prc", "psh", "psc", "srh", "src", "ssh", "ssc")}
    lead = SAMPLE_ROWS - Ls
    for l in range(depth):
        wi = w_in[l]
        w_main = jnp.concatenate([wi[:, o_xbc:o_dt], wi[:, o_x:o_g], wi[:, o_z:o_xbc], wi[:, o_g:o_z],
                                  wi[:, o_m:]], axis=1).astype(BF16)
        wdtT = wi[:, o_dt:o_m].T
        wdt = _pad_lanes(wi[:, o_dt:o_m])
        g_mix = norm_mix[l][None]
        rg_w = (rg_conv_w[l], rg_conv_b[l][None], rg_wa[l].astype(BF16), rg_wx[l].astype(BF16),
                rg_ba[l][None], rg_bx[l][None], rg_lam[l][None], w_rg_out[l].astype(BF16))
        ssd_w = (ssd_conv_w[l], ssd_conv_b[l][None], _pad_lanes(ssd_dt_bias[l][None]), ssd_dt_bias[l][:, None],
                 _pad_lanes(ssd_a_log[l][None]), ssd_a_log[l][:, None], jnp.repeat(ssd_d[l], P)[None],
                 ssd_norm[l][None], w_ssd_out[l].astype(BF16))
        wout = w_out[l].astype(BF16)
        nf = norm_ffn[l][None]
        wq = peer_wq[l].astype(BF16)
        keys = peer_keys[l].astype(BF16)
        u = peer_u[l].astype(BF16)
        vt = peer_v[l].astype(BF16).T
        nfin = norm_final[None]

        proj, dt, dtT = _inproj(yp, g_mix, w_main, wdt, wdtT)
        ya, rgh = _rg_prompt(proj, B, L, col_x, col_g, *rg_w)
        yb, ssh = _ssd_prompt(proj, dt, dtT, B, L, col_xbc, col_z, dims, *ssd_w)
        p3 = proj.reshape(B, L, -1)
        outs["prh"].append(rgh.reshape(B, W))
        outs["prc"].append(p3[:, L - 3:, CD:CD + W])
        outs["psh"].append(ssh)
        outs["psc"].append(p3[:, L - 3:, :CD])
        x1, h2 = _merge(yp, ya, yb, proj, col_ga, col_gb, wout, nf)
        sa, ea, sb, eb, tau = _route(h2, wq, keys)
        yp = _peer(h2, x1, u, vt, sa, ea, sb, eb, tau, nfin)

        xe = jnp.pad(ys.reshape(Bs, Ls, D), ((0, 0), (lead, 0), (0, 0))).reshape(Bs * SAMPLE_ROWS, D)
        proj_e, dt_e, dtT_e = _inproj(xe, g_mix, w_main, wdt, wdtT)
        rows = lambda a3: a3.reshape(Bs * SAMPLE_ROWS, a3.shape[-1])
        rg_cpad = rows(jnp.pad(state_rg_conv[l], ((0, 0), (lead - 3, Ls), (0, 0))))
        rg_hpad = rows(jnp.pad(state_rg_h[l][:, None, :], ((0, 0), (lead - 1, Ls), (0, 0))))
        ssd_cpad = rows(jnp.pad(state_ssd_conv[l], ((0, 0), (lead - 3, Ls), (0, 0))))
        ya_e, h_e = _rg_sample(proj_e, col_x, col_g, rg_cpad, rg_hpad, *rg_w)
        yb_e, ssh_s = _ssd_sample(proj_e, dt_e, dtT_e, ssd_cpad, state_ssd_h[l], col_xbc, col_z, dims, *ssd_w)
        toks = lambda a2: a2.reshape(Bs, SAMPLE_ROWS, -1)[:, lead:].reshape(Bs * Ls, -1)
        p3 = proj_e.reshape(Bs, SAMPLE_ROWS, -1)
        outs["srh"].append(h_e.reshape(Bs, SAMPLE_ROWS, W)[:, -1])
        outs["src"].append(p3[:, SAMPLE_ROWS - 3:, CD:CD + W])
        outs["ssh"].append(ssh_s)
        outs["ssc"].append(p3[:, SAMPLE_ROWS - 3:, :CD])
        proj_s = toks(proj_e)
        x1, h2 = _merge(ys, toks(ya_e), toks(yb_e), proj_s, col_ga, col_gb, wout, nf)
        sa, ea, sb, eb, tau = _route(h2, wq, keys)
        ys = _peer(h2, x1, u, vt, sa, ea, sb, eb, tau, nfin)

    st = lambda k: jnp.stack(outs[k])
    return (yp.reshape(B, L, D), ys.reshape(Bs, Ls, D), st("prh"), st("prc"), st("psh"), st("psc"),
            st("srh"), st("src"), st("ssh"), st("ssc"))
```

```python
import functools

import jax
import jax.numpy as jnp
from jax import lax
from jax.experimental import pallas as pl
from jax.experimental.pallas import tpu as pltpu

F32 = jnp.float32
BF16 = jnp.bfloat16
EPS = 1e-6
RG_C = 8.0
PEER_TOPK = 16
SSD_CHUNK = 128
LANES = 128
SUBLANES = 8
SAMPLE_ROWS = 8
NEG_INF = float("-inf")
HIGHEST = lax.Precision.HIGHEST
NT_DIMS = (((1,), (1,)), ((), ()))
VMEM_LIMIT = 56 * 1024 * 1024


def _cparams(sem):
    return pltpu.CompilerParams(dimension_semantics=sem, vmem_limit_bytes=VMEM_LIMIT)


def _pick(n, pref):
    t = min(n, pref)
    while n % t:
        t -= SUBLANES
    return t


def _sigmoid(x):
    return 1.0 / (1.0 + jnp.exp(-x))


def _silu(x):
    return x * _sigmoid(x)


def _softplus(x):
    return jnp.maximum(x, 0.0) + jnp.log1p(jnp.exp(-jnp.abs(x)))


def _gelu(x):
    k1 = -2.0 * 0.7978845608028654 * 1.4426950408889634
    return x / (1.0 + jnp.exp2(x * (k1 + (k1 * 0.044715) * (x * x))))


def _rms(x, g):
    return x * lax.rsqrt(jnp.mean(x * x, axis=-1, keepdims=True) + EPS) * g


def _dot(a, b):
    return jnp.dot(a, b, preferred_element_type=F32)


def _dot_nt(a, b):
    return lax.dot_general(a, b, NT_DIMS, preferred_element_type=F32)


def _inproj_body(x_ref, g_ref, w_ref, wdt_ref, wdtT_ref, o_ref, odt_ref, odtT_ref, xn_ref):
    @pl.when(pl.program_id(1) == 0)
    def _():
        xn = _rms(x_ref[...], g_ref[...])
        xn_ref[...] = xn.astype(BF16)
        odt_ref[...] = jnp.dot(xn, wdt_ref[...], precision=HIGHEST, preferred_element_type=F32)
        odtT_ref[...] = lax.dot_general(wdtT_ref[...], xn, NT_DIMS, precision=HIGHEST,
                                        preferred_element_type=F32)

    o_ref[...] = _dot(xn_ref[...], w_ref[...])


def _inproj(x2d, g, w_main, wdt, wdtT):
    T, D = x2d.shape
    N = w_main.shape[1]
    H = wdtT.shape[0]
    TM = _pick(T, 512)
    TN = 1024
    return pl.pallas_call(
        _inproj_body,
        grid=(T // TM, N // TN),
        in_specs=[
            pl.BlockSpec((TM, D), lambda i, j: (i, 0)),
            pl.BlockSpec((1, D), lambda i, j: (0, 0)),
            pl.BlockSpec((D, TN), lambda i, j: (0, j)),
            pl.BlockSpec((D, LANES), lambda i, j: (0, 0)),
            pl.BlockSpec((H, D), lambda i, j: (0, 0)),
        ],
        out_specs=[
            pl.BlockSpec((TM, TN), lambda i, j: (i, j)),
            pl.BlockSpec((TM, LANES), lambda i, j: (i, 0)),
            pl.BlockSpec((H, TM), lambda i, j: (0, i)),
        ],
        out_shape=[
            jax.ShapeDtypeStruct((T, N), F32),
            jax.ShapeDtypeStruct((T, LANES), F32),
            jax.ShapeDtypeStruct((H, T), F32),
        ],
        scratch_shapes=[pltpu.VMEM((TM, D), BF16)],
        compiler_params=_cparams(("parallel", "arbitrary")),
        name="inproj",
    )(x2d, g, w_main, wdt, wdtT)


def _rg_gates(xc, wa_ref, wx_ref, ba, bx, lam):
    xb = xc.astype(BF16)
    nb, bw = wa_ref.shape[0], wa_ref.shape[1]
    rs, gs = [], []
    for k in range(nb):
        xk = xb[:, k * bw:(k + 1) * bw]
        rs.append(_dot(xk, wa_ref[k]))
        gs.append(_dot(xk, wx_ref[k]))
    r = _sigmoid(jnp.concatenate(rs, axis=1) + ba)
    i = _sigmoid(jnp.concatenate(gs, axis=1) + bx)
    log_a = -RG_C * r * _softplus(-lam)
    a = jnp.exp(log_a)
    em1 = jnp.tanh(log_a) * (a * a + 1.0)
    return a, jnp.sqrt(-em1) * (i * xc)


def _rg_prompt_body(x_ref, gate_ref, cw_ref, cb_ref, wa_ref, wx_ref, ba_ref, bx_ref, lam_ref, wo_ref,
                    ya_ref, hfin_ref, pad_s, a_s, b_s, h_s):
    Lc, W = x_ref.shape

    @pl.when(pl.program_id(1) == 0)
    def _():
        pad_s[0:SUBLANES, :] = jnp.zeros((SUBLANES, W), F32)
        h_s[...] = jnp.zeros(h_s.shape, F32)

    pad_s[SUBLANES:SUBLANES + Lc, :] = x_ref[...]
    xc = (cb_ref[...]
          + cw_ref[3:4, :] * pad_s[8:8 + Lc, :]
          + cw_ref[2:3, :] * pad_s[7:7 + Lc, :]
          + cw_ref[1:2, :] * pad_s[6:6 + Lc, :]
          + cw_ref[0:1, :] * pad_s[5:5 + Lc, :])
    pad_s[0:SUBLANES, :] = pad_s[Lc:Lc + SUBLANES, :]
    a, bt = _rg_gates(xc, wa_ref, wx_ref, ba_ref[...], bx_ref[...], lam_ref[...])
    a_s[...] = a
    b_s[...] = bt

    def step(t, h):
        h = a_s[pl.ds(t, 1), :] * h + b_s[pl.ds(t, 1), :]
        b_s[pl.ds(t, 1), :] = h
        return h

    h = lax.fori_loop(0, Lc, step, h_s[0:1, :], unroll=8)
    h_s[0:1, :] = h
    y = b_s[...] * _gelu(gate_ref[...])
    ya_ref[...] = _dot(y.astype(BF16), wo_ref[...])
    hfin_ref[0] = h


def _rg_prompt(proj, B, L, col_x, col_g, cw, cb, wa, wx, ba, bx, lam, wo):
    W = cw.shape[1]
    Lc = _pick(L, 256)
    nC = L // Lc
    full = lambda *s: pl.BlockSpec(s, lambda b, c: (0,) * len(s))
    return pl.pallas_call(
        _rg_prompt_body,
        grid=(B, nC),
        in_specs=[
            pl.BlockSpec((Lc, W), lambda b, c: (b * nC + c, col_x)),
            pl.BlockSpec((Lc, W), lambda b, c: (b * nC + c, col_g)),
            full(4, W), full(1, W), full(*wa.shape), full(*wx.shape), full(1, W), full(1, W), full(1, W),
            full(*wo.shape),
        ],
        out_specs=[
            pl.BlockSpec((Lc, wo.shape[1]), lambda b, c: (b * nC + c, 0)),
            pl.BlockSpec((1, 1, W), lambda b, c: (b, 0, 0)),
        ],
        out_shape=[
            jax.ShapeDtypeStruct((B * L, wo.shape[1]), F32),
            jax.ShapeDtypeStruct((B, 1, W), F32),
        ],
        scratch_shapes=[
            pltpu.VMEM((Lc + SUBLANES, W), F32),
            pltpu.VMEM((Lc, W), F32),
            pltpu.VMEM((Lc, W), F32),
            pltpu.VMEM((SUBLANES, W), F32),
        ],
        compiler_params=_cparams(("parallel", "arbitrary")),
        name="rg_prompt",
    )(proj, proj, cw, cb, wa, wx, ba, bx, lam, wo)


def _conv_rolled(u, cw_ref, cb_ref):
    return (cb_ref[...]
            + cw_ref[3:4, :] * u
            + cw_ref[2:3, :] * pltpu.roll(u, 1, axis=0)
            + cw_ref[1:2, :] * pltpu.roll(u, 2, axis=0)
            + cw_ref[0:1, :] * pltpu.roll(u, 3, axis=0))


def _rg_sample_body(x_ref, gate_ref, cpad_ref, hpad_ref, cw_ref, cb_ref, wa_ref, wx_ref, ba_ref, bx_ref,
                    lam_ref, wo_ref, ya_ref, h_ref):
    xc = _conv_rolled(x_ref[...] + cpad_ref[...], cw_ref, cb_ref)
    a, bt = _rg_gates(xc, wa_ref, wx_ref, ba_ref[...], bx_ref[...], lam_ref[...])
    row = lax.broadcasted_iota(jnp.int32, a.shape, 0) & (SAMPLE_ROWS - 1)
    h = hpad_ref[...]
    for k in range(SAMPLE_ROWS // 2, SAMPLE_ROWS):
        h = jnp.where(row == k, a * pltpu.roll(h, 1, axis=0) + bt, h)
    h_ref[...] = h
    ya_ref[...] = _dot((h * _gelu(gate_ref[...])).astype(BF16), wo_ref[...])


def _rg_sample(proj, col_x, col_g, cpad, hpad, cw, cb, wa, wx, ba, bx, lam, wo):
    T = proj.shape[0]
    W = cw.shape[1]
    TM = _pick(T, 256)
    full = lambda *s: pl.BlockSpec(s, lambda i: (0,) * len(s))
    return pl.pallas_call(
        _rg_sample_body,
        grid=(T // TM,),
        in_specs=[
            pl.BlockSpec((TM, W), lambda i: (i, col_x)),
            pl.BlockSpec((TM, W), lambda i: (i, col_g)),
            pl.BlockSpec((TM, W), lambda i: (i, 0)),
            pl.BlockSpec((TM, W), lambda i: (i, 0)),
            full(4, W), full(1, W), full(*wa.shape), full(*wx.shape), full(1, W), full(1, W), full(1, W),
            full(*wo.shape),
        ],
        out_specs=[
            pl.BlockSpec((TM, wo.shape[1]), lambda i: (i, 0)),
            pl.BlockSpec((TM, W), lambda i: (i, 0)),
        ],
        out_shape=[
            jax.ShapeDtypeStruct((T, wo.shape[1]), F32),
            jax.ShapeDtypeStruct((T, W), F32),
        ],
        compiler_params=_cparams(("parallel",)),
        name="rg_sample",
    )(proj, proj, cpad, hpad, cw, cb, wa, wx, ba, bx, lam, wo)


def _ssd_chunk(act_s, dt, dtT, alog_ref, alogT_ref, dexp_ref, seq_shift, G, R, P, N,
               yint_fn, y_s, xw_s):
    Q = act_s.shape[0]
    inner = G * R * P
    a = dt * (-jnp.exp(alog_ref[...]))
    aT = dtT * (-jnp.exp(alogT_ref[...]))
    ri = lax.broadcasted_iota(jnp.int32, (Q, Q), 0)
    ci = lax.broadcasted_iota(jnp.int32, (Q, Q), 1)
    same = (ri >> seq_shift) == (ci >> seq_shift)
    causal = same & (ci <= ri)
    causal_f = jnp.where(causal, 1.0, 0.0).astype(F32)
    same_f = jnp.where(same, 1.0, 0.0).astype(F32)
    acum = jnp.dot(causal_f, a, precision=HIGHEST, preferred_element_type=F32)
    tot = jnp.dot(same_f, a, precision=HIGHEST, preferred_element_type=F32)
    acumT = lax.dot_general(aT, causal_f, NT_DIMS, precision=HIGHEST,
                            preferred_element_type=F32)
    eacum = jnp.exp(acum)
    toend = jnp.exp(tot - acum) * dt
    for g in range(G):
        Bg = act_s[:, inner + g * N:inner + (g + 1) * N].astype(BF16)
        Cg = act_s[:, inner + (G + g) * N:inner + (G + g + 1) * N].astype(BF16)
        cb = _dot_nt(Cg, Bg)
        yint = yint_fn(g, Cg)
        for r in range(R):
            h = g * R + r
            sl = slice(h * P, (h + 1) * P)
            seg = acum[:, h:h + 1] - acumT[h:h + 1, :]
            decay = jnp.exp(jnp.where(causal, seg, NEG_INF))
            wgt = (cb * decay * dtT[h:h + 1, :]).astype(BF16)
            xh = act_s[:, sl]
            yh = _dot(wgt, xh.astype(BF16)) + dexp_ref[:, sl] * xh
            if yint is not None:
                yh = yh + yint[:, r * P:(r + 1) * P] * eacum[:, h:h + 1]
            y_s[:, sl] = yh
            xw_s[:, sl] = xh * toend[:, h:h + 1]
    return tot, eacum


def _ssd_finish(y, z, nw_ref, wo_ref):
    y = y * _silu(z)
    return _dot(_rms(y, nw_ref[...]).astype(BF16), wo_ref[...])


def _ssd_prompt_body(G, R, P, N,
                     xbc_ref, z_ref, dt_ref, dtT_ref, cw_ref, cb_ref, dtb_ref, dtbT_ref, alog_ref, alogT_ref,
                     dexp_ref, nw_ref, wo_ref, yb_ref, st_ref, pad_s, act_s, y_s, xw_s):
    Q = xbc_ref.shape[0]

    @pl.when(pl.program_id(1) == 0)
    def _():
        pad_s[0:SUBLANES, :] = jnp.zeros((SUBLANES, pad_s.shape[1]), F32)
        st_ref[...] = jnp.zeros(st_ref.shape, F32)

    pad_s[SUBLANES:SUBLANES + Q, :] = xbc_ref[...]
    conv = (cb_ref[...]
            + cw_ref[3:4, :] * pad_s[8:8 + Q, :]
            + cw_ref[2:3, :] * pad_s[7:7 + Q, :]
            + cw_ref[1:2, :] * pad_s[6:6 + Q, :]
            + cw_ref[0:1, :] * pad_s[5:5 + Q, :])
    pad_s[0:SUBLANES, :] = pad_s[Q:Q + SUBLANES, :]
    act_s[...] = _silu(conv)
    dt = _softplus(dt_ref[...] + dtb_ref[...])
    dtT = _softplus(dtT_ref[...] + dtbT_ref[...])

    def yint_fn(g, Cg):
        stg = st_ref[0, g * R:(g + 1) * R].reshape(R * P, N)
        return _dot_nt(Cg, stg.astype(BF16))

    seq_shift = Q.bit_length() - 1
    tot, _ = _ssd_chunk(act_s, dt, dtT, alog_ref, alogT_ref, dexp_ref, seq_shift, G, R, P, N,
                        yint_fn, y_s, xw_s)
    dec = jnp.exp(tot[0:1, :])
    inner = G * R * P
    for g in range(G):
        Bg = act_s[:, inner + g * N:inner + (g + 1) * N].astype(BF16)
        xwT = jnp.transpose(xw_s[:, g * R * P:(g + 1) * R * P])
        S = _dot(xwT.astype(BF16), Bg)
        for r in range(R):
            h = g * R + r
            st_ref[0, h] = dec[:, h:h + 1] * st_ref[0, h] + S[r * P:(r + 1) * P, :]
    yb_ref[...] = _ssd_finish(y_s[...], z_ref[...], nw_ref, wo_ref)


def _ssd_prompt(proj, dt, dtT, B, L, col_xbc, col_z, dims, cw, cb, dtb, dtbT, alog, alogT, dexp, nw, wo):
    G, R, P, N = dims
    H = G * R
    inner = H * P
    CD = cw.shape[1]
    Q = _pick(L, SSD_CHUNK)
    nC = L // Q
    full = lambda *s: pl.BlockSpec(s, lambda b, c: (0,) * len(s))
    return pl.pallas_call(
        functools.partial(_ssd_prompt_body, G, R, P, N),
        grid=(B, nC),
        in_specs=[
            pl.BlockSpec((Q, CD), lambda b, c: (b * nC + c, col_xbc)),
            pl.BlockSpec((Q, inner), lambda b, c: (b * nC + c, col_z)),
            pl.BlockSpec((Q, LANES), lambda b, c: (b * nC + c, 0)),
            pl.BlockSpec((H, Q), lambda b, c: (0, b * nC + c)),
            full(4, CD), full(1, CD), full(1, LANES), full(H, 1), full(1, LANES), full(H, 1),
            full(1, inner), full(1, inner), full(*wo.shape),
        ],
        out_specs=[
            pl.BlockSpec((Q, wo.shape[1]), lambda b, c: (b * nC + c, 0)),
            pl.BlockSpec((1, H, P, N), lambda b, c: (b, 0, 0, 0)),
        ],
        out_shape=[
            jax.ShapeDtypeStruct((B * L, wo.shape[1]), F32),
            jax.ShapeDtypeStruct((B, H, P, N), F32),
        ],
        scratch_shapes=[
            pltpu.VMEM((Q + SUBLANES, CD), F32),
            pltpu.VMEM((Q, CD), F32),
            pltpu.VMEM((Q, inner), F32),
            pltpu.VMEM((Q, inner), F32),
        ],
        compiler_params=_cparams(("parallel", "arbitrary")),
        name="ssd_prompt",
    )(proj, proj, dt, dtT, cw, cb, dtb, dtbT, alog, alogT, dexp, nw, wo)


def _ssd_sample_body(G, R, P, N,
                     xbc_ref, cpad_ref, z_ref, dt_ref, dtT_ref, h0_ref, cw_ref, cb_ref, dtb_ref, dtbT_ref,
                     alog_ref, alogT_ref, dexp_ref, nw_ref, wo_ref, yb_ref, st_ref,
                     act_s, y_s, xw_s, yint_s, xwT_s, tot_s, eacum_s):
    Q = xbc_ref.shape[0]
    inner = G * R * P
    bi = pl.program_id(1)
    half = SAMPLE_ROWS // 2

    @pl.when(bi == 0)
    def _():
        act_s[...] = _silu(_conv_rolled(xbc_ref[...] + cpad_ref[...], cw_ref, cb_ref))
        rows = lax.broadcasted_iota(jnp.int32, (Q, LANES), 0) & (SAMPLE_ROWS - 1)
        cols = lax.broadcasted_iota(jnp.int32, dtT_ref.shape, 1) & (SAMPLE_ROWS - 1)
        dt = jnp.where(rows >= half, _softplus(dt_ref[...] + dtb_ref[...]), 0.0)
        dtT = jnp.where(cols >= half, _softplus(dtT_ref[...] + dtbT_ref[...]), 0.0)
        seq_shift = SAMPLE_ROWS.bit_length() - 1
        tot, eacum = _ssd_chunk(act_s, dt, dtT, alog_ref, alogT_ref, dexp_ref, seq_shift, G, R, P, N,
                                lambda g, Cg: None, y_s, xw_s)
        tot_s[...] = tot
        eacum_s[...] = eacum
        for g in range(G):
            xwT_s[g] = jnp.transpose(xw_s[:, g * R * P:(g + 1) * R * P]).astype(BF16)

    r0 = pl.multiple_of(bi * SAMPLE_ROWS, SAMPLE_ROWS)
    rid = lax.broadcasted_iota(jnp.int32, (Q, N), 0)
    mine = (rid >= r0) & (rid < r0 + SAMPLE_ROWS)
    dec = jnp.exp(tot_s[pl.ds(r0, 1), :])
    for g in range(G):
        Cb = act_s[pl.ds(r0, SAMPLE_ROWS), inner + (G + g) * N:inner + (G + g + 1) * N]
        stg = h0_ref[0, g * R:(g + 1) * R].reshape(R * P, N)
        yint_s[pl.ds(r0, SAMPLE_ROWS), g * R * P:(g + 1) * R * P] = _dot_nt(Cb, stg)
        Bg = jnp.where(mine, act_s[:, inner + g * N:inner + (g + 1) * N], 0.0).astype(BF16)
        S = _dot(xwT_s[g], Bg)
        for r in range(R):
            h = g * R + r
            st_ref[0, h] = dec[:, h:h + 1] * h0_ref[0, h] + S[r * P:(r + 1) * P, :]

    @pl.when(bi == pl.num_programs(1) - 1)
    def _():
        for h in range(G * R):
            sl = slice(h * P, (h + 1) * P)
            y_s[:, sl] = y_s[:, sl] + yint_s[:, sl] * eacum_s[:, h:h + 1]
        yb_ref[...] = _ssd_finish(y_s[...], z_ref[...], nw_ref, wo_ref)


def _ssd_sample(proj, dt, dtT, cpad, h0, col_xbc, col_z, dims, cw, cb, dtb, dtbT, alog, alogT, dexp, nw, wo):
    G, R, P, N = dims
    H = G * R
    inner = H * P
    CD = cw.shape[1]
    T = proj.shape[0]
    Bs = h0.shape[0]
    Q = _pick(T, SSD_CHUNK)
    nb = Q // SAMPLE_ROWS
    full = lambda *s: pl.BlockSpec(s, lambda i, j: (0,) * len(s))
    return pl.pallas_call(
        functools.partial(_ssd_sample_body, G, R, P, N),
        grid=(T // Q, nb),
        in_specs=[
            pl.BlockSpec((Q, CD), lambda i, j: (i, col_xbc)),
            pl.BlockSpec((Q, CD), lambda i, j: (i, 0)),
            pl.BlockSpec((Q, inner), lambda i, j: (i, col_z)),
            pl.BlockSpec((Q, LANES), lambda i, j: (i, 0)),
            pl.BlockSpec((H, Q), lambda i, j: (0, i)),
            pl.BlockSpec((1, H, P, N), lambda i, j: (i * nb + j, 0, 0, 0)),
            full(4, CD), full(1, CD), full(1, LANES), full(H, 1), full(1, LANES), full(H, 1),
            full(1, inner), full(1, inner), full(*wo.shape),
        ],
        out_specs=[
            pl.BlockSpec((Q, wo.shape[1]), lambda i, j: (i, 0)),
            pl.BlockSpec((1, H, P, N), lambda i, j: (i * nb + j, 0, 0, 0)),
        ],
        out_shape=[
            jax.ShapeDtypeStruct((T, wo.shape[1]), F32),
            jax.ShapeDtypeStruct((Bs, H, P, N), F32),
        ],
        scratch_shapes=[
            pltpu.VMEM((Q, CD), F32),
            pltpu.VMEM((Q, inner), F32),
            pltpu.VMEM((Q, inner), F32),
            pltpu.VMEM((Q, inner), F32),
            pltpu.VMEM((G, R * P, Q), BF16),
            pltpu.VMEM((Q, LANES), F32),
            pltpu.VMEM((Q, LANES), F32),
        ],
        compiler_params=_cparams(("parallel", "arbitrary")),
        name="ssd_sample",
    )(proj, cpad, proj, dt, dtT, h0, cw, cb, dtb, dtbT, alog, alogT, dexp, nw, wo)


def _merge_body(x_ref, ya_ref, yb_ref, ga_ref, gb_ref, wout_ref, nf_ref, x1_ref, h2_ref):
    m = _sigmoid(ga_ref[...]) * ya_ref[...] + _sigmoid(gb_ref[...]) * yb_ref[...]
    x1 = x_ref[...] + _dot(m.astype(BF16), wout_ref[...])
    x1_ref[...] = x1
    h2_ref[...] = _rms(x1, nf_ref[...]).astype(BF16)


def _merge(x2d, ya, yb, proj, col_ga, col_gb, wout, nf):
    T, D = x2d.shape
    TM = _pick(T, 512)
    row = lambda c: pl.BlockSpec((TM, D), lambda i: (i, c))
    return pl.pallas_call(
        _merge_body,
        grid=(T // TM,),
        in_specs=[row(0), row(0), row(0), row(col_ga), row(col_gb),
                  pl.BlockSpec(wout.shape, lambda i: (0, 0)), pl.BlockSpec((1, D), lambda i: (0, 0))],
        out_specs=[row(0), row(0)],
        out_shape=[jax.ShapeDtypeStruct((T, D), F32), jax.ShapeDtypeStruct((T, D), BF16)],
        compiler_params=_cparams(("parallel",)),
        name="merge",
    )(x2d, ya, yb, proj, proj, wout, nf)


def _staircase(k):
    return [(ka, k // (ka + 1)) for ka in range(k)]


def _kth_largest(cur, k):
    m = None
    for _ in range(k):
        m = jnp.max(cur, axis=0, keepdims=True)
        cur = jnp.where(cur == m, NEG_INF, cur)
    return m


def _double_bf16(x):
    bits = pltpu.bitcast(x.astype(BF16).astype(F32), jnp.uint32)
    return bits | (bits >> 16)


def _route_body(h2_ref, wq_ref, keys_ref, na_ref, ea_ref, rb_ref, eb_ref, sv_s, cand_s):
    NH = keys_ref.shape[0]
    NK, KH = keys_ref.shape[2], keys_ref.shape[3]
    K = PEER_TOPK
    q = _dot(h2_ref[...], wq_ref[...]).astype(BF16)
    cand_s[...] = jnp.full(cand_s.shape, NEG_INF, F32)
    for h in range(NH):
        rank, ex = [], []
        for s in range(2):
            off = (h * 2 + s) * KH
            sT = _dot_nt(keys_ref[h, s], q[:, off:off + KH])
            cur = sT
            rk = jnp.full(sT.shape, float(K), F32)
            for k in range(K):
                m = jnp.max(cur, axis=0, keepdims=True)
                sv_s[s, k:k + 1, :] = m
                eq = cur == m
                rk = jnp.where(eq, float(k), rk)
                cur = jnp.where(eq, NEG_INF, cur)
            rank.append(rk)
            ex.append(jnp.where(rk < K, jnp.exp(sT - sv_s[s, 0:1, :]), 0.0))
        off = 0
        for ka, nb in _staircase(K):
            cand_s[off:off + nb, :] = sv_s[0, ka:ka + 1, :] + sv_s[1, 0:nb, :]
            off += nb
        cand = cand_s[...]
        tau = _kth_largest(cand, K)
        top = sv_s[0, 0:1, :] + sv_s[1, 0:1, :]
        z = jnp.sum(jnp.where(cand >= tau, jnp.exp(cand - top), 0.0), axis=0, keepdims=True)
        na = jnp.zeros(rank[0].shape, F32)
        off = 0
        for ka, nb in _staircase(K):
            cnt = jnp.sum(jnp.where(cand_s[off:off + nb, :] >= tau, 1.0, 0.0), axis=0, keepdims=True)
            na = jnp.where(rank[0] == float(ka), cnt, na)
            off += nb
        na_ref[h * NK:(h + 1) * NK, :] = _double_bf16(na)
        ea_ref[h * NK:(h + 1) * NK, :] = _double_bf16(ex[0] / z)
        rb_ref[h] = rank[1].astype(BF16)
        eb_ref[h] = ex[1].astype(BF16)


def _route(h2, wq, keys):
    T, D = h2.shape
    NH, _, NK, KH = keys.shape
    TB = _pick(T, 512)
    ncand = sum(nb for _, nb in _staircase(PEER_TOPK))
    ncand_pad = -(-ncand // SUBLANES) * SUBLANES
    flat = pl.BlockSpec((NH * NK, TB), lambda i: (0, i))
    tok = pl.BlockSpec((NH, NK, TB), lambda i: (0, 0, i))
    return pl.pallas_call(
        _route_body,
        grid=(T // TB,),
        in_specs=[pl.BlockSpec((TB, D), lambda i: (i, 0)),
                  pl.BlockSpec(wq.shape, lambda i: (0, 0)),
                  pl.BlockSpec(keys.shape, lambda i: (0, 0, 0, 0))],
        out_specs=[flat, flat, tok, tok],
        out_shape=[jax.ShapeDtypeStruct((NH * NK, T), jnp.uint32), jax.ShapeDtypeStruct((NH * NK, T), jnp.uint32),
                   jax.ShapeDtypeStruct((NH, NK, T), BF16), jax.ShapeDtypeStruct((NH, NK, T), BF16)],
        scratch_shapes=[pltpu.VMEM((2, PEER_TOPK, TB), F32), pltpu.VMEM((ncand_pad, TB), F32)],
        compiler_params=_cparams(("parallel",)),
        name="peer_route",
    )(h2, wq, keys)


def _peer_body(h2_ref, x1_ref, u_ref, vt_ref, na_ref, ea_ref, rb_ref, eb_ref, nfin_ref, y_ref,
               s_s, a_s, acc_s):
    ec = pl.program_id(1)
    NH, NK, TB = rb_ref.shape
    ni = u_ref.shape[0] // NK

    @pl.when(ec == 0)
    def _():
        acc_s[...] = jnp.zeros(acc_s.shape, F32)

    s_s[...] = _dot_nt(u_ref[...], h2_ref[...])

    def rows_bf16(grp, il):
        return pltpu.bitcast(jnp.broadcast_to(grp[il:il + 1, :], (NK // 2, LANES)), BF16)

    for tl in range(TB // LANES):
        ln = slice(tl * LANES, (tl + 1) * LANES)
        rows = [pl.ds(pl.multiple_of(h * NK + ec * ni, SUBLANES), ni) for h in range(NH)]
        na = [na_ref[rows[h], ln] for h in range(NH)]
        ea = [ea_ref[rows[h], ln] for h in range(NH)]
        for il in range(ni):
            w = jnp.zeros((NK, LANES), BF16)
            for h in range(NH):
                hit = rb_ref[h, :, ln] < rows_bf16(na[h], il)
                w = w + rows_bf16(ea[h], il) * jnp.where(hit, eb_ref[h, :, ln], 0.0)
            e = slice(il * NK, (il + 1) * NK)
            a_s[e, ln] = _gelu(s_s[e, ln]).astype(BF16) * w
    acc_s[...] += _dot(vt_ref[...], a_s[...])

    @pl.when(ec == pl.num_programs(1) - 1)
    def _():
        x2 = x1_ref[...] + jnp.transpose(acc_s[...])
        y_ref[...] = _rms(x2, nfin_ref[...])


def _peer(h2, x1, u, vt, na, ea, rb, eb, nfin):
    T, D = h2.shape
    E = u.shape[0]
    NH, NK, _ = rb.shape
    TB = _pick(T, 512)
    EC = SUBLANES * NK
    assert TB % LANES == 0 and E % EC == 0
    tok = pl.BlockSpec((NH, NK, TB), lambda i, j: (0, 0, i))
    flat = pl.BlockSpec((NH * NK, TB), lambda i, j: (0, i))
    return pl.pallas_call(
        _peer_body,
        grid=(T // TB, E // EC),
        in_specs=[pl.BlockSpec((TB, D), lambda i, j: (i, 0)),
                  pl.BlockSpec((TB, D), lambda i, j: (i, 0)),
                  pl.BlockSpec((EC, D), lambda i, j: (j, 0)),
                  pl.BlockSpec((D, EC), lambda i, j: (0, j)),
                  flat, flat, tok, tok,
                  pl.BlockSpec((1, D), lambda i, j: (0, 0))],
        out_specs=pl.BlockSpec((TB, D), lambda i, j: (i, 0)),
        out_shape=jax.ShapeDtypeStruct((T, D), F32),
        scratch_shapes=[pltpu.VMEM((EC, TB), F32), pltpu.VMEM((EC, TB), BF16), pltpu.VMEM((D, TB), F32)],
        compiler_params=_cparams(("parallel", "arbitrary")),
        name="peer_mix",
    )(h2, x1, u, vt, na, ea, rb, eb, nfin)


def _pad_lanes(row):
    return jnp.pad(row, ((0, 0), (0, LANES - row.shape[1])))


def kernel(x_prompt, x_sample, state_rg_h, state_rg_conv, state_ssd_h, state_ssd_conv, norm_mix, w_in, rg_conv_w, rg_conv_b, rg_wa, rg_ba, rg_wx, rg_bx, rg_lam, w_rg_out, ssd_conv_w, ssd_conv_b, ssd_dt_bias, ssd_a_log, ssd_d, ssd_norm, w_ssd_out, w_out, norm_ffn, peer_wq, peer_keys, peer_u, peer_v, norm_final):
    depth = w_in.shape[0]
    B, L, D = x_prompt.shape
    Bs, Ls, _ = x_sample.shape
    assert depth == 1 and Ls == SAMPLE_ROWS // 2 and rg_conv_w.shape[1] == 4 and L >= 3
    W = rg_conv_w.shape[2]
    H = ssd_a_log.shape[1]
    inner = ssd_norm.shape[1]
    CD = ssd_conv_w.shape[2]
    P = inner // H
    N = state_ssd_h.shape[-1]
    G = (CD - inner) // (2 * N)
    R = H // G
    dims = (G, R, P, N)
    assert W == D and inner == 2 * D and CD == 3 * D and H % SUBLANES == 0 and H <= LANES
    o_x, o_g, o_z, o_xbc, o_dt, o_m = 0, W, 2 * W, 2 * W + inner, 2 * W + inner + CD, 2 * W + inner + CD + H
    col_xbc, col_x, col_z, col_g, col_ga, col_gb = 0, CD // W, (CD + W) // inner, (CD + W + inner) // W, \
        (CD + 2 * W + inner) // D, (CD + 2 * W + inner + D) // D

    yp = x_prompt.reshape(B * L, D)
    ys = x_sample.reshape(Bs * Ls, D)
    outs = {k: [] for k in ("prh", "prc", "psh", "psc", "srh", "src", "ssh", "ssc")}
    lead = SAMPLE_ROWS - Ls
    for l in range(depth):
        wi = w_in[l]
        w_main = jnp.concatenate([wi[:, o_xbc:o_dt], wi[:, o_x:o_g], wi[:, o_z:o_xbc], wi[:, o_g:o_z],
                                  wi[:, o_m:]], axis=1).astype(BF16)
        wdtT = wi[:, o_dt:o_m].T
        wdt = _pad_lanes(wi[:, o_dt:o_m])
        g_mix = norm_mix[l][None]
        rg_w = (rg_conv_w[l], rg_conv_b[l][None], rg_wa[l].astype(BF16), rg_wx[l].astype(BF16),
                rg_ba[l][None], rg_bx[l][None], rg_lam[l][None], w_rg_out[l].astype(BF16))
        ssd_w = (ssd_conv_w[l], ssd_conv_b[l][None], _pad_lanes(ssd_dt_bias[l][None]), ssd_dt_bias[l][:, None],
                 _pad_lanes(ssd_a_log[l][None]), ssd_a_log[l][:, None], jnp.repeat(ssd_d[l], P)[None],
                 ssd_norm[l][None], w_ssd_out[l].astype(BF16))
        wout = w_out[l].astype(BF16)
        nf = norm_ffn[l][None]
        wq = peer_wq[l].astype(BF16)
        keys = peer_keys[l].astype(BF16)
        u = peer_u[l].astype(BF16)
        vt = peer_v[l].astype(BF16).T
        nfin = norm_final[None]

        proj, dt, dtT = _inproj(yp, g_mix, w_main, wdt, wdtT)
        ya, rgh = _rg_prompt(proj, B, L, col_x, col_g, *rg_w)
        yb, ssh = _ssd_prompt(proj, dt, dtT, B, L, col_xbc, col_z, dims, *ssd_w)
        p3 = proj.reshape(B, L, -1)
        outs["prh"].append(rgh.reshape(B, W))
        outs["prc"].append(p3[:, L - 3:, CD:CD + W])
        outs["psh"].append(ssh)
        outs["psc"].append(p3[:, L - 3:, :CD])
        x1, h2 = _merge(yp, ya, yb, proj, col_ga, col_gb, wout, nf)
        yp = _peer(h2, x1, u, vt, *_route(h2, wq, keys), nfin)

        xe = jnp.pad(ys.reshape(Bs, Ls, D), ((0, 0), (lead, 0), (0, 0))).reshape(Bs * SAMPLE_ROWS, D)
        proj_e, dt_e, dtT_e = _inproj(xe, g_mix, w_main, wdt, wdtT)
        rows = lambda a3: a3.reshape(Bs * SAMPLE_ROWS, a3.shape[-1])
        rg_cpad = rows(jnp.pad(state_rg_conv[l], ((0, 0), (lead - 3, Ls), (0, 0))))
        rg_hpad = rows(jnp.pad(state_rg_h[l][:, None, :], ((0, 0), (lead - 1, Ls), (0, 0))))
        ssd_cpad = rows(jnp.pad(state_ssd_conv[l], ((0, 0), (lead - 3, Ls), (0, 0))))
        ya_e, h_e = _rg_sample(proj_e, col_x, col_g, rg_cpad, rg_hpad, *rg_w)
        yb_e, ssh_s = _ssd_sample(proj_e, dt_e, dtT_e, ssd_cpad, state_ssd_h[l], col_xbc, col_z, dims, *ssd_w)
        toks = lambda a2: a2.reshape(Bs, SAMPLE_ROWS, -1)[:, lead:].reshape(Bs * Ls, -1)
        p3 = proj_e.reshape(Bs, SAMPLE_ROWS, -1)
        outs["srh"].append(h_e.reshape(Bs, SAMPLE_ROWS, W)[:, -1])
        outs["src"].append(p3[:, SAMPLE_ROWS - 3:, CD:CD + W])
        outs["ssh"].append(ssh_s)
        outs["ssc"].append(p3[:, SAMPLE_ROWS - 3:, :CD])
        proj_s = toks(proj_e)
        x1, h2 = _merge(ys, toks(ya_e), toks(yb_e), proj_s, col_ga, col_gb, wout, nf)
        ys = _peer(h2, x1, u, vt, *_route(h2, wq, keys), nfin)

    st = lambda k: jnp.stack(outs[k])
    return (yp.reshape(B, L, D), ys.reshape(Bs, Ls, D), st("prh"), st("prc"), st("psh"), st("psc"),
            st("srh"), st("src"), st("ssh"), st("ssc"))
```

```python
import functools

import jax
import jax.numpy as jnp
from jax import lax
from jax.experimental import pallas as pl
from jax.experimental.pallas import tpu as pltpu

F32 = jnp.float32
BF16 = jnp.bfloat16
EPS = 1e-6
RG_C = 8.0
PEER_TOPK = 16
PEER_PIECE = 256
SSD_CHUNK = 128
LANES = 128
SUBLANES = 8
SAMPLE_ROWS = 8
NEG_INF = float("-inf")
HIGHEST = lax.Precision.HIGHEST
NT_DIMS = (((1,), (1,)), ((), ()))
VMEM_LIMIT = 56 * 1024 * 1024


def _cparams(sem):
    return pltpu.CompilerParams(dimension_semantics=sem, vmem_limit_bytes=VMEM_LIMIT)


def _pick(n, pref):
    t = min(n, pref)
    while n % t:
        t -= SUBLANES
    return t


def _sigmoid(x):
    return 1.0 / (1.0 + jnp.exp(-x))


def _silu(x):
    return x * _sigmoid(x)


def _softplus(x):
    return jnp.maximum(x, 0.0) + jnp.log1p(jnp.exp(-jnp.abs(x)))


def _gelu(x):
    k1 = -2.0 * 0.7978845608028654 * 1.4426950408889634
    return x / (1.0 + jnp.exp2(x * (k1 + (k1 * 0.044715) * (x * x))))


def _rms(x, g):
    return x * lax.rsqrt(jnp.mean(x * x, axis=-1, keepdims=True) + EPS) * g


def _dot(a, b):
    return jnp.dot(a, b, preferred_element_type=F32)


def _dot_nt(a, b):
    return lax.dot_general(a, b, NT_DIMS, preferred_element_type=F32)


def _inproj_body(x_ref, g_ref, w_ref, wdt_ref, wdtT_ref, o_ref, odt_ref, odtT_ref, xn_ref):
    @pl.when(pl.program_id(1) == 0)
    def _():
        xn = _rms(x_ref[...], g_ref[...])
        xn_ref[...] = xn.astype(BF16)
        odt_ref[...] = jnp.dot(xn, wdt_ref[...], precision=HIGHEST, preferred_element_type=F32)
        odtT_ref[...] = lax.dot_general(wdtT_ref[...], xn, NT_DIMS, precision=HIGHEST,
                                        preferred_element_type=F32)

    o_ref[...] = _dot(xn_ref[...], w_ref[...])


def _inproj(x2d, g, w_main, wdt, wdtT):
    T, D = x2d.shape
    N = w_main.shape[1]
    H = wdtT.shape[0]
    TM = _pick(T, 1024)
    TN = 1536
    assert N % TN == 0 and TM % LANES == 0
    return pl.pallas_call(
        _inproj_body,
        grid=(T // TM, N // TN),
        in_specs=[
            pl.BlockSpec((TM, D), lambda i, j: (i, 0)),
            pl.BlockSpec((1, D), lambda i, j: (0, 0)),
            pl.BlockSpec((D, TN), lambda i, j: (0, j)),
            pl.BlockSpec((D, LANES), lambda i, j: (0, 0)),
            pl.BlockSpec((H, D), lambda i, j: (0, 0)),
        ],
        out_specs=[
            pl.BlockSpec((TM, TN), lambda i, j: (i, j)),
            pl.BlockSpec((TM, LANES), lambda i, j: (i, 0)),
            pl.BlockSpec((H, TM), lambda i, j: (0, i)),
        ],
        out_shape=[
            jax.ShapeDtypeStruct((T, N), F32),
            jax.ShapeDtypeStruct((T, LANES), F32),
            jax.ShapeDtypeStruct((H, T), F32),
        ],
        scratch_shapes=[pltpu.VMEM((TM, D), BF16)],
        compiler_params=_cparams(("parallel", "arbitrary")),
        name="inproj",
    )(x2d, g, w_main, wdt, wdtT)


def _rg_gates(xc, wa_ref, wx_ref, ba, bx, lam):
    xb = xc.astype(BF16)
    nb, bw = wa_ref.shape[0], wa_ref.shape[1]
    rs, gs = [], []
    for k in range(nb):
        xk = xb[:, k * bw:(k + 1) * bw]
        rs.append(_dot(xk, wa_ref[k]))
        gs.append(_dot(xk, wx_ref[k]))
    r = _sigmoid(jnp.concatenate(rs, axis=1) + ba)
    i = _sigmoid(jnp.concatenate(gs, axis=1) + bx)
    log_a = -RG_C * r * _softplus(-lam)
    a = jnp.exp(log_a)
    em1 = jnp.tanh(log_a) * (a * a + 1.0)
    return a, jnp.sqrt(-em1) * (i * xc)


def _rg_prompt_body(x_ref, gate_ref, cw_ref, cb_ref, wa_ref, wx_ref, ba_ref, bx_ref, lam_ref, wo_ref,
                    ya_ref, hfin_ref, pad_s, a_s, b_s, h_s):
    Lc, W = x_ref.shape

    @pl.when(pl.program_id(1) == 0)
    def _():
        pad_s[0:SUBLANES, :] = jnp.zeros((SUBLANES, W), F32)
        h_s[...] = jnp.zeros(h_s.shape, F32)

    pad_s[SUBLANES:SUBLANES + Lc, :] = x_ref[...]
    xc = (cb_ref[...]
          + cw_ref[3:4, :] * pad_s[8:8 + Lc, :]
          + cw_ref[2:3, :] * pad_s[7:7 + Lc, :]
          + cw_ref[1:2, :] * pad_s[6:6 + Lc, :]
          + cw_ref[0:1, :] * pad_s[5:5 + Lc, :])
    pad_s[0:SUBLANES, :] = pad_s[Lc:Lc + SUBLANES, :]
    a, bt = _rg_gates(xc, wa_ref, wx_ref, ba_ref[...], bx_ref[...], lam_ref[...])
    a_s[...] = a
    b_s[...] = bt

    def step(t, h):
        h = a_s[pl.ds(t, 1), :] * h + b_s[pl.ds(t, 1), :]
        b_s[pl.ds(t, 1), :] = h
        return h

    h = lax.fori_loop(0, Lc, step, h_s[0:1, :], unroll=8)
    h_s[0:1, :] = h
    y = b_s[...] * _gelu(gate_ref[...])
    ya_ref[...] = _dot(y.astype(BF16), wo_ref[...])
    hfin_ref[0] = h


def _rg_prompt(proj, B, L, col_x, col_g, cw, cb, wa, wx, ba, bx, lam, wo):
    W = cw.shape[1]
    Lc = _pick(L, 256)
    nC = L // Lc
    full = lambda *s: pl.BlockSpec(s, lambda b, c: (0,) * len(s))
    return pl.pallas_call(
        _rg_prompt_body,
        grid=(B, nC),
        in_specs=[
            pl.BlockSpec((Lc, W), lambda b, c: (b * nC + c, col_x)),
            pl.BlockSpec((Lc, W), lambda b, c: (b * nC + c, col_g)),
            full(4, W), full(1, W), full(*wa.shape), full(*wx.shape), full(1, W), full(1, W), full(1, W),
            full(*wo.shape),
        ],
        out_specs=[
            pl.BlockSpec((Lc, wo.shape[1]), lambda b, c: (b * nC + c, 0)),
            pl.BlockSpec((1, 1, W), lambda b, c: (b, 0, 0)),
        ],
        out_shape=[
            jax.ShapeDtypeStruct((B * L, wo.shape[1]), F32),
            jax.ShapeDtypeStruct((B, 1, W), F32),
        ],
        scratch_shapes=[
            pltpu.VMEM((Lc + SUBLANES, W), F32),
            pltpu.VMEM((Lc, W), F32),
            pltpu.VMEM((Lc, W), F32),
            pltpu.VMEM((SUBLANES, W), F32),
        ],
        compiler_params=_cparams(("parallel", "arbitrary")),
        name="rg_prompt",
    )(proj, proj, cw, cb, wa, wx, ba, bx, lam, wo)


def _conv_rolled(u, cw_ref, cb_ref):
    return (cb_ref[...]
            + cw_ref[3:4, :] * u
            + cw_ref[2:3, :] * pltpu.roll(u, 1, axis=0)
            + cw_ref[1:2, :] * pltpu.roll(u, 2, axis=0)
            + cw_ref[0:1, :] * pltpu.roll(u, 3, axis=0))


def _rg_sample_body(x_ref, gate_ref, cpad_ref, hpad_ref, cw_ref, cb_ref, wa_ref, wx_ref, ba_ref, bx_ref,
                    lam_ref, wo_ref, ya_ref, h_ref):
    xc = _conv_rolled(x_ref[...] + cpad_ref[...], cw_ref, cb_ref)
    a, bt = _rg_gates(xc, wa_ref, wx_ref, ba_ref[...], bx_ref[...], lam_ref[...])
    row = lax.broadcasted_iota(jnp.int32, a.shape, 0) & (SAMPLE_ROWS - 1)
    h = hpad_ref[...]
    for k in range(SAMPLE_ROWS // 2, SAMPLE_ROWS):
        h = jnp.where(row == k, a * pltpu.roll(h, 1, axis=0) + bt, h)
    h_ref[...] = h
    ya_ref[...] = _dot((h * _gelu(gate_ref[...])).astype(BF16), wo_ref[...])


def _rg_sample(proj, col_x, col_g, cpad, hpad, cw, cb, wa, wx, ba, bx, lam, wo):
    T = proj.shape[0]
    W = cw.shape[1]
    TM = _pick(T, 256)
    full = lambda *s: pl.BlockSpec(s, lambda i: (0,) * len(s))
    return pl.pallas_call(
        _rg_sample_body,
        grid=(T // TM,),
        in_specs=[
            pl.BlockSpec((TM, W), lambda i: (i, col_x)),
            pl.BlockSpec((TM, W), lambda i: (i, col_g)),
            pl.BlockSpec((TM, W), lambda i: (i, 0)),
            pl.BlockSpec((TM, W), lambda i: (i, 0)),
            full(4, W), full(1, W), full(*wa.shape), full(*wx.shape), full(1, W), full(1, W), full(1, W),
            full(*wo.shape),
        ],
        out_specs=[
            pl.BlockSpec((TM, wo.shape[1]), lambda i: (i, 0)),
            pl.BlockSpec((TM, W), lambda i: (i, 0)),
        ],
        out_shape=[
            jax.ShapeDtypeStruct((T, wo.shape[1]), F32),
            jax.ShapeDtypeStruct((T, W), F32),
        ],
        compiler_params=_cparams(("parallel",)),
        name="rg_sample",
    )(proj, proj, cpad, hpad, cw, cb, wa, wx, ba, bx, lam, wo)


def _ssd_chunk(act_s, dt, dtT, alog_ref, alogT_ref, dexp_ref, seq_shift, G, R, P, N,
               yint_fn, y_s, xw_s):
    Q = act_s.shape[0]
    inner = G * R * P
    a = dt * (-jnp.exp(alog_ref[...]))
    aT = dtT * (-jnp.exp(alogT_ref[...]))
    ri = lax.broadcasted_iota(jnp.int32, (Q, Q), 0)
    ci = lax.broadcasted_iota(jnp.int32, (Q, Q), 1)
    same = (ri >> seq_shift) == (ci >> seq_shift)
    causal = same & (ci <= ri)
    causal_f = jnp.where(causal, 1.0, 0.0).astype(F32)
    same_f = jnp.where(same, 1.0, 0.0).astype(F32)
    acum = jnp.dot(causal_f, a, precision=HIGHEST, preferred_element_type=F32)
    tot = jnp.dot(same_f, a, precision=HIGHEST, preferred_element_type=F32)
    acumT = lax.dot_general(aT, causal_f, NT_DIMS, precision=HIGHEST,
                            preferred_element_type=F32)
    eacum = jnp.exp(acum)
    toend = jnp.exp(tot - acum) * dt
    for g in range(G):
        Bg = act_s[:, inner + g * N:inner + (g + 1) * N].astype(BF16)
        Cg = act_s[:, inner + (G + g) * N:inner + (G + g + 1) * N].astype(BF16)
        cb = _dot_nt(Cg, Bg)
        yint = yint_fn(g, Cg)
        for r in range(R):
            h = g * R + r
            sl = slice(h * P, (h + 1) * P)
            seg = acum[:, h:h + 1] - acumT[h:h + 1, :]
            decay = jnp.exp(jnp.where(causal, seg, NEG_INF))
            wgt = (cb * decay * dtT[h:h + 1, :]).astype(BF16)
            xh = act_s[:, sl]
            yh = _dot(wgt, xh.astype(BF16)) + dexp_ref[:, sl] * xh
            if yint is not None:
                yh = yh + yint[:, r * P:(r + 1) * P] * eacum[:, h:h + 1]
            y_s[:, sl] = yh
            xw_s[:, sl] = xh * toend[:, h:h + 1]
    return tot, eacum


def _ssd_finish(y, z, nw_ref, wo_ref):
    y = y * _silu(z)
    return _dot(_rms(y, nw_ref[...]).astype(BF16), wo_ref[...])


def _ssd_prompt_body(G, R, P, N,
                     xbc_ref, z_ref, dt_ref, dtT_ref, cw_ref, cb_ref, dtb_ref, dtbT_ref, alog_ref, alogT_ref,
                     dexp_ref, nw_ref, wo_ref, yb_ref, st_ref, pad_s, act_s, y_s, xw_s):
    Q = xbc_ref.shape[0]

    @pl.when(pl.program_id(1) == 0)
    def _():
        pad_s[0:SUBLANES, :] = jnp.zeros((SUBLANES, pad_s.shape[1]), F32)
        st_ref[...] = jnp.zeros(st_ref.shape, F32)

    pad_s[SUBLANES:SUBLANES + Q, :] = xbc_ref[...]
    conv = (cb_ref[...]
            + cw_ref[3:4, :] * pad_s[8:8 + Q, :]
            + cw_ref[2:3, :] * pad_s[7:7 + Q, :]
            + cw_ref[1:2, :] * pad_s[6:6 + Q, :]
            + cw_ref[0:1, :] * pad_s[5:5 + Q, :])
    pad_s[0:SUBLANES, :] = pad_s[Q:Q + SUBLANES, :]
    act_s[...] = _silu(conv)
    dt = _softplus(dt_ref[...] + dtb_ref[...])
    dtT = _softplus(dtT_ref[...] + dtbT_ref[...])

    def yint_fn(g, Cg):
        stg = st_ref[0, g * R:(g + 1) * R].reshape(R * P, N)
        return _dot_nt(Cg, stg.astype(BF16))

    seq_shift = Q.bit_length() - 1
    tot, _ = _ssd_chunk(act_s, dt, dtT, alog_ref, alogT_ref, dexp_ref, seq_shift, G, R, P, N,
                        yint_fn, y_s, xw_s)
    dec = jnp.exp(tot[0:1, :])
    inner = G * R * P
    for g in range(G):
        Bg = act_s[:, inner + g * N:inner + (g + 1) * N].astype(BF16)
        xwT = jnp.transpose(xw_s[:, g * R * P:(g + 1) * R * P])
        S = _dot(xwT.astype(BF16), Bg)
        for r in range(R):
            h = g * R + r
            st_ref[0, h] = dec[:, h:h + 1] * st_ref[0, h] + S[r * P:(r + 1) * P, :]
    yb_ref[...] = _ssd_finish(y_s[...], z_ref[...], nw_ref, wo_ref)


def _ssd_prompt(proj, dt, dtT, B, L, col_xbc, col_z, dims, cw, cb, dtb, dtbT, alog, alogT, dexp, nw, wo):
    G, R, P, N = dims
    H = G * R
    inner = H * P
    CD = cw.shape[1]
    Q = _pick(L, SSD_CHUNK)
    nC = L // Q
    full = lambda *s: pl.BlockSpec(s, lambda b, c: (0,) * len(s))
    return pl.pallas_call(
        functools.partial(_ssd_prompt_body, G, R, P, N),
        grid=(B, nC),
        in_specs=[
            pl.BlockSpec((Q, CD), lambda b, c: (b * nC + c, col_xbc)),
            pl.BlockSpec((Q, inner), lambda b, c: (b * nC + c, col_z)),
            pl.BlockSpec((Q, LANES), lambda b, c: (b * nC + c, 0)),
            pl.BlockSpec((H, Q), lambda b, c: (0, b * nC + c)),
            full(4, CD), full(1, CD), full(1, LANES), full(H, 1), full(1, LANES), full(H, 1),
            full(1, inner), full(1, inner), full(*wo.shape),
        ],
        out_specs=[
            pl.BlockSpec((Q, wo.shape[1]), lambda b, c: (b * nC + c, 0)),
            pl.BlockSpec((1, H, P, N), lambda b, c: (b, 0, 0, 0)),
        ],
        out_shape=[
            jax.ShapeDtypeStruct((B * L, wo.shape[1]), F32),
            jax.ShapeDtypeStruct((B, H, P, N), F32),
        ],
        scratch_shapes=[
            pltpu.VMEM((Q + SUBLANES, CD), F32),
            pltpu.VMEM((Q, CD), F32),
            pltpu.VMEM((Q, inner), F32),
            pltpu.VMEM((Q, inner), F32),
        ],
        compiler_params=_cparams(("parallel", "arbitrary")),
        name="ssd_prompt",
    )(proj, proj, dt, dtT, cw, cb, dtb, dtbT, alog, alogT, dexp, nw, wo)


def _ssd_sample_body(G, R, P, N,
                     xbc_ref, cpad_ref, z_ref, dt_ref, dtT_ref, h0_ref, cw_ref, cb_ref, dtb_ref, dtbT_ref,
                     alog_ref, alogT_ref, dexp_ref, nw_ref, wo_ref, yb_ref, st_ref,
                     act_s, y_s, xw_s, yint_s, xwT_s, tot_s, eacum_s):
    Q = xbc_ref.shape[0]
    inner = G * R * P
    bi = pl.program_id(1)
    half = SAMPLE_ROWS // 2

    @pl.when(bi == 0)
    def _():
        act_s[...] = _silu(_conv_rolled(xbc_ref[...] + cpad_ref[...], cw_ref, cb_ref))
        rows = lax.broadcasted_iota(jnp.int32, (Q, LANES), 0) & (SAMPLE_ROWS - 1)
        cols = lax.broadcasted_iota(jnp.int32, dtT_ref.shape, 1) & (SAMPLE_ROWS - 1)
        dt = jnp.where(rows >= half, _softplus(dt_ref[...] + dtb_ref[...]), 0.0)
        dtT = jnp.where(cols >= half, _softplus(dtT_ref[...] + dtbT_ref[...]), 0.0)
        seq_shift = SAMPLE_ROWS.bit_length() - 1
        tot, eacum = _ssd_chunk(act_s, dt, dtT, alog_ref, alogT_ref, dexp_ref, seq_shift, G, R, P, N,
                                lambda g, Cg: None, y_s, xw_s)
        tot_s[...] = tot
        eacum_s[...] = eacum
        for g in range(G):
            xwT_s[g] = jnp.transpose(xw_s[:, g * R * P:(g + 1) * R * P]).astype(BF16)

    r0 = pl.multiple_of(bi * SAMPLE_ROWS, SAMPLE_ROWS)
    rid = lax.broadcasted_iota(jnp.int32, (Q, N), 0)
    mine = (rid >= r0) & (rid < r0 + SAMPLE_ROWS)
    dec = jnp.exp(tot_s[pl.ds(r0, 1), :])
    for g in range(G):
        Cb = act_s[pl.ds(r0, SAMPLE_ROWS), inner + (G + g) * N:inner + (G + g + 1) * N]
        stg = h0_ref[0, g * R:(g + 1) * R].reshape(R * P, N)
        yint_s[pl.ds(r0, SAMPLE_ROWS), g * R * P:(g + 1) * R * P] = _dot_nt(Cb, stg)
        Bg = jnp.where(mine, act_s[:, inner + g * N:inner + (g + 1) * N], 0.0).astype(BF16)
        S = _dot(xwT_s[g], Bg)
        for r in range(R):
            h = g * R + r
            st_ref[0, h] = dec[:, h:h + 1] * h0_ref[0, h] + S[r * P:(r + 1) * P, :]

    @pl.when(bi == pl.num_programs(1) - 1)
    def _():
        for h in range(G * R):
            sl = slice(h * P, (h + 1) * P)
            y_s[:, sl] = y_s[:, sl] + yint_s[:, sl] * eacum_s[:, h:h + 1]
        yb_ref[...] = _ssd_finish(y_s[...], z_ref[...], nw_ref, wo_ref)


def _ssd_sample(proj, dt, dtT, cpad, h0, col_xbc, col_z, dims, cw, cb, dtb, dtbT, alog, alogT, dexp, nw, wo):
    G, R, P, N = dims
    H = G * R
    inner = H * P
    CD = cw.shape[1]
    T = proj.shape[0]
    Bs = h0.shape[0]
    Q = _pick(T, SSD_CHUNK)
    nb = Q // SAMPLE_ROWS
    full = lambda *s: pl.BlockSpec(s, lambda i, j: (0,) * len(s))
    return pl.pallas_call(
        functools.partial(_ssd_sample_body, G, R, P, N),
        grid=(T // Q, nb),
        in_specs=[
            pl.BlockSpec((Q, CD), lambda i, j: (i, col_xbc)),
            pl.BlockSpec((Q, CD), lambda i, j: (i, 0)),
            pl.BlockSpec((Q, inner), lambda i, j: (i, col_z)),
            pl.BlockSpec((Q, LANES), lambda i, j: (i, 0)),
            pl.BlockSpec((H, Q), lambda i, j: (0, i)),
            pl.BlockSpec((1, H, P, N), lambda i, j: (i * nb + j, 0, 0, 0)),
            full(4, CD), full(1, CD), full(1, LANES), full(H, 1), full(1, LANES), full(H, 1),
            full(1, inner), full(1, inner), full(*wo.shape),
        ],
        out_specs=[
            pl.BlockSpec((Q, wo.shape[1]), lambda i, j: (i, 0)),
            pl.BlockSpec((1, H, P, N), lambda i, j: (i * nb + j, 0, 0, 0)),
        ],
        out_shape=[
            jax.ShapeDtypeStruct((T, wo.shape[1]), F32),
            jax.ShapeDtypeStruct((Bs, H, P, N), F32),
        ],
        scratch_shapes=[
            pltpu.VMEM((Q, CD), F32),
            pltpu.VMEM((Q, inner), F32),
            pltpu.VMEM((Q, inner), F32),
            pltpu.VMEM((Q, inner), F32),
            pltpu.VMEM((G, R * P, Q), BF16),
            pltpu.VMEM((Q, LANES), F32),
            pltpu.VMEM((Q, LANES), F32),
        ],
        compiler_params=_cparams(("parallel", "arbitrary")),
        name="ssd_sample",
    )(proj, cpad, proj, dt, dtT, h0, cw, cb, dtb, dtbT, alog, alogT, dexp, nw, wo)


def _merge_body(x_ref, ya_ref, yb_ref, ga_ref, gb_ref, wout_ref, nf_ref, x1_ref, h2_ref, h2t_ref):
    m = _sigmoid(ga_ref[...]) * ya_ref[...] + _sigmoid(gb_ref[...]) * yb_ref[...]
    x1 = x_ref[...] + _dot(m.astype(BF16), wout_ref[...])
    x1_ref[...] = x1
    h2 = _rms(x1, nf_ref[...])
    h2_ref[...] = h2.astype(BF16)
    h2t_ref[...] = jnp.transpose(h2).astype(BF16)


def _merge(x2d, ya, yb, proj, col_ga, col_gb, wout, nf):
    T, D = x2d.shape
    TM = _pick(T, 512)
    row = lambda c: pl.BlockSpec((TM, D), lambda i: (i, c))
    return pl.pallas_call(
        _merge_body,
        grid=(T // TM,),
        in_specs=[row(0), row(0), row(0), row(col_ga), row(col_gb),
                  pl.BlockSpec(wout.shape, lambda i: (0, 0)), pl.BlockSpec((1, D), lambda i: (0, 0))],
        out_specs=[row(0), row(0), pl.BlockSpec((D, TM), lambda i: (0, i))],
        out_shape=[jax.ShapeDtypeStruct((T, D), F32), jax.ShapeDtypeStruct((T, D), BF16),
                   jax.ShapeDtypeStruct((D, T), BF16)],
        compiler_params=_cparams(("parallel",)),
        name="merge",
    )(x2d, ya, yb, proj, proj, wout, nf)


def _staircase(k):
    return [(ka, k // (ka + 1)) for ka in range(k)]


def _kth_largest(cur, k):
    m = None
    for _ in range(k):
        m = jnp.max(cur, axis=0, keepdims=True)
        cur = jnp.where(cur == m, NEG_INF, cur)
    return m


def _double_bf16(x):
    bits = pltpu.bitcast(x.astype(BF16).astype(F32), jnp.uint32)
    return bits | (bits >> 16)


def _route_body(h2_ref, wq_ref, keys_ref, na_ref, ea_ref, rb_ref, eb_ref, sv_s, cand_s):
    NH = keys_ref.shape[0]
    NK, KH = keys_ref.shape[2], keys_ref.shape[3]
    K = PEER_TOPK
    q = _dot(h2_ref[...], wq_ref[...]).astype(BF16)
    cand_s[...] = jnp.full(cand_s.shape, NEG_INF, F32)
    for h in range(NH):
        rank, ex = [], []
        for s in range(2):
            off = (h * 2 + s) * KH
            sT = _dot_nt(keys_ref[h, s], q[:, off:off + KH])
            cur = sT
            rk = jnp.full(sT.shape, float(K), F32)
            for k in range(K):
                m = jnp.max(cur, axis=0, keepdims=True)
                sv_s[s, k:k + 1, :] = m
                eq = cur == m
                rk = jnp.where(eq, float(k), rk)
                cur = jnp.where(eq, NEG_INF, cur)
            rank.append(rk)
            ex.append(jnp.where(rk < K, jnp.exp(sT - sv_s[s, 0:1, :]), 0.0))
        off = 0
        for ka, nb in _staircase(K):
            cand_s[off:off + nb, :] = sv_s[0, ka:ka + 1, :] + sv_s[1, 0:nb, :]
            off += nb
        cand = cand_s[...]
        tau = _kth_largest(cand, K)
        top = sv_s[0, 0:1, :] + sv_s[1, 0:1, :]
        z = jnp.sum(jnp.where(cand >= tau, jnp.exp(cand - top), 0.0), axis=0, keepdims=True)
        na = jnp.zeros(rank[0].shape, F32)
        off = 0
        for ka, nb in _staircase(K):
            cnt = jnp.sum(jnp.where(cand_s[off:off + nb, :] >= tau, 1.0, 0.0), axis=0, keepdims=True)
            na = jnp.where(rank[0] == float(ka), cnt, na)
            off += nb
        na_ref[h * NK:(h + 1) * NK, :] = _double_bf16(2.0 * na)
        ea_ref[h * NK:(h + 1) * NK, :] = _double_bf16(ex[0] / z)
        rb_ref[h] = (2.0 * rank[1]).astype(BF16)
        eb_ref[h] = ex[1].astype(BF16)


def _route(h2, wq, keys):
    T, D = h2.shape
    NH, _, NK, KH = keys.shape
    TB = _pick(T, 512)
    ncand = sum(nb for _, nb in _staircase(PEER_TOPK))
    ncand_pad = -(-ncand // SUBLANES) * SUBLANES
    flat = pl.BlockSpec((NH * NK, TB), lambda i: (0, i))
    tok = pl.BlockSpec((NH, NK, TB), lambda i: (0, 0, i))
    return pl.pallas_call(
        _route_body,
        grid=(T // TB,),
        in_specs=[pl.BlockSpec((TB, D), lambda i: (i, 0)),
                  pl.BlockSpec(wq.shape, lambda i: (0, 0)),
                  pl.BlockSpec(keys.shape, lambda i: (0, 0, 0, 0))],
        out_specs=[flat, flat, tok, tok],
        out_shape=[jax.ShapeDtypeStruct((NH * NK, T), jnp.uint32), jax.ShapeDtypeStruct((NH * NK, T), jnp.uint32),
                   jax.ShapeDtypeStruct((NH, NK, T), BF16), jax.ShapeDtypeStruct((NH, NK, T), BF16)],
        scratch_shapes=[pltpu.VMEM((2, PEER_TOPK, TB), F32), pltpu.VMEM((ncand_pad, TB), F32)],
        compiler_params=_cparams(("parallel",)),
        name="peer_route",
    )(h2, wq, keys)


def _peer_stage(ev, h2t_ref, u_ref, vt_ref, na_ref, ea_ref, rb_ref, eb_ref, acc_s, s_prev, s_next, a_prev, a_next):
    NH, NK, TB = rb_ref.shape
    ni = u_ref.shape[0] // NK
    npc, PR = vt_ref.shape[0], vt_ref.shape[2]
    ipp = PR // NK

    def rows_bf16(grp, il):
        return pltpu.bitcast(jnp.broadcast_to(grp[il:il + 1, :], (NK // 2, LANES)), BF16)

    for k in range(npc):
        pr = slice(k * PR, (k + 1) * PR)
        s_next[pr, :] = _dot(u_ref[pr, :], h2t_ref[...])
        acc_s[...] += _dot(vt_ref[k], a_prev[pr, :])
        for tl in range(TB // LANES):
            ln = slice(tl * LANES, (tl + 1) * LANES)
            rows = [pl.ds(pl.multiple_of(h * NK + ev * ni, SUBLANES), ni) for h in range(NH)]
            na = [na_ref[rows[h], ln] for h in range(NH)]
            ea = [ea_ref[rows[h], ln] for h in range(NH)]
            for il in range(k * ipp, (k + 1) * ipp):
                w = jnp.zeros((NK, LANES), BF16)
                for h in range(NH):
                    d = rows_bf16(na[h], il) - rb_ref[h, :, ln]
                    w = w + rows_bf16(ea[h], il) * jnp.maximum(jnp.minimum(eb_ref[h, :, ln], d), 0.0)
                e = slice(il * NK, (il + 1) * NK)
                a_next[e, ln] = _gelu(s_prev[e, ln]).astype(BF16) * w


def _peer_body(nE, NC, h2t_ref, x1_ref, u_ref, vt_ref, na_ref, ea_ref, rb_ref, eb_ref, nfin_ref, y_ref,
               s0, s1, a0, a1, acc_s):
    g = pl.program_id(0)
    ev = lax.rem(jnp.clip(g - 1, 0, NC - 1), nE)
    ca = g - 2

    @pl.when(g == 0)
    def _():
        for ref in (s0, s1, a0, a1, acc_s):
            ref[...] = jnp.zeros(ref.shape, ref.dtype)

    @pl.when(jnp.logical_and(ca >= 0, lax.rem(ca, nE) == 0))
    def _():
        acc_s[...] = jnp.zeros(acc_s.shape, F32)

    args = (ev, h2t_ref, u_ref, vt_ref, na_ref, ea_ref, rb_ref, eb_ref, acc_s)

    @pl.when(lax.rem(g, 2) == 0)
    def _():
        _peer_stage(*args, s1, s0, a1, a0)

    @pl.when(lax.rem(g, 2) == 1)
    def _():
        _peer_stage(*args, s0, s1, a0, a1)

    @pl.when(jnp.logical_and(ca >= 0, lax.rem(ca, nE) == nE - 1))
    def _():
        x2 = x1_ref[...] + jnp.transpose(acc_s[...])
        y_ref[...] = _rms(x2, nfin_ref[...])


def _peer(h2t, x1, u, vt, na, ea, rb, eb, nfin):
    D, T = h2t.shape
    PR = vt.shape[2]
    E = u.shape[0]
    NH, NK, _ = rb.shape
    TB = _pick(T, 512)
    EC = SUBLANES * NK
    assert TB % LANES == 0 and E % EC == 0
    nE = E // EC
    NC = (T // TB) * nE
    cs = lambda g: jnp.minimum(g, NC - 1)
    cv = lambda g: jnp.clip(g - 1, 0, NC - 1)
    ca = lambda g: jnp.clip(g - 2, 0, NC - 1)
    tok = pl.BlockSpec((NH, NK, TB), lambda g: (0, 0, cv(g) // nE))
    flat = pl.BlockSpec((NH * NK, TB), lambda g: (0, cv(g) // nE))
    return pl.pallas_call(
        functools.partial(_peer_body, nE, NC),
        grid=(NC + 2,),
        in_specs=[pl.BlockSpec((D, TB), lambda g: (0, cs(g) // nE)),
                  pl.BlockSpec((TB, D), lambda g: (ca(g) // nE, 0)),
                  pl.BlockSpec((EC, D), lambda g: (cs(g) % nE, 0)),
                  pl.BlockSpec((EC // PR, D, PR), lambda g: (ca(g) % nE, 0, 0)),
                  flat, flat, tok, tok,
                  pl.BlockSpec((1, D), lambda g: (0, 0))],
        out_specs=pl.BlockSpec((TB, D), lambda g: (ca(g) // nE, 0)),
        out_shape=jax.ShapeDtypeStruct((T, D), F32),
        scratch_shapes=[pltpu.VMEM((EC, TB), F32), pltpu.VMEM((EC, TB), F32),
                        pltpu.VMEM((EC, TB), BF16), pltpu.VMEM((EC, TB), BF16),
                        pltpu.VMEM((D, TB), F32)],
        compiler_params=_cparams(("arbitrary",)),
        name="peer_mix",
    )(h2t, x1, u, vt, na, ea, rb, eb, nfin)


def _pad_lanes(row):
    return jnp.pad(row, ((0, 0), (0, LANES - row.shape[1])))


def kernel(x_prompt, x_sample, state_rg_h, state_rg_conv, state_ssd_h, state_ssd_conv, norm_mix, w_in, rg_conv_w, rg_conv_b, rg_wa, rg_ba, rg_wx, rg_bx, rg_lam, w_rg_out, ssd_conv_w, ssd_conv_b, ssd_dt_bias, ssd_a_log, ssd_d, ssd_norm, w_ssd_out, w_out, norm_ffn, peer_wq, peer_keys, peer_u, peer_v, norm_final):
    depth = w_in.shape[0]
    B, L, D = x_prompt.shape
    Bs, Ls, _ = x_sample.shape
    assert depth == 1 and Ls == SAMPLE_ROWS // 2 and rg_conv_w.shape[1] == 4 and L >= 3
    W = rg_conv_w.shape[2]
    H = ssd_a_log.shape[1]
    inner = ssd_norm.shape[1]
    CD = ssd_conv_w.shape[2]
    P = inner // H
    N = state_ssd_h.shape[-1]
    G = (CD - inner) // (2 * N)
    R = H // G
    dims = (G, R, P, N)
    assert W == D and inner == 2 * D and CD == 3 * D and H % SUBLANES == 0 and H <= LANES
    o_x, o_g, o_z, o_xbc, o_dt, o_m = 0, W, 2 * W, 2 * W + inner, 2 * W + inner + CD, 2 * W + inner + CD + H
    col_xbc, col_x, col_z, col_g, col_ga, col_gb = 0, CD // W, (CD + W) // inner, (CD + W + inner) // W, \
        (CD + 2 * W + inner) // D, (CD + 2 * W + inner + D) // D

    yp = x_prompt.reshape(B * L, D)
    ys = x_sample.reshape(Bs * Ls, D)
    outs = {k: [] for k in ("prh", "prc", "psh", "psc", "srh", "src", "ssh", "ssc")}
    lead = SAMPLE_ROWS - Ls
    for l in range(depth):
        wi = w_in[l]
        w_main = jnp.concatenate([wi[:, o_xbc:o_dt], wi[:, o_x:o_g], wi[:, o_z:o_xbc], wi[:, o_g:o_z],
                                  wi[:, o_m:]], axis=1).astype(BF16)
        wdtT = wi[:, o_dt:o_m].T
        wdt = _pad_lanes(wi[:, o_dt:o_m])
        g_mix = norm_mix[l][None]
        rg_w = (rg_conv_w[l], rg_conv_b[l][None], rg_wa[l].astype(BF16), rg_wx[l].astype(BF16),
                rg_ba[l][None], rg_bx[l][None], rg_lam[l][None], w_rg_out[l].astype(BF16))
        ssd_w = (ssd_conv_w[l], ssd_conv_b[l][None], _pad_lanes(ssd_dt_bias[l][None]), ssd_dt_bias[l][:, None],
                 _pad_lanes(ssd_a_log[l][None]), ssd_a_log[l][:, None], jnp.repeat(ssd_d[l], P)[None],
                 ssd_norm[l][None], w_ssd_out[l].astype(BF16))
        wout = w_out[l].astype(BF16)
        nf = norm_ffn[l][None]
        wq = peer_wq[l].astype(BF16)
        keys = peer_keys[l].astype(BF16)
        u = peer_u[l].astype(BF16)
        vt = peer_v[l].astype(BF16).reshape(-1, PEER_PIECE, D).transpose(0, 2, 1)
        nfin = norm_final[None]

        proj, dt, dtT = _inproj(yp, g_mix, w_main, wdt, wdtT)
        ya, rgh = _rg_prompt(proj, B, L, col_x, col_g, *rg_w)
        yb, ssh = _ssd_prompt(proj, dt, dtT, B, L, col_xbc, col_z, dims, *ssd_w)
        p3 = proj.reshape(B, L, -1)
        outs["prh"].append(rgh.reshape(B, W))
        outs["prc"].append(p3[:, L - 3:, CD:CD + W])
        outs["psh"].append(ssh)
        outs["psc"].append(p3[:, L - 3:, :CD])
        x1, h2, h2t = _merge(yp, ya, yb, proj, col_ga, col_gb, wout, nf)
        yp = _peer(h2t, x1, u, vt, *_route(h2, wq, keys), nfin)

        xe = jnp.pad(ys.reshape(Bs, Ls, D), ((0, 0), (lead, 0), (0, 0))).reshape(Bs * SAMPLE_ROWS, D)
        proj_e, dt_e, dtT_e = _inproj(xe, g_mix, w_main, wdt, wdtT)
        rows = lambda a3: a3.reshape(Bs * SAMPLE_ROWS, a3.shape[-1])
        rg_cpad = rows(jnp.pad(state_rg_conv[l], ((0, 0), (lead - 3, Ls), (0, 0))))
        rg_hpad = rows(jnp.pad(state_rg_h[l][:, None, :], ((0, 0), (lead - 1, Ls), (0, 0))))
        ssd_cpad = rows(jnp.pad(state_ssd_conv[l], ((0, 0), (lead - 3, Ls), (0, 0))))
        ya_e, h_e = _rg_sample(proj_e, col_x, col_g, rg_cpad, rg_hpad, *rg_w)
        yb_e, ssh_s = _ssd_sample(proj_e, dt_e, dtT_e, ssd_cpad, state_ssd_h[l], col_xbc, col_z, dims, *ssd_w)
        toks = lambda a2: a2.reshape(Bs, SAMPLE_ROWS, -1)[:, lead:].reshape(Bs * Ls, -1)
        p3 = proj_e.reshape(Bs, SAMPLE_ROWS, -1)
        outs["srh"].append(h_e.reshape(Bs, SAMPLE_ROWS, W)[:, -1])
        outs["src"].append(p3[:, SAMPLE_ROWS - 3:, CD:CD + W])
        outs["ssh"].append(ssh_s)
        outs["ssc"].append(p3[:, SAMPLE_ROWS - 3:, :CD])
        proj_s = toks(proj_e)
        x1, h2, h2t = _merge(ys, toks(ya_e), toks(yb_e), proj_s, col_ga, col_gb, wout, nf)
        ys = _peer(h2t, x1, u, vt, *_route(h2, wq, keys), nfin)

    st = lambda k: jnp.stack(outs[k])
    return (yp.reshape(B, L, D), ys.reshape(Bs, Ls, D), st("prh"), st("prc"), st("psh"), st("psc"),
            st("srh"), st("src"), st("ssh"), st("ssc"))
```

```python
import functools

import jax
import jax.numpy as jnp
from jax import lax
from jax.experimental import pallas as pl
from jax.experimental.pallas import tpu as pltpu

F32 = jnp.float32
BF16 = jnp.bfloat16
EPS = 1e-6
RG_C = 8.0
PEER_TOPK = 16
SSD_CHUNK = 128
LANES = 128
SUBLANES = 8
BF16_ROWS = 16
SAMPLE_ROWS = 8
NEG_INF = float("-inf")
HIGHEST = lax.Precision.HIGHEST
NT_DIMS = (((1,), (1,)), ((), ()))
VMEM_LIMIT = 56 * 1024 * 1024


def _cparams(sem):
    return pltpu.CompilerParams(dimension_semantics=sem, vmem_limit_bytes=VMEM_LIMIT)


def _pick(n, pref):
    t = min(n, pref)
    while n % t:
        t -= SUBLANES
    return t


def _sigmoid(x):
    return 1.0 / (1.0 + jnp.exp(-x))


def _silu(x):
    return x * _sigmoid(x)


def _softplus(x):
    return jnp.maximum(x, 0.0) + jnp.log1p(jnp.exp(-jnp.abs(x)))


def _gelu(x):
    k1 = -2.0 * 0.7978845608028654 * 1.4426950408889634
    return x / (1.0 + jnp.exp2(x * (k1 + (k1 * 0.044715) * (x * x))))


def _rms(x, g):
    return x * lax.rsqrt(jnp.mean(x * x, axis=-1, keepdims=True) + EPS) * g


def _dot(a, b):
    return jnp.dot(a, b, preferred_element_type=F32)


def _dot_nt(a, b):
    return lax.dot_general(a, b, NT_DIMS, preferred_element_type=F32)


def _inproj_body(x_ref, g_ref, w_ref, wdt_ref, wdtT_ref, o_ref, odt_ref, odtT_ref, xn_ref):
    @pl.when(pl.program_id(1) == 0)
    def _():
        xn = _rms(x_ref[...], g_ref[...])
        xn_ref[...] = xn.astype(BF16)
        odt_ref[...] = jnp.dot(xn, wdt_ref[...], precision=HIGHEST, preferred_element_type=F32)
        odtT_ref[...] = lax.dot_general(wdtT_ref[...], xn, NT_DIMS, precision=HIGHEST,
                                        preferred_element_type=F32)

    o_ref[...] = _dot(xn_ref[...], w_ref[...])


def _inproj(x2d, g, w_main, wdt, wdtT):
    T, D = x2d.shape
    N = w_main.shape[1]
    H = wdtT.shape[0]
    TM = _pick(T, 1024)
    TN = 1536
    assert N % TN == 0 and TM % LANES == 0
    return pl.pallas_call(
        _inproj_body,
        grid=(T // TM, N // TN),
        in_specs=[
            pl.BlockSpec((TM, D), lambda i, j: (i, 0)),
            pl.BlockSpec((1, D), lambda i, j: (0, 0)),
            pl.BlockSpec((D, TN), lambda i, j: (0, j)),
            pl.BlockSpec((D, LANES), lambda i, j: (0, 0)),
            pl.BlockSpec((H, D), lambda i, j: (0, 0)),
        ],
        out_specs=[
            pl.BlockSpec((TM, TN), lambda i, j: (i, j)),
            pl.BlockSpec((TM, LANES), lambda i, j: (i, 0)),
            pl.BlockSpec((H, TM), lambda i, j: (0, i)),
        ],
        out_shape=[
            jax.ShapeDtypeStruct((T, N), F32),
            jax.ShapeDtypeStruct((T, LANES), F32),
            jax.ShapeDtypeStruct((H, T), F32),
        ],
        scratch_shapes=[pltpu.VMEM((TM, D), BF16)],
        compiler_params=_cparams(("parallel", "arbitrary")),
        name="inproj",
    )(x2d, g, w_main, wdt, wdtT)


def _rg_gates(xc, wa_ref, wx_ref, ba, bx, lam):
    xb = xc.astype(BF16)
    nb, bw = wa_ref.shape[0], wa_ref.shape[1]
    rs, gs = [], []
    for k in range(nb):
        xk = xb[:, k * bw:(k + 1) * bw]
        rs.append(_dot(xk, wa_ref[k]))
        gs.append(_dot(xk, wx_ref[k]))
    r = _sigmoid(jnp.concatenate(rs, axis=1) + ba)
    i = _sigmoid(jnp.concatenate(gs, axis=1) + bx)
    log_a = -RG_C * r * _softplus(-lam)
    a = jnp.exp(log_a)
    em1 = jnp.tanh(log_a) * (a * a + 1.0)
    return a, jnp.sqrt(-em1) * (i * xc)


def _rg_prompt_body(x_ref, gate_ref, cw_ref, cb_ref, wa_ref, wx_ref, ba_ref, bx_ref, lam_ref, wo_ref,
                    ya_ref, hfin_ref, pad_s, a_s, b_s, h_s):
    Lc, W = x_ref.shape

    @pl.when(pl.program_id(1) == 0)
    def _():
        pad_s[0:SUBLANES, :] = jnp.zeros((SUBLANES, W), F32)
        h_s[...] = jnp.zeros(h_s.shape, F32)

    pad_s[SUBLANES:SUBLANES + Lc, :] = x_ref[...]
    xc = (cb_ref[...]
          + cw_ref[3:4, :] * pad_s[8:8 + Lc, :]
          + cw_ref[2:3, :] * pad_s[7:7 + Lc, :]
          + cw_ref[1:2, :] * pad_s[6:6 + Lc, :]
          + cw_ref[0:1, :] * pad_s[5:5 + Lc, :])
    pad_s[0:SUBLANES, :] = pad_s[Lc:Lc + SUBLANES, :]
    a, bt = _rg_gates(xc, wa_ref, wx_ref, ba_ref[...], bx_ref[...], lam_ref[...])
    a_s[...] = a
    b_s[...] = bt

    def step(t, h):
        h = a_s[pl.ds(t, 1), :] * h + b_s[pl.ds(t, 1), :]
        b_s[pl.ds(t, 1), :] = h
        return h

    h = lax.fori_loop(0, Lc, step, h_s[0:1, :], unroll=8)
    h_s[0:1, :] = h
    y = b_s[...] * _gelu(gate_ref[...])
    ya_ref[...] = _dot(y.astype(BF16), wo_ref[...])
    hfin_ref[0] = h


def _rg_prompt(proj, B, L, col_x, col_g, cw, cb, wa, wx, ba, bx, lam, wo):
    W = cw.shape[1]
    Lc = _pick(L, 256)
    nC = L // Lc
    full = lambda *s: pl.BlockSpec(s, lambda b, c: (0,) * len(s))
    return pl.pallas_call(
        _rg_prompt_body,
        grid=(B, nC),
        in_specs=[
            pl.BlockSpec((Lc, W), lambda b, c: (b * nC + c, col_x)),
            pl.BlockSpec((Lc, W), lambda b, c: (b * nC + c, col_g)),
            full(4, W), full(1, W), full(*wa.shape), full(*wx.shape), full(1, W), full(1, W), full(1, W),
            full(*wo.shape),
        ],
        out_specs=[
            pl.BlockSpec((Lc, wo.shape[1]), lambda b, c: (b * nC + c, 0)),
            pl.BlockSpec((1, 1, W), lambda b, c: (b, 0, 0)),
        ],
        out_shape=[
            jax.ShapeDtypeStruct((B * L, wo.shape[1]), F32),
            jax.ShapeDtypeStruct((B, 1, W), F32),
        ],
        scratch_shapes=[
            pltpu.VMEM((Lc + SUBLANES, W), F32),
            pltpu.VMEM((Lc, W), F32),
            pltpu.VMEM((Lc, W), F32),
            pltpu.VMEM((SUBLANES, W), F32),
        ],
        compiler_params=_cparams(("parallel", "arbitrary")),
        name="rg_prompt",
    )(proj, proj, cw, cb, wa, wx, ba, bx, lam, wo)


def _conv_rolled(u, cw_ref, cb_ref):
    return (cb_ref[...]
            + cw_ref[3:4, :] * u
            + cw_ref[2:3, :] * pltpu.roll(u, 1, axis=0)
            + cw_ref[1:2, :] * pltpu.roll(u, 2, axis=0)
            + cw_ref[0:1, :] * pltpu.roll(u, 3, axis=0))


def _rg_sample_body(x_ref, gate_ref, cpad_ref, hpad_ref, cw_ref, cb_ref, wa_ref, wx_ref, ba_ref, bx_ref,
                    lam_ref, wo_ref, ya_ref, h_ref):
    xc = _conv_rolled(x_ref[...] + cpad_ref[...], cw_ref, cb_ref)
    a, bt = _rg_gates(xc, wa_ref, wx_ref, ba_ref[...], bx_ref[...], lam_ref[...])
    row = lax.broadcasted_iota(jnp.int32, a.shape, 0) & (SAMPLE_ROWS - 1)
    h = hpad_ref[...]
    for k in range(SAMPLE_ROWS // 2, SAMPLE_ROWS):
        h = jnp.where(row == k, a * pltpu.roll(h, 1, axis=0) + bt, h)
    h_ref[...] = h
    ya_ref[...] = _dot((h * _gelu(gate_ref[...])).astype(BF16), wo_ref[...])


def _rg_sample(proj, col_x, col_g, cpad, hpad, cw, cb, wa, wx, ba, bx, lam, wo):
    T = proj.shape[0]
    W = cw.shape[1]
    TM = _pick(T, 256)
    full = lambda *s: pl.BlockSpec(s, lambda i: (0,) * len(s))
    return pl.pallas_call(
        _rg_sample_body,
        grid=(T // TM,),
        in_specs=[
            pl.BlockSpec((TM, W), lambda i: (i, col_x)),
            pl.BlockSpec((TM, W), lambda i: (i, col_g)),
            pl.BlockSpec((TM, W), lambda i: (i, 0)),
            pl.BlockSpec((TM, W), lambda i: (i, 0)),
            full(4, W), full(1, W), full(*wa.shape), full(*wx.shape), full(1, W), full(1, W), full(1, W),
            full(*wo.shape),
        ],
        out_specs=[
            pl.BlockSpec((TM, wo.shape[1]), lambda i: (i, 0)),
            pl.BlockSpec((TM, W), lambda i: (i, 0)),
        ],
        out_shape=[
            jax.ShapeDtypeStruct((T, wo.shape[1]), F32),
            jax.ShapeDtypeStruct((T, W), F32),
        ],
        compiler_params=_cparams(("parallel",)),
        name="rg_sample",
    )(proj, proj, cpad, hpad, cw, cb, wa, wx, ba, bx, lam, wo)


def _ssd_chunk(act_s, dt, dtT, alog_ref, alogT_ref, dexp_ref, seq_shift, G, R, P, N,
               yint_fn, y_s, xw_s):
    Q = act_s.shape[0]
    inner = G * R * P
    a = dt * (-jnp.exp(alog_ref[...]))
    aT = dtT * (-jnp.exp(alogT_ref[...]))
    ri = lax.broadcasted_iota(jnp.int32, (Q, Q), 0)
    ci = lax.broadcasted_iota(jnp.int32, (Q, Q), 1)
    same = (ri >> seq_shift) == (ci >> seq_shift)
    causal = same & (ci <= ri)
    causal_f = jnp.where(causal, 1.0, 0.0).astype(F32)
    same_f = jnp.where(same, 1.0, 0.0).astype(F32)
    acum = jnp.dot(causal_f, a, precision=HIGHEST, preferred_element_type=F32)
    tot = jnp.dot(same_f, a, precision=HIGHEST, preferred_element_type=F32)
    acumT = lax.dot_general(aT, causal_f, NT_DIMS, precision=HIGHEST,
                            preferred_element_type=F32)
    eacum = jnp.exp(acum)
    toend = jnp.exp(tot - acum) * dt
    for g in range(G):
        Bg = act_s[:, inner + g * N:inner + (g + 1) * N].astype(BF16)
        Cg = act_s[:, inner + (G + g) * N:inner + (G + g + 1) * N].astype(BF16)
        cb = _dot_nt(Cg, Bg)
        yint = yint_fn(g, Cg)
        for r in range(R):
            h = g * R + r
            sl = slice(h * P, (h + 1) * P)
            seg = acum[:, h:h + 1] - acumT[h:h + 1, :]
            decay = jnp.exp(jnp.where(causal, seg, NEG_INF))
            wgt = (cb * decay * dtT[h:h + 1, :]).astype(BF16)
            xh = act_s[:, sl]
            yh = _dot(wgt, xh.astype(BF16)) + dexp_ref[:, sl] * xh
            if yint is not None:
                yh = yh + yint[:, r * P:(r + 1) * P] * eacum[:, h:h + 1]
            y_s[:, sl] = yh
            xw_s[:, sl] = xh * toend[:, h:h + 1]
    return tot, eacum


def _ssd_finish(y, z, nw_ref, wo_ref):
    y = y * _silu(z)
    return _dot(_rms(y, nw_ref[...]).astype(BF16), wo_ref[...])


def _ssd_prompt_body(G, R, P, N,
                     xbc_ref, z_ref, dt_ref, dtT_ref, cw_ref, cb_ref, dtb_ref, dtbT_ref, alog_ref, alogT_ref,
                     dexp_ref, nw_ref, wo_ref, yb_ref, st_ref, pad_s, act_s, y_s, xw_s):
    Q = xbc_ref.shape[0]

    @pl.when(pl.program_id(1) == 0)
    def _():
        pad_s[0:SUBLANES, :] = jnp.zeros((SUBLANES, pad_s.shape[1]), F32)
        st_ref[...] = jnp.zeros(st_ref.shape, F32)

    pad_s[SUBLANES:SUBLANES + Q, :] = xbc_ref[...]
    conv = (cb_ref[...]
            + cw_ref[3:4, :] * pad_s[8:8 + Q, :]
            + cw_ref[2:3, :] * pad_s[7:7 + Q, :]
            + cw_ref[1:2, :] * pad_s[6:6 + Q, :]
            + cw_ref[0:1, :] * pad_s[5:5 + Q, :])
    pad_s[0:SUBLANES, :] = pad_s[Q:Q + SUBLANES, :]
    act_s[...] = _silu(conv)
    dt = _softplus(dt_ref[...] + dtb_ref[...])
    dtT = _softplus(dtT_ref[...] + dtbT_ref[...])

    def yint_fn(g, Cg):
        stg = st_ref[0, g * R:(g + 1) * R].reshape(R * P, N)
        return _dot_nt(Cg, stg.astype(BF16))

    seq_shift = Q.bit_length() - 1
    tot, _ = _ssd_chunk(act_s, dt, dtT, alog_ref, alogT_ref, dexp_ref, seq_shift, G, R, P, N,
                        yint_fn, y_s, xw_s)
    dec = jnp.exp(tot[0:1, :])
    inner = G * R * P
    for g in range(G):
        Bg = act_s[:, inner + g * N:inner + (g + 1) * N].astype(BF16)
        xwT = jnp.transpose(xw_s[:, g * R * P:(g + 1) * R * P])
        S = _dot(xwT.astype(BF16), Bg)
        for r in range(R):
            h = g * R + r
            st_ref[0, h] = dec[:, h:h + 1] * st_ref[0, h] + S[r * P:(r + 1) * P, :]
    yb_ref[...] = _ssd_finish(y_s[...], z_ref[...], nw_ref, wo_ref)


def _ssd_prompt(proj, dt, dtT, B, L, col_xbc, col_z, dims, cw, cb, dtb, dtbT, alog, alogT, dexp, nw, wo):
    G, R, P, N = dims
    H = G * R
    inner = H * P
    CD = cw.shape[1]
    Q = _pick(L, SSD_CHUNK)
    nC = L // Q
    full = lambda *s: pl.BlockSpec(s, lambda b, c: (0,) * len(s))
    return pl.pallas_call(
        functools.partial(_ssd_prompt_body, G, R, P, N),
        grid=(B, nC),
        in_specs=[
            pl.BlockSpec((Q, CD), lambda b, c: (b * nC + c, col_xbc)),
            pl.BlockSpec((Q, inner), lambda b, c: (b * nC + c, col_z)),
            pl.BlockSpec((Q, LANES), lambda b, c: (b * nC + c, 0)),
            pl.BlockSpec((H, Q), lambda b, c: (0, b * nC + c)),
            full(4, CD), full(1, CD), full(1, LANES), full(H, 1), full(1, LANES), full(H, 1),
            full(1, inner), full(1, inner), full(*wo.shape),
        ],
        out_specs=[
            pl.BlockSpec((Q, wo.shape[1]), lambda b, c: (b * nC + c, 0)),
            pl.BlockSpec((1, H, P, N), lambda b, c: (b, 0, 0, 0)),
        ],
        out_shape=[
            jax.ShapeDtypeStruct((B * L, wo.shape[1]), F32),
            jax.ShapeDtypeStruct((B, H, P, N), F32),
        ],
        scratch_shapes=[
            pltpu.VMEM((Q + SUBLANES, CD), F32),
            pltpu.VMEM((Q, CD), F32),
            pltpu.VMEM((Q, inner), F32),
            pltpu.VMEM((Q, inner), F32),
        ],
        compiler_params=_cparams(("parallel", "arbitrary")),
        name="ssd_prompt",
    )(proj, proj, dt, dtT, cw, cb, dtb, dtbT, alog, alogT, dexp, nw, wo)


def _ssd_sample_body(G, R, P, N,
                     xbc_ref, cpad_ref, z_ref, dt_ref, dtT_ref, h0_ref, cw_ref, cb_ref, dtb_ref, dtbT_ref,
                     alog_ref, alogT_ref, dexp_ref, nw_ref, wo_ref, yb_ref, st_ref,
                     act_s, y_s, xw_s, yint_s, xwT_s, tot_s, eacum_s):
    Q = xbc_ref.shape[0]
    inner = G * R * P
    bi = pl.program_id(1)
    half = SAMPLE_ROWS // 2

    @pl.when(bi == 0)
    def _():
        act_s[...] = _silu(_conv_rolled(xbc_ref[...] + cpad_ref[...], cw_ref, cb_ref))
        rows = lax.broadcasted_iota(jnp.int32, (Q, LANES), 0) & (SAMPLE_ROWS - 1)
        cols = lax.broadcasted_iota(jnp.int32, dtT_ref.shape, 1) & (SAMPLE_ROWS - 1)
        dt = jnp.where(rows >= half, _softplus(dt_ref[...] + dtb_ref[...]), 0.0)
        dtT = jnp.where(cols >= half, _softplus(dtT_ref[...] + dtbT_ref[...]), 0.0)
        seq_shift = SAMPLE_ROWS.bit_length() - 1
        tot, eacum = _ssd_chunk(act_s, dt, dtT, alog_ref, alogT_ref, dexp_ref, seq_shift, G, R, P, N,
                                lambda g, Cg: None, y_s, xw_s)
        tot_s[...] = tot
        eacum_s[...] = eacum
        for g in range(G):
            xwT_s[g] = jnp.transpose(xw_s[:, g * R * P:(g + 1) * R * P]).astype(BF16)

    r0 = pl.multiple_of(bi * SAMPLE_ROWS, SAMPLE_ROWS)
    rid = lax.broadcasted_iota(jnp.int32, (Q, N), 0)
    mine = (rid >= r0) & (rid < r0 + SAMPLE_ROWS)
    dec = jnp.exp(tot_s[pl.ds(r0, 1), :])
    for g in range(G):
        Cb = act_s[pl.ds(r0, SAMPLE_ROWS), inner + (G + g) * N:inner + (G + g + 1) * N]
        stg = h0_ref[0, g * R:(g + 1) * R].reshape(R * P, N)
        yint_s[pl.ds(r0, SAMPLE_ROWS), g * R * P:(g + 1) * R * P] = _dot_nt(Cb, stg)
        Bg = jnp.where(mine, act_s[:, inner + g * N:inner + (g + 1) * N], 0.0).astype(BF16)
        S = _dot(xwT_s[g], Bg)
        for r in range(R):
            h = g * R + r
            st_ref[0, h] = dec[:, h:h + 1] * h0_ref[0, h] + S[r * P:(r + 1) * P, :]

    @pl.when(bi == pl.num_programs(1) - 1)
    def _():
        for h in range(G * R):
            sl = slice(h * P, (h + 1) * P)
            y_s[:, sl] = y_s[:, sl] + yint_s[:, sl] * eacum_s[:, h:h + 1]
        yb_ref[...] = _ssd_finish(y_s[...], z_ref[...], nw_ref, wo_ref)


def _ssd_sample(proj, dt, dtT, cpad, h0, col_xbc, col_z, dims, cw, cb, dtb, dtbT, alog, alogT, dexp, nw, wo):
    G, R, P, N = dims
    H = G * R
    inner = H * P
    CD = cw.shape[1]
    T = proj.shape[0]
    Bs = h0.shape[0]
    Q = _pick(T, SSD_CHUNK)
    nb = Q // SAMPLE_ROWS
    full = lambda *s: pl.BlockSpec(s, lambda i, j: (0,) * len(s))
    return pl.pallas_call(
        functools.partial(_ssd_sample_body, G, R, P, N),
        grid=(T // Q, nb),
        in_specs=[
            pl.BlockSpec((Q, CD), lambda i, j: (i, col_xbc)),
            pl.BlockSpec((Q, CD), lambda i, j: (i, 0)),
            pl.BlockSpec((Q, inner), lambda i, j: (i, col_z)),
            pl.BlockSpec((Q, LANES), lambda i, j: (i, 0)),
            pl.BlockSpec((H, Q), lambda i, j: (0, i)),
            pl.BlockSpec((1, H, P, N), lambda i, j: (i * nb + j, 0, 0, 0)),
            full(4, CD), full(1, CD), full(1, LANES), full(H, 1), full(1, LANES), full(H, 1),
            full(1, inner), full(1, inner), full(*wo.shape),
        ],
        out_specs=[
            pl.BlockSpec((Q, wo.shape[1]), lambda i, j: (i, 0)),
            pl.BlockSpec((1, H, P, N), lambda i, j: (i * nb + j, 0, 0, 0)),
        ],
        out_shape=[
            jax.ShapeDtypeStruct((T, wo.shape[1]), F32),
            jax.ShapeDtypeStruct((Bs, H, P, N), F32),
        ],
        scratch_shapes=[
            pltpu.VMEM((Q, CD), F32),
            pltpu.VMEM((Q, inner), F32),
            pltpu.VMEM((Q, inner), F32),
            pltpu.VMEM((Q, inner), F32),
            pltpu.VMEM((G, R * P, Q), BF16),
            pltpu.VMEM((Q, LANES), F32),
            pltpu.VMEM((Q, LANES), F32),
        ],
        compiler_params=_cparams(("parallel", "arbitrary")),
        name="ssd_sample",
    )(proj, cpad, proj, dt, dtT, h0, cw, cb, dtb, dtbT, alog, alogT, dexp, nw, wo)


def _merge_body(x_ref, ya_ref, yb_ref, ga_ref, gb_ref, wout_ref, nf_ref, x1_ref, h2_ref, h2t_ref):
    m = _sigmoid(ga_ref[...]) * ya_ref[...] + _sigmoid(gb_ref[...]) * yb_ref[...]
    x1 = x_ref[...] + _dot(m.astype(BF16), wout_ref[...])
    x1_ref[...] = x1
    h2 = _rms(x1, nf_ref[...])
    h2_ref[...] = h2.astype(BF16)
    h2t_ref[...] = jnp.transpose(h2).astype(BF16)


def _merge(x2d, ya, yb, proj, col_ga, col_gb, wout, nf):
    T, D = x2d.shape
    TM = _pick(T, 512)
    row = lambda c: pl.BlockSpec((TM, D), lambda i: (i, c))
    return pl.pallas_call(
        _merge_body,
        grid=(T // TM,),
        in_specs=[row(0), row(0), row(0), row(col_ga), row(col_gb),
                  pl.BlockSpec(wout.shape, lambda i: (0, 0)), pl.BlockSpec((1, D), lambda i: (0, 0))],
        out_specs=[row(0), row(0), pl.BlockSpec((D, TM), lambda i: (0, i))],
        out_shape=[jax.ShapeDtypeStruct((T, D), F32), jax.ShapeDtypeStruct((T, D), BF16),
                   jax.ShapeDtypeStruct((D, T), BF16)],
        compiler_params=_cparams(("parallel",)),
        name="merge",
    )(x2d, ya, yb, proj, proj, wout, nf)


def _staircase(k):
    return [(ka, k // (ka + 1)) for ka in range(k)]


def _kth_largest(cur, k):
    m = None
    for _ in range(k):
        m = jnp.max(cur, axis=0, keepdims=True)
        cur = jnp.where(cur == m, NEG_INF, cur)
    return m


def _route_body(h2_ref, wq_ref, keys_ref, na_ref, ea_ref, rb_ref, eb_ref, sv_s, cand_s):
    NH = keys_ref.shape[0]
    NK, KH = keys_ref.shape[2], keys_ref.shape[3]
    K = PEER_TOPK
    q = _dot(h2_ref[...], wq_ref[...]).astype(BF16)
    cand_s[...] = jnp.full(cand_s.shape, NEG_INF, F32)
    for h in range(NH):
        rank, ex = [], []
        for s in range(2):
            off = (h * 2 + s) * KH
            sT = _dot_nt(keys_ref[h, s], q[:, off:off + KH])
            cur = sT
            rk = jnp.full(sT.shape, float(K), F32)
            for k in range(K):
                m = jnp.max(cur, axis=0, keepdims=True)
                sv_s[s, k:k + 1, :] = m
                eq = cur == m
                rk = jnp.where(eq, float(k), rk)
                cur = jnp.where(eq, NEG_INF, cur)
            rank.append(rk)
            ex.append(jnp.where(rk < K, jnp.exp(sT - sv_s[s, 0:1, :]), 0.0))
        off = 0
        for ka, nb in _staircase(K):
            cand_s[off:off + nb, :] = sv_s[0, ka:ka + 1, :] + sv_s[1, 0:nb, :]
            off += nb
        cand = cand_s[...]
        tau = _kth_largest(cand, K)
        top = sv_s[0, 0:1, :] + sv_s[1, 0:1, :]
        z = jnp.sum(jnp.where(cand >= tau, jnp.exp(cand - top), 0.0), axis=0, keepdims=True)
        na = jnp.zeros(rank[0].shape, F32)
        off = 0
        for ka, nb in _staircase(K):
            cnt = jnp.sum(jnp.where(cand_s[off:off + nb, :] >= tau, 1.0, 0.0), axis=0, keepdims=True)
            na = jnp.where(rank[0] == float(ka), cnt, na)
            off += nb
        na_ref[h * NK:(h + 1) * NK, :] = na
        ea_ref[h * NK:(h + 1) * NK, :] = ex[0] / z
        rb_ref[h] = rank[1].astype(BF16)
        eb_ref[h] = ex[1].astype(BF16)


def _route(h2, wq, keys):
    T, D = h2.shape
    NH, _, NK, KH = keys.shape
    TB = _pick(T, 512)
    ncand = sum(nb for _, nb in _staircase(PEER_TOPK))
    ncand_pad = -(-ncand // SUBLANES) * SUBLANES
    flat = pl.BlockSpec((NH * NK, TB), lambda i: (0, i))
    tok = pl.BlockSpec((NH, NK, TB), lambda i: (0, 0, i))
    return pl.pallas_call(
        _route_body,
        grid=(T // TB,),
        in_specs=[pl.BlockSpec((TB, D), lambda i: (i, 0)),
                  pl.BlockSpec(wq.shape, lambda i: (0, 0)),
                  pl.BlockSpec(keys.shape, lambda i: (0, 0, 0, 0))],
        out_specs=[flat, flat, tok, tok],
        out_shape=[jax.ShapeDtypeStruct((NH * NK, T), F32), jax.ShapeDtypeStruct((NH * NK, T), F32),
                   jax.ShapeDtypeStruct((NH, NK, T), BF16), jax.ShapeDtypeStruct((NH, NK, T), BF16)],
        scratch_shapes=[pltpu.VMEM((2, PEER_TOPK, TB), F32), pltpu.VMEM((ncand_pad, TB), F32)],
        compiler_params=_cparams(("parallel",)),
        name="peer_route",
    )(h2, wq, keys)


def _peer_body(h2t_ref, x1_ref, u_ref, vt_ref, na_ref, ea_ref, rb_ref, eb_ref, nfin_ref, y_ref,
               s_s, a_s, acc_s):
    ec = pl.program_id(1)
    NH, NK, TB = rb_ref.shape
    ni = u_ref.shape[0] // NK

    @pl.when(ec == 0)
    def _():
        acc_s[...] = jnp.zeros(acc_s.shape, F32)

    s_s[...] = _dot(u_ref[...], h2t_ref[...])

    def row_tile(grp, il):
        return jnp.broadcast_to(grp[il:il + 1, :], (BF16_ROWS, LANES)).astype(BF16)

    for tl in range(TB // LANES):
        ln = slice(tl * LANES, (tl + 1) * LANES)
        rows = [pl.ds(pl.multiple_of(h * NK + ec * ni, SUBLANES), ni) for h in range(NH)]
        na = [na_ref[rows[h], ln] for h in range(NH)]
        ea = [ea_ref[rows[h], ln] for h in range(NH)]
        for il in range(ni):
            na_t = [row_tile(na[h], il) for h in range(NH)]
            ea_t = [row_tile(ea[h], il) for h in range(NH)]
            for jt in range(NK // BF16_ROWS):
                js = slice(jt * BF16_ROWS, (jt + 1) * BF16_ROWS)
                w = jnp.zeros((BF16_ROWS, LANES), BF16)
                for h in range(NH):
                    w = w + ea_t[h] * jnp.where(rb_ref[h, js, ln] < na_t[h], eb_ref[h, js, ln], 0.0)
                e = slice(il * NK + jt * BF16_ROWS, il * NK + (jt + 1) * BF16_ROWS)
                a_s[e, ln] = _gelu(s_s[e, ln]).astype(BF16) * w
    acc_s[...] += _dot(vt_ref[...], a_s[...])

    @pl.when(ec == pl.num_programs(1) - 1)
    def _():
        x2 = x1_ref[...] + jnp.transpose(acc_s[...])
        y_ref[...] = _rms(x2, nfin_ref[...])


def _peer(h2t, x1, u, vt, na, ea, rb, eb, nfin):
    D, T = h2t.shape
    E = u.shape[0]
    NH, NK, _ = rb.shape
    TB = _pick(T, 512)
    EC = SUBLANES * NK
    assert TB % LANES == 0 and E % EC == 0 and NK % BF16_ROWS == 0
    tok = pl.BlockSpec((NH, NK, TB), lambda i, j: (0, 0, i))
    flat = pl.BlockSpec((NH * NK, TB), lambda i, j: (0, i))
    return pl.pallas_call(
        _peer_body,
        grid=(T // TB, E // EC),
        in_specs=[pl.BlockSpec((D, TB), lambda i, j: (0, i)),
                  pl.BlockSpec((TB, D), lambda i, j: (i, 0)),
                  pl.BlockSpec((EC, D), lambda i, j: (j, 0)),
                  pl.BlockSpec((D, EC), lambda i, j: (0, j)),
                  flat, flat, tok, tok,
                  pl.BlockSpec((1, D), lambda i, j: (0, 0))],
        out_specs=pl.BlockSpec((TB, D), lambda i, j: (i, 0)),
        out_shape=jax.ShapeDtypeStruct((T, D), F32),
        scratch_shapes=[pltpu.VMEM((EC, TB), F32), pltpu.VMEM((EC, TB), BF16), pltpu.VMEM((D, TB), F32)],
        compiler_params=_cparams(("parallel", "arbitrary")),
        name="peer_mix",
    )(h2t, x1, u, vt, na, ea, rb, eb, nfin)


def _pad_lanes(row):
    return jnp.pad(row, ((0, 0), (0, LANES - row.shape[1])))


def kernel(x_prompt, x_sample, state_rg_h, state_rg_conv, state_ssd_h, state_ssd_conv, norm_mix, w_in, rg_conv_w, rg_conv_b, rg_wa, rg_ba, rg_wx, rg_bx, rg_lam, w_rg_out, ssd_conv_w, ssd_conv_b, ssd_dt_bias, ssd_a_log, ssd_d, ssd_norm, w_ssd_out, w_out, norm_ffn, peer_wq, peer_keys, peer_u, peer_v, norm_final):
    depth = w_in.shape[0]
    B, L, D = x_prompt.shape
    Bs, Ls, _ = x_sample.shape
    assert depth == 1 and Ls == SAMPLE_ROWS // 2 and rg_conv_w.shape[1] == 4 and L >= 3
    W = rg_conv_w.shape[2]
    H = ssd_a_log.shape[1]
    inner = ssd_norm.shape[1]
    CD = ssd_conv_w.shape[2]
    P = inner // H
    N = state_ssd_h.shape[-1]
    G = (CD - inner) // (2 * N)
    R = H // G
    dims = (G, R, P, N)
    assert W == D and inner == 2 * D and CD == 3 * D and H % SUBLANES == 0 and H <= LANES
    o_x, o_g, o_z, o_xbc, o_dt, o_m = 0, W, 2 * W, 2 * W + inner, 2 * W + inner + CD, 2 * W + inner + CD + H
    col_xbc, col_x, col_z, col_g, col_ga, col_gb = 0, CD // W, (CD + W) // inner, (CD + W + inner) // W, \
        (CD + 2 * W + inner) // D, (CD + 2 * W + inner + D) // D

    yp = x_prompt.reshape(B * L, D)
    ys = x_sample.reshape(Bs * Ls, D)
    outs = {k: [] for k in ("prh", "prc", "psh", "psc", "srh", "src", "ssh", "ssc")}
    lead = SAMPLE_ROWS - Ls
    for l in range(depth):
        wi = w_in[l]
        w_main = jnp.concatenate([wi[:, o_xbc:o_dt], wi[:, o_x:o_g], wi[:, o_z:o_xbc], wi[:, o_g:o_z],
                                  wi[:, o_m:]], axis=1).astype(BF16)
        wdtT = wi[:, o_dt:o_m].T
        wdt = _pad_lanes(wi[:, o_dt:o_m])
        g_mix = norm_mix[l][None]
        rg_w = (rg_conv_w[l], rg_conv_b[l][None], rg_wa[l].astype(BF16), rg_wx[l].astype(BF16),
                rg_ba[l][None], rg_bx[l][None], rg_lam[l][None], w_rg_out[l].astype(BF16))
        ssd_w = (ssd_conv_w[l], ssd_conv_b[l][None], _pad_lanes(ssd_dt_bias[l][None]), ssd_dt_bias[l][:, None],
                 _pad_lanes(ssd_a_log[l][None]), ssd_a_log[l][:, None], jnp.repeat(ssd_d[l], P)[None],
                 ssd_norm[l][None], w_ssd_out[l].astype(BF16))
        wout = w_out[l].astype(BF16)
        nf = norm_ffn[l][None]
        wq = peer_wq[l].astype(BF16)
        keys = peer_keys[l].astype(BF16)
        u = peer_u[l].astype(BF16)
        vt = peer_v[l].astype(BF16).T
        nfin = norm_final[None]

        proj, dt, dtT = _inproj(yp, g_mix, w_main, wdt, wdtT)
        ya, rgh = _rg_prompt(proj, B, L, col_x, col_g, *rg_w)
        yb, ssh = _ssd_prompt(proj, dt, dtT, B, L, col_xbc, col_z, dims, *ssd_w)
        p3 = proj.reshape(B, L, -1)
        outs["prh"].append(rgh.reshape(B, W))
        outs["prc"].append(p3[:, L - 3:, CD:CD + W])
        outs["psh"].append(ssh)
        outs["psc"].append(p3[:, L - 3:, :CD])
        x1, h2, h2t = _merge(yp, ya, yb, proj, col_ga, col_gb, wout, nf)
        yp = _peer(h2t, x1, u, vt, *_route(h2, wq, keys), nfin)

        xe = jnp.pad(ys.reshape(Bs, Ls, D), ((0, 0), (lead, 0), (0, 0))).reshape(Bs * SAMPLE_ROWS, D)
        proj_e, dt_e, dtT_e = _inproj(xe, g_mix, w_main, wdt, wdtT)
        rows = lambda a3: a3.reshape(Bs * SAMPLE_ROWS, a3.shape[-1])
        rg_cpad = rows(jnp.pad(state_rg_conv[l], ((0, 0), (lead - 3, Ls), (0, 0))))
        rg_hpad = rows(jnp.pad(state_rg_h[l][:, None, :], ((0, 0), (lead - 1, Ls), (0, 0))))
        ssd_cpad = rows(jnp.pad(state_ssd_conv[l], ((0, 0), (lead - 3, Ls), (0, 0))))
        ya_e, h_e = _rg_sample(proj_e, col_x, col_g, rg_cpad, rg_hpad, *rg_w)
        yb_e, ssh_s = _ssd_sample(proj_e, dt_e, dtT_e, ssd_cpad, state_ssd_h[l], col_xbc, col_z, dims, *ssd_w)
        toks = lambda a2: a2.reshape(Bs, SAMPLE_ROWS, -1)[:, lead:].reshape(Bs * Ls, -1)
        p3 = proj_e.reshape(Bs, SAMPLE_ROWS, -1)
        outs["srh"].append(h_e.reshape(Bs, SAMPLE_ROWS, W)[:, -1])
        outs["src"].append(p3[:, SAMPLE_ROWS - 3:, CD:CD + W])
        outs["ssh"].append(ssh_s)
        outs["ssc"].append(p3[:, SAMPLE_ROWS - 3:, :CD])
        proj_s = toks(proj_e)
        x1, h2, h2t = _merge(ys, toks(ya_e), toks(yb_e), proj_s, col_ga, col_gb, wout, nf)
        ys = _peer(h2t, x1, u, vt, *_route(h2, wq, keys), nfin)

    st = lambda k: jnp.stack(outs[k])
    return (yp.reshape(B, L, D), ys.reshape(Bs, Ls, D), st("prh"), st("prc"), st("psh"), st("psc"),
            st("srh"), st("src"), st("ssh"), st("ssc"))
```

```python
import functools

import jax
import jax.numpy as jnp
from jax import lax
from jax.experimental import pallas as pl
from jax.experimental.pallas import tpu as pltpu

F32 = jnp.float32
BF16 = jnp.bfloat16
EPS = 1e-6
RG_C = 8.0
PEER_TOPK = 16
SSD_CHUNK = 128
LANES = 128
SUBLANES = 8
BF16_ROWS = 16
SAMPLE_ROWS = 8
NEG_INF = float("-inf")
HIGHEST = lax.Precision.HIGHEST
NT_DIMS = (((1,), (1,)), ((), ()))
VMEM_LIMIT = 56 * 1024 * 1024


def _cparams(sem):
    return pltpu.CompilerParams(dimension_semantics=sem, vmem_limit_bytes=VMEM_LIMIT)


def _pick(n, pref):
    t = min(n, pref)
    while n % t:
        t -= SUBLANES
    return t


def _sigmoid(x):
    return 1.0 / (1.0 + jnp.exp(-x))


def _silu(x):
    return x * _sigmoid(x)


def _softplus(x):
    return jnp.maximum(x, 0.0) + jnp.log1p(jnp.exp(-jnp.abs(x)))


def _gelu(x):
    k1 = -2.0 * 0.7978845608028654 * 1.4426950408889634
    return x / (1.0 + jnp.exp2(x * (k1 + (k1 * 0.044715) * (x * x))))


def _rms(x, g):
    return x * lax.rsqrt(jnp.mean(x * x, axis=-1, keepdims=True) + EPS) * g


def _dot(a, b):
    return jnp.dot(a, b, preferred_element_type=F32)


def _dot_nt(a, b):
    return lax.dot_general(a, b, NT_DIMS, preferred_element_type=F32)


def _inproj_body(x_ref, g_ref, w_ref, wdt_ref, o_ref, odt_ref, odtT_ref, xn_ref):
    @pl.when(pl.program_id(1) == 0)
    def _():
        xn = _rms(x_ref[...], g_ref[...])
        xn_ref[...] = xn.astype(BF16)
        dt = jnp.dot(xn, wdt_ref[...], precision=HIGHEST, preferred_element_type=F32)
        odt_ref[...] = dt
        odtT_ref[...] = jnp.transpose(dt)[:odtT_ref.shape[0], :]

    o_ref[...] = _dot(xn_ref[...], w_ref[...])


def _inproj(x2d, g, w_main, wdt, H):
    T, D = x2d.shape
    N = w_main.shape[1]
    TM = _pick(T, 1024)
    TN = 1536
    assert N % TN == 0 and TM % LANES == 0
    return pl.pallas_call(
        _inproj_body,
        grid=(T // TM, N // TN),
        in_specs=[
            pl.BlockSpec((TM, D), lambda i, j: (i, 0)),
            pl.BlockSpec((1, D), lambda i, j: (0, 0)),
            pl.BlockSpec((D, TN), lambda i, j: (0, j)),
            pl.BlockSpec((D, LANES), lambda i, j: (0, 0)),
        ],
        out_specs=[
            pl.BlockSpec((TM, TN), lambda i, j: (i, j)),
            pl.BlockSpec((TM, LANES), lambda i, j: (i, 0)),
            pl.BlockSpec((H, TM), lambda i, j: (0, i)),
        ],
        out_shape=[
            jax.ShapeDtypeStruct((T, N), F32),
            jax.ShapeDtypeStruct((T, LANES), F32),
            jax.ShapeDtypeStruct((H, T), F32),
        ],
        scratch_shapes=[pltpu.VMEM((TM, D), BF16)],
        compiler_params=_cparams(("parallel", "arbitrary")),
        name="inproj",
    )(x2d, g, w_main, wdt)


def _rg_gates(xc, wa_ref, wx_ref, ba, bx, lam):
    xb = xc.astype(BF16)
    nb, bw = wa_ref.shape[0], wa_ref.shape[1]
    rs, gs = [], []
    for k in range(nb):
        xk = xb[:, k * bw:(k + 1) * bw]
        rs.append(_dot(xk, wa_ref[k]))
        gs.append(_dot(xk, wx_ref[k]))
    r = _sigmoid(jnp.concatenate(rs, axis=1) + ba)
    i = _sigmoid(jnp.concatenate(gs, axis=1) + bx)
    log_a = -RG_C * r * _softplus(-lam)
    a = jnp.exp(log_a)
    em1 = jnp.tanh(log_a) * (a * a + 1.0)
    return a, jnp.sqrt(-em1) * (i * xc)


def _rg_prompt_body(x_ref, gate_ref, cw_ref, cb_ref, wa_ref, wx_ref, ba_ref, bx_ref, lam_ref, wo_ref,
                    ya_ref, hfin_ref, pad_s, a_s, b_s, h_s):
    B, Lc, W = x_ref.shape
    NS = W // LANES
    pitch = a_s.shape[1] // B

    @pl.when(pl.program_id(0) == 0)
    def _():
        pad_s[:, 0:SUBLANES, :] = jnp.zeros((B, SUBLANES, W), F32)
        h_s[...] = jnp.zeros(h_s.shape, F32)

    for b in range(B):
        pad_s[b, SUBLANES:SUBLANES + Lc, :] = x_ref[b]
        xc = (cb_ref[...]
              + cw_ref[3:4, :] * pad_s[b, 8:8 + Lc, :]
              + cw_ref[2:3, :] * pad_s[b, 7:7 + Lc, :]
              + cw_ref[1:2, :] * pad_s[b, 6:6 + Lc, :]
              + cw_ref[0:1, :] * pad_s[b, 5:5 + Lc, :])
        pad_s[b, 0:SUBLANES, :] = pad_s[b, Lc:Lc + SUBLANES, :]
        a, bt = _rg_gates(xc, wa_ref, wx_ref, ba_ref[...], bx_ref[...], lam_ref[...])
        for s in range(NS):
            a_s[s, b * pitch:b * pitch + Lc, :] = a[:, s * LANES:(s + 1) * LANES]
            b_s[s, b * pitch:b * pitch + Lc, :] = bt[:, s * LANES:(s + 1) * LANES]

    def step(t, hs):
        out = []
        for s in range(NS):
            rows = pl.ds(t, B, stride=pitch)
            h = a_s[s, rows, :] * hs[s] + b_s[s, rows, :]
            b_s[s, rows, :] = h
            out.append(h)
        return tuple(out)

    hs = lax.fori_loop(0, Lc, step, tuple(h_s[:, s * LANES:(s + 1) * LANES] for s in range(NS)), unroll=4)
    for s in range(NS):
        h_s[:, s * LANES:(s + 1) * LANES] = hs[s]
    hfin_ref[...] = h_s[...]
    for b in range(B):
        hb = jnp.concatenate([b_s[s, b * pitch:b * pitch + Lc, :] for s in range(NS)], axis=1)
        ya_ref[b] = _dot((hb * _gelu(gate_ref[b])).astype(BF16), wo_ref[...])


def _rg_prompt(proj, B, L, col_x, col_g, cw, cb, wa, wx, ba, bx, lam, wo):
    W = cw.shape[1]
    Lc = _pick(L, 128)
    pitch = Lc + SUBLANES // 2
    assert B == SUBLANES and W % LANES == 0
    proj3 = proj.reshape(B, L, proj.shape[1])
    full = lambda *s: pl.BlockSpec(s, lambda c: (0,) * len(s))
    ya, hfin = pl.pallas_call(
        _rg_prompt_body,
        grid=(L // Lc,),
        in_specs=[
            pl.BlockSpec((B, Lc, W), lambda c: (0, c, col_x)),
            pl.BlockSpec((B, Lc, W), lambda c: (0, c, col_g)),
            full(4, W), full(1, W), full(*wa.shape), full(*wx.shape), full(1, W), full(1, W), full(1, W),
            full(*wo.shape),
        ],
        out_specs=[
            pl.BlockSpec((B, Lc, wo.shape[1]), lambda c: (0, c, 0)),
            pl.BlockSpec((B, W), lambda c: (0, 0)),
        ],
        out_shape=[
            jax.ShapeDtypeStruct((B, L, wo.shape[1]), F32),
            jax.ShapeDtypeStruct((B, W), F32),
        ],
        scratch_shapes=[
            pltpu.VMEM((B, Lc + SUBLANES, W), F32),
            pltpu.VMEM((W // LANES, B * pitch, LANES), F32),
            pltpu.VMEM((W // LANES, B * pitch, LANES), F32),
            pltpu.VMEM((B, W), F32),
        ],
        compiler_params=_cparams(("arbitrary",)),
        name="rg_prompt",
    )(proj3, proj3, cw, cb, wa, wx, ba, bx, lam, wo)
    return ya.reshape(B * L, wo.shape[1]), hfin


def _conv_rolled(u, cw_ref, cb_ref):
    return (cb_ref[...]
            + cw_ref[3:4, :] * u
            + cw_ref[2:3, :] * pltpu.roll(u, 1, axis=0)
            + cw_ref[1:2, :] * pltpu.roll(u, 2, axis=0)
            + cw_ref[0:1, :] * pltpu.roll(u, 3, axis=0))


def _rg_sample_body(x_ref, gate_ref, cpad_ref, hpad_ref, cw_ref, cb_ref, wa_ref, wx_ref, ba_ref, bx_ref,
                    lam_ref, wo_ref, ya_ref, h_ref):
    xc = _conv_rolled(x_ref[...] + cpad_ref[...], cw_ref, cb_ref)
    a, bt = _rg_gates(xc, wa_ref, wx_ref, ba_ref[...], bx_ref[...], lam_ref[...])
    row = lax.broadcasted_iota(jnp.int32, a.shape, 0) & (SAMPLE_ROWS - 1)
    h = hpad_ref[...]
    for k in range(SAMPLE_ROWS // 2, SAMPLE_ROWS):
        h = jnp.where(row == k, a * pltpu.roll(h, 1, axis=0) + bt, h)
    h_ref[...] = h
    ya_ref[...] = _dot((h * _gelu(gate_ref[...])).astype(BF16), wo_ref[...])


def _rg_sample(proj, col_x, col_g, cpad, hpad, cw, cb, wa, wx, ba, bx, lam, wo):
    T = proj.shape[0]
    W = cw.shape[1]
    TM = _pick(T, 256)
    full = lambda *s: pl.BlockSpec(s, lambda i: (0,) * len(s))
    return pl.pallas_call(
        _rg_sample_body,
        grid=(T // TM,),
        in_specs=[
            pl.BlockSpec((TM, W), lambda i: (i, col_x)),
            pl.BlockSpec((TM, W), lambda i: (i, col_g)),
            pl.BlockSpec((TM, W), lambda i: (i, 0)),
            pl.BlockSpec((TM, W), lambda i: (i, 0)),
            full(4, W), full(1, W), full(*wa.shape), full(*wx.shape), full(1, W), full(1, W), full(1, W),
            full(*wo.shape),
        ],
        out_specs=[
            pl.BlockSpec((TM, wo.shape[1]), lambda i: (i, 0)),
            pl.BlockSpec((TM, W), lambda i: (i, 0)),
        ],
        out_shape=[
            jax.ShapeDtypeStruct((T, wo.shape[1]), F32),
            jax.ShapeDtypeStruct((T, W), F32),
        ],
        compiler_params=_cparams(("parallel",)),
        name="rg_sample",
    )(proj, proj, cpad, hpad, cw, cb, wa, wx, ba, bx, lam, wo)


def _ssd_chunk(act_s, dt, dtT, alog_ref, alogT_ref, dexp_ref, seq_shift, G, R, P, N,
               yint_fn, y_s, xw_s):
    Q = act_s.shape[0]
    inner = G * R * P
    a = dt * (-jnp.exp(alog_ref[...]))
    aT = dtT * (-jnp.exp(alogT_ref[...]))
    ri = lax.broadcasted_iota(jnp.int32, (Q, Q), 0)
    ci = lax.broadcasted_iota(jnp.int32, (Q, Q), 1)
    same = (ri >> seq_shift) == (ci >> seq_shift)
    causal = same & (ci <= ri)
    causal_f = jnp.where(causal, 1.0, 0.0).astype(F32)
    same_f = jnp.where(same, 1.0, 0.0).astype(F32)
    acum = jnp.dot(causal_f, a, precision=HIGHEST, preferred_element_type=F32)
    tot = jnp.dot(same_f, a, precision=HIGHEST, preferred_element_type=F32)
    acumT = lax.dot_general(aT, causal_f, NT_DIMS, precision=HIGHEST,
                            preferred_element_type=F32)
    eacum = jnp.exp(acum)
    toend = jnp.exp(tot - acum) * dt
    for g in range(G):
        Bg = act_s[:, inner + g * N:inner + (g + 1) * N].astype(BF16)
        Cg = act_s[:, inner + (G + g) * N:inner + (G + g + 1) * N].astype(BF16)
        cb = _dot_nt(Cg, Bg)
        yint = yint_fn(g, Cg)
        for r in range(R):
            h = g * R + r
            sl = slice(h * P, (h + 1) * P)
            seg = acum[:, h:h + 1] - acumT[h:h + 1, :]
            decay = jnp.exp(jnp.where(causal, seg, NEG_INF))
            wgt = (cb * decay * dtT[h:h + 1, :]).astype(BF16)
            xh = act_s[:, sl]
            yh = _dot(wgt, xh.astype(BF16)) + dexp_ref[:, sl] * xh
            if yint is not None:
                yh = yh + yint[:, r * P:(r + 1) * P] * eacum[:, h:h + 1]
            y_s[:, sl] = yh
            xw_s[:, sl] = xh * toend[:, h:h + 1]
    return tot, eacum


def _ssd_finish(y, z, nw_ref, wo_ref):
    y = y * _silu(z)
    return _dot(_rms(y, nw_ref[...]).astype(BF16), wo_ref[...])


def _ssd_prompt_body(G, R, P, N,
                     xbc_ref, z_ref, dt_ref, dtT_ref, cw_ref, cb_ref, dtb_ref, dtbT_ref, alog_ref, alogT_ref,
                     dexp_ref, nw_ref, wo_ref, yb_ref, st_ref, pad_s, act_s, y_s, xw_s):
    Q = xbc_ref.shape[0]

    @pl.when(pl.program_id(1) == 0)
    def _():
        pad_s[0:SUBLANES, :] = jnp.zeros((SUBLANES, pad_s.shape[1]), F32)
        st_ref[...] = jnp.zeros(st_ref.shape, F32)

    pad_s[SUBLANES:SUBLANES + Q, :] = xbc_ref[...]
    conv = (cb_ref[...]
            + cw_ref[3:4, :] * pad_s[8:8 + Q, :]
            + cw_ref[2:3, :] * pad_s[7:7 + Q, :]
            + cw_ref[1:2, :] * pad_s[6:6 + Q, :]
            + cw_ref[0:1, :] * pad_s[5:5 + Q, :])
    pad_s[0:SUBLANES, :] = pad_s[Q:Q + SUBLANES, :]
    act_s[...] = _silu(conv)
    dt = _softplus(dt_ref[...] + dtb_ref[...])
    dtT = _softplus(dtT_ref[...] + dtbT_ref[...])

    def yint_fn(g, Cg):
        stg = st_ref[0, g * R:(g + 1) * R].reshape(R * P, N)
        return _dot_nt(Cg, stg.astype(BF16))

    seq_shift = Q.bit_length() - 1
    tot, _ = _ssd_chunk(act_s, dt, dtT, alog_ref, alogT_ref, dexp_ref, seq_shift, G, R, P, N,
                        yint_fn, y_s, xw_s)
    dec = jnp.exp(tot[0:1, :])
    inner = G * R * P
    for g in range(G):
        Bg = act_s[:, inner + g * N:inner + (g + 1) * N].astype(BF16)
        xwT = jnp.transpose(xw_s[:, g * R * P:(g + 1) * R * P])
        S = _dot(xwT.astype(BF16), Bg)
        for r in range(R):
            h = g * R + r
            st_ref[0, h] = dec[:, h:h + 1] * st_ref[0, h] + S[r * P:(r + 1) * P, :]
    yb_ref[...] = _ssd_finish(y_s[...], z_ref[...], nw_ref, wo_ref)


def _ssd_prompt(proj, dt, dtT, B, L, col_xbc, col_z, dims, cw, cb, dtb, dtbT, alog, alogT, dexp, nw, wo):
    G, R, P, N = dims
    H = G * R
    inner = H * P
    CD = cw.shape[1]
    Q = _pick(L, SSD_CHUNK)
    nC = L // Q
    full = lambda *s: pl.BlockSpec(s, lambda b, c: (0,) * len(s))
    return pl.pallas_call(
        functools.partial(_ssd_prompt_body, G, R, P, N),
        grid=(B, nC),
        in_specs=[
            pl.BlockSpec((Q, CD), lambda b, c: (b * nC + c, col_xbc)),
            pl.BlockSpec((Q, inner), lambda b, c: (b * nC + c, col_z)),
            pl.BlockSpec((Q, LANES), lambda b, c: (b * nC + c, 0)),
            pl.BlockSpec((H, Q), lambda b, c: (0, b * nC + c)),
            full(4, CD), full(1, CD), full(1, LANES), full(H, 1), full(1, LANES), full(H, 1),
            full(1, inner), full(1, inner), full(*wo.shape),
        ],
        out_specs=[
            pl.BlockSpec((Q, wo.shape[1]), lambda b, c: (b * nC + c, 0)),
            pl.BlockSpec((1, H, P, N), lambda b, c: (b, 0, 0, 0)),
        ],
        out_shape=[
            jax.ShapeDtypeStruct((B * L, wo.shape[1]), F32),
            jax.ShapeDtypeStruct((B, H, P, N), F32),
        ],
        scratch_shapes=[
            pltpu.VMEM((Q + SUBLANES, CD), F32),
            pltpu.VMEM((Q, CD), F32),
            pltpu.VMEM((Q, inner), F32),
            pltpu.VMEM((Q, inner), F32),
        ],
        compiler_params=_cparams(("parallel", "arbitrary")),
        name="ssd_prompt",
    )(proj, proj, dt, dtT, cw, cb, dtb, dtbT, alog, alogT, dexp, nw, wo)


def _ssd_sample_body(G, R, P, N,
                     xbc_ref, cpad_ref, z_ref, dt_ref, dtT_ref, h0_ref, cw_ref, cb_ref, dtb_ref, dtbT_ref,
                     alog_ref, alogT_ref, dexp_ref, nw_ref, wo_ref, yb_ref, st_ref,
                     act_s, y_s, xw_s, yint_s, xwT_s, tot_s, eacum_s):
    Q = xbc_ref.shape[0]
    inner = G * R * P
    bi = pl.program_id(1)
    half = SAMPLE_ROWS // 2

    @pl.when(bi == 0)
    def _():
        act_s[...] = _silu(_conv_rolled(xbc_ref[...] + cpad_ref[...], cw_ref, cb_ref))
        rows = lax.broadcasted_iota(jnp.int32, (Q, LANES), 0) & (SAMPLE_ROWS - 1)
        cols = lax.broadcasted_iota(jnp.int32, dtT_ref.shape, 1) & (SAMPLE_ROWS - 1)
        dt = jnp.where(rows >= half, _softplus(dt_ref[...] + dtb_ref[...]), 0.0)
        dtT = jnp.where(cols >= half, _softplus(dtT_ref[...] + dtbT_ref[...]), 0.0)
        seq_shift = SAMPLE_ROWS.bit_length() - 1
        tot, eacum = _ssd_chunk(act_s, dt, dtT, alog_ref, alogT_ref, dexp_ref, seq_shift, G, R, P, N,
                                lambda g, Cg: None, y_s, xw_s)
        tot_s[...] = tot
        eacum_s[...] = eacum
        for g in range(G):
            xwT_s[g] = jnp.transpose(xw_s[:, g * R * P:(g + 1) * R * P]).astype(BF16)

    r0 = pl.multiple_of(bi * SAMPLE_ROWS, SAMPLE_ROWS)
    rid = lax.broadcasted_iota(jnp.int32, (Q, N), 0)
    mine = (rid >= r0) & (rid < r0 + SAMPLE_ROWS)
    dec = jnp.exp(tot_s[pl.ds(r0, 1), :])
    for g in range(G):
        Cb = act_s[pl.ds(r0, SAMPLE_ROWS), inner + (G + g) * N:inner + (G + g + 1) * N]
        stg = h0_ref[0, g * R:(g + 1) * R].reshape(R * P, N)
        yint_s[pl.ds(r0, SAMPLE_ROWS), g * R * P:(g + 1) * R * P] = _dot_nt(Cb, stg)
        Bg = jnp.where(mine, act_s[:, inner + g * N:inner + (g + 1) * N], 0.0).astype(BF16)
        S = _dot(xwT_s[g], Bg)
        for r in range(R):
            h = g * R + r
            st_ref[0, h] = dec[:, h:h + 1] * h0_ref[0, h] + S[r * P:(r + 1) * P, :]

    @pl.when(bi == pl.num_programs(1) - 1)
    def _():
        for h in range(G * R):
            sl = slice(h * P, (h + 1) * P)
            y_s[:, sl] = y_s[:, sl] + yint_s[:, sl] * eacum_s[:, h:h + 1]
        yb_ref[...] = _ssd_finish(y_s[...], z_ref[...], nw_ref, wo_ref)


def _ssd_sample(proj, dt, dtT, cpad, h0, col_xbc, col_z, dims, cw, cb, dtb, dtbT, alog, alogT, dexp, nw, wo):
    G, R, P, N = dims
    H = G * R
    inner = H * P
    CD = cw.shape[1]
    T = proj.shape[0]
    Bs = h0.shape[0]
    Q = _pick(T, SSD_CHUNK)
    nb = Q // SAMPLE_ROWS
    full = lambda *s: pl.BlockSpec(s, lambda i, j: (0,) * len(s))
    return pl.pallas_call(
        functools.partial(_ssd_sample_body, G, R, P, N),
        grid=(T // Q, nb),
        in_specs=[
            pl.BlockSpec((Q, CD), lambda i, j: (i, col_xbc)),
            pl.BlockSpec((Q, CD), lambda i, j: (i, 0)),
            pl.BlockSpec((Q, inner), lambda i, j: (i, col_z)),
            pl.BlockSpec((Q, LANES), lambda i, j: (i, 0)),
            pl.BlockSpec((H, Q), lambda i, j: (0, i)),
            pl.BlockSpec((1, H, P, N), lambda i, j: (i * nb + j, 0, 0, 0)),
            full(4, CD), full(1, CD), full(1, LANES), full(H, 1), full(1, LANES), full(H, 1),
            full(1, inner), full(1, inner), full(*wo.shape),
        ],
        out_specs=[
            pl.BlockSpec((Q, wo.shape[1]), lambda i, j: (i, 0)),
            pl.BlockSpec((1, H, P, N), lambda i, j: (i * nb + j, 0, 0, 0)),
        ],
        out_shape=[
            jax.ShapeDtypeStruct((T, wo.shape[1]), F32),
            jax.ShapeDtypeStruct((Bs, H, P, N), F32),
        ],
        scratch_shapes=[
            pltpu.VMEM((Q, CD), F32),
            pltpu.VMEM((Q, inner), F32),
            pltpu.VMEM((Q, inner), F32),
            pltpu.VMEM((Q, inner), F32),
            pltpu.VMEM((G, R * P, Q), BF16),
            pltpu.VMEM((Q, LANES), F32),
            pltpu.VMEM((Q, LANES), F32),
        ],
        compiler_params=_cparams(("parallel", "arbitrary")),
        name="ssd_sample",
    )(proj, cpad, proj, dt, dtT, h0, cw, cb, dtb, dtbT, alog, alogT, dexp, nw, wo)


def _merge_body(x_ref, ya_ref, yb_ref, ga_ref, gb_ref, wout_ref, nf_ref, x1_ref, h2_ref, h2t_ref):
    m = _sigmoid(ga_ref[...]) * ya_ref[...] + _sigmoid(gb_ref[...]) * yb_ref[...]
    x1 = x_ref[...] + _dot(m.astype(BF16), wout_ref[...])
    x1_ref[...] = x1
    h2 = _rms(x1, nf_ref[...])
    h2_ref[...] = h2.astype(BF16)
    h2t_ref[...] = jnp.transpose(h2).astype(BF16)


def _merge(x2d, ya, yb, proj, col_ga, col_gb, wout, nf):
    T, D = x2d.shape
    TM = _pick(T, 512)
    row = lambda c: pl.BlockSpec((TM, D), lambda i: (i, c))
    return pl.pallas_call(
        _merge_body,
        grid=(T // TM,),
        in_specs=[row(0), row(0), row(0), row(col_ga), row(col_gb),
                  pl.BlockSpec(wout.shape, lambda i: (0, 0)), pl.BlockSpec((1, D), lambda i: (0, 0))],
        out_specs=[row(0), row(0), pl.BlockSpec((D, TM), lambda i: (0, i))],
        out_shape=[jax.ShapeDtypeStruct((T, D), F32), jax.ShapeDtypeStruct((T, D), BF16),
                   jax.ShapeDtypeStruct((D, T), BF16)],
        compiler_params=_cparams(("parallel",)),
        name="merge",
    )(x2d, ya, yb, proj, proj, wout, nf)


def _staircase(k):
    return [(ka, k // (ka + 1)) for ka in range(k)]


def _kth_largest(cur, k):
    m = None
    for _ in range(k):
        m = jnp.max(cur, axis=0, keepdims=True)
        cur = jnp.where(cur == m, NEG_INF, cur)
    return m


def _route_body(h2_ref, wq_ref, keys_ref, na_ref, ea_ref, rb_ref, eb_ref, sv_s, cand_s):
    NH = keys_ref.shape[0]
    NK, KH = keys_ref.shape[2], keys_ref.shape[3]
    K = PEER_TOPK
    q = _dot(h2_ref[...], wq_ref[...]).astype(BF16)
    cand_s[...] = jnp.full(cand_s.shape, NEG_INF, F32)
    for h in range(NH):
        sc, ex = [], []
        rbs = []
        for s in range(2):
            off = (h * 2 + s) * KH
            sT = _dot_nt(keys_ref[h, s], q[:, off:off + KH])
            for tl in range(sT.shape[1] // LANES):
                ln = slice(tl * LANES, (tl + 1) * LANES)
                cur = sT[:, ln]
                rb = jnp.full(cur.shape, float(K), F32)
                for k in range(K):
                    m = jnp.max(cur, axis=0, keepdims=True)
                    sv_s[s, k:k + 1, ln] = m
                    eq = cur == m
                    if s == 1:
                        rb = jnp.where(eq, float(k), rb)
                    cur = jnp.where(eq, NEG_INF, cur)
                if s == 1:
                    rbs.append(rb)
            sc.append(sT)
            ex.append(jnp.where(sT >= sv_s[s, K - 1:K, :], jnp.exp(sT - sv_s[s, 0:1, :]), 0.0))
        rb = jnp.concatenate(rbs, axis=1)
        off = 0
        for ka, nb in _staircase(K):
            cand_s[off:off + nb, :] = sv_s[0, ka:ka + 1, :] + sv_s[1, 0:nb, :]
            off += nb
        cand = cand_s[...]
        tau = _kth_largest(cand, K)
        top = sv_s[0, 0:1, :] + sv_s[1, 0:1, :]
        z = jnp.sum(jnp.where(cand >= tau, jnp.exp(cand - top), 0.0), axis=0, keepdims=True)
        na = jnp.zeros(ex[0].shape, F32)
        off = 0
        for ka, nb in _staircase(K):
            cnt = jnp.sum(jnp.where(cand_s[off:off + nb, :] >= tau, 1.0, 0.0), axis=0, keepdims=True)
            na = jnp.where(sc[0] == sv_s[0, ka:ka + 1, :], cnt, na)
            off += nb
        na_ref[h * NK:(h + 1) * NK, :] = na
        ea_ref[h * NK:(h + 1) * NK, :] = ex[0] / z
        rb_ref[h] = rb.astype(BF16)
        eb_ref[h] = ex[1].astype(BF16)


def _route(h2, wq, keys):
    T, D = h2.shape
    NH, _, NK, KH = keys.shape
    TB = _pick(T, 512)
    ncand = sum(nb for _, nb in _staircase(PEER_TOPK))
    ncand_pad = -(-ncand // SUBLANES) * SUBLANES
    flat = pl.BlockSpec((NH * NK, TB), lambda i: (0, i))
    tok = pl.BlockSpec((NH, NK, TB), lambda i: (0, 0, i))
    return pl.pallas_call(
        _route_body,
        grid=(T // TB,),
        in_specs=[pl.BlockSpec((TB, D), lambda i: (i, 0)),
                  pl.BlockSpec(wq.shape, lambda i: (0, 0)),
                  pl.BlockSpec(keys.shape, lambda i: (0, 0, 0, 0))],
        out_specs=[flat, flat, tok, tok],
        out_shape=[jax.ShapeDtypeStruct((NH * NK, T), F32), jax.ShapeDtypeStruct((NH * NK, T), F32),
                   jax.ShapeDtypeStruct((NH, NK, T), BF16), jax.ShapeDtypeStruct((NH, NK, T), BF16)],
        scratch_shapes=[pltpu.VMEM((2, PEER_TOPK, TB), F32), pltpu.VMEM((ncand_pad, TB), F32)],
        compiler_params=_cparams(("parallel",)),
        name="peer_route",
    )(h2, wq, keys)


def _peer_body(h2t_ref, x1_ref, u_ref, vt_ref, na_ref, ea_ref, rb_ref, eb_ref, nfin_ref, y_ref,
               s_s, a_s, acc_s):
    ec = pl.program_id(1)
    NH, NK, TB = rb_ref.shape
    ni = u_ref.shape[0] // NK

    @pl.when(ec == 0)
    def _():
        acc_s[...] = jnp.zeros(acc_s.shape, F32)

    s_s[...] = _dot(u_ref[...], h2t_ref[...])

    def row_tile(grp, il):
        return jnp.broadcast_to(grp[il:il + 1, :], (BF16_ROWS, LANES)).astype(BF16)

    for tl in range(TB // LANES):
        ln = slice(tl * LANES, (tl + 1) * LANES)
        rows = [pl.ds(pl.multiple_of(h * NK + ec * ni, SUBLANES), ni) for h in range(NH)]
        na = [na_ref[rows[h], ln] for h in range(NH)]
        ea = [ea_ref[rows[h], ln] for h in range(NH)]
        for il in range(ni):
            na_t = [row_tile(na[h], il) for h in range(NH)]
            ea_t = [row_tile(ea[h], il) for h in range(NH)]
            for jt in range(NK // BF16_ROWS):
                js = slice(jt * BF16_ROWS, (jt + 1) * BF16_ROWS)
                w = jnp.zeros((BF16_ROWS, LANES), BF16)
                for h in range(NH):
                    w = w + ea_t[h] * jnp.where(rb_ref[h, js, ln] < na_t[h], eb_ref[h, js, ln], 0.0)
                e = slice(il * NK + jt * BF16_ROWS, il * NK + (jt + 1) * BF16_ROWS)
                a_s[e, ln] = _gelu(s_s[e, ln]).astype(BF16) * w
    acc_s[...] += _dot(vt_ref[...], a_s[...])

    @pl.when(ec == pl.num_programs(1) - 1)
    def _():
        x2 = x1_ref[...] + jnp.transpose(acc_s[...])
        y_ref[...] = _rms(x2, nfin_ref[...])


def _peer(h2t, x1, u, vt, na, ea, rb, eb, nfin):
    D, T = h2t.shape
    E = u.shape[0]
    NH, NK, _ = rb.shape
    TB = _pick(T, 512)
    EC = 2 * SUBLANES * NK
    assert TB % LANES == 0 and E % EC == 0 and NK % BF16_ROWS == 0
    tok = pl.BlockSpec((NH, NK, TB), lambda i, j: (0, 0, i))
    flat = pl.BlockSpec((NH * NK, TB), lambda i, j: (0, i))
    return pl.pallas_call(
        _peer_body,
        grid=(T // TB, E // EC),
        in_specs=[pl.BlockSpec((D, TB), lambda i, j: (0, i)),
                  pl.BlockSpec((TB, D), lambda i, j: (i, 0)),
                  pl.BlockSpec((EC, D), lambda i, j: (j, 0)),
                  pl.BlockSpec((D, EC), lambda i, j: (0, j)),
                  flat, flat, tok, tok,
                  pl.BlockSpec((1, D), lambda i, j: (0, 0))],
        out_specs=pl.BlockSpec((TB, D), lambda i, j: (i, 0)),
        out_shape=jax.ShapeDtypeStruct((T, D), F32),
        scratch_shapes=[pltpu.VMEM((EC, TB), F32), pltpu.VMEM((EC, TB), BF16), pltpu.VMEM((D, TB), F32)],
        compiler_params=_cparams(("parallel", "arbitrary")),
        name="peer_mix",
    )(h2t, x1, u, vt, na, ea, rb, eb, nfin)


def _pad_lanes(row):
    return jnp.pad(row, ((0, 0), (0, LANES - row.shape[1])))


def kernel(x_prompt, x_sample, state_rg_h, state_rg_conv, state_ssd_h, state_ssd_conv, norm_mix, w_in, rg_conv_w, rg_conv_b, rg_wa, rg_ba, rg_wx, rg_bx, rg_lam, w_rg_out, ssd_conv_w, ssd_conv_b, ssd_dt_bias, ssd_a_log, ssd_d, ssd_norm, w_ssd_out, w_out, norm_ffn, peer_wq, peer_keys, peer_u, peer_v, norm_final):
    depth = w_in.shape[0]
    B, L, D = x_prompt.shape
    Bs, Ls, _ = x_sample.shape
    assert depth == 1 and Ls == SAMPLE_ROWS // 2 and rg_conv_w.shape[1] == 4 and L >= 3
    W = rg_conv_w.shape[2]
    H = ssd_a_log.shape[1]
    inner = ssd_norm.shape[1]
    CD = ssd_conv_w.shape[2]
    P = inner // H
    N = state_ssd_h.shape[-1]
    G = (CD - inner) // (2 * N)
    R = H // G
    dims = (G, R, P, N)
    assert W == D and inner == 2 * D and CD == 3 * D and H % SUBLANES == 0 and H <= LANES
    o_x, o_g, o_z, o_xbc, o_dt, o_m = 0, W, 2 * W, 2 * W + inner, 2 * W + inner + CD, 2 * W + inner + CD + H
    col_xbc, col_x, col_z, col_g, col_ga, col_gb = 0, CD // W, (CD + W) // inner, (CD + W + inner) // W, \
        (CD + 2 * W + inner) // D, (CD + 2 * W + inner + D) // D

    yp = x_prompt.reshape(B * L, D)
    ys = x_sample.reshape(Bs * Ls, D)
    outs = {k: [] for k in ("prh", "prc", "psh", "psc", "srh", "src", "ssh", "ssc")}
    lead = SAMPLE_ROWS - Ls
    for l in range(depth):
        wi = w_in[l]
        w_main = jnp.concatenate([wi[:, o_xbc:o_dt], wi[:, o_x:o_g], wi[:, o_z:o_xbc], wi[:, o_g:o_z],
                                  wi[:, o_m:]], axis=1).astype(BF16)
        wdt = _pad_lanes(wi[:, o_dt:o_m])
        g_mix = norm_mix[l][None]
        rg_w = (rg_conv_w[l], rg_conv_b[l][None], rg_wa[l].astype(BF16), rg_wx[l].astype(BF16),
                rg_ba[l][None], rg_bx[l][None], rg_lam[l][None], w_rg_out[l].astype(BF16))
        ssd_w = (ssd_conv_w[l], ssd_conv_b[l][None], _pad_lanes(ssd_dt_bias[l][None]), ssd_dt_bias[l][:, None],
                 _pad_lanes(ssd_a_log[l][None]), ssd_a_log[l][:, None], jnp.repeat(ssd_d[l], P)[None],
                 ssd_norm[l][None], w_ssd_out[l].astype(BF16))
        wout = w_out[l].astype(BF16)
        nf = norm_ffn[l][None]
        wq = peer_wq[l].astype(BF16)
        keys = peer_keys[l].astype(BF16)
        u = peer_u[l].astype(BF16)
        vt = peer_v[l].astype(BF16).T
        nfin = norm_final[None]

        proj, dt, dtT = _inproj(yp, g_mix, w_main, wdt, H)
        ya, rgh = _rg_prompt(proj, B, L, col_x, col_g, *rg_w)
        yb, ssh = _ssd_prompt(proj, dt, dtT, B, L, col_xbc, col_z, dims, *ssd_w)
        p3 = proj.reshape(B, L, -1)
        outs["prh"].append(rgh.reshape(B, W))
        outs["prc"].append(p3[:, L - 3:, CD:CD + W])
        outs["psh"].append(ssh)
        outs["psc"].append(p3[:, L - 3:, :CD])
        x1, h2, h2t = _merge(yp, ya, yb, proj, col_ga, col_gb, wout, nf)
        yp = _peer(h2t, x1, u, vt, *_route(h2, wq, keys), nfin)

        xe = jnp.pad(ys.reshape(Bs, Ls, D), ((0, 0), (lead, 0), (0, 0))).reshape(Bs * SAMPLE_ROWS, D)
        proj_e, dt_e, dtT_e = _inproj(xe, g_mix, w_main, wdt, H)
        rows = lambda a3: a3.reshape(Bs * SAMPLE_ROWS, a3.shape[-1])
        rg_cpad = rows(jnp.pad(state_rg_conv[l], ((0, 0), (lead - 3, Ls), (0, 0))))
        rg_hpad = rows(jnp.pad(state_rg_h[l][:, None, :], ((0, 0), (lead - 1, Ls), (0, 0))))
        ssd_cpad = rows(jnp.pad(state_ssd_conv[l], ((0, 0), (lead - 3, Ls), (0, 0))))
        ya_e, h_e = _rg_sample(proj_e, col_x, col_g, rg_cpad, rg_hpad, *rg_w)
        yb_e, ssh_s = _ssd_sample(proj_e, dt_e, dtT_e, ssd_cpad, state_ssd_h[l], col_xbc, col_z, dims, *ssd_w)
        toks = lambda a2: a2.reshape(Bs, SAMPLE_ROWS, -1)[:, lead:].reshape(Bs * Ls, -1)
        p3 = proj_e.reshape(Bs, SAMPLE_ROWS, -1)
        outs["srh"].append(h_e.reshape(Bs, SAMPLE_ROWS, W)[:, -1])
        outs["src"].append(p3[:, SAMPLE_ROWS - 3:, CD:CD + W])
        outs["ssh"].append(ssh_s)
        outs["ssc"].append(p3[:, SAMPLE_ROWS - 3:, :CD])
        proj_s = toks(proj_e)
        x1, h2, h2t = _merge(ys, toks(ya_e), toks(yb_e), proj_s, col_ga, col_gb, wout, nf)
        ys = _peer(h2t, x1, u, vt, *_route(h2, wq, keys), nfin)

    st = lambda k: jnp.stack(outs[k])
    return (yp.reshape(B, L, D), ys.reshape(Bs, Ls, D), st("prh"), st("prc"), st("psh"), st("psc"),
            st("srh"), st("src"), st("ssh"), st("ssc"))
```

```python
import functools

import jax
import jax.numpy as jnp
from jax import lax
from jax.experimental import pallas as pl
from jax.experimental.pallas import tpu as pltpu

F32 = jnp.float32
BF16 = jnp.bfloat16
EPS = 1e-6
RG_C = 8.0
PEER_TOPK = 16
SSD_CHUNK = 128
LANES = 128
SUBLANES = 8
BF16_ROWS = 16
SAMPLE_ROWS = 8
NEG_INF = float("-inf")
LOG2E = 1.4426950408889634
HIGHEST = lax.Precision.HIGHEST
NT_DIMS = (((1,), (1,)), ((), ()))
VMEM_LIMIT = 56 * 1024 * 1024


def _cparams(sem):
    return pltpu.CompilerParams(dimension_semantics=sem, vmem_limit_bytes=VMEM_LIMIT)


def _pick(n, pref):
    t = min(n, pref)
    while n % t:
        t -= SUBLANES
    return t


def _sigmoid(x):
    return 1.0 / (1.0 + jnp.exp2(x * -LOG2E))


def _silu(x):
    return x * _sigmoid(x)


def _softplus(x):
    return jnp.maximum(x, 0.0) + jnp.log1p(jnp.exp(-jnp.abs(x)))


def _gelu(x):
    k1 = -2.0 * 0.7978845608028654 * LOG2E
    return x / (1.0 + jnp.exp2(x * (k1 + (k1 * 0.044715) * (x * x))))


def _rms(x, g):
    return x * lax.rsqrt(jnp.mean(x * x, axis=-1, keepdims=True) + EPS) * g


def _dot(a, b):
    return jnp.dot(a, b, preferred_element_type=F32)


def _dot_nt(a, b):
    return lax.dot_general(a, b, NT_DIMS, preferred_element_type=F32)


def _inproj_body(x_ref, g_ref, w_ref, wdt_ref, o_ref, odt_ref, odtT_ref, xn_ref):
    @pl.when(pl.program_id(1) == 0)
    def _():
        xn = _rms(x_ref[...], g_ref[...])
        xn_ref[...] = xn.astype(BF16)
        dt = jnp.dot(xn, wdt_ref[...], precision=HIGHEST, preferred_element_type=F32)
        odt_ref[...] = dt
        odtT_ref[...] = jnp.transpose(dt)[:odtT_ref.shape[0], :]

    o_ref[...] = _dot(xn_ref[...], w_ref[...])


def _inproj(x2d, g, w_main, wdt, H):
    T, D = x2d.shape
    N = w_main.shape[1]
    TM = _pick(T, 1024)
    TN = 1536
    assert N % TN == 0 and TM % LANES == 0
    return pl.pallas_call(
        _inproj_body,
        grid=(T // TM, N // TN),
        in_specs=[
            pl.BlockSpec((TM, D), lambda i, j: (i, 0)),
            pl.BlockSpec((1, D), lambda i, j: (0, 0)),
            pl.BlockSpec((D, TN), lambda i, j: (0, j)),
            pl.BlockSpec((D, LANES), lambda i, j: (0, 0)),
        ],
        out_specs=[
            pl.BlockSpec((TM, TN), lambda i, j: (i, j)),
            pl.BlockSpec((TM, LANES), lambda i, j: (i, 0)),
            pl.BlockSpec((H, TM), lambda i, j: (0, i)),
        ],
        out_shape=[
            jax.ShapeDtypeStruct((T, N), F32),
            jax.ShapeDtypeStruct((T, LANES), F32),
            jax.ShapeDtypeStruct((H, T), F32),
        ],
        scratch_shapes=[pltpu.VMEM((TM, D), BF16)],
        compiler_params=_cparams(("parallel", "arbitrary")),
        name="inproj",
    )(x2d, g, w_main, wdt)


def _conv_slab(x, pad, cw_ref, cb_ref, ls):
    L = x.shape[0]
    pad[SUBLANES:SUBLANES + L, :] = x
    out = (cb_ref[:, ls]
           + cw_ref[3:4, ls] * pad[8:8 + L, :]
           + cw_ref[2:3, ls] * pad[7:7 + L, :]
           + cw_ref[1:2, ls] * pad[6:6 + L, :]
           + cw_ref[0:1, ls] * pad[5:5 + L, :])
    pad[0:SUBLANES, :] = pad[L:L + SUBLANES, :]
    return out


def _rg_gates(xc, wa_ref, wx_ref, ba, bx, lam):
    xb = xc.astype(BF16)
    nb, bw = wa_ref.shape[0], wa_ref.shape[1]
    rs, gs = [], []
    for k in range(nb):
        xk = xb[:, k * bw:(k + 1) * bw]
        rs.append(_dot(xk, wa_ref[k]))
        gs.append(_dot(xk, wx_ref[k]))
    r = _sigmoid(jnp.concatenate(rs, axis=1) + ba)
    i = _sigmoid(jnp.concatenate(gs, axis=1) + bx)
    log_a = -RG_C * r * _softplus(-lam)
    a = jnp.exp(log_a)
    em1 = jnp.tanh(log_a) * (a * a + 1.0)
    return a, jnp.sqrt(-em1) * (i * xc)


def _rg_prompt_body(x_ref, gate_ref, cw_ref, cb_ref, wa_ref, wx_ref, ba_ref, bx_ref, lam_ref, wo_ref,
                    ya_ref, hfin_ref, pad_s, a_s, b_s, h_s):
    B, Lc, W = x_ref.shape
    NS = W // LANES
    pitch = a_s.shape[1] // B

    @pl.when(pl.program_id(0) == 0)
    def _():
        pad_s[:, :, 0:SUBLANES, :] = jnp.zeros((B, NS, SUBLANES, LANES), F32)
        h_s[...] = jnp.zeros(h_s.shape, F32)

    for b in range(B):
        xc = jnp.concatenate(
            [_conv_slab(x_ref[b, :, s * LANES:(s + 1) * LANES], pad_s.at[b, s], cw_ref, cb_ref,
                        slice(s * LANES, (s + 1) * LANES)) for s in range(NS)], axis=1)
        a, bt = _rg_gates(xc, wa_ref, wx_ref, ba_ref[...], bx_ref[...], lam_ref[...])
        for s in range(NS):
            a_s[s, b * pitch:b * pitch + Lc, :] = a[:, s * LANES:(s + 1) * LANES]
            b_s[s, b * pitch:b * pitch + Lc, :] = bt[:, s * LANES:(s + 1) * LANES]

    def step(t, hs):
        out = []
        for s in range(NS):
            rows = pl.ds(t, B, stride=pitch)
            h = a_s[s, rows, :] * hs[s] + b_s[s, rows, :]
            b_s[s, rows, :] = h
            out.append(h)
        return tuple(out)

    hs = lax.fori_loop(0, Lc, step, tuple(h_s[:, s * LANES:(s + 1) * LANES] for s in range(NS)), unroll=4)
    for s in range(NS):
        h_s[:, s * LANES:(s + 1) * LANES] = hs[s]
    hfin_ref[...] = h_s[...]
    for b in range(B):
        hb = jnp.concatenate([b_s[s, b * pitch:b * pitch + Lc, :] for s in range(NS)], axis=1)
        ya_ref[b] = _dot((hb * _gelu(gate_ref[b])).astype(BF16), wo_ref[...])


def _rg_prompt(proj, B, L, col_x, col_g, cw, cb, wa, wx, ba, bx, lam, wo):
    W = cw.shape[1]
    Lc = _pick(L, 128)
    pitch = Lc + SUBLANES // 2
    assert B == SUBLANES and W % LANES == 0
    proj3 = proj.reshape(B, L, proj.shape[1])
    full = lambda *s: pl.BlockSpec(s, lambda c: (0,) * len(s))
    ya, hfin = pl.pallas_call(
        _rg_prompt_body,
        grid=(L // Lc,),
        in_specs=[
            pl.BlockSpec((B, Lc, W), lambda c: (0, c, col_x)),
            pl.BlockSpec((B, Lc, W), lambda c: (0, c, col_g)),
            full(4, W), full(1, W), full(*wa.shape), full(*wx.shape), full(1, W), full(1, W), full(1, W),
            full(*wo.shape),
        ],
        out_specs=[
            pl.BlockSpec((B, Lc, wo.shape[1]), lambda c: (0, c, 0)),
            pl.BlockSpec((B, W), lambda c: (0, 0)),
        ],
        out_shape=[
            jax.ShapeDtypeStruct((B, L, wo.shape[1]), F32),
            jax.ShapeDtypeStruct((B, W), F32),
        ],
        scratch_shapes=[
            pltpu.VMEM((B, W // LANES, Lc + SUBLANES, LANES), F32),
            pltpu.VMEM((W // LANES, B * pitch, LANES), F32),
            pltpu.VMEM((W // LANES, B * pitch, LANES), F32),
            pltpu.VMEM((B, W), F32),
        ],
        compiler_params=_cparams(("arbitrary",)),
        name="rg_prompt",
    )(proj3, proj3, cw, cb, wa, wx, ba, bx, lam, wo)
    return ya.reshape(B * L, wo.shape[1]), hfin


def _conv_rolled(u, cw_ref, cb_ref):
    return (cb_ref[...]
            + cw_ref[3:4, :] * u
            + cw_ref[2:3, :] * pltpu.roll(u, 1, axis=0)
            + cw_ref[1:2, :] * pltpu.roll(u, 2, axis=0)
            + cw_ref[0:1, :] * pltpu.roll(u, 3, axis=0))


def _rg_sample_body(x_ref, gate_ref, cpad_ref, hpad_ref, cw_ref, cb_ref, wa_ref, wx_ref, ba_ref, bx_ref,
                    lam_ref, wo_ref, ya_ref, h_ref):
    xc = _conv_rolled(x_ref[...] + cpad_ref[...], cw_ref, cb_ref)
    a, bt = _rg_gates(xc, wa_ref, wx_ref, ba_ref[...], bx_ref[...], lam_ref[...])
    row = lax.broadcasted_iota(jnp.int32, a.shape, 0) & (SAMPLE_ROWS - 1)
    h = hpad_ref[...]
    for k in range(SAMPLE_ROWS // 2, SAMPLE_ROWS):
        h = jnp.where(row == k, a * pltpu.roll(h, 1, axis=0) + bt, h)
    h_ref[...] = h
    ya_ref[...] = _dot((h * _gelu(gate_ref[...])).astype(BF16), wo_ref[...])


def _rg_sample(proj, col_x, col_g, cpad, hpad, cw, cb, wa, wx, ba, bx, lam, wo):
    T = proj.shape[0]
    W = cw.shape[1]
    TM = _pick(T, 256)
    full = lambda *s: pl.BlockSpec(s, lambda i: (0,) * len(s))
    return pl.pallas_call(
        _rg_sample_body,
        grid=(T // TM,),
        in_specs=[
            pl.BlockSpec((TM, W), lambda i: (i, col_x)),
            pl.BlockSpec((TM, W), lambda i: (i, col_g)),
            pl.BlockSpec((TM, W), lambda i: (i, 0)),
            pl.BlockSpec((TM, W), lambda i: (i, 0)),
            full(4, W), full(1, W), full(*wa.shape), full(*wx.shape), full(1, W), full(1, W), full(1, W),
            full(*wo.shape),
        ],
        out_specs=[
            pl.BlockSpec((TM, wo.shape[1]), lambda i: (i, 0)),
            pl.BlockSpec((TM, W), lambda i: (i, 0)),
        ],
        out_shape=[
            jax.ShapeDtypeStruct((T, wo.shape[1]), F32),
            jax.ShapeDtypeStruct((T, W), F32),
        ],
        compiler_params=_cparams(("parallel",)),
        name="rg_sample",
    )(proj, proj, cpad, hpad, cw, cb, wa, wx, ba, bx, lam, wo)


def _ssd_chunk(act_s, dt, dtT, alog_ref, alogT_ref, dexp_ref, seq_shift, G, R, P, N,
               yint_fn, y_s, xw_s):
    Q = act_s.shape[0]
    inner = G * R * P
    a = dt * (-jnp.exp(alog_ref[...]))
    aT = dtT * (-jnp.exp(alogT_ref[...]))
    ri = lax.broadcasted_iota(jnp.int32, (Q, Q), 0)
    ci = lax.broadcasted_iota(jnp.int32, (Q, Q), 1)
    same = (ri >> seq_shift) == (ci >> seq_shift)
    causal = same & (ci <= ri)
    causal_f = jnp.where(causal, 1.0, 0.0).astype(F32)
    same_f = jnp.where(same, 1.0, 0.0).astype(F32)
    acum = jnp.dot(causal_f, a, precision=HIGHEST, preferred_element_type=F32)
    tot = jnp.dot(same_f, a, precision=HIGHEST, preferred_element_type=F32)
    acumT = lax.dot_general(aT, causal_f, NT_DIMS, precision=HIGHEST,
                            preferred_element_type=F32)
    eacum = jnp.exp(acum)
    toend = jnp.exp(tot - acum) * dt
    for g in range(G):
        Bg = act_s[:, inner + g * N:inner + (g + 1) * N].astype(BF16)
        Cg = act_s[:, inner + (G + g) * N:inner + (G + g + 1) * N].astype(BF16)
        cb = _dot_nt(Cg, Bg)
        yint = yint_fn(g, Cg)
        for r in range(R):
            h = g * R + r
            sl = slice(h * P, (h + 1) * P)
            seg = acum[:, h:h + 1] - acumT[h:h + 1, :]
            decay = jnp.exp(jnp.where(causal, seg, NEG_INF))
            wgt = (cb * decay * dtT[h:h + 1, :]).astype(BF16)
            xh = act_s[:, sl]
            yh = _dot(wgt, xh.astype(BF16)) + dexp_ref[:, sl] * xh
            if yint is not None:
                yh = yh + yint[:, r * P:(r + 1) * P] * eacum[:, h:h + 1]
            y_s[:, sl] = yh
            xw_s[:, sl] = xh * toend[:, h:h + 1]
    return tot, eacum


def _expand_heads(x, ex_ref):
    hi = x.astype(BF16)
    r1 = x - hi.astype(F32)
    mid = r1.astype(BF16)
    lo = (r1 - mid.astype(F32)).astype(BF16)
    e = ex_ref[...]
    return _dot(hi, e) + _dot(mid, e) + _dot(lo, e)


def _ssd_finish(y, z, nw_ref, wo_ref):
    y = y * _silu(z)
    return _dot(_rms(y, nw_ref[...]).astype(BF16), wo_ref[...])


def _ssd_prompt_body(G, R, P, N,
                     xbc_ref, z_ref, dt_ref, dtT_ref, cw_ref, cb_ref, dtb_ref, dtbT_ref, alog_ref, alogT_ref,
                     dexp_ref, nw_ref, wo_ref, ex_ref, yb_ref, st_ref, pad_s, act_s, y_s, stT_s):
    Q = xbc_ref.shape[0]
    inner = G * R * P
    GW = R * P

    @pl.when(pl.program_id(1) == 0)
    def _():
        pad_s[:, 0:SUBLANES, :] = jnp.zeros((pad_s.shape[0], SUBLANES, LANES), F32)
        stT_s[...] = jnp.zeros(stT_s.shape, F32)

    for s in range(pad_s.shape[0]):
        ls = slice(s * LANES, (s + 1) * LANES)
        act_s[:, ls] = _silu(_conv_slab(xbc_ref[:, ls], pad_s.at[s], cw_ref, cb_ref, ls))
    dt = _softplus(dt_ref[...] + dtb_ref[...])
    dtT = _softplus(dtT_ref[...] + dtbT_ref[...])
    a = dt * (-jnp.exp(alog_ref[...]))
    aT = dtT * (-jnp.exp(alogT_ref[...]))
    ri = lax.broadcasted_iota(jnp.int32, (Q, Q), 0)
    ci = lax.broadcasted_iota(jnp.int32, (Q, Q), 1)
    causal = ci <= ri
    causal_f = jnp.where(causal, 1.0, 0.0).astype(F32)
    hi = dict(precision=HIGHEST, preferred_element_type=F32)
    acum = jnp.dot(causal_f, a, **hi)
    tot = jnp.dot(jnp.ones((SUBLANES, Q), F32), a, **hi)
    acumT = lax.dot_general(aT, causal_f, NT_DIMS, **hi)
    fx = _expand_heads(jnp.concatenate([jnp.exp(acum), jnp.exp(tot[0:1, :] - acum) * dt, jnp.exp(tot)], axis=0),
                       ex_ref)
    eacum_x, toend_x, dec_x = fx[0:Q], fx[Q:2 * Q], fx[2 * Q:2 * Q + 1]
    col2 = acum * LOG2E
    row2 = (acumT - jnp.log(dtT)) * LOG2E
    lane = lax.broadcasted_iota(jnp.int32, (Q, LANES), 1)
    for g in range(G):
        gs = slice(g * GW, (g + 1) * GW)
        Bf = act_s[:, inner + g * N:inner + (g + 1) * N]
        Bg = Bf.astype(BF16)
        Cg = act_s[:, inner + (G + g) * N:inner + (G + g + 1) * N].astype(BF16)
        cb = jnp.where(causal, _dot_nt(Cg, Bg), 0.0)
        y_s[:, gs] = (_dot(Cg, stT_s[:, gs].astype(BF16)) * eacum_x[:, gs]
                      + dexp_ref[:, gs] * act_s[:, gs])
        for pr in range(R // 2):
            h0 = g * R + 2 * pr
            ps = slice(h0 * P, (h0 + 2) * P)
            wg = []
            for h in (h0, h0 + 1):
                e = jnp.exp2(jnp.where(causal, col2[:, h:h + 1] - row2[h:h + 1, :], NEG_INF))
                wg.append((cb * e).astype(BF16))
            xp = act_s[:, ps]
            xa = jnp.where(lane < P, xp, 0.0).astype(BF16)
            xb = jnp.where(lane < P, 0.0, xp).astype(BF16)
            y_s[:, ps] = y_s[:, ps] + _dot(jnp.concatenate(wg, axis=1), jnp.concatenate([xa, xb], axis=0))
        xw = (act_s[:, gs] * toend_x[:, gs]).astype(BF16)
        ST = _dot(jnp.transpose(Bf).astype(BF16), xw)
        stT_s[:, gs] = dec_x[:, gs] * stT_s[:, gs] + ST
    yb_ref[...] = _ssd_finish(y_s[...], z_ref[...], nw_ref, wo_ref)

    @pl.when(pl.program_id(1) == pl.num_programs(1) - 1)
    def _():
        for g in range(G):
            st_ref[0, g * R:(g + 1) * R] = jnp.transpose(stT_s[:, g * GW:(g + 1) * GW]).reshape(R, P, N)


def _ssd_prompt(proj, dt, dtT, B, L, col_xbc, col_z, dims, cw, cb, dtb, dtbT, alog, alogT, dexp, nw, wo):
    G, R, P, N = dims
    H = G * R
    inner = H * P
    CD = cw.shape[1]
    Q = _pick(L, SSD_CHUNK)
    nC = L // Q
    assert 2 * P == LANES and R % 2 == 0 and N == LANES
    head_of_lane = jnp.arange(inner, dtype=jnp.int32) // P
    ex = (jnp.arange(LANES, dtype=jnp.int32)[:, None] == head_of_lane[None, :]).astype(BF16)
    full = lambda *s: pl.BlockSpec(s, lambda b, c: (0,) * len(s))
    return pl.pallas_call(
        functools.partial(_ssd_prompt_body, G, R, P, N),
        grid=(B, nC),
        in_specs=[
            pl.BlockSpec((Q, CD), lambda b, c: (b * nC + c, col_xbc)),
            pl.BlockSpec((Q, inner), lambda b, c: (b * nC + c, col_z)),
            pl.BlockSpec((Q, LANES), lambda b, c: (b * nC + c, 0)),
            pl.BlockSpec((H, Q), lambda b, c: (0, b * nC + c)),
            full(4, CD), full(1, CD), full(1, LANES), full(H, 1), full(1, LANES), full(H, 1),
            full(1, inner), full(1, inner), full(*wo.shape), full(LANES, inner),
        ],
        out_specs=[
            pl.BlockSpec((Q, wo.shape[1]), lambda b, c: (b * nC + c, 0)),
            pl.BlockSpec((1, H, P, N), lambda b, c: (b, 0, 0, 0)),
        ],
        out_shape=[
            jax.ShapeDtypeStruct((B * L, wo.shape[1]), F32),
            jax.ShapeDtypeStruct((B, H, P, N), F32),
        ],
        scratch_shapes=[
            pltpu.VMEM((CD // LANES, Q + SUBLANES, LANES), F32),
            pltpu.VMEM((Q, CD), F32),
            pltpu.VMEM((Q, inner), F32),
            pltpu.VMEM((N, inner), F32),
        ],
        compiler_params=_cparams(("parallel", "arbitrary")),
        name="ssd_prompt",
    )(proj, proj, dt, dtT, cw, cb, dtb, dtbT, alog, alogT, dexp, nw, wo, ex)


def _ssd_sample_body(G, R, P, N,
                     xbc_ref, cpad_ref, z_ref, dt_ref, dtT_ref, h0_ref, cw_ref, cb_ref, dtb_ref, dtbT_ref,
                     alog_ref, alogT_ref, dexp_ref, nw_ref, wo_ref, yb_ref, st_ref,
                     act_s, y_s, xw_s, yint_s, xwT_s, tot_s, eacum_s):
    Q = xbc_ref.shape[0]
    inner = G * R * P
    bi = pl.program_id(1)
    half = SAMPLE_ROWS // 2

    @pl.when(bi == 0)
    def _():
        act_s[...] = _silu(_conv_rolled(xbc_ref[...] + cpad_ref[...], cw_ref, cb_ref))
        rows = lax.broadcasted_iota(jnp.int32, (Q, LANES), 0) & (SAMPLE_ROWS - 1)
        cols = lax.broadcasted_iota(jnp.int32, dtT_ref.shape, 1) & (SAMPLE_ROWS - 1)
        dt = jnp.where(rows >= half, _softplus(dt_ref[...] + dtb_ref[...]), 0.0)
        dtT = jnp.where(cols >= half, _softplus(dtT_ref[...] + dtbT_ref[...]), 0.0)
        seq_shift = SAMPLE_ROWS.bit_length() - 1
        tot, eacum = _ssd_chunk(act_s, dt, dtT, alog_ref, alogT_ref, dexp_ref, seq_shift, G, R, P, N,
                                lambda g, Cg: None, y_s, xw_s)
        tot_s[...] = tot
        eacum_s[...] = eacum
        for g in range(G):
            xwT_s[g] = jnp.transpose(xw_s[:, g * R * P:(g + 1) * R * P]).astype(BF16)

    r0 = pl.multiple_of(bi * SAMPLE_ROWS, SAMPLE_ROWS)
    rid = lax.broadcasted_iota(jnp.int32, (Q, N), 0)
    mine = (rid >= r0) & (rid < r0 + SAMPLE_ROWS)
    dec = jnp.exp(tot_s[pl.ds(r0, 1), :])
    for g in range(G):
        Cb = act_s[pl.ds(r0, SAMPLE_ROWS), inner + (G + g) * N:inner + (G + g + 1) * N]
        stg = h0_ref[0, g * R:(g + 1) * R].reshape(R * P, N)
        yint_s[pl.ds(r0, SAMPLE_ROWS), g * R * P:(g + 1) * R * P] = _dot_nt(Cb, stg)
        Bg = jnp.where(mine, act_s[:, inner + g * N:inner + (g + 1) * N], 0.0).astype(BF16)
        S = _dot(xwT_s[g], Bg)
        for r in range(R):
            h = g * R + r
            st_ref[0, h] = dec[:, h:h + 1] * h0_ref[0, h] + S[r * P:(r + 1) * P, :]

    @pl.when(bi == pl.num_programs(1) - 1)
    def _():
        for h in range(G * R):
            sl = slice(h * P, (h + 1) * P)
            y_s[:, sl] = y_s[:, sl] + yint_s[:, sl] * eacum_s[:, h:h + 1]
        yb_ref[...] = _ssd_finish(y_s[...], z_ref[...], nw_ref, wo_ref)


def _ssd_sample(proj, dt, dtT, cpad, h0, col_xbc, col_z, dims, cw, cb, dtb, dtbT, alog, alogT, dexp, nw, wo):
    G, R, P, N = dims
    H = G * R
    inner = H * P
    CD = cw.shape[1]
    T = proj.shape[0]
    Bs = h0.shape[0]
    Q = _pick(T, SSD_CHUNK)
    nb = Q // SAMPLE_ROWS
    full = lambda *s: pl.BlockSpec(s, lambda i, j: (0,) * len(s))
    return pl.pallas_call(
        functools.partial(_ssd_sample_body, G, R, P, N),
        grid=(T // Q, nb),
        in_specs=[
            pl.BlockSpec((Q, CD), lambda i, j: (i, col_xbc)),
            pl.BlockSpec((Q, CD), lambda i, j: (i, 0)),
            pl.BlockSpec((Q, inner), lambda i, j: (i, col_z)),
            pl.BlockSpec((Q, LANES), lambda i, j: (i, 0)),
            pl.BlockSpec((H, Q), lambda i, j: (0, i)),
            pl.BlockSpec((1, H, P, N), lambda i, j: (i * nb + j, 0, 0, 0)),
            full(4, CD), full(1, CD), full(1, LANES), full(H, 1), full(1, LANES), full(H, 1),
            full(1, inner), full(1, inner), full(*wo.shape),
        ],
        out_specs=[
            pl.BlockSpec((Q, wo.shape[1]), lambda i, j: (i, 0)),
            pl.BlockSpec((1, H, P, N), lambda i, j: (i * nb + j, 0, 0, 0)),
        ],
        out_shape=[
            jax.ShapeDtypeStruct((T, wo.shape[1]), F32),
            jax.ShapeDtypeStruct((Bs, H, P, N), F32),
        ],
        scratch_shapes=[
            pltpu.VMEM((Q, CD), F32),
            pltpu.VMEM((Q, inner), F32),
            pltpu.VMEM((Q, inner), F32),
            pltpu.VMEM((Q, inner), F32),
            pltpu.VMEM((G, R * P, Q), BF16),
            pltpu.VMEM((Q, LANES), F32),
            pltpu.VMEM((Q, LANES), F32),
        ],
        compiler_params=_cparams(("parallel", "arbitrary")),
        name="ssd_sample",
    )(proj, cpad, proj, dt, dtT, h0, cw, cb, dtb, dtbT, alog, alogT, dexp, nw, wo)


def _merge_body(x_ref, ya_ref, yb_ref, ga_ref, gb_ref, wout_ref, nf_ref, x1_ref, h2_ref, h2t_ref):
    m = _sigmoid(ga_ref[...]) * ya_ref[...] + _sigmoid(gb_ref[...]) * yb_ref[...]
    x1 = x_ref[...] + _dot(m.astype(BF16), wout_ref[...])
    x1_ref[...] = x1
    h2 = _rms(x1, nf_ref[...])
    h2_ref[...] = h2.astype(BF16)
    h2t_ref[...] = jnp.transpose(h2).astype(BF16)


def _merge(x2d, ya, yb, proj, col_ga, col_gb, wout, nf):
    T, D = x2d.shape
    TM = _pick(T, 512)
    row = lambda c: pl.BlockSpec((TM, D), lambda i: (i, c))
    return pl.pallas_call(
        _merge_body,
        grid=(T // TM,),
        in_specs=[row(0), row(0), row(0), row(col_ga), row(col_gb),
                  pl.BlockSpec(wout.shape, lambda i: (0, 0)), pl.BlockSpec((1, D), lambda i: (0, 0))],
        out_specs=[row(0), row(0), pl.BlockSpec((D, TM), lambda i: (0, i))],
        out_shape=[jax.ShapeDtypeStruct((T, D), F32), jax.ShapeDtypeStruct((T, D), BF16),
                   jax.ShapeDtypeStruct((D, T), BF16)],
        compiler_params=_cparams(("parallel",)),
        name="merge",
    )(x2d, ya, yb, proj, proj, wout, nf)


def _staircase(k):
    return [(ka, k // (ka + 1)) for ka in range(k)]


def _take_top(cur, k_top):
    rank = jnp.full(cur.shape, float(k_top), F32)
    vals = []
    for k in range(k_top):
        m = jnp.max(cur, axis=0, keepdims=True)
        hit = cur == m
        vals.append(m)
        rank = jnp.where(hit, float(k), rank)
        cur = jnp.where(hit, NEG_INF, cur)
    return vals, rank


def _take_top_ties(x, k_top, val_ref, rank_ref):
    rows = lax.broadcasted_iota(jnp.int32, x.shape, 0)
    kk = lax.broadcasted_iota(jnp.int32, (k_top, x.shape[1]), 0)

    def body(k, carry):
        cur, rank, vals = carry
        m = jnp.max(cur, axis=0, keepdims=True)
        first = jnp.min(jnp.where(cur == m, rows, x.shape[0]), axis=0, keepdims=True)
        hit = rows == first
        return (jnp.where(hit, NEG_INF, cur), jnp.where(hit, k.astype(F32), rank),
                jnp.where(kk == k, m, vals))

    init = (x, jnp.full(x.shape, float(k_top), F32), jnp.zeros((k_top, x.shape[1]), F32))
    _, rank, vals = lax.fori_loop(0, k_top, body, init)
    val_ref[...] = vals
    rank_ref[...] = rank


def _ranked(x, k_top, val_ref, rank_ref):
    vals, rank = _take_top(x, k_top)
    for k in range(k_top):
        val_ref[k:k + 1, :] = vals[k]
    rank_ref[...] = rank
    return jnp.sum(jnp.where(rank < k_top, 1.0, 0.0), axis=0, keepdims=True)


def _route_body(h2_ref, wq_ref, keys_ref, na_ref, ea_ref, rb_ref, eb_ref, sv_s, rk_s, cand_s, cv_s, cr_s):
    NH = keys_ref.shape[0]
    NK, KH = keys_ref.shape[2], keys_ref.shape[3]
    K = PEER_TOPK
    TB = h2_ref.shape[0]
    tiles = [slice(tl * LANES, (tl + 1) * LANES) for tl in range(TB // LANES)]
    q = _dot(h2_ref[...], wq_ref[...]).astype(BF16)
    cand_s[...] = jnp.full(cand_s.shape, NEG_INF, F32)

    def candidates():
        off = 0
        for ka, nb in _staircase(K):
            cand_s[off:off + nb, :] = sv_s[0, ka:ka + 1, :] + sv_s[1, 0:nb, :]
            off += nb

    for h in range(NH):
        sT = [_dot_nt(keys_ref[h, s], q[:, (h * 2 + s) * KH:(h * 2 + s + 1) * KH]) for s in range(2)]
        most = jnp.zeros((1, LANES), F32)
        for s in range(2):
            for ln in tiles:
                most = jnp.maximum(most, _ranked(sT[s][:, ln], K, sv_s.at[s, :, ln], rk_s.at[s, :, ln]))
        candidates()
        for ln in tiles:
            most = jnp.maximum(most, _ranked(cand_s[:, ln], K, cv_s.at[:, ln], cr_s.at[:, ln]))

        @pl.when(jnp.max(most) > K)
        def _():
            for s in range(2):
                for ln in tiles:
                    _take_top_ties(sT[s][:, ln], K, sv_s.at[s, :, ln], rk_s.at[s, :, ln])
            candidates()
            for ln in tiles:
                _take_top_ties(cand_s[:, ln], K, cv_s.at[:, ln], cr_s.at[:, ln])

        ex = [jnp.where(rk_s[s] < K, jnp.exp(sT[s] - sv_s[s, 0:1, :]), 0.0) for s in range(2)]
        sel = cr_s[...] < K
        z = jnp.sum(jnp.where(sel, jnp.exp(cand_s[...] - cv_s[0:1, :]), 0.0), axis=0, keepdims=True)
        ra = rk_s[0]
        na = jnp.zeros(ra.shape, F32)
        off = 0
        for ka, nb in _staircase(K):
            cnt = jnp.sum(jnp.where(cr_s[off:off + nb, :] < K, 1.0, 0.0), axis=0, keepdims=True)
            na = jnp.where(ra == float(ka), cnt, na)
            off += nb
        na_ref[h * NK:(h + 1) * NK, :] = na
        ea_ref[h * NK:(h + 1) * NK, :] = ex[0] / z
        rb_ref[h] = rk_s[1].astype(BF16)
        eb_ref[h] = ex[1].astype(BF16)


def _route(h2, wq, keys):
    T, D = h2.shape
    NH, _, NK, KH = keys.shape
    TB = _pick(T, 512)
    assert TB % LANES == 0
    ncand = sum(nb for _, nb in _staircase(PEER_TOPK))
    ncand_pad = -(-ncand // SUBLANES) * SUBLANES
    flat = pl.BlockSpec((NH * NK, TB), lambda i: (0, i))
    tok = pl.BlockSpec((NH, NK, TB), lambda i: (0, 0, i))
    return pl.pallas_call(
        _route_body,
        grid=(T // TB,),
        in_specs=[pl.BlockSpec((TB, D), lambda i: (i, 0)),
                  pl.BlockSpec(wq.shape, lambda i: (0, 0)),
                  pl.BlockSpec(keys.shape, lambda i: (0, 0, 0, 0))],
        out_specs=[flat, flat, tok, tok],
        out_shape=[jax.ShapeDtypeStruct((NH * NK, T), F32), jax.ShapeDtypeStruct((NH * NK, T), F32),
                   jax.ShapeDtypeStruct((NH, NK, T), BF16), jax.ShapeDtypeStruct((NH, NK, T), BF16)],
        scratch_shapes=[pltpu.VMEM((2, PEER_TOPK, TB), F32), pltpu.VMEM((2, NK, TB), F32),
                        pltpu.VMEM((ncand_pad, TB), F32), pltpu.VMEM((PEER_TOPK, TB), F32),
                        pltpu.VMEM((ncand_pad, TB), F32)],
        compiler_params=_cparams(("parallel",)),
        name="peer_route",
    )(h2, wq, keys)


def _peer_body(h2t_ref, x1_ref, u_ref, vt_ref, na_ref, ea_ref, rb_ref, eb_ref, nfin_ref, y_ref,
               s_s, a_s, acc_s):
    ec = pl.program_id(1)
    NH, NK, TB = rb_ref.shape
    ni = u_ref.shape[0] // NK

    @pl.when(ec == 0)
    def _():
        acc_s[...] = jnp.zeros(acc_s.shape, F32)

    s_s[...] = _dot(u_ref[...], h2t_ref[...])

    def row_tile(grp, il):
        return jnp.broadcast_to(grp[il:il + 1, :], (BF16_ROWS, LANES)).astype(BF16)

    for tl in range(TB // LANES):
        ln = slice(tl * LANES, (tl + 1) * LANES)
        rows = [pl.ds(pl.multiple_of(h * NK + ec * ni, SUBLANES), ni) for h in range(NH)]
        na = [na_ref[rows[h], ln] for h in range(NH)]
        ea = [ea_ref[rows[h], ln] for h in range(NH)]
        for il in range(ni):
            na_t = [row_tile(na[h], il) for h in range(NH)]
            ea_t = [row_tile(ea[h], il) for h in range(NH)]
            for jt in range(NK // BF16_ROWS):
                js = slice(jt * BF16_ROWS, (jt + 1) * BF16_ROWS)
                w = jnp.zeros((BF16_ROWS, LANES), BF16)
                for h in range(NH):
                    w = w + ea_t[h] * jnp.where(rb_ref[h, js, ln] < na_t[h], eb_ref[h, js, ln], 0.0)
                e = slice(il * NK + jt * BF16_ROWS, il * NK + (jt + 1) * BF16_ROWS)
                a_s[e, ln] = _gelu(s_s[e, ln]).astype(BF16) * w
    acc_s[...] += _dot(vt_ref[...], a_s[...])

    @pl.when(ec == pl.num_programs(1) - 1)
    def _():
        x2 = x1_ref[...] + jnp.transpose(acc_s[...])
        y_ref[...] = _rms(x2, nfin_ref[...])


def _peer(h2t, x1, u, vt, na, ea, rb, eb, nfin):
    D, T = h2t.shape
    E = u.shape[0]
    NH, NK, _ = rb.shape
    TB = _pick(T, 512)
    EC = 2 * SUBLANES * NK
    assert TB % LANES == 0 and E % EC == 0 and NK % BF16_ROWS == 0
    tok = pl.BlockSpec((NH, NK, TB), lambda i, j: (0, 0, i))
    flat = pl.BlockSpec((NH * NK, TB), lambda i, j: (0, i))
    return pl.pallas_call(
        _peer_body,
        grid=(T // TB, E // EC),
        in_specs=[pl.BlockSpec((D, TB), lambda i, j: (0, i)),
                  pl.BlockSpec((TB, D), lambda i, j: (i, 0)),
                  pl.BlockSpec((EC, D), lambda i, j: (j, 0)),
                  pl.BlockSpec((D, EC), lambda i, j: (0, j)),
                  flat, flat, tok, tok,
                  pl.BlockSpec((1, D), lambda i, j: (0, 0))],
        out_specs=pl.BlockSpec((TB, D), lambda i, j: (i, 0)),
        out_shape=jax.ShapeDtypeStruct((T, D), F32),
        scratch_shapes=[pltpu.VMEM((EC, TB), F32), pltpu.VMEM((EC, TB), BF16), pltpu.VMEM((D, TB), F32)],
        compiler_params=_cparams(("parallel", "arbitrary")),
        name="peer_mix",
    )(h2t, x1, u, vt, na, ea, rb, eb, nfin)


def _pad_lanes(row):
    return jnp.pad(row, ((0, 0), (0, LANES - row.shape[1])))


def kernel(x_prompt, x_sample, state_rg_h, state_rg_conv, state_ssd_h, state_ssd_conv, norm_mix, w_in, rg_conv_w, rg_conv_b, rg_wa, rg_ba, rg_wx, rg_bx, rg_lam, w_rg_out, ssd_conv_w, ssd_conv_b, ssd_dt_bias, ssd_a_log, ssd_d, ssd_norm, w_ssd_out, w_out, norm_ffn, peer_wq, peer_keys, peer_u, peer_v, norm_final):
    depth = w_in.shape[0]
    B, L, D = x_prompt.shape
    Bs, Ls, _ = x_sample.shape
    assert depth == 1 and Ls == SAMPLE_ROWS // 2 and rg_conv_w.shape[1] == 4 and L >= 3
    W = rg_conv_w.shape[2]
    H = ssd_a_log.shape[1]
    inner = ssd_norm.shape[1]
    CD = ssd_conv_w.shape[2]
    P = inner // H
    N = state_ssd_h.shape[-1]
    G = (CD - inner) // (2 * N)
    R = H // G
    dims = (G, R, P, N)
    assert W == D and inner == 2 * D and CD == 3 * D and H % SUBLANES == 0 and H <= LANES
    o_x, o_g, o_z, o_xbc, o_dt, o_m = 0, W, 2 * W, 2 * W + inner, 2 * W + inner + CD, 2 * W + inner + CD + H
    col_xbc, col_x, col_z, col_g, col_ga, col_gb = 0, CD // W, (CD + W) // inner, (CD + W + inner) // W, \
        (CD + 2 * W + inner) // D, (CD + 2 * W + inner + D) // D

    yp = x_prompt.reshape(B * L, D)
    ys = x_sample.reshape(Bs * Ls, D)
    outs = {k: [] for k in ("prh", "prc", "psh", "psc", "srh", "src", "ssh", "ssc")}
    lead = SAMPLE_ROWS - Ls
    for l in range(depth):
        wi = w_in[l]
        w_main = jnp.concatenate([wi[:, o_xbc:o_dt], wi[:, o_x:o_g], wi[:, o_z:o_xbc], wi[:, o_g:o_z],
                                  wi[:, o_m:]], axis=1).astype(BF16)
        wdt = _pad_lanes(wi[:, o_dt:o_m])
        g_mix = norm_mix[l][None]
        rg_w = (rg_conv_w[l], rg_conv_b[l][None], rg_wa[l].astype(BF16), rg_wx[l].astype(BF16),
                rg_ba[l][None], rg_bx[l][None], rg_lam[l][None], w_rg_out[l].astype(BF16))
        ssd_w = (ssd_conv_w[l], ssd_conv_b[l][None], _pad_lanes(ssd_dt_bias[l][None]), ssd_dt_bias[l][:, None],
                 _pad_lanes(ssd_a_log[l][None]), ssd_a_log[l][:, None], jnp.repeat(ssd_d[l], P)[None],
                 ssd_norm[l][None], w_ssd_out[l].astype(BF16))
        wout = w_out[l].astype(BF16)
        nf = norm_ffn[l][None]
        wq = peer_wq[l].astype(BF16)
        keys = peer_keys[l].astype(BF16)
        u = peer_u[l].astype(BF16)
        vt = peer_v[l].astype(BF16).T
        nfin = norm_final[None]

        proj, dt, dtT = _inproj(yp, g_mix, w_main, wdt, H)
        ya, rgh = _rg_prompt(proj, B, L, col_x, col_g, *rg_w)
        yb, ssh = _ssd_prompt(proj, dt, dtT, B, L, col_xbc, col_z, dims, *ssd_w)
        p3 = proj.reshape(B, L, -1)
        outs["prh"].append(rgh.reshape(B, W))
        outs["prc"].append(p3[:, L - 3:, CD:CD + W])
        outs["psh"].append(ssh)
        outs["psc"].append(p3[:, L - 3:, :CD])
        x1, h2, h2t = _merge(yp, ya, yb, proj, col_ga, col_gb, wout, nf)
        yp = _peer(h2t, x1, u, vt, *_route(h2, wq, keys), nfin)

        xe = jnp.pad(ys.reshape(Bs, Ls, D), ((0, 0), (lead, 0), (0, 0))).reshape(Bs * SAMPLE_ROWS, D)
        proj_e, dt_e, dtT_e = _inproj(xe, g_mix, w_main, wdt, H)
        rows = lambda a3: a3.reshape(Bs * SAMPLE_ROWS, a3.shape[-1])
        rg_cpad = rows(jnp.pad(state_rg_conv[l], ((0, 0), (lead - 3, Ls), (0, 0))))
        rg_hpad = rows(jnp.pad(state_rg_h[l][:, None, :], ((0, 0), (lead - 1, Ls), (0, 0))))
        ssd_cpad = rows(jnp.pad(state_ssd_conv[l], ((0, 0), (lead - 3, Ls), (0, 0))))
        ya_e, h_e = _rg_sample(proj_e, col_x, col_g, rg_cpad, rg_hpad, *rg_w)
        yb_e, ssh_s = _ssd_sample(proj_e, dt_e, dtT_e, ssd_cpad, state_ssd_h[l], col_xbc, col_z, dims, *ssd_w)
        toks = lambda a2: a2.reshape(Bs, SAMPLE_ROWS, -1)[:, lead:].reshape(Bs * Ls, -1)
        p3 = proj_e.reshape(Bs, SAMPLE_ROWS, -1)
        outs["srh"].append(h_e.reshape(Bs, SAMPLE_ROWS, W)[:, -1])
        outs["src"].append(p3[:, SAMPLE_ROWS - 3:, CD:CD + W])
        outs["ssh"].append(ssh_s)
        outs["ssc"].append(p3[:, SAMPLE_ROWS - 3:, :CD])
        proj_s = toks(proj_e)
        x1, h2, h2t = _merge(ys, toks(ya_e), toks(yb_e), proj_s, col_ga, col_gb, wout, nf)
        ys = _peer(h2t, x1, u, vt, *_route(h2, wq, keys), nfin)

    st = lambda k: jnp.stack(outs[k])
    return (yp.reshape(B, L, D), ys.reshape(Bs, Ls, D), st("prh"), st("prc"), st("psh"), st("psc"),
            st("srh"), st("src"), st("ssh"), st("ssc"))
```

```python
import functools

import jax
import jax.numpy as jnp
from jax import lax
from jax.experimental import pallas as pl
from jax.experimental.pallas import tpu as pltpu

F32 = jnp.float32
BF16 = jnp.bfloat16
EPS = 1e-6
RG_C = 8.0
PEER_TOPK = 16
SSD_CHUNK = 128
LANES = 128
SUBLANES = 8
BF16_ROWS = 16
SAMPLE_ROWS = 8
NEG_INF = float("-inf")
LOG2E = 1.4426950408889634
HIGHEST = lax.Precision.HIGHEST
NT_DIMS = (((1,), (1,)), ((), ()))
VMEM_LIMIT = 56 * 1024 * 1024


def _cparams(sem):
    return pltpu.CompilerParams(dimension_semantics=sem, vmem_limit_bytes=VMEM_LIMIT)


def _pick(n, pref):
    t = min(n, pref)
    while n % t:
        t -= SUBLANES
    return t


def _sigmoid(x):
    return 1.0 / (1.0 + jnp.exp2(x * -LOG2E))


def _silu(x):
    return x * _sigmoid(x)


def _softplus(x):
    return jnp.maximum(x, 0.0) + jnp.log1p(jnp.exp(-jnp.abs(x)))


def _gelu(x):
    k1 = -2.0 * 0.7978845608028654 * LOG2E
    return x / (1.0 + jnp.exp2(x * (k1 + (k1 * 0.044715) * (x * x))))


def _rms(x, g):
    return x * lax.rsqrt(jnp.mean(x * x, axis=-1, keepdims=True) + EPS) * g


def _dot(a, b):
    return jnp.dot(a, b, preferred_element_type=F32)


def _dot_nt(a, b):
    return lax.dot_general(a, b, NT_DIMS, preferred_element_type=F32)


def _inproj_body(x_ref, g_ref, w_ref, wdt_ref, o_ref, odt_ref, odtT_ref, xn_ref):
    @pl.when(pl.program_id(1) == 0)
    def _():
        xn = _rms(x_ref[...], g_ref[...])
        xn_ref[...] = xn.astype(BF16)
        dt = jnp.dot(xn, wdt_ref[...], precision=HIGHEST, preferred_element_type=F32)
        odt_ref[...] = dt
        odtT_ref[...] = jnp.transpose(dt)[:odtT_ref.shape[0], :]

    o_ref[...] = _dot(xn_ref[...], w_ref[...]).astype(o_ref.dtype)


def _inproj(x2d, g, w_main, wdt, H):
    T, D = x2d.shape
    N = w_main.shape[1]
    TM = _pick(T, 1024)
    TN = 1536
    assert N % TN == 0 and TM % LANES == 0
    return pl.pallas_call(
        _inproj_body,
        grid=(T // TM, N // TN),
        in_specs=[
            pl.BlockSpec((TM, D), lambda i, j: (i, 0)),
            pl.BlockSpec((1, D), lambda i, j: (0, 0)),
            pl.BlockSpec((D, TN), lambda i, j: (0, j)),
            pl.BlockSpec((D, LANES), lambda i, j: (0, 0)),
        ],
        out_specs=[
            pl.BlockSpec((TM, TN), lambda i, j: (i, j)),
            pl.BlockSpec((TM, LANES), lambda i, j: (i, 0)),
            pl.BlockSpec((H, TM), lambda i, j: (0, i)),
        ],
        out_shape=[
            jax.ShapeDtypeStruct((T, N), BF16),
            jax.ShapeDtypeStruct((T, LANES), F32),
            jax.ShapeDtypeStruct((H, T), F32),
        ],
        scratch_shapes=[pltpu.VMEM((TM, D), BF16)],
        compiler_params=_cparams(("parallel", "arbitrary")),
        name="inproj",
    )(x2d, g, w_main, wdt)


def _conv_slab(x, pad, cw_ref, cb_ref, ls):
    L = x.shape[0]
    pad[SUBLANES:SUBLANES + L, :] = x
    out = (cb_ref[:, ls]
           + cw_ref[3:4, ls] * pad[8:8 + L, :]
           + cw_ref[2:3, ls] * pad[7:7 + L, :]
           + cw_ref[1:2, ls] * pad[6:6 + L, :]
           + cw_ref[0:1, ls] * pad[5:5 + L, :])
    pad[0:SUBLANES, :] = pad[L:L + SUBLANES, :]
    return out


def _rg_gates(xc, wa_ref, wx_ref, ba, bx, lam):
    xb = xc.astype(BF16)
    nb, bw = wa_ref.shape[0], wa_ref.shape[1]
    rs, gs = [], []
    for k in range(nb):
        xk = xb[:, k * bw:(k + 1) * bw]
        rs.append(_dot(xk, wa_ref[k]))
        gs.append(_dot(xk, wx_ref[k]))
    r = _sigmoid(jnp.concatenate(rs, axis=1) + ba)
    i = _sigmoid(jnp.concatenate(gs, axis=1) + bx)
    log_a = -RG_C * r * _softplus(-lam)
    a = jnp.exp(log_a)
    em1 = jnp.tanh(log_a) * (a * a + 1.0)
    return a, jnp.sqrt(-em1) * (i * xc)


def _rg_prompt_body(x_ref, gate_ref, cw_ref, cb_ref, wa_ref, wx_ref, ba_ref, bx_ref, lam_ref, wo_ref,
                    ya_ref, hfin_ref, pad_s, a_s, b_s, h_s):
    B, Lc, W = x_ref.shape
    NS = W // LANES
    pitch = a_s.shape[1] // B

    @pl.when(pl.program_id(0) == 0)
    def _():
        pad_s[:, :, 0:SUBLANES, :] = jnp.zeros((B, NS, SUBLANES, LANES), F32)
        h_s[...] = jnp.zeros(h_s.shape, F32)

    for b in range(B):
        xc = jnp.concatenate(
            [_conv_slab(x_ref[b, :, s * LANES:(s + 1) * LANES].astype(F32), pad_s.at[b, s], cw_ref, cb_ref,
                        slice(s * LANES, (s + 1) * LANES)) for s in range(NS)], axis=1)
        a, bt = _rg_gates(xc, wa_ref, wx_ref, ba_ref[...], bx_ref[...], lam_ref[...])
        for s in range(NS):
            a_s[s, b * pitch:b * pitch + Lc, :] = a[:, s * LANES:(s + 1) * LANES]
            b_s[s, b * pitch:b * pitch + Lc, :] = bt[:, s * LANES:(s + 1) * LANES]

    def step(t, hs):
        out = []
        for s in range(NS):
            rows = pl.ds(t, B, stride=pitch)
            h = a_s[s, rows, :] * hs[s] + b_s[s, rows, :]
            b_s[s, rows, :] = h
            out.append(h)
        return tuple(out)

    hs = lax.fori_loop(0, Lc, step, tuple(h_s[:, s * LANES:(s + 1) * LANES] for s in range(NS)), unroll=4)
    for s in range(NS):
        h_s[:, s * LANES:(s + 1) * LANES] = hs[s]
    hfin_ref[...] = h_s[...]
    for b in range(B):
        hb = jnp.concatenate([b_s[s, b * pitch:b * pitch + Lc, :] for s in range(NS)], axis=1)
        ya_ref[b] = _dot((hb * _gelu(gate_ref[b].astype(F32))).astype(BF16), wo_ref[...]).astype(ya_ref.dtype)


def _rg_prompt(proj, B, L, col_x, col_g, cw, cb, wa, wx, ba, bx, lam, wo):
    W = cw.shape[1]
    Lc = _pick(L, 128)
    pitch = Lc + SUBLANES // 2
    assert B == SUBLANES and W % LANES == 0
    proj3 = proj.reshape(B, L, proj.shape[1])
    full = lambda *s: pl.BlockSpec(s, lambda c: (0,) * len(s))
    ya, hfin = pl.pallas_call(
        _rg_prompt_body,
        grid=(L // Lc,),
        in_specs=[
            pl.BlockSpec((B, Lc, W), lambda c: (0, c, col_x)),
            pl.BlockSpec((B, Lc, W), lambda c: (0, c, col_g)),
            full(4, W), full(1, W), full(*wa.shape), full(*wx.shape), full(1, W), full(1, W), full(1, W),
            full(*wo.shape),
        ],
        out_specs=[
            pl.BlockSpec((B, Lc, wo.shape[1]), lambda c: (0, c, 0)),
            pl.BlockSpec((B, W), lambda c: (0, 0)),
        ],
        out_shape=[
            jax.ShapeDtypeStruct((B, L, wo.shape[1]), BF16),
            jax.ShapeDtypeStruct((B, W), F32),
        ],
        scratch_shapes=[
            pltpu.VMEM((B, W // LANES, Lc + SUBLANES, LANES), F32),
            pltpu.VMEM((W // LANES, B * pitch, LANES), F32),
            pltpu.VMEM((W // LANES, B * pitch, LANES), F32),
            pltpu.VMEM((B, W), F32),
        ],
        compiler_params=_cparams(("arbitrary",)),
        name="rg_prompt",
    )(proj3, proj3, cw, cb, wa, wx, ba, bx, lam, wo)
    return ya.reshape(B * L, wo.shape[1]), hfin


def _conv_rolled(u, cw_ref, cb_ref):
    return (cb_ref[...]
            + cw_ref[3:4, :] * u
            + cw_ref[2:3, :] * pltpu.roll(u, 1, axis=0)
            + cw_ref[1:2, :] * pltpu.roll(u, 2, axis=0)
            + cw_ref[0:1, :] * pltpu.roll(u, 3, axis=0))


def _rg_sample_body(x_ref, gate_ref, cpad_ref, hpad_ref, cw_ref, cb_ref, wa_ref, wx_ref, ba_ref, bx_ref,
                    lam_ref, wo_ref, ya_ref, h_ref):
    xc = _conv_rolled(x_ref[...].astype(F32) + cpad_ref[...], cw_ref, cb_ref)
    a, bt = _rg_gates(xc, wa_ref, wx_ref, ba_ref[...], bx_ref[...], lam_ref[...])
    row = lax.broadcasted_iota(jnp.int32, a.shape, 0) & (SAMPLE_ROWS - 1)
    h = hpad_ref[...]
    for k in range(SAMPLE_ROWS // 2, SAMPLE_ROWS):
        h = jnp.where(row == k, a * pltpu.roll(h, 1, axis=0) + bt, h)
    h_ref[...] = h
    ya_ref[...] = _dot((h * _gelu(gate_ref[...].astype(F32))).astype(BF16), wo_ref[...]).astype(ya_ref.dtype)


def _rg_sample(proj, col_x, col_g, cpad, hpad, cw, cb, wa, wx, ba, bx, lam, wo):
    T = proj.shape[0]
    W = cw.shape[1]
    TM = _pick(T, 256)
    full = lambda *s: pl.BlockSpec(s, lambda i: (0,) * len(s))
    return pl.pallas_call(
        _rg_sample_body,
        grid=(T // TM,),
        in_specs=[
            pl.BlockSpec((TM, W), lambda i: (i, col_x)),
            pl.BlockSpec((TM, W), lambda i: (i, col_g)),
            pl.BlockSpec((TM, W), lambda i: (i, 0)),
            pl.BlockSpec((TM, W), lambda i: (i, 0)),
            full(4, W), full(1, W), full(*wa.shape), full(*wx.shape), full(1, W), full(1, W), full(1, W),
            full(*wo.shape),
        ],
        out_specs=[
            pl.BlockSpec((TM, wo.shape[1]), lambda i: (i, 0)),
            pl.BlockSpec((TM, W), lambda i: (i, 0)),
        ],
        out_shape=[
            jax.ShapeDtypeStruct((T, wo.shape[1]), BF16),
            jax.ShapeDtypeStruct((T, W), F32),
        ],
        compiler_params=_cparams(("parallel",)),
        name="rg_sample",
    )(proj, proj, cpad, hpad, cw, cb, wa, wx, ba, bx, lam, wo)


def _ssd_chunk(act_s, dt, dtT, alog_ref, alogT_ref, dexp_ref, seq_shift, G, R, P, N,
               yint_fn, y_s, xw_s):
    Q = act_s.shape[0]
    inner = G * R * P
    a = dt * (-jnp.exp(alog_ref[...]))
    aT = dtT * (-jnp.exp(alogT_ref[...]))
    ri = lax.broadcasted_iota(jnp.int32, (Q, Q), 0)
    ci = lax.broadcasted_iota(jnp.int32, (Q, Q), 1)
    same = (ri >> seq_shift) == (ci >> seq_shift)
    causal = same & (ci <= ri)
    causal_f = jnp.where(causal, 1.0, 0.0).astype(F32)
    same_f = jnp.where(same, 1.0, 0.0).astype(F32)
    acum = jnp.dot(causal_f, a, precision=HIGHEST, preferred_element_type=F32)
    tot = jnp.dot(same_f, a, precision=HIGHEST, preferred_element_type=F32)
    acumT = lax.dot_general(aT, causal_f, NT_DIMS, precision=HIGHEST,
                            preferred_element_type=F32)
    eacum = jnp.exp(acum)
    toend = jnp.exp(tot - acum) * dt
    for g in range(G):
        Bg = act_s[:, inner + g * N:inner + (g + 1) * N].astype(BF16)
        Cg = act_s[:, inner + (G + g) * N:inner + (G + g + 1) * N].astype(BF16)
        cb = _dot_nt(Cg, Bg)
        yint = yint_fn(g, Cg)
        for r in range(R):
            h = g * R + r
            sl = slice(h * P, (h + 1) * P)
            seg = acum[:, h:h + 1] - acumT[h:h + 1, :]
            decay = jnp.exp(jnp.where(causal, seg, NEG_INF))
            wgt = (cb * decay * dtT[h:h + 1, :]).astype(BF16)
            xh = act_s[:, sl]
            yh = _dot(wgt, xh.astype(BF16)) + dexp_ref[:, sl] * xh
            if yint is not None:
                yh = yh + yint[:, r * P:(r + 1) * P] * eacum[:, h:h + 1]
            y_s[:, sl] = yh
            xw_s[:, sl] = xh * toend[:, h:h + 1]
    return tot, eacum


def _expand_heads(x, ex_ref):
    hi = x.astype(BF16)
    r1 = x - hi.astype(F32)
    mid = r1.astype(BF16)
    lo = (r1 - mid.astype(F32)).astype(BF16)
    e = ex_ref[...]
    return _dot(hi, e) + _dot(mid, e) + _dot(lo, e)


def _ssd_finish(y, z, nw_ref, wo_ref):
    y = y * _silu(z.astype(F32))
    return _dot(_rms(y, nw_ref[...]).astype(BF16), wo_ref[...]).astype(BF16)


def _ssd_prompt_body(G, R, P, N,
                     xbc_ref, z_ref, dt_ref, dtT_ref, cw_ref, cb_ref, dtb_ref, dtbT_ref, alog_ref, alogT_ref,
                     dexp_ref, nw_ref, wo_ref, ex_ref, yb_ref, st_ref, pad_s, act_s, y_s, stT_s):
    Q = xbc_ref.shape[0]
    inner = G * R * P
    GW = R * P

    @pl.when(pl.program_id(1) == 0)
    def _():
        pad_s[:, 0:SUBLANES, :] = jnp.zeros((pad_s.shape[0], SUBLANES, LANES), F32)
        stT_s[...] = jnp.zeros(stT_s.shape, F32)

    for s in range(pad_s.shape[0]):
        ls = slice(s * LANES, (s + 1) * LANES)
        act_s[:, ls] = _silu(_conv_slab(xbc_ref[:, ls].astype(F32), pad_s.at[s], cw_ref, cb_ref, ls))
    dt = _softplus(dt_ref[...] + dtb_ref[...])
    dtT = _softplus(dtT_ref[...] + dtbT_ref[...])
    a = dt * (-jnp.exp(alog_ref[...]))
    aT = dtT * (-jnp.exp(alogT_ref[...]))
    ri = lax.broadcasted_iota(jnp.int32, (Q, Q), 0)
    ci = lax.broadcasted_iota(jnp.int32, (Q, Q), 1)
    causal = ci <= ri
    causal_f = jnp.where(causal, 1.0, 0.0).astype(F32)
    hi = dict(precision=HIGHEST, preferred_element_type=F32)
    acum = jnp.dot(causal_f, a, **hi)
    tot = jnp.dot(jnp.ones((SUBLANES, Q), F32), a, **hi)
    acumT = lax.dot_general(aT, causal_f, NT_DIMS, **hi)
    fx = _expand_heads(jnp.concatenate([jnp.exp(acum), jnp.exp(tot[0:1, :] - acum) * dt, jnp.exp(tot)], axis=0),
                       ex_ref)
    eacum_x, toend_x, dec_x = fx[0:Q], fx[Q:2 * Q], fx[2 * Q:2 * Q + 1]
    col2 = acum * LOG2E
    row2 = (acumT - jnp.log(dtT)) * LOG2E
    lane = lax.broadcasted_iota(jnp.int32, (Q, LANES), 1)
    for g in range(G):
        gs = slice(g * GW, (g + 1) * GW)
        Bf = act_s[:, inner + g * N:inner + (g + 1) * N]
        Bg = Bf.astype(BF16)
        Cg = act_s[:, inner + (G + g) * N:inner + (G + g + 1) * N].astype(BF16)
        cb = jnp.where(causal, _dot_nt(Cg, Bg), 0.0)
        y_s[:, gs] = (_dot(Cg, stT_s[:, gs].astype(BF16)) * eacum_x[:, gs]
                      + dexp_ref[:, gs] * act_s[:, gs])
        for pr in range(R // 2):
            h0 = g * R + 2 * pr
            ps = slice(h0 * P, (h0 + 2) * P)
            wg = []
            for h in (h0, h0 + 1):
                e = jnp.exp2(jnp.where(causal, col2[:, h:h + 1] - row2[h:h + 1, :], NEG_INF))
                wg.append((cb * e).astype(BF16))
            xp = act_s[:, ps]
            xa = jnp.where(lane < P, xp, 0.0).astype(BF16)
            xb = jnp.where(lane < P, 0.0, xp).astype(BF16)
            y_s[:, ps] = y_s[:, ps] + _dot(jnp.concatenate(wg, axis=1), jnp.concatenate([xa, xb], axis=0))
        xw = (act_s[:, gs] * toend_x[:, gs]).astype(BF16)
        ST = _dot(jnp.transpose(Bf).astype(BF16), xw)
        stT_s[:, gs] = dec_x[:, gs] * stT_s[:, gs] + ST
    yb_ref[...] = _ssd_finish(y_s[...], z_ref[...], nw_ref, wo_ref)

    @pl.when(pl.program_id(1) == pl.num_programs(1) - 1)
    def _():
        for g in range(G):
            st_ref[0, g * R:(g + 1) * R] = jnp.transpose(stT_s[:, g * GW:(g + 1) * GW]).reshape(R, P, N)


def _ssd_prompt(proj, dt, dtT, B, L, col_xbc, col_z, dims, cw, cb, dtb, dtbT, alog, alogT, dexp, nw, wo):
    G, R, P, N = dims
    H = G * R
    inner = H * P
    CD = cw.shape[1]
    Q = _pick(L, SSD_CHUNK)
    nC = L // Q
    assert 2 * P == LANES and R % 2 == 0 and N == LANES
    head_of_lane = jnp.arange(inner, dtype=jnp.int32) // P
    ex = (jnp.arange(LANES, dtype=jnp.int32)[:, None] == head_of_lane[None, :]).astype(BF16)
    full = lambda *s: pl.BlockSpec(s, lambda b, c: (0,) * len(s))
    return pl.pallas_call(
        functools.partial(_ssd_prompt_body, G, R, P, N),
        grid=(B, nC),
        in_specs=[
            pl.BlockSpec((Q, CD), lambda b, c: (b * nC + c, col_xbc)),
            pl.BlockSpec((Q, inner), lambda b, c: (b * nC + c, col_z)),
            pl.BlockSpec((Q, LANES), lambda b, c: (b * nC + c, 0)),
            pl.BlockSpec((H, Q), lambda b, c: (0, b * nC + c)),
            full(4, CD), full(1, CD), full(1, LANES), full(H, 1), full(1, LANES), full(H, 1),
            full(1, inner), full(1, inner), full(*wo.shape), full(LANES, inner),
        ],
        out_specs=[
            pl.BlockSpec((Q, wo.shape[1]), lambda b, c: (b * nC + c, 0)),
            pl.BlockSpec((1, H, P, N), lambda b, c: (b, 0, 0, 0)),
        ],
        out_shape=[
            jax.ShapeDtypeStruct((B * L, wo.shape[1]), BF16),
            jax.ShapeDtypeStruct((B, H, P, N), F32),
        ],
        scratch_shapes=[
            pltpu.VMEM((CD // LANES, Q + SUBLANES, LANES), F32),
            pltpu.VMEM((Q, CD), F32),
            pltpu.VMEM((Q, inner), F32),
            pltpu.VMEM((N, inner), F32),
        ],
        compiler_params=_cparams(("parallel", "arbitrary")),
        name="ssd_prompt",
    )(proj, proj, dt, dtT, cw, cb, dtb, dtbT, alog, alogT, dexp, nw, wo, ex)


def _ssd_sample_body(G, R, P, N,
                     xbc_ref, cpad_ref, z_ref, dt_ref, dtT_ref, h0_ref, cw_ref, cb_ref, dtb_ref, dtbT_ref,
                     alog_ref, alogT_ref, dexp_ref, nw_ref, wo_ref, yb_ref, st_ref,
                     act_s, y_s, xw_s, yint_s, xwT_s, tot_s, eacum_s):
    Q = xbc_ref.shape[0]
    inner = G * R * P
    bi = pl.program_id(1)
    half = SAMPLE_ROWS // 2

    @pl.when(bi == 0)
    def _():
        act_s[...] = _silu(_conv_rolled(xbc_ref[...].astype(F32) + cpad_ref[...], cw_ref, cb_ref))
        rows = lax.broadcasted_iota(jnp.int32, (Q, LANES), 0) & (SAMPLE_ROWS - 1)
        cols = lax.broadcasted_iota(jnp.int32, dtT_ref.shape, 1) & (SAMPLE_ROWS - 1)
        dt = jnp.where(rows >= half, _softplus(dt_ref[...] + dtb_ref[...]), 0.0)
        dtT = jnp.where(cols >= half, _softplus(dtT_ref[...] + dtbT_ref[...]), 0.0)
        seq_shift = SAMPLE_ROWS.bit_length() - 1
        tot, eacum = _ssd_chunk(act_s, dt, dtT, alog_ref, alogT_ref, dexp_ref, seq_shift, G, R, P, N,
                                lambda g, Cg: None, y_s, xw_s)
        tot_s[...] = tot
        eacum_s[...] = eacum
        for g in range(G):
            xwT_s[g] = jnp.transpose(xw_s[:, g * R * P:(g + 1) * R * P]).astype(BF16)

    r0 = pl.multiple_of(bi * SAMPLE_ROWS, SAMPLE_ROWS)
    rid = lax.broadcasted_iota(jnp.int32, (Q, N), 0)
    mine = (rid >= r0) & (rid < r0 + SAMPLE_ROWS)
    dec = jnp.exp(tot_s[pl.ds(r0, 1), :])
    for g in range(G):
        Cb = act_s[pl.ds(r0, SAMPLE_ROWS), inner + (G + g) * N:inner + (G + g + 1) * N]
        stg = h0_ref[0, g * R:(g + 1) * R].reshape(R * P, N)
        yint_s[pl.ds(r0, SAMPLE_ROWS), g * R * P:(g + 1) * R * P] = _dot_nt(Cb, stg)
        Bg = jnp.where(mine, act_s[:, inner + g * N:inner + (g + 1) * N], 0.0).astype(BF16)
        S = _dot(xwT_s[g], Bg)
        for r in range(R):
            h = g * R + r
            st_ref[0, h] = dec[:, h:h + 1] * h0_ref[0, h] + S[r * P:(r + 1) * P, :]

    @pl.when(bi == pl.num_programs(1) - 1)
    def _():
        for h in range(G * R):
            sl = slice(h * P, (h + 1) * P)
            y_s[:, sl] = y_s[:, sl] + yint_s[:, sl] * eacum_s[:, h:h + 1]
        yb_ref[...] = _ssd_finish(y_s[...], z_ref[...], nw_ref, wo_ref)


def _ssd_sample(proj, dt, dtT, cpad, h0, col_xbc, col_z, dims, cw, cb, dtb, dtbT, alog, alogT, dexp, nw, wo):
    G, R, P, N = dims
    H = G * R
    inner = H * P
    CD = cw.shape[1]
    T = proj.shape[0]
    Bs = h0.shape[0]
    Q = _pick(T, SSD_CHUNK)
    nb = Q // SAMPLE_ROWS
    full = lambda *s: pl.BlockSpec(s, lambda i, j: (0,) * len(s))
    return pl.pallas_call(
        functools.partial(_ssd_sample_body, G, R, P, N),
        grid=(T // Q, nb),
        in_specs=[
            pl.BlockSpec((Q, CD), lambda i, j: (i, col_xbc)),
            pl.BlockSpec((Q, CD), lambda i, j: (i, 0)),
            pl.BlockSpec((Q, inner), lambda i, j: (i, col_z)),
            pl.BlockSpec((Q, LANES), lambda i, j: (i, 0)),
            pl.BlockSpec((H, Q), lambda i, j: (0, i)),
            pl.BlockSpec((1, H, P, N), lambda i, j: (i * nb + j, 0, 0, 0)),
            full(4, CD), full(1, CD), full(1, LANES), full(H, 1), full(1, LANES), full(H, 1),
            full(1, inner), full(1, inner), full(*wo.shape),
        ],
        out_specs=[
            pl.BlockSpec((Q, wo.shape[1]), lambda i, j: (i, 0)),
            pl.BlockSpec((1, H, P, N), lambda i, j: (i * nb + j, 0, 0, 0)),
        ],
        out_shape=[
            jax.ShapeDtypeStruct((T, wo.shape[1]), BF16),
            jax.ShapeDtypeStruct((Bs, H, P, N), F32),
        ],
        scratch_shapes=[
            pltpu.VMEM((Q, CD), F32),
            pltpu.VMEM((Q, inner), F32),
            pltpu.VMEM((Q, inner), F32),
            pltpu.VMEM((Q, inner), F32),
            pltpu.VMEM((G, R * P, Q), BF16),
            pltpu.VMEM((Q, LANES), F32),
            pltpu.VMEM((Q, LANES), F32),
        ],
        compiler_params=_cparams(("parallel", "arbitrary")),
        name="ssd_sample",
    )(proj, cpad, proj, dt, dtT, h0, cw, cb, dtb, dtbT, alog, alogT, dexp, nw, wo)


def _merge_body(x_ref, ya_ref, yb_ref, ga_ref, gb_ref, wout_ref, nf_ref, x1_ref, h2_ref, h2t_ref):
    f = lambda ref: ref[...].astype(F32)
    m = _sigmoid(f(ga_ref)) * f(ya_ref) + _sigmoid(f(gb_ref)) * f(yb_ref)
    x1 = x_ref[...] + _dot(m.astype(BF16), wout_ref[...])
    x1_ref[...] = x1
    h2 = _rms(x1, nf_ref[...])
    h2_ref[...] = h2.astype(BF16)
    h2t_ref[...] = jnp.transpose(h2).astype(BF16)


def _merge(x2d, ya, yb, proj, col_ga, col_gb, wout, nf):
    T, D = x2d.shape
    TM = _pick(T, 512)
    row = lambda c: pl.BlockSpec((TM, D), lambda i: (i, c))
    return pl.pallas_call(
        _merge_body,
        grid=(T // TM,),
        in_specs=[row(0), row(0), row(0), row(col_ga), row(col_gb),
                  pl.BlockSpec(wout.shape, lambda i: (0, 0)), pl.BlockSpec((1, D), lambda i: (0, 0))],
        out_specs=[row(0), row(0), pl.BlockSpec((D, TM), lambda i: (0, i))],
        out_shape=[jax.ShapeDtypeStruct((T, D), F32), jax.ShapeDtypeStruct((T, D), BF16),
                   jax.ShapeDtypeStruct((D, T), BF16)],
        compiler_params=_cparams(("parallel",)),
        name="merge",
    )(x2d, ya, yb, proj, proj, wout, nf)


def _staircase(k):
    return [(ka, k // (ka + 1)) for ka in range(k)]


def _take_top(cur, k_top, want_rank):
    rank = jnp.full(cur.shape, float(k_top), F32) if want_rank else None
    vals = []
    for k in range(k_top):
        m = jnp.max(cur, axis=0, keepdims=True)
        hit = cur == m
        vals.append(m)
        if want_rank:
            rank = jnp.where(hit, float(k), rank)
        cur = jnp.where(hit, NEG_INF, cur)
    taken = jnp.sum(jnp.where(cur == NEG_INF, 1.0, 0.0), axis=0, keepdims=True)
    return vals, rank, taken


def _take_top_ties(x, k_top, val_ref, rank_ref):
    rows = lax.broadcasted_iota(jnp.int32, x.shape, 0)
    kk = lax.broadcasted_iota(jnp.int32, (k_top, x.shape[1]), 0)

    def body(k, carry):
        cur, rank, vals = carry
        m = jnp.max(cur, axis=0, keepdims=True)
        first = jnp.min(jnp.where(cur == m, rows, x.shape[0]), axis=0, keepdims=True)
        hit = rows == first
        return (jnp.where(hit, NEG_INF, cur), jnp.where(hit, k.astype(F32), rank),
                jnp.where(kk == k, m, vals))

    init = (x, jnp.full(x.shape, float(k_top), F32), jnp.zeros((k_top, x.shape[1]), F32))
    _, rank, vals = lax.fori_loop(0, k_top, body, init)
    val_ref[...] = vals
    rank_ref[...] = rank


def _route_body(h2_ref, wq_ref, keys_ref, na_ref, ea_ref, rb_ref, eb_ref, sv_s, rk_s, cand_s, cv_s, cr_s):
    NH = keys_ref.shape[0]
    NK, KH = keys_ref.shape[2], keys_ref.shape[3]
    K = PEER_TOPK
    TB = h2_ref.shape[0]
    tiles = [slice(tl * LANES, (tl + 1) * LANES) for tl in range(TB // LANES)]
    q = _dot(h2_ref[...], wq_ref[...]).astype(BF16)
    cand_s[...] = jnp.full(cand_s.shape, NEG_INF, F32)

    def candidates():
        off = 0
        for ka, nb in _staircase(K):
            cand_s[off:off + nb, :] = sv_s[0, ka:ka + 1, :] + sv_s[1, 0:nb, :]
            off += nb

    def emit(h, sT, in_top, rb, sel, na_of):
        top = sv_s[0, 0:1, :] + sv_s[1, 0:1, :]
        z = jnp.sum(jnp.where(sel, jnp.exp(cand_s[...] - top), 0.0), axis=0, keepdims=True)
        ex = [jnp.where(in_top[s], jnp.exp(sT[s] - sv_s[s, 0:1, :]), 0.0) for s in range(2)]
        na = jnp.zeros(sT[0].shape, F32)
        off = 0
        for ka, nb in _staircase(K):
            cnt = jnp.sum(jnp.where(sel[off:off + nb, :], 1.0, 0.0), axis=0, keepdims=True)
            na = jnp.where(na_of(ka), cnt, na)
            off += nb
        na_ref[h * NK:(h + 1) * NK, :] = na
        ea_ref[h * NK:(h + 1) * NK, :] = ex[0] / z
        rb_ref[h] = rb.astype(BF16)
        eb_ref[h] = ex[1].astype(BF16)

    for h in range(NH):
        sT = [_dot_nt(keys_ref[h, s], q[:, (h * 2 + s) * KH:(h * 2 + s + 1) * KH]) for s in range(2)]
        most = jnp.zeros((1, LANES), F32)
        rbs = []
        for s in range(2):
            for ln in tiles:
                vals, rank, taken = _take_top(sT[s][:, ln], K, s == 1)
                for k in range(K):
                    sv_s[s, k:k + 1, ln] = vals[k]
                most = jnp.maximum(most, taken)
                if s == 1:
                    rbs.append(rank)
        candidates()
        cand = cand_s[...]
        tau = _take_top(cand, K, False)[0][K - 1]
        sel = cand >= tau
        most = jnp.maximum(most, jnp.max(jnp.sum(jnp.where(sel, 1.0, 0.0), axis=0, keepdims=True),
                                         axis=1, keepdims=True))
        emit(h, sT, [sT[s] >= sv_s[s, K - 1:K, :] for s in range(2)], jnp.concatenate(rbs, axis=1), sel,
             lambda ka: sT[0] == sv_s[0, ka:ka + 1, :])

        @pl.when(jnp.max(most) > K)
        def _():
            for s in range(2):
                for ln in tiles:
                    _take_top_ties(sT[s][:, ln], K, sv_s.at[s, :, ln], rk_s.at[s, :, ln])
            candidates()
            for ln in tiles:
                _take_top_ties(cand_s[:, ln], K, cv_s.at[:, ln], cr_s.at[:, ln])
            emit(h, sT, [rk_s[s] < K for s in range(2)], rk_s[1], cr_s[...] < K,
                 lambda ka: rk_s[0] == float(ka))


def _route(h2, wq, keys):
    T, D = h2.shape
    NH, _, NK, KH = keys.shape
    TB = _pick(T, 512)
    assert TB % LANES == 0
    ncand = sum(nb for _, nb in _staircase(PEER_TOPK))
    ncand_pad = -(-ncand // SUBLANES) * SUBLANES
    flat = pl.BlockSpec((NH * NK, TB), lambda i: (0, i))
    tok = pl.BlockSpec((NH, NK, TB), lambda i: (0, 0, i))
    return pl.pallas_call(
        _route_body,
        grid=(T // TB,),
        in_specs=[pl.BlockSpec((TB, D), lambda i: (i, 0)),
                  pl.BlockSpec(wq.shape, lambda i: (0, 0)),
                  pl.BlockSpec(keys.shape, lambda i: (0, 0, 0, 0))],
        out_specs=[flat, flat, tok, tok],
        out_shape=[jax.ShapeDtypeStruct((NH * NK, T), F32), jax.ShapeDtypeStruct((NH * NK, T), F32),
                   jax.ShapeDtypeStruct((NH, NK, T), BF16), jax.ShapeDtypeStruct((NH, NK, T), BF16)],
        scratch_shapes=[pltpu.VMEM((2, PEER_TOPK, TB), F32), pltpu.VMEM((2, NK, TB), F32),
                        pltpu.VMEM((ncand_pad, TB), F32), pltpu.VMEM((PEER_TOPK, TB), F32),
                        pltpu.VMEM((ncand_pad, TB), F32)],
        compiler_params=_cparams(("parallel",)),
        name="peer_route",
    )(h2, wq, keys)


def _peer_body(h2t_ref, x1_ref, u_ref, vt_ref, na_ref, ea_ref, rb_ref, eb_ref, nfin_ref, y_ref,
               s_s, a_s, acc_s):
    ec = pl.program_id(1)
    NH, NK, TB = rb_ref.shape
    ni = u_ref.shape[0] // NK

    @pl.when(ec == 0)
    def _():
        acc_s[...] = jnp.zeros(acc_s.shape, F32)

    s_s[...] = _dot(u_ref[...], h2t_ref[...])

    def row_tile(grp, il):
        return jnp.broadcast_to(grp[il:il + 1, :], (BF16_ROWS, LANES)).astype(BF16)

    for tl in range(TB // LANES):
        ln = slice(tl * LANES, (tl + 1) * LANES)
        rows = [pl.ds(pl.multiple_of(h * NK + ec * ni, SUBLANES), ni) for h in range(NH)]
        na = [na_ref[rows[h], ln] for h in range(NH)]
        ea = [ea_ref[rows[h], ln] for h in range(NH)]
        for il in range(ni):
            na_t = [row_tile(na[h], il) for h in range(NH)]
            ea_t = [row_tile(ea[h], il) for h in range(NH)]
            for jt in range(NK // BF16_ROWS):
                js = slice(jt * BF16_ROWS, (jt + 1) * BF16_ROWS)
                w = jnp.zeros((BF16_ROWS, LANES), BF16)
                for h in range(NH):
                    w = w + ea_t[h] * jnp.where(rb_ref[h, js, ln] < na_t[h], eb_ref[h, js, ln], 0.0)
                e = slice(il * NK + jt * BF16_ROWS, il * NK + (jt + 1) * BF16_ROWS)
                a_s[e, ln] = _gelu(s_s[e, ln]).astype(BF16) * w
    acc_s[...] += _dot(vt_ref[...], a_s[...])

    @pl.when(ec == pl.num_programs(1) - 1)
    def _():
        x2 = x1_ref[...] + jnp.transpose(acc_s[...])
        y_ref[...] = _rms(x2, nfin_ref[...])


def _peer(h2t, x1, u, vt, na, ea, rb, eb, nfin):
    D, T = h2t.shape
    E = u.shape[0]
    NH, NK, _ = rb.shape
    TB = _pick(T, 512)
    EC = 2 * SUBLANES * NK
    assert TB % LANES == 0 and E % EC == 0 and NK % BF16_ROWS == 0
    tok = pl.BlockSpec((NH, NK, TB), lambda i, j: (0, 0, i))
    flat = pl.BlockSpec((NH * NK, TB), lambda i, j: (0, i))
    return pl.pallas_call(
        _peer_body,
        grid=(T // TB, E // EC),
        in_specs=[pl.BlockSpec((D, TB), lambda i, j: (0, i)),
                  pl.BlockSpec((TB, D), lambda i, j: (i, 0)),
                  pl.BlockSpec((EC, D), lambda i, j: (j, 0)),
                  pl.BlockSpec((D, EC), lambda i, j: (0, j)),
                  flat, flat, tok, tok,
                  pl.BlockSpec((1, D), lambda i, j: (0, 0))],
        out_specs=pl.BlockSpec((TB, D), lambda i, j: (i, 0)),
        out_shape=jax.ShapeDtypeStruct((T, D), F32),
        scratch_shapes=[pltpu.VMEM((EC, TB), F32), pltpu.VMEM((EC, TB), BF16), pltpu.VMEM((D, TB), F32)],
        compiler_params=_cparams(("parallel", "arbitrary")),
        name="peer_mix",
    )(h2t, x1, u, vt, na, ea, rb, eb, nfin)


def _pad_lanes(row):
    return jnp.pad(row, ((0, 0), (0, LANES - row.shape[1])))


def kernel(x_prompt, x_sample, state_rg_h, state_rg_conv, state_ssd_h, state_ssd_conv, norm_mix, w_in, rg_conv_w, rg_conv_b, rg_wa, rg_ba, rg_wx, rg_bx, rg_lam, w_rg_out, ssd_conv_w, ssd_conv_b, ssd_dt_bias, ssd_a_log, ssd_d, ssd_norm, w_ssd_out, w_out, norm_ffn, peer_wq, peer_keys, peer_u, peer_v, norm_final):
    depth = w_in.shape[0]
    B, L, D = x_prompt.shape
    Bs, Ls, _ = x_sample.shape
    assert depth == 1 and Ls == SAMPLE_ROWS // 2 and rg_conv_w.shape[1] == 4 and L >= 3
    W = rg_conv_w.shape[2]
    H = ssd_a_log.shape[1]
    inner = ssd_norm.shape[1]
    CD = ssd_conv_w.shape[2]
    P = inner // H
    N = state_ssd_h.shape[-1]
    G = (CD - inner) // (2 * N)
    R = H // G
    dims = (G, R, P, N)
    assert W == D and inner == 2 * D and CD == 3 * D and H % SUBLANES == 0 and H <= LANES
    o_x, o_g, o_z, o_xbc, o_dt, o_m = 0, W, 2 * W, 2 * W + inner, 2 * W + inner + CD, 2 * W + inner + CD + H
    col_xbc, col_x, col_z, col_g, col_ga, col_gb = 0, CD // W, (CD + W) // inner, (CD + W + inner) // W, \
        (CD + 2 * W + inner) // D, (CD + 2 * W + inner + D) // D

    yp = x_prompt.reshape(B * L, D)
    ys = x_sample.reshape(Bs * Ls, D)
    outs = {k: [] for k in ("prh", "prc", "psh", "psc", "srh", "src", "ssh", "ssc")}
    lead = SAMPLE_ROWS - Ls
    for l in range(depth):
        wi = w_in[l]
        w_main = jnp.concatenate([wi[:, o_xbc:o_dt], wi[:, o_x:o_g], wi[:, o_z:o_xbc], wi[:, o_g:o_z],
                                  wi[:, o_m:]], axis=1).astype(BF16)
        wdt = _pad_lanes(wi[:, o_dt:o_m])
        g_mix = norm_mix[l][None]
        rg_w = (rg_conv_w[l], rg_conv_b[l][None], rg_wa[l].astype(BF16), rg_wx[l].astype(BF16),
                rg_ba[l][None], rg_bx[l][None], rg_lam[l][None], w_rg_out[l].astype(BF16))
        ssd_w = (ssd_conv_w[l], ssd_conv_b[l][None], _pad_lanes(ssd_dt_bias[l][None]), ssd_dt_bias[l][:, None],
                 _pad_lanes(ssd_a_log[l][None]), ssd_a_log[l][:, None], jnp.repeat(ssd_d[l], P)[None],
                 ssd_norm[l][None], w_ssd_out[l].astype(BF16))
        wout = w_out[l].astype(BF16)
        nf = norm_ffn[l][None]
        wq = peer_wq[l].astype(BF16)
        keys = peer_keys[l].astype(BF16)
        u = peer_u[l].astype(BF16)
        vt = peer_v[l].astype(BF16).T
        nfin = norm_final[None]

        proj, dt, dtT = _inproj(yp, g_mix, w_main, wdt, H)
        ya, rgh = _rg_prompt(proj, B, L, col_x, col_g, *rg_w)
        yb, ssh = _ssd_prompt(proj, dt, dtT, B, L, col_xbc, col_z, dims, *ssd_w)
        p3 = proj.reshape(B, L, -1)
        outs["prh"].append(rgh.reshape(B, W))
        outs["prc"].append(p3[:, L - 3:, CD:CD + W].astype(F32))
        outs["psh"].append(ssh)
        outs["psc"].append(p3[:, L - 3:, :CD].astype(F32))
        x1, h2, h2t = _merge(yp, ya, yb, proj, col_ga, col_gb, wout, nf)
        yp = _peer(h2t, x1, u, vt, *_route(h2, wq, keys), nfin)

        xe = jnp.pad(ys.reshape(Bs, Ls, D), ((0, 0), (lead, 0), (0, 0))).reshape(Bs * SAMPLE_ROWS, D)
        proj_e, dt_e, dtT_e = _inproj(xe, g_mix, w_main, wdt, H)
        rows = lambda a3: a3.reshape(Bs * SAMPLE_ROWS, a3.shape[-1])
        rg_cpad = rows(jnp.pad(state_rg_conv[l], ((0, 0), (lead - 3, Ls), (0, 0))))
        rg_hpad = rows(jnp.pad(state_rg_h[l][:, None, :], ((0, 0), (lead - 1, Ls), (0, 0))))
        ssd_cpad = rows(jnp.pad(state_ssd_conv[l], ((0, 0), (lead - 3, Ls), (0, 0))))
        ya_e, h_e = _rg_sample(proj_e, col_x, col_g, rg_cpad, rg_hpad, *rg_w)
        yb_e, ssh_s = _ssd_sample(proj_e, dt_e, dtT_e, ssd_cpad, state_ssd_h[l], col_xbc, col_z, dims, *ssd_w)
        toks = lambda a2: a2.reshape(Bs, SAMPLE_ROWS, -1)[:, lead:].reshape(Bs * Ls, -1)
        p3 = proj_e.reshape(Bs, SAMPLE_ROWS, -1)
        outs["srh"].append(h_e.reshape(Bs, SAMPLE_ROWS, W)[:, -1])
        outs["src"].append(p3[:, SAMPLE_ROWS - 3:, CD:CD + W].astype(F32))
        outs["ssh"].append(ssh_s)
        outs["ssc"].append(p3[:, SAMPLE_ROWS - 3:, :CD].astype(F32))
        proj_s = toks(proj_e)
        x1, h2, h2t = _merge(ys, toks(ya_e), toks(yb_e), proj_s, col_ga, col_gb, wout, nf)
        ys = _peer(h2t, x1, u, vt, *_route(h2, wq, keys), nfin)

    st = lambda k: jnp.stack(outs[k])
    return (yp.reshape(B, L, D), ys.reshape(Bs, Ls, D), st("prh"), st("prc"), st("psh"), st("psc"),
            st("srh"), st("src"), st("ssh"), st("ssc"))
```

```python
import functools

import jax
import jax.numpy as jnp
from jax import lax
from jax.experimental import pallas as pl
from jax.experimental.pallas import tpu as pltpu

F32 = jnp.float32
BF16 = jnp.bfloat16
EPS = 1e-6
RG_C = 8.0
PEER_TOPK = 16
SSD_CHUNK = 128
LANES = 128
SUBLANES = 8
BF16_ROWS = 16
SAMPLE_ROWS = 8
NEG_INF = float("-inf")
LOG2E = 1.4426950408889634
HIGHEST = lax.Precision.HIGHEST
NT_DIMS = (((1,), (1,)), ((), ()))
VMEM_LIMIT = 56 * 1024 * 1024


def _cparams(sem):
    return pltpu.CompilerParams(dimension_semantics=sem, vmem_limit_bytes=VMEM_LIMIT)


def _pick(n, pref):
    t = min(n, pref)
    while n % t:
        t -= SUBLANES
    return t


def _sigmoid(x):
    return 1.0 / (1.0 + jnp.exp2(x * -LOG2E))


def _silu(x):
    return x * _sigmoid(x)


def _softplus(x):
    return jnp.maximum(x, 0.0) + jnp.log1p(jnp.exp(-jnp.abs(x)))


def _gelu(x):
    k1 = -2.0 * 0.7978845608028654 * LOG2E
    return x / (1.0 + jnp.exp2(x * (k1 + (k1 * 0.044715) * (x * x))))


def _rms(x, g):
    return x * lax.rsqrt(jnp.mean(x * x, axis=-1, keepdims=True) + EPS) * g


def _dot(a, b):
    return jnp.dot(a, b, preferred_element_type=F32)


def _dot_nt(a, b):
    return lax.dot_general(a, b, NT_DIMS, preferred_element_type=F32)


def _inproj_body(x_ref, g_ref, w_ref, wdt_ref, o_ref, odt_ref, odtT_ref, xn_ref):
    @pl.when(pl.program_id(1) == 0)
    def _():
        xn = _rms(x_ref[...], g_ref[...])
        xn_ref[...] = xn.astype(BF16)
        dt = jnp.dot(xn, wdt_ref[...], precision=HIGHEST, preferred_element_type=F32)
        odt_ref[...] = dt
        odtT_ref[...] = jnp.transpose(dt)[:odtT_ref.shape[0], :]

    o_ref[...] = _dot(xn_ref[...], w_ref[...])


def _inproj(x2d, g, w_main, wdt, H):
    T, D = x2d.shape
    N = w_main.shape[1]
    TM = _pick(T, 1024)
    TN = 1536
    assert N % TN == 0 and TM % LANES == 0
    return pl.pallas_call(
        _inproj_body,
        grid=(T // TM, N // TN),
        in_specs=[
            pl.BlockSpec((TM, D), lambda i, j: (i, 0)),
            pl.BlockSpec((1, D), lambda i, j: (0, 0)),
            pl.BlockSpec((D, TN), lambda i, j: (0, j)),
            pl.BlockSpec((D, LANES), lambda i, j: (0, 0)),
        ],
        out_specs=[
            pl.BlockSpec((TM, TN), lambda i, j: (i, j)),
            pl.BlockSpec((TM, LANES), lambda i, j: (i, 0)),
            pl.BlockSpec((H, TM), lambda i, j: (0, i)),
        ],
        out_shape=[
            jax.ShapeDtypeStruct((T, N), F32),
            jax.ShapeDtypeStruct((T, LANES), F32),
            jax.ShapeDtypeStruct((H, T), F32),
        ],
        scratch_shapes=[pltpu.VMEM((TM, D), BF16)],
        compiler_params=_cparams(("parallel", "arbitrary")),
        name="inproj",
    )(x2d, g, w_main, wdt)


def _conv_slab(x, pad, cw_ref, cb_ref, ls):
    L = x.shape[0]
    pad[SUBLANES:SUBLANES + L, :] = x
    out = (cb_ref[:, ls]
           + cw_ref[3:4, ls] * pad[8:8 + L, :]
           + cw_ref[2:3, ls] * pad[7:7 + L, :]
           + cw_ref[1:2, ls] * pad[6:6 + L, :]
           + cw_ref[0:1, ls] * pad[5:5 + L, :])
    pad[0:SUBLANES, :] = pad[L:L + SUBLANES, :]
    return out


def _rg_gates(xc, wa_ref, wx_ref, ba, bx, lam):
    xb = xc.astype(BF16)
    nb, bw = wa_ref.shape[0], wa_ref.shape[1]
    rs, gs = [], []
    for k in range(nb):
        xk = xb[:, k * bw:(k + 1) * bw]
        rs.append(_dot(xk, wa_ref[k]))
        gs.append(_dot(xk, wx_ref[k]))
    r = _sigmoid(jnp.concatenate(rs, axis=1) + ba)
    i = _sigmoid(jnp.concatenate(gs, axis=1) + bx)
    log_a = -RG_C * r * _softplus(-lam)
    a = jnp.exp(log_a)
    em1 = jnp.tanh(log_a) * (a * a + 1.0)
    return a, jnp.sqrt(-em1) * (i * xc)


def _rg_prompt_body(x_ref, gate_ref, cw_ref, cb_ref, wa_ref, wx_ref, ba_ref, bx_ref, lam_ref, wo_ref,
                    ya_ref, hfin_ref, pad_s, a_s, b_s, h_s):
    B, Lc, W = x_ref.shape
    NS = W // LANES
    pitch = a_s.shape[1] // B

    @pl.when(pl.program_id(0) == 0)
    def _():
        pad_s[:, :, 0:SUBLANES, :] = jnp.zeros((B, NS, SUBLANES, LANES), F32)
        h_s[...] = jnp.zeros(h_s.shape, F32)

    for b in range(B):
        xc = jnp.concatenate(
            [_conv_slab(x_ref[b, :, s * LANES:(s + 1) * LANES], pad_s.at[b, s], cw_ref, cb_ref,
                        slice(s * LANES, (s + 1) * LANES)) for s in range(NS)], axis=1)
        a, bt = _rg_gates(xc, wa_ref, wx_ref, ba_ref[...], bx_ref[...], lam_ref[...])
        for s in range(NS):
            a_s[s, b * pitch:b * pitch + Lc, :] = a[:, s * LANES:(s + 1) * LANES]
            b_s[s, b * pitch:b * pitch + Lc, :] = bt[:, s * LANES:(s + 1) * LANES]

    def step(t, hs):
        out = []
        for s in range(NS):
            rows = pl.ds(t, B, stride=pitch)
            h = a_s[s, rows, :] * hs[s] + b_s[s, rows, :]
            b_s[s, rows, :] = h
            out.append(h)
        return tuple(out)

    hs = lax.fori_loop(0, Lc, step, tuple(h_s[:, s * LANES:(s + 1) * LANES] for s in range(NS)), unroll=4)
    for s in range(NS):
        h_s[:, s * LANES:(s + 1) * LANES] = hs[s]
    hfin_ref[...] = h_s[...]
    for b in range(B):
        hb = jnp.concatenate([b_s[s, b * pitch:b * pitch + Lc, :] for s in range(NS)], axis=1)
        ya_ref[b] = _dot((hb * _gelu(gate_ref[b])).astype(BF16), wo_ref[...])


def _rg_prompt(proj, B, L, col_x, col_g, cw, cb, wa, wx, ba, bx, lam, wo):
    W = cw.shape[1]
    Lc = _pick(L, 128)
    pitch = Lc + SUBLANES // 2
    assert B == SUBLANES and W % LANES == 0
    proj3 = proj.reshape(B, L, proj.shape[1])
    full = lambda *s: pl.BlockSpec(s, lambda c: (0,) * len(s))
    ya, hfin = pl.pallas_call(
        _rg_prompt_body,
        grid=(L // Lc,),
        in_specs=[
            pl.BlockSpec((B, Lc, W), lambda c: (0, c, col_x)),
            pl.BlockSpec((B, Lc, W), lambda c: (0, c, col_g)),
            full(4, W), full(1, W), full(*wa.shape), full(*wx.shape), full(1, W), full(1, W), full(1, W),
            full(*wo.shape),
        ],
        out_specs=[
            pl.BlockSpec((B, Lc, wo.shape[1]), lambda c: (0, c, 0)),
            pl.BlockSpec((B, W), lambda c: (0, 0)),
        ],
        out_shape=[
            jax.ShapeDtypeStruct((B, L, wo.shape[1]), F32),
            jax.ShapeDtypeStruct((B, W), F32),
        ],
        scratch_shapes=[
            pltpu.VMEM((B, W // LANES, Lc + SUBLANES, LANES), F32),
            pltpu.VMEM((W // LANES, B * pitch, LANES), F32),
            pltpu.VMEM((W // LANES, B * pitch, LANES), F32),
            pltpu.VMEM((B, W), F32),
        ],
        compiler_params=_cparams(("arbitrary",)),
        name="rg_prompt",
    )(proj3, proj3, cw, cb, wa, wx, ba, bx, lam, wo)
    return ya.reshape(B * L, wo.shape[1]), hfin


def _conv_rolled(u, cw_ref, cb_ref):
    return (cb_ref[...]
            + cw_ref[3:4, :] * u
            + cw_ref[2:3, :] * pltpu.roll(u, 1, axis=0)
            + cw_ref[1:2, :] * pltpu.roll(u, 2, axis=0)
            + cw_ref[0:1, :] * pltpu.roll(u, 3, axis=0))


def _rg_sample_body(x_ref, gate_ref, cpad_ref, hpad_ref, cw_ref, cb_ref, wa_ref, wx_ref, ba_ref, bx_ref,
                    lam_ref, wo_ref, ya_ref, h_ref):
    xc = _conv_rolled(x_ref[...] + cpad_ref[...], cw_ref, cb_ref)
    a, bt = _rg_gates(xc, wa_ref, wx_ref, ba_ref[...], bx_ref[...], lam_ref[...])
    row = lax.broadcasted_iota(jnp.int32, a.shape, 0) & (SAMPLE_ROWS - 1)
    h = hpad_ref[...]
    for k in range(SAMPLE_ROWS // 2, SAMPLE_ROWS):
        h = jnp.where(row == k, a * pltpu.roll(h, 1, axis=0) + bt, h)
    h_ref[...] = h
    ya_ref[...] = _dot((h * _gelu(gate_ref[...])).astype(BF16), wo_ref[...])


def _rg_sample(proj, col_x, col_g, cpad, hpad, cw, cb, wa, wx, ba, bx, lam, wo):
    T = proj.shape[0]
    W = cw.shape[1]
    TM = _pick(T, 256)
    full = lambda *s: pl.BlockSpec(s, lambda i: (0,) * len(s))
    return pl.pallas_call(
        _rg_sample_body,
        grid=(T // TM,),
        in_specs=[
            pl.BlockSpec((TM, W), lambda i: (i, col_x)),
            pl.BlockSpec((TM, W), lambda i: (i, col_g)),
            pl.BlockSpec((TM, W), lambda i: (i, 0)),
            pl.BlockSpec((TM, W), lambda i: (i, 0)),
            full(4, W), full(1, W), full(*wa.shape), full(*wx.shape), full(1, W), full(1, W), full(1, W),
            full(*wo.shape),
        ],
        out_specs=[
            pl.BlockSpec((TM, wo.shape[1]), lambda i: (i, 0)),
            pl.BlockSpec((TM, W), lambda i: (i, 0)),
        ],
        out_shape=[
            jax.ShapeDtypeStruct((T, wo.shape[1]), F32),
            jax.ShapeDtypeStruct((T, W), F32),
        ],
        compiler_params=_cparams(("parallel",)),
        name="rg_sample",
    )(proj, proj, cpad, hpad, cw, cb, wa, wx, ba, bx, lam, wo)


def _ssd_chunk(act_s, dt, dtT, alog_ref, alogT_ref, dexp_ref, seq_shift, G, R, P, N,
               yint_fn, y_s, xw_s):
    Q = act_s.shape[0]
    inner = G * R * P
    a = dt * (-jnp.exp(alog_ref[...]))
    aT = dtT * (-jnp.exp(alogT_ref[...]))
    ri = lax.broadcasted_iota(jnp.int32, (Q, Q), 0)
    ci = lax.broadcasted_iota(jnp.int32, (Q, Q), 1)
    same = (ri >> seq_shift) == (ci >> seq_shift)
    causal = same & (ci <= ri)
    causal_f = jnp.where(causal, 1.0, 0.0).astype(F32)
    same_f = jnp.where(same, 1.0, 0.0).astype(F32)
    acum = jnp.dot(causal_f, a, precision=HIGHEST, preferred_element_type=F32)
    tot = jnp.dot(same_f, a, precision=HIGHEST, preferred_element_type=F32)
    acumT = lax.dot_general(aT, causal_f, NT_DIMS, precision=HIGHEST,
                            preferred_element_type=F32)
    eacum = jnp.exp(acum)
    toend = jnp.exp(tot - acum) * dt
    for g in range(G):
        Bg = act_s[:, inner + g * N:inner + (g + 1) * N].astype(BF16)
        Cg = act_s[:, inner + (G + g) * N:inner + (G + g + 1) * N].astype(BF16)
        cb = _dot_nt(Cg, Bg)
        yint = yint_fn(g, Cg)
        for r in range(R):
            h = g * R + r
            sl = slice(h * P, (h + 1) * P)
            seg = acum[:, h:h + 1] - acumT[h:h + 1, :]
            decay = jnp.exp(jnp.where(causal, seg, NEG_INF))
            wgt = (cb * decay * dtT[h:h + 1, :]).astype(BF16)
            xh = act_s[:, sl]
            yh = _dot(wgt, xh.astype(BF16)) + dexp_ref[:, sl] * xh
            if yint is not None:
                yh = yh + yint[:, r * P:(r + 1) * P] * eacum[:, h:h + 1]
            y_s[:, sl] = yh
            xw_s[:, sl] = xh * toend[:, h:h + 1]
    return tot, eacum


def _expand_heads(x, ex_ref):
    hi = x.astype(BF16)
    r1 = x - hi.astype(F32)
    mid = r1.astype(BF16)
    lo = (r1 - mid.astype(F32)).astype(BF16)
    e = ex_ref[...]
    return _dot(hi, e) + _dot(mid, e) + _dot(lo, e)


def _ssd_finish(y, z, nw_ref, wo_ref):
    y = y * _silu(z)
    return _dot(_rms(y, nw_ref[...]).astype(BF16), wo_ref[...])


def _ssd_prompt_body(G, R, P, N,
                     xbc_ref, z_ref, dt_ref, dtT_ref, cw_ref, cb_ref, dtb_ref, dtbT_ref, alog_ref, alogT_ref,
                     dexp_ref, nw_ref, wo_ref, ex_ref, yb_ref, st_ref, pad_s, act_s, y_s, stT_s):
    Q = xbc_ref.shape[0]
    inner = G * R * P
    GW = R * P

    @pl.when(pl.program_id(1) == 0)
    def _():
        pad_s[:, 0:SUBLANES, :] = jnp.zeros((pad_s.shape[0], SUBLANES, LANES), F32)
        stT_s[...] = jnp.zeros(stT_s.shape, F32)

    for s in range(pad_s.shape[0]):
        ls = slice(s * LANES, (s + 1) * LANES)
        act_s[:, ls] = _silu(_conv_slab(xbc_ref[:, ls], pad_s.at[s], cw_ref, cb_ref, ls))
    dt = _softplus(dt_ref[...] + dtb_ref[...])
    dtT = _softplus(dtT_ref[...] + dtbT_ref[...])
    a = dt * (-jnp.exp(alog_ref[...]))
    aT = dtT * (-jnp.exp(alogT_ref[...]))
    ri = lax.broadcasted_iota(jnp.int32, (Q, Q), 0)
    ci = lax.broadcasted_iota(jnp.int32, (Q, Q), 1)
    causal = ci <= ri
    causal_f = jnp.where(causal, 1.0, 0.0).astype(F32)
    hi = dict(precision=HIGHEST, preferred_element_type=F32)
    acum = jnp.dot(causal_f, a, **hi)
    tot = jnp.dot(jnp.ones((SUBLANES, Q), F32), a, **hi)
    acumT = lax.dot_general(aT, causal_f, NT_DIMS, **hi)
    fx = _expand_heads(jnp.concatenate([jnp.exp(acum), jnp.exp(tot[0:1, :] - acum) * dt, jnp.exp(tot)], axis=0),
                       ex_ref)
    eacum_x, toend_x, dec_x = fx[0:Q], fx[Q:2 * Q], fx[2 * Q:2 * Q + 1]
    col2 = acum * LOG2E
    row2 = (acumT - jnp.log(dtT)) * LOG2E
    lane = lax.broadcasted_iota(jnp.int32, (Q, LANES), 1)
    for g in range(G):
        gs = slice(g * GW, (g + 1) * GW)
        Bf = act_s[:, inner + g * N:inner + (g + 1) * N]
        Bg = Bf.astype(BF16)
        Cg = act_s[:, inner + (G + g) * N:inner + (G + g + 1) * N].astype(BF16)
        cb = jnp.where(causal, _dot_nt(Cg, Bg), 0.0)
        y_s[:, gs] = (_dot(Cg, stT_s[:, gs].astype(BF16)) * eacum_x[:, gs]
                      + dexp_ref[:, gs] * act_s[:, gs])
        for pr in range(R // 2):
            h0 = g * R + 2 * pr
            ps = slice(h0 * P, (h0 + 2) * P)
            wg = []
            for h in (h0, h0 + 1):
                e = jnp.exp2(jnp.where(causal, col2[:, h:h + 1] - row2[h:h + 1, :], NEG_INF))
                wg.append((cb * e).astype(BF16))
            xp = act_s[:, ps]
            xa = jnp.where(lane < P, xp, 0.0).astype(BF16)
            xb = jnp.where(lane < P, 0.0, xp).astype(BF16)
            y_s[:, ps] = y_s[:, ps] + _dot(jnp.concatenate(wg, axis=1), jnp.concatenate([xa, xb], axis=0))
        xw = (act_s[:, gs] * toend_x[:, gs]).astype(BF16)
        ST = _dot(jnp.transpose(Bf).astype(BF16), xw)
        stT_s[:, gs] = dec_x[:, gs] * stT_s[:, gs] + ST
    yb_ref[...] = _ssd_finish(y_s[...], z_ref[...], nw_ref, wo_ref)

    @pl.when(pl.program_id(1) == pl.num_programs(1) - 1)
    def _():
        for g in range(G):
            st_ref[0, g * R:(g + 1) * R] = jnp.transpose(stT_s[:, g * GW:(g + 1) * GW]).reshape(R, P, N)


def _ssd_prompt(proj, dt, dtT, B, L, col_xbc, col_z, dims, cw, cb, dtb, dtbT, alog, alogT, dexp, nw, wo):
    G, R, P, N = dims
    H = G * R
    inner = H * P
    CD = cw.shape[1]
    Q = _pick(L, SSD_CHUNK)
    nC = L // Q
    assert 2 * P == LANES and R % 2 == 0 and N == LANES
    head_of_lane = jnp.arange(inner, dtype=jnp.int32) // P
    ex = (jnp.arange(LANES, dtype=jnp.int32)[:, None] == head_of_lane[None, :]).astype(BF16)
    full = lambda *s: pl.BlockSpec(s, lambda b, c: (0,) * len(s))
    return pl.pallas_call(
        functools.partial(_ssd_prompt_body, G, R, P, N),
        grid=(B, nC),
        in_specs=[
            pl.BlockSpec((Q, CD), lambda b, c: (b * nC + c, col_xbc)),
            pl.BlockSpec((Q, inner), lambda b, c: (b * nC + c, col_z)),
            pl.BlockSpec((Q, LANES), lambda b, c: (b * nC + c, 0)),
            pl.BlockSpec((H, Q), lambda b, c: (0, b * nC + c)),
            full(4, CD), full(1, CD), full(1, LANES), full(H, 1), full(1, LANES), full(H, 1),
            full(1, inner), full(1, inner), full(*wo.shape), full(LANES, inner),
        ],
        out_specs=[
            pl.BlockSpec((Q, wo.shape[1]), lambda b, c: (b * nC + c, 0)),
            pl.BlockSpec((1, H, P, N), lambda b, c: (b, 0, 0, 0)),
        ],
        out_shape=[
            jax.ShapeDtypeStruct((B * L, wo.shape[1]), F32),
            jax.ShapeDtypeStruct((B, H, P, N), F32),
        ],
        scratch_shapes=[
            pltpu.VMEM((CD // LANES, Q + SUBLANES, LANES), F32),
            pltpu.VMEM((Q, CD), F32),
            pltpu.VMEM((Q, inner), F32),
            pltpu.VMEM((N, inner), F32),
        ],
        compiler_params=_cparams(("parallel", "arbitrary")),
        name="ssd_prompt",
    )(proj, proj, dt, dtT, cw, cb, dtb, dtbT, alog, alogT, dexp, nw, wo, ex)


def _ssd_sample_body(G, R, P, N,
                     xbc_ref, cpad_ref, z_ref, dt_ref, dtT_ref, h0_ref, cw_ref, cb_ref, dtb_ref, dtbT_ref,
                     alog_ref, alogT_ref, dexp_ref, nw_ref, wo_ref, yb_ref, st_ref,
                     act_s, y_s, xw_s, yint_s, xwT_s, tot_s, eacum_s):
    Q = xbc_ref.shape[0]
    inner = G * R * P
    bi = pl.program_id(1)
    half = SAMPLE_ROWS // 2

    @pl.when(bi == 0)
    def _():
        act_s[...] = _silu(_conv_rolled(xbc_ref[...] + cpad_ref[...], cw_ref, cb_ref))
        rows = lax.broadcasted_iota(jnp.int32, (Q, LANES), 0) & (SAMPLE_ROWS - 1)
        cols = lax.broadcasted_iota(jnp.int32, dtT_ref.shape, 1) & (SAMPLE_ROWS - 1)
        dt = jnp.where(rows >= half, _softplus(dt_ref[...] + dtb_ref[...]), 0.0)
        dtT = jnp.where(cols >= half, _softplus(dtT_ref[...] + dtbT_ref[...]), 0.0)
        seq_shift = SAMPLE_ROWS.bit_length() - 1
        tot, eacum = _ssd_chunk(act_s, dt, dtT, alog_ref, alogT_ref, dexp_ref, seq_shift, G, R, P, N,
                                lambda g, Cg: None, y_s, xw_s)
        tot_s[...] = tot
        eacum_s[...] = eacum
        for g in range(G):
            xwT_s[g] = jnp.transpose(xw_s[:, g * R * P:(g + 1) * R * P]).astype(BF16)

    r0 = pl.multiple_of(bi * SAMPLE_ROWS, SAMPLE_ROWS)
    rid = lax.broadcasted_iota(jnp.int32, (Q, N), 0)
    mine = (rid >= r0) & (rid < r0 + SAMPLE_ROWS)
    dec = jnp.exp(tot_s[pl.ds(r0, 1), :])
    for g in range(G):
        Cb = act_s[pl.ds(r0, SAMPLE_ROWS), inner + (G + g) * N:inner + (G + g + 1) * N]
        stg = h0_ref[0, g * R:(g + 1) * R].reshape(R * P, N)
        yint_s[pl.ds(r0, SAMPLE_ROWS), g * R * P:(g + 1) * R * P] = _dot_nt(Cb, stg)
        Bg = jnp.where(mine, act_s[:, inner + g * N:inner + (g + 1) * N], 0.0).astype(BF16)
        S = _dot(xwT_s[g], Bg)
        for r in range(R):
            h = g * R + r
            st_ref[0, h] = dec[:, h:h + 1] * h0_ref[0, h] + S[r * P:(r + 1) * P, :]

    @pl.when(bi == pl.num_programs(1) - 1)
    def _():
        for h in range(G * R):
            sl = slice(h * P, (h + 1) * P)
            y_s[:, sl] = y_s[:, sl] + yint_s[:, sl] * eacum_s[:, h:h + 1]
        yb_ref[...] = _ssd_finish(y_s[...], z_ref[...], nw_ref, wo_ref)


def _ssd_sample(proj, dt, dtT, cpad, h0, col_xbc, col_z, dims, cw, cb, dtb, dtbT, alog, alogT, dexp, nw, wo):
    G, R, P, N = dims
    H = G * R
    inner = H * P
    CD = cw.shape[1]
    T = proj.shape[0]
    Bs = h0.shape[0]
    Q = _pick(T, SSD_CHUNK)
    nb = Q // SAMPLE_ROWS
    full = lambda *s: pl.BlockSpec(s, lambda i, j: (0,) * len(s))
    return pl.pallas_call(
        functools.partial(_ssd_sample_body, G, R, P, N),
        grid=(T // Q, nb),
        in_specs=[
            pl.BlockSpec((Q, CD), lambda i, j: (i, col_xbc)),
            pl.BlockSpec((Q, CD), lambda i, j: (i, 0)),
            pl.BlockSpec((Q, inner), lambda i, j: (i, col_z)),
            pl.BlockSpec((Q, LANES), lambda i, j: (i, 0)),
            pl.BlockSpec((H, Q), lambda i, j: (0, i)),
            pl.BlockSpec((1, H, P, N), lambda i, j: (i * nb + j, 0, 0, 0)),
            full(4, CD), full(1, CD), full(1, LANES), full(H, 1), full(1, LANES), full(H, 1),
            full(1, inner), full(1, inner), full(*wo.shape),
        ],
        out_specs=[
            pl.BlockSpec((Q, wo.shape[1]), lambda i, j: (i, 0)),
            pl.BlockSpec((1, H, P, N), lambda i, j: (i * nb + j, 0, 0, 0)),
        ],
        out_shape=[
            jax.ShapeDtypeStruct((T, wo.shape[1]), F32),
            jax.ShapeDtypeStruct((Bs, H, P, N), F32),
        ],
        scratch_shapes=[
            pltpu.VMEM((Q, CD), F32),
            pltpu.VMEM((Q, inner), F32),
            pltpu.VMEM((Q, inner), F32),
            pltpu.VMEM((Q, inner), F32),
            pltpu.VMEM((G, R * P, Q), BF16),
            pltpu.VMEM((Q, LANES), F32),
            pltpu.VMEM((Q, LANES), F32),
        ],
        compiler_params=_cparams(("parallel", "arbitrary")),
        name="ssd_sample",
    )(proj, cpad, proj, dt, dtT, h0, cw, cb, dtb, dtbT, alog, alogT, dexp, nw, wo)


def _merge_body(x_ref, ya_ref, yb_ref, ga_ref, gb_ref, wout_ref, nf_ref, x1_ref, h2t_ref):
    m = _sigmoid(ga_ref[...]) * ya_ref[...] + _sigmoid(gb_ref[...]) * yb_ref[...]
    x1 = x_ref[...] + _dot(m.astype(BF16), wout_ref[...])
    x1_ref[...] = x1
    h2t_ref[...] = jnp.transpose(_rms(x1, nf_ref[...])).astype(BF16)


def _merge(x2d, ya, yb, proj, col_ga, col_gb, wout, nf):
    T, D = x2d.shape
    TM = _pick(T, 512)
    row = lambda c: pl.BlockSpec((TM, D), lambda i: (i, c))
    return pl.pallas_call(
        _merge_body,
        grid=(T // TM,),
        in_specs=[row(0), row(0), row(0), row(col_ga), row(col_gb),
                  pl.BlockSpec(wout.shape, lambda i: (0, 0)), pl.BlockSpec((1, D), lambda i: (0, 0))],
        out_specs=[row(0), pl.BlockSpec((D, TM), lambda i: (0, i))],
        out_shape=[jax.ShapeDtypeStruct((T, D), F32), jax.ShapeDtypeStruct((D, T), BF16)],
        compiler_params=_cparams(("parallel",)),
        name="merge",
    )(x2d, ya, yb, proj, proj, wout, nf)


def _staircase(k):
    return [(ka, k // (ka + 1)) for ka in range(k)]


def _take_top(cur, k_top, want_rank):
    rank = jnp.full(cur.shape, float(k_top), F32) if want_rank else None
    vals = []
    for k in range(k_top):
        m = jnp.max(cur, axis=0, keepdims=True)
        hit = cur == m
        vals.append(m)
        if want_rank:
            rank = jnp.where(hit, float(k), rank)
        cur = jnp.where(hit, NEG_INF, cur)
    taken = jnp.sum(jnp.where(cur == NEG_INF, 1.0, 0.0), axis=0, keepdims=True)
    return vals, rank, taken


def _take_top_ties(x, k_top, val_ref, rank_ref):
    rows = lax.broadcasted_iota(jnp.int32, x.shape, 0)
    kk = lax.broadcasted_iota(jnp.int32, (k_top, x.shape[1]), 0)

    def body(k, carry):
        cur, rank, vals = carry
        m = jnp.max(cur, axis=0, keepdims=True)
        first = jnp.min(jnp.where(cur == m, rows, x.shape[0]), axis=0, keepdims=True)
        hit = rows == first
        return (jnp.where(hit, NEG_INF, cur), jnp.where(hit, k.astype(F32), rank),
                jnp.where(kk == k, m, vals))

    init = (x, jnp.full(x.shape, float(k_top), F32), jnp.zeros((k_top, x.shape[1]), F32))
    _, rank, vals = lax.fori_loop(0, k_top, body, init)
    val_ref[...] = vals
    rank_ref[...] = rank


def _route_body(h2t_ref, wqt_ref, keys_ref, na_ref, ea_ref, rb_ref, eb_ref, sv_s, rk_s, cand_s, cv_s, cr_s):
    NH = keys_ref.shape[0]
    NK, KH = keys_ref.shape[2], keys_ref.shape[3]
    K = PEER_TOPK
    TB = h2t_ref.shape[1]
    tiles = [slice(tl * LANES, (tl + 1) * LANES) for tl in range(TB // LANES)]
    qt = _dot(wqt_ref[...], h2t_ref[...]).astype(BF16)
    cand_s[...] = jnp.full(cand_s.shape, NEG_INF, F32)

    def candidates():
        off = 0
        for ka, nb in _staircase(K):
            cand_s[off:off + nb, :] = sv_s[0, ka:ka + 1, :] + sv_s[1, 0:nb, :]
            off += nb

    def emit(h, sT, in_top, rb, sel, na_of):
        top = sv_s[0, 0:1, :] + sv_s[1, 0:1, :]
        z = jnp.sum(jnp.where(sel, jnp.exp(cand_s[...] - top), 0.0), axis=0, keepdims=True)
        ex = [jnp.where(in_top[s], jnp.exp(sT[s] - sv_s[s, 0:1, :]), 0.0) for s in range(2)]
        na = jnp.zeros(sT[0].shape, F32)
        off = 0
        for ka, nb in _staircase(K):
            cnt = jnp.sum(jnp.where(sel[off:off + nb, :], 1.0, 0.0), axis=0, keepdims=True)
            na = jnp.where(na_of(ka), cnt, na)
            off += nb
        na_ref[h * NK:(h + 1) * NK, :] = na
        ea_ref[h * NK:(h + 1) * NK, :] = ex[0] / z
        rb_ref[h] = rb.astype(BF16)
        eb_ref[h] = ex[1].astype(BF16)

    def scores(h):
        return [_dot(keys_ref[h, s], qt[(h * 2 + s) * KH:(h * 2 + s + 1) * KH, :]) for s in range(2)]

    most = []
    for h in range(NH):
        sT = scores(h)
        seen = jnp.zeros((1, LANES), F32)
        rbs = []
        for s in range(2):
            for ln in tiles:
                vals, rank, taken = _take_top(sT[s][:, ln], K, s == 1)
                for k in range(K):
                    sv_s[s, k:k + 1, ln] = vals[k]
                seen = jnp.maximum(seen, taken)
                if s == 1:
                    rbs.append(rank)
        candidates()
        cand = cand_s[...]
        tau = _take_top(cand, K, False)[0][K - 1]
        sel = cand >= tau
        most.append(jnp.maximum(seen, jnp.max(jnp.sum(jnp.where(sel, 1.0, 0.0), axis=0, keepdims=True),
                                              axis=1, keepdims=True)))
        emit(h, sT, [sT[s] >= sv_s[s, K - 1:K, :] for s in range(2)], jnp.concatenate(rbs, axis=1), sel,
             lambda ka: sT[0] == sv_s[0, ka:ka + 1, :])

    for h in range(NH):
        @pl.when(jnp.max(most[h]) > K)
        def _(h=h):
            sT = scores(h)
            for s in range(2):
                for ln in tiles:
                    _take_top_ties(sT[s][:, ln], K, sv_s.at[s, :, ln], rk_s.at[s, :, ln])
            candidates()
            for ln in tiles:
                _take_top_ties(cand_s[:, ln], K, cv_s.at[:, ln], cr_s.at[:, ln])
            emit(h, sT, [rk_s[s] < K for s in range(2)], rk_s[1], cr_s[...] < K,
                 lambda ka: rk_s[0] == float(ka))


def _route(h2t, wqt, keys):
    D, T = h2t.shape
    NH, _, NK, KH = keys.shape
    TB = _pick(T, 512)
    assert TB % LANES == 0
    ncand = sum(nb for _, nb in _staircase(PEER_TOPK))
    ncand_pad = -(-ncand // SUBLANES) * SUBLANES
    flat = pl.BlockSpec((NH * NK, TB), lambda i: (0, i))
    tok = pl.BlockSpec((NH, NK, TB), lambda i: (0, 0, i))
    return pl.pallas_call(
        _route_body,
        grid=(T // TB,),
        in_specs=[pl.BlockSpec((D, TB), lambda i: (0, i)),
                  pl.BlockSpec(wqt.shape, lambda i: (0, 0)),
                  pl.BlockSpec(keys.shape, lambda i: (0, 0, 0, 0))],
        out_specs=[flat, flat, tok, tok],
        out_shape=[jax.ShapeDtypeStruct((NH * NK, T), F32), jax.ShapeDtypeStruct((NH * NK, T), F32),
                   jax.ShapeDtypeStruct((NH, NK, T), BF16), jax.ShapeDtypeStruct((NH, NK, T), BF16)],
        scratch_shapes=[pltpu.VMEM((2, PEER_TOPK, TB), F32), pltpu.VMEM((2, NK, TB), F32),
                        pltpu.VMEM((ncand_pad, TB), F32), pltpu.VMEM((PEER_TOPK, TB), F32),
                        pltpu.VMEM((ncand_pad, TB), F32)],
        compiler_params=_cparams(("parallel",)),
        name="peer_route",
    )(h2t, wqt, keys)


def _peer_body(h2t_ref, x1_ref, u_ref, vt_ref, na_ref, ea_ref, rb_ref, eb_ref, nfin_ref, y_ref,
               s_s, a_s, acc_s):
    ec = pl.program_id(1)
    NH, NK, TB = rb_ref.shape
    ni = u_ref.shape[0] // NK

    @pl.when(ec == 0)
    def _():
        acc_s[...] = jnp.zeros(acc_s.shape, F32)

    s_s[...] = _dot(u_ref[...], h2t_ref[...])

    def row_tile(grp, il):
        return jnp.broadcast_to(grp[il:il + 1, :], (BF16_ROWS, LANES)).astype(BF16)

    for tl in range(TB // LANES):
        ln = slice(tl * LANES, (tl + 1) * LANES)
        rows = [pl.ds(pl.multiple_of(h * NK + ec * ni, SUBLANES), ni) for h in range(NH)]
        na = [na_ref[rows[h], ln] for h in range(NH)]
        ea = [ea_ref[rows[h], ln] for h in range(NH)]
        for il in range(ni):
            na_t = [row_tile(na[h], il) for h in range(NH)]
            ea_t = [row_tile(ea[h], il) for h in range(NH)]
            for jt in range(NK // BF16_ROWS):
                js = slice(jt * BF16_ROWS, (jt + 1) * BF16_ROWS)
                w = jnp.zeros((BF16_ROWS, LANES), BF16)
                for h in range(NH):
                    w = w + ea_t[h] * jnp.where(rb_ref[h, js, ln] < na_t[h], eb_ref[h, js, ln], 0.0)
                e = slice(il * NK + jt * BF16_ROWS, il * NK + (jt + 1) * BF16_ROWS)
                a_s[e, ln] = _gelu(s_s[e, ln]).astype(BF16) * w
    acc_s[...] += _dot(vt_ref[...], a_s[...])

    @pl.when(ec == pl.num_programs(1) - 1)
    def _():
        x2 = x1_ref[...] + jnp.transpose(acc_s[...])
        y_ref[...] = _rms(x2, nfin_ref[...])


def _peer(h2t, x1, u, vt, na, ea, rb, eb, nfin):
    D, T = h2t.shape
    E = u.shape[0]
    NH, NK, _ = rb.shape
    TB = _pick(T, 512)
    EC = 2 * SUBLANES * NK
    assert TB % LANES == 0 and E % EC == 0 and NK % BF16_ROWS == 0
    tok = pl.BlockSpec((NH, NK, TB), lambda i, j: (0, 0, i))
    flat = pl.BlockSpec((NH * NK, TB), lambda i, j: (0, i))
    return pl.pallas_call(
        _peer_body,
        grid=(T // TB, E // EC),
        in_specs=[pl.BlockSpec((D, TB), lambda i, j: (0, i)),
                  pl.BlockSpec((TB, D), lambda i, j: (i, 0)),
                  pl.BlockSpec((EC, D), lambda i, j: (j, 0)),
                  pl.BlockSpec((D, EC), lambda i, j: (0, j)),
                  flat, flat, tok, tok,
                  pl.BlockSpec((1, D), lambda i, j: (0, 0))],
        out_specs=pl.BlockSpec((TB, D), lambda i, j: (i, 0)),
        out_shape=jax.ShapeDtypeStruct((T, D), F32),
        scratch_shapes=[pltpu.VMEM((EC, TB), F32), pltpu.VMEM((EC, TB), BF16), pltpu.VMEM((D, TB), F32)],
        compiler_params=_cparams(("parallel", "arbitrary")),
        name="peer_mix",
    )(h2t, x1, u, vt, na, ea, rb, eb, nfin)


def _pad_lanes(row):
    return jnp.pad(row, ((0, 0), (0, LANES - row.shape[1])))


def kernel(x_prompt, x_sample, state_rg_h, state_rg_conv, state_ssd_h, state_ssd_conv, norm_mix, w_in, rg_conv_w, rg_conv_b, rg_wa, rg_ba, rg_wx, rg_bx, rg_lam, w_rg_out, ssd_conv_w, ssd_conv_b, ssd_dt_bias, ssd_a_log, ssd_d, ssd_norm, w_ssd_out, w_out, norm_ffn, peer_wq, peer_keys, peer_u, peer_v, norm_final):
    depth = w_in.shape[0]
    B, L, D = x_prompt.shape
    Bs, Ls, _ = x_sample.shape
    assert depth == 1 and Ls == SAMPLE_ROWS // 2 and rg_conv_w.shape[1] == 4 and L >= 3
    W = rg_conv_w.shape[2]
    H = ssd_a_log.shape[1]
    inner = ssd_norm.shape[1]
    CD = ssd_conv_w.shape[2]
    P = inner // H
    N = state_ssd_h.shape[-1]
    G = (CD - inner) // (2 * N)
    R = H // G
    dims = (G, R, P, N)
    assert W == D and inner == 2 * D and CD == 3 * D and H % SUBLANES == 0 and H <= LANES
    o_x, o_g, o_z, o_xbc, o_dt, o_m = 0, W, 2 * W, 2 * W + inner, 2 * W + inner + CD, 2 * W + inner + CD + H
    col_xbc, col_x, col_z, col_g, col_ga, col_gb = 0, CD // W, (CD + W) // inner, (CD + W + inner) // W, \
        (CD + 2 * W + inner) // D, (CD + 2 * W + inner + D) // D

    yp = x_prompt.reshape(B * L, D)
    ys = x_sample.reshape(Bs * Ls, D)
    outs = {k: [] for k in ("prh", "prc", "psh", "psc", "srh", "src", "ssh", "ssc")}
    lead = SAMPLE_ROWS - Ls
    for l in range(depth):
        wi = w_in[l]
        w_main = jnp.concatenate([wi[:, o_xbc:o_dt], wi[:, o_x:o_g], wi[:, o_z:o_xbc], wi[:, o_g:o_z],
                                  wi[:, o_m:]], axis=1).astype(BF16)
        wdt = _pad_lanes(wi[:, o_dt:o_m])
        g_mix = norm_mix[l][None]
        rg_w = (rg_conv_w[l], rg_conv_b[l][None], rg_wa[l].astype(BF16), rg_wx[l].astype(BF16),
                rg_ba[l][None], rg_bx[l][None], rg_lam[l][None], w_rg_out[l].astype(BF16))
        ssd_w = (ssd_conv_w[l], ssd_conv_b[l][None], _pad_lanes(ssd_dt_bias[l][None]), ssd_dt_bias[l][:, None],
                 _pad_lanes(ssd_a_log[l][None]), ssd_a_log[l][:, None], jnp.repeat(ssd_d[l], P)[None],
                 ssd_norm[l][None], w_ssd_out[l].astype(BF16))
        wout = w_out[l].astype(BF16)
        nf = norm_ffn[l][None]
        wqt = peer_wq[l].astype(BF16).T
        keys = peer_keys[l].astype(BF16)
        u = peer_u[l].astype(BF16)
        vt = peer_v[l].astype(BF16).T
        nfin = norm_final[None]

        proj, dt, dtT = _inproj(yp, g_mix, w_main, wdt, H)
        ya, rgh = _rg_prompt(proj, B, L, col_x, col_g, *rg_w)
        yb, ssh = _ssd_prompt(proj, dt, dtT, B, L, col_xbc, col_z, dims, *ssd_w)
        p3 = proj.reshape(B, L, -1)
        outs["prh"].append(rgh.reshape(B, W))
        outs["prc"].append(p3[:, L - 3:, CD:CD + W])
        outs["psh"].append(ssh)
        outs["psc"].append(p3[:, L - 3:, :CD])
        x1, h2t = _merge(yp, ya, yb, proj, col_ga, col_gb, wout, nf)
        yp = _peer(h2t, x1, u, vt, *_route(h2t, wqt, keys), nfin)

        xe = jnp.pad(ys.reshape(Bs, Ls, D), ((0, 0), (lead, 0), (0, 0))).reshape(Bs * SAMPLE_ROWS, D)
        proj_e, dt_e, dtT_e = _inproj(xe, g_mix, w_main, wdt, H)
        rows = lambda a3: a3.reshape(Bs * SAMPLE_ROWS, a3.shape[-1])
        rg_cpad = rows(jnp.pad(state_rg_conv[l], ((0, 0), (lead - 3, Ls), (0, 0))))
        rg_hpad = rows(jnp.pad(state_rg_h[l][:, None, :], ((0, 0), (lead - 1, Ls), (0, 0))))
        ssd_cpad = rows(jnp.pad(state_ssd_conv[l], ((0, 0), (lead - 3, Ls), (0, 0))))
        ya_e, h_e = _rg_sample(proj_e, col_x, col_g, rg_cpad, rg_hpad, *rg_w)
        yb_e, ssh_s = _ssd_sample(proj_e, dt_e, dtT_e, ssd_cpad, state_ssd_h[l], col_xbc, col_z, dims, *ssd_w)
        toks = lambda a2: a2.reshape(Bs, SAMPLE_ROWS, -1)[:, lead:].reshape(Bs * Ls, -1)
        p3 = proj_e.reshape(Bs, SAMPLE_ROWS, -1)
        outs["srh"].append(h_e.reshape(Bs, SAMPLE_ROWS, W)[:, -1])
        outs["src"].append(p3[:, SAMPLE_ROWS - 3:, CD:CD + W])
        outs["ssh"].append(ssh_s)
        outs["ssc"].append(p3[:, SAMPLE_ROWS - 3:, :CD])
        proj_s = toks(proj_e)
        x1, h2t = _merge(ys, toks(ya_e), toks(yb_e), proj_s, col_ga, col_gb, wout, nf)
        ys = _peer(h2t, x1, u, vt, *_route(h2t, wqt, keys), nfin)

    st = lambda k: jnp.stack(outs[k])
    return (yp.reshape(B, L, D), ys.reshape(Bs, Ls, D), st("prh"), st("prc"), st("psh"), st("psc"),
            st("srh"), st("src"), st("ssh"), st("ssc"))
```

```python
import functools

import jax
import jax.numpy as jnp
from jax import lax
from jax.experimental import pallas as pl
from jax.experimental.pallas import tpu as pltpu

F32 = jnp.float32
BF16 = jnp.bfloat16
EPS = 1e-6
RG_C = 8.0
PEER_TOPK = 16
SSD_CHUNK = 128
LANES = 128
SUBLANES = 8
BF16_ROWS = 16
SAMPLE_ROWS = 8
NEG_INF = float("-inf")
LOG2E = 1.4426950408889634
HIGHEST = lax.Precision.HIGHEST
NT_DIMS = (((1,), (1,)), ((), ()))
VMEM_LIMIT = 56 * 1024 * 1024


def _cparams(sem):
    return pltpu.CompilerParams(dimension_semantics=sem, vmem_limit_bytes=VMEM_LIMIT)


def _pick(n, pref):
    t = min(n, pref)
    while n % t:
        t -= SUBLANES
    return t


def _sigmoid(x):
    return 1.0 / (1.0 + jnp.exp2(x * -LOG2E))


def _silu(x):
    return x * _sigmoid(x)


def _softplus(x):
    return jnp.maximum(x, 0.0) + jnp.log1p(jnp.exp(-jnp.abs(x)))


def _gelu(x):
    k1 = -2.0 * 0.7978845608028654 * LOG2E
    return x / (1.0 + jnp.exp2(x * (k1 + (k1 * 0.044715) * (x * x))))


def _rms(x, g):
    return x * lax.rsqrt(jnp.mean(x * x, axis=-1, keepdims=True) + EPS) * g


def _dot(a, b):
    return jnp.dot(a, b, preferred_element_type=F32)


def _dot_nt(a, b):
    return lax.dot_general(a, b, NT_DIMS, preferred_element_type=F32)


def _inproj_body(x_ref, g_ref, w_ref, wdt_ref, o_ref, odt_ref, odtT_ref, xn_ref):
    @pl.when(pl.program_id(1) == 0)
    def _():
        xn = _rms(x_ref[...], g_ref[...])
        xn_ref[...] = xn.astype(BF16)
        dt = jnp.dot(xn, wdt_ref[...], precision=HIGHEST, preferred_element_type=F32)
        odt_ref[...] = dt
        odtT_ref[...] = jnp.transpose(dt)[:odtT_ref.shape[0], :]

    o_ref[...] = _dot(xn_ref[...], w_ref[...])


def _inproj(x2d, g, w_main, wdt, H):
    T, D = x2d.shape
    N = w_main.shape[1]
    TM = _pick(T, 1024)
    TN = 2304
    assert N % TN == 0 and TM % LANES == 0
    return pl.pallas_call(
        _inproj_body,
        grid=(T // TM, N // TN),
        in_specs=[
            pl.BlockSpec((TM, D), lambda i, j: (i, 0)),
            pl.BlockSpec((1, D), lambda i, j: (0, 0)),
            pl.BlockSpec((D, TN), lambda i, j: (0, j)),
            pl.BlockSpec((D, LANES), lambda i, j: (0, 0)),
        ],
        out_specs=[
            pl.BlockSpec((TM, TN), lambda i, j: (i, j)),
            pl.BlockSpec((TM, LANES), lambda i, j: (i, 0)),
            pl.BlockSpec((H, TM), lambda i, j: (0, i)),
        ],
        out_shape=[
            jax.ShapeDtypeStruct((T, N), F32),
            jax.ShapeDtypeStruct((T, LANES), F32),
            jax.ShapeDtypeStruct((H, T), F32),
        ],
        scratch_shapes=[pltpu.VMEM((TM, D), BF16)],
        compiler_params=_cparams(("parallel", "arbitrary")),
        name="inproj",
    )(x2d, g, w_main, wdt)


def _conv_slab(x, pad, cw_ref, cb_ref, ls):
    L = x.shape[0]
    pad[SUBLANES:SUBLANES + L, :] = x
    out = (cb_ref[:, ls]
           + cw_ref[3:4, ls] * pad[8:8 + L, :]
           + cw_ref[2:3, ls] * pad[7:7 + L, :]
           + cw_ref[1:2, ls] * pad[6:6 + L, :]
           + cw_ref[0:1, ls] * pad[5:5 + L, :])
    pad[0:SUBLANES, :] = pad[L:L + SUBLANES, :]
    return out


def _rg_gates(xc, wa_ref, wx_ref, ba, bx, lam):
    xb = xc.astype(BF16)
    nb, bw = wa_ref.shape[0], wa_ref.shape[1]
    rs, gs = [], []
    for k in range(nb):
        xk = xb[:, k * bw:(k + 1) * bw]
        rs.append(_dot(xk, wa_ref[k]))
        gs.append(_dot(xk, wx_ref[k]))
    r = _sigmoid(jnp.concatenate(rs, axis=1) + ba)
    i = _sigmoid(jnp.concatenate(gs, axis=1) + bx)
    log_a = -RG_C * r * _softplus(-lam)
    a = jnp.exp(log_a)
    em1 = jnp.tanh(log_a) * (a * a + 1.0)
    return a, jnp.sqrt(-em1) * (i * xc)


def _rg_prompt_body(x_ref, gate_ref, cw_ref, cb_ref, wa_ref, wx_ref, ba_ref, bx_ref, lam_ref, wo_ref,
                    ya_ref, hfin_ref, pad_s, a_s, b_s, h_s):
    B, Lc, W = x_ref.shape
    NS = W // LANES
    pitch = a_s.shape[1] // B

    @pl.when(pl.program_id(0) == 0)
    def _():
        pad_s[:, :, 0:SUBLANES, :] = jnp.zeros((B, NS, SUBLANES, LANES), F32)
        h_s[...] = jnp.zeros(h_s.shape, F32)

    for b in range(B):
        xc = jnp.concatenate(
            [_conv_slab(x_ref[b, :, s * LANES:(s + 1) * LANES], pad_s.at[b, s], cw_ref, cb_ref,
                        slice(s * LANES, (s + 1) * LANES)) for s in range(NS)], axis=1)
        a, bt = _rg_gates(xc, wa_ref, wx_ref, ba_ref[...], bx_ref[...], lam_ref[...])
        for s in range(NS):
            a_s[s, b * pitch:b * pitch + Lc, :] = a[:, s * LANES:(s + 1) * LANES]
            b_s[s, b * pitch:b * pitch + Lc, :] = bt[:, s * LANES:(s + 1) * LANES]

    def step(t, hs):
        out = []
        for s in range(NS):
            rows = pl.ds(t, B, stride=pitch)
            h = a_s[s, rows, :] * hs[s] + b_s[s, rows, :]
            b_s[s, rows, :] = h
            out.append(h)
        return tuple(out)

    hs = lax.fori_loop(0, Lc, step, tuple(h_s[:, s * LANES:(s + 1) * LANES] for s in range(NS)), unroll=4)
    for s in range(NS):
        h_s[:, s * LANES:(s + 1) * LANES] = hs[s]
    hfin_ref[...] = h_s[...]
    for b in range(B):
        hb = jnp.concatenate([b_s[s, b * pitch:b * pitch + Lc, :] for s in range(NS)], axis=1)
        ya_ref[b] = _dot((hb * _gelu(gate_ref[b])).astype(BF16), wo_ref[...])


def _rg_prompt(proj, B, L, col_x, col_g, cw, cb, wa, wx, ba, bx, lam, wo):
    W = cw.shape[1]
    Lc = _pick(L, 128)
    pitch = Lc + SUBLANES // 2
    assert B == SUBLANES and W % LANES == 0
    proj3 = proj.reshape(B, L, proj.shape[1])
    full = lambda *s: pl.BlockSpec(s, lambda c: (0,) * len(s))
    ya, hfin = pl.pallas_call(
        _rg_prompt_body,
        grid=(L // Lc,),
        in_specs=[
            pl.BlockSpec((B, Lc, W), lambda c: (0, c, col_x)),
            pl.BlockSpec((B, Lc, W), lambda c: (0, c, col_g)),
            full(4, W), full(1, W), full(*wa.shape), full(*wx.shape), full(1, W), full(1, W), full(1, W),
            full(*wo.shape),
        ],
        out_specs=[
            pl.BlockSpec((B, Lc, wo.shape[1]), lambda c: (0, c, 0)),
            pl.BlockSpec((B, W), lambda c: (0, 0)),
        ],
        out_shape=[
            jax.ShapeDtypeStruct((B, L, wo.shape[1]), F32),
            jax.ShapeDtypeStruct((B, W), F32),
        ],
        scratch_shapes=[
            pltpu.VMEM((B, W // LANES, Lc + SUBLANES, LANES), F32),
            pltpu.VMEM((W // LANES, B * pitch, LANES), F32),
            pltpu.VMEM((W // LANES, B * pitch, LANES), F32),
            pltpu.VMEM((B, W), F32),
        ],
        compiler_params=_cparams(("arbitrary",)),
        name="rg_prompt",
    )(proj3, proj3, cw, cb, wa, wx, ba, bx, lam, wo)
    return ya.reshape(B * L, wo.shape[1]), hfin


def _conv_rolled(u, cw_ref, cb_ref):
    return (cb_ref[...]
            + cw_ref[3:4, :] * u
            + cw_ref[2:3, :] * pltpu.roll(u, 1, axis=0)
            + cw_ref[1:2, :] * pltpu.roll(u, 2, axis=0)
            + cw_ref[0:1, :] * pltpu.roll(u, 3, axis=0))


def _rg_sample_body(x_ref, gate_ref, cpad_ref, hpad_ref, cw_ref, cb_ref, wa_ref, wx_ref, ba_ref, bx_ref,
                    lam_ref, wo_ref, ya_ref, h_ref):
    xc = _conv_rolled(x_ref[...] + cpad_ref[...], cw_ref, cb_ref)
    a, bt = _rg_gates(xc, wa_ref, wx_ref, ba_ref[...], bx_ref[...], lam_ref[...])
    row = lax.broadcasted_iota(jnp.int32, a.shape, 0) & (SAMPLE_ROWS - 1)
    h = hpad_ref[...]
    for k in range(SAMPLE_ROWS // 2, SAMPLE_ROWS):
        h = jnp.where(row == k, a * pltpu.roll(h, 1, axis=0) + bt, h)
    h_ref[...] = h
    ya_ref[...] = _dot((h * _gelu(gate_ref[...])).astype(BF16), wo_ref[...])


def _rg_sample(proj, col_x, col_g, cpad, hpad, cw, cb, wa, wx, ba, bx, lam, wo):
    T = proj.shape[0]
    W = cw.shape[1]
    TM = _pick(T, 256)
    full = lambda *s: pl.BlockSpec(s, lambda i: (0,) * len(s))
    return pl.pallas_call(
        _rg_sample_body,
        grid=(T // TM,),
        in_specs=[
            pl.BlockSpec((TM, W), lambda i: (i, col_x)),
            pl.BlockSpec((TM, W), lambda i: (i, col_g)),
            pl.BlockSpec((TM, W), lambda i: (i, 0)),
            pl.BlockSpec((TM, W), lambda i: (i, 0)),
            full(4, W), full(1, W), full(*wa.shape), full(*wx.shape), full(1, W), full(1, W), full(1, W),
            full(*wo.shape),
        ],
        out_specs=[
            pl.BlockSpec((TM, wo.shape[1]), lambda i: (i, 0)),
            pl.BlockSpec((TM, W), lambda i: (i, 0)),
        ],
        out_shape=[
            jax.ShapeDtypeStruct((T, wo.shape[1]), F32),
            jax.ShapeDtypeStruct((T, W), F32),
        ],
        compiler_params=_cparams(("parallel",)),
        name="rg_sample",
    )(proj, proj, cpad, hpad, cw, cb, wa, wx, ba, bx, lam, wo)


def _ssd_chunk(act_s, dt, dtT, alog_ref, alogT_ref, dexp_ref, seq_shift, G, R, P, N,
               yint_fn, y_s, xw_s):
    Q = act_s.shape[0]
    inner = G * R * P
    a = dt * (-jnp.exp(alog_ref[...]))
    aT = dtT * (-jnp.exp(alogT_ref[...]))
    ri = lax.broadcasted_iota(jnp.int32, (Q, Q), 0)
    ci = lax.broadcasted_iota(jnp.int32, (Q, Q), 1)
    same = (ri >> seq_shift) == (ci >> seq_shift)
    causal = same & (ci <= ri)
    causal_f = jnp.where(causal, 1.0, 0.0).astype(F32)
    same_f = jnp.where(same, 1.0, 0.0).astype(F32)
    acum = jnp.dot(causal_f, a, precision=HIGHEST, preferred_element_type=F32)
    tot = jnp.dot(same_f, a, precision=HIGHEST, preferred_element_type=F32)
    acumT = lax.dot_general(aT, causal_f, NT_DIMS, precision=HIGHEST,
                            preferred_element_type=F32)
    eacum = jnp.exp(acum)
    toend = jnp.exp(tot - acum) * dt
    for g in range(G):
        Bg = act_s[:, inner + g * N:inner + (g + 1) * N].astype(BF16)
        Cg = act_s[:, inner + (G + g) * N:inner + (G + g + 1) * N].astype(BF16)
        cb = _dot_nt(Cg, Bg)
        yint = yint_fn(g, Cg)
        for r in range(R):
            h = g * R + r
            sl = slice(h * P, (h + 1) * P)
            seg = acum[:, h:h + 1] - acumT[h:h + 1, :]
            decay = jnp.exp(jnp.where(causal, seg, NEG_INF))
            wgt = (cb * decay * dtT[h:h + 1, :]).astype(BF16)
            xh = act_s[:, sl]
            yh = _dot(wgt, xh.astype(BF16)) + dexp_ref[:, sl] * xh
            if yint is not None:
                yh = yh + yint[:, r * P:(r + 1) * P] * eacum[:, h:h + 1]
            y_s[:, sl] = yh
            xw_s[:, sl] = xh * toend[:, h:h + 1]
    return tot, eacum


def _expand_heads(x, ex_ref):
    hi = x.astype(BF16)
    r1 = x - hi.astype(F32)
    mid = r1.astype(BF16)
    lo = (r1 - mid.astype(F32)).astype(BF16)
    e = ex_ref[...]
    return _dot(hi, e) + _dot(mid, e) + _dot(lo, e)


def _ssd_finish(y, z, nw_ref, wo_ref):
    y = y * _silu(z)
    return _dot(_rms(y, nw_ref[...]).astype(BF16), wo_ref[...])


def _ssd_prompt_body(G, R, P, N,
                     xbc_ref, z_ref, dt_ref, dtT_ref, cw_ref, cb_ref, dtb_ref, dtbT_ref, alog_ref, alogT_ref,
                     dexp_ref, nw_ref, wo_ref, ex_ref, yb_ref, st_ref, pad_s, act_s, y_s, stT_s):
    Q = xbc_ref.shape[0]
    inner = G * R * P
    GW = R * P

    @pl.when(pl.program_id(1) == 0)
    def _():
        pad_s[:, 0:SUBLANES, :] = jnp.zeros((pad_s.shape[0], SUBLANES, LANES), F32)
        stT_s[...] = jnp.zeros(stT_s.shape, F32)

    for s in range(pad_s.shape[0]):
        ls = slice(s * LANES, (s + 1) * LANES)
        act_s[:, ls] = _silu(_conv_slab(xbc_ref[:, ls], pad_s.at[s], cw_ref, cb_ref, ls))
    dt = _softplus(dt_ref[...] + dtb_ref[...])
    dtT = _softplus(dtT_ref[...] + dtbT_ref[...])
    a = dt * (-jnp.exp(alog_ref[...]))
    aT = dtT * (-jnp.exp(alogT_ref[...]))
    ri = lax.broadcasted_iota(jnp.int32, (Q, Q), 0)
    ci = lax.broadcasted_iota(jnp.int32, (Q, Q), 1)
    causal = ci <= ri
    causal_f = jnp.where(causal, 1.0, 0.0).astype(F32)
    hi = dict(precision=HIGHEST, preferred_element_type=F32)
    acum = jnp.dot(causal_f, a, **hi)
    tot = jnp.dot(jnp.ones((SUBLANES, Q), F32), a, **hi)
    acumT = lax.dot_general(aT, causal_f, NT_DIMS, **hi)
    fx = _expand_heads(jnp.concatenate([jnp.exp(acum), jnp.exp(tot[0:1, :] - acum) * dt, jnp.exp(tot)], axis=0),
                       ex_ref)
    eacum_x, toend_x, dec_x = fx[0:Q], fx[Q:2 * Q], fx[2 * Q:2 * Q + 1]
    col2 = acum * LOG2E
    row2 = (acumT - jnp.log(dtT)) * LOG2E
    lane = lax.broadcasted_iota(jnp.int32, (Q, LANES), 1)
    for g in range(G):
        gs = slice(g * GW, (g + 1) * GW)
        Bf = act_s[:, inner + g * N:inner + (g + 1) * N]
        Bg = Bf.astype(BF16)
        Cg = act_s[:, inner + (G + g) * N:inner + (G + g + 1) * N].astype(BF16)
        cb = jnp.where(causal, _dot_nt(Cg, Bg), 0.0)
        y_s[:, gs] = (_dot(Cg, stT_s[:, gs].astype(BF16)) * eacum_x[:, gs]
                      + dexp_ref[:, gs] * act_s[:, gs])
        for pr in range(R // 2):
            h0 = g * R + 2 * pr
            ps = slice(h0 * P, (h0 + 2) * P)
            wg = []
            for h in (h0, h0 + 1):
                e = jnp.exp2(jnp.where(causal, col2[:, h:h + 1] - row2[h:h + 1, :], NEG_INF))
                wg.append((cb * e).astype(BF16))
            xp = act_s[:, ps]
            xa = jnp.where(lane < P, xp, 0.0).astype(BF16)
            xb = jnp.where(lane < P, 0.0, xp).astype(BF16)
            y_s[:, ps] = y_s[:, ps] + _dot(jnp.concatenate(wg, axis=1), jnp.concatenate([xa, xb], axis=0))
        xw = (act_s[:, gs] * toend_x[:, gs]).astype(BF16)
        ST = _dot(jnp.transpose(Bf).astype(BF16), xw)
        stT_s[:, gs] = dec_x[:, gs] * stT_s[:, gs] + ST
    yb_ref[...] = _ssd_finish(y_s[...], z_ref[...], nw_ref, wo_ref)

    @pl.when(pl.program_id(1) == pl.num_programs(1) - 1)
    def _():
        for g in range(G):
            st_ref[0, g * R:(g + 1) * R] = jnp.transpose(stT_s[:, g * GW:(g + 1) * GW]).reshape(R, P, N)


def _ssd_prompt(proj, dt, dtT, B, L, col_xbc, col_z, dims, cw, cb, dtb, dtbT, alog, alogT, dexp, nw, wo):
    G, R, P, N = dims
    H = G * R
    inner = H * P
    CD = cw.shape[1]
    Q = _pick(L, SSD_CHUNK)
    nC = L // Q
    assert 2 * P == LANES and R % 2 == 0 and N == LANES
    head_of_lane = jnp.arange(inner, dtype=jnp.int32) // P
    ex = (jnp.arange(LANES, dtype=jnp.int32)[:, None] == head_of_lane[None, :]).astype(BF16)
    full = lambda *s: pl.BlockSpec(s, lambda b, c: (0,) * len(s))
    return pl.pallas_call(
        functools.partial(_ssd_prompt_body, G, R, P, N),
        grid=(B, nC),
        in_specs=[
            pl.BlockSpec((Q, CD), lambda b, c: (b * nC + c, col_xbc)),
            pl.BlockSpec((Q, inner), lambda b, c: (b * nC + c, col_z)),
            pl.BlockSpec((Q, LANES), lambda b, c: (b * nC + c, 0)),
            pl.BlockSpec((H, Q), lambda b, c: (0, b * nC + c)),
            full(4, CD), full(1, CD), full(1, LANES), full(H, 1), full(1, LANES), full(H, 1),
            full(1, inner), full(1, inner), full(*wo.shape), full(LANES, inner),
        ],
        out_specs=[
            pl.BlockSpec((Q, wo.shape[1]), lambda b, c: (b * nC + c, 0)),
            pl.BlockSpec((1, H, P, N), lambda b, c: (b, 0, 0, 0)),
        ],
        out_shape=[
            jax.ShapeDtypeStruct((B * L, wo.shape[1]), F32),
            jax.ShapeDtypeStruct((B, H, P, N), F32),
        ],
        scratch_shapes=[
            pltpu.VMEM((CD // LANES, Q + SUBLANES, LANES), F32),
            pltpu.VMEM((Q, CD), F32),
            pltpu.VMEM((Q, inner), F32),
            pltpu.VMEM((N, inner), F32),
        ],
        compiler_params=_cparams(("parallel", "arbitrary")),
        name="ssd_prompt",
    )(proj, proj, dt, dtT, cw, cb, dtb, dtbT, alog, alogT, dexp, nw, wo, ex)


def _ssd_sample_body(G, R, P, N,
                     xbc_ref, cpad_ref, z_ref, dt_ref, dtT_ref, h0_ref, cw_ref, cb_ref, dtb_ref, dtbT_ref,
                     alog_ref, alogT_ref, dexp_ref, nw_ref, wo_ref, yb_ref, st_ref,
                     act_s, y_s, xw_s, yint_s, xwT_s, tot_s, eacum_s):
    Q = xbc_ref.shape[0]
    inner = G * R * P
    bi = pl.program_id(1)
    half = SAMPLE_ROWS // 2

    @pl.when(bi == 0)
    def _():
        act_s[...] = _silu(_conv_rolled(xbc_ref[...] + cpad_ref[...], cw_ref, cb_ref))
        rows = lax.broadcasted_iota(jnp.int32, (Q, LANES), 0) & (SAMPLE_ROWS - 1)
        cols = lax.broadcasted_iota(jnp.int32, dtT_ref.shape, 1) & (SAMPLE_ROWS - 1)
        dt = jnp.where(rows >= half, _softplus(dt_ref[...] + dtb_ref[...]), 0.0)
        dtT = jnp.where(cols >= half, _softplus(dtT_ref[...] + dtbT_ref[...]), 0.0)
        seq_shift = SAMPLE_ROWS.bit_length() - 1
        tot, eacum = _ssd_chunk(act_s, dt, dtT, alog_ref, alogT_ref, dexp_ref, seq_shift, G, R, P, N,
                                lambda g, Cg: None, y_s, xw_s)
        tot_s[...] = tot
        eacum_s[...] = eacum
        for g in range(G):
            xwT_s[g] = jnp.transpose(xw_s[:, g * R * P:(g + 1) * R * P]).astype(BF16)

    r0 = pl.multiple_of(bi * SAMPLE_ROWS, SAMPLE_ROWS)
    rid = lax.broadcasted_iota(jnp.int32, (Q, N), 0)
    mine = (rid >= r0) & (rid < r0 + SAMPLE_ROWS)
    dec = jnp.exp(tot_s[pl.ds(r0, 1), :])
    for g in range(G):
        Cb = act_s[pl.ds(r0, SAMPLE_ROWS), inner + (G + g) * N:inner + (G + g + 1) * N]
        stg = h0_ref[0, g * R:(g + 1) * R].reshape(R * P, N)
        yint_s[pl.ds(r0, SAMPLE_ROWS), g * R * P:(g + 1) * R * P] = _dot_nt(Cb, stg)
        Bg = jnp.where(mine, act_s[:, inner + g * N:inner + (g + 1) * N], 0.0).astype(BF16)
        S = _dot(xwT_s[g], Bg)
        for r in range(R):
            h = g * R + r
            st_ref[0, h] = dec[:, h:h + 1] * h0_ref[0, h] + S[r * P:(r + 1) * P, :]

    @pl.when(bi == pl.num_programs(1) - 1)
    def _():
        for h in range(G * R):
            sl = slice(h * P, (h + 1) * P)
            y_s[:, sl] = y_s[:, sl] + yint_s[:, sl] * eacum_s[:, h:h + 1]
        yb_ref[...] = _ssd_finish(y_s[...], z_ref[...], nw_ref, wo_ref)


def _ssd_sample(proj, dt, dtT, cpad, h0, col_xbc, col_z, dims, cw, cb, dtb, dtbT, alog, alogT, dexp, nw, wo):
    G, R, P, N = dims
    H = G * R
    inner = H * P
    CD = cw.shape[1]
    T = proj.shape[0]
    Bs = h0.shape[0]
    Q = _pick(T, SSD_CHUNK)
    nb = Q // SAMPLE_ROWS
    full = lambda *s: pl.BlockSpec(s, lambda i, j: (0,) * len(s))
    return pl.pallas_call(
        functools.partial(_ssd_sample_body, G, R, P, N),
        grid=(T // Q, nb),
        in_specs=[
            pl.BlockSpec((Q, CD), lambda i, j: (i, col_xbc)),
            pl.BlockSpec((Q, CD), lambda i, j: (i, 0)),
            pl.BlockSpec((Q, inner), lambda i, j: (i, col_z)),
            pl.BlockSpec((Q, LANES), lambda i, j: (i, 0)),
            pl.BlockSpec((H, Q), lambda i, j: (0, i)),
            pl.BlockSpec((1, H, P, N), lambda i, j: (i * nb + j, 0, 0, 0)),
            full(4, CD), full(1, CD), full(1, LANES), full(H, 1), full(1, LANES), full(H, 1),
            full(1, inner), full(1, inner), full(*wo.shape),
        ],
        out_specs=[
            pl.BlockSpec((Q, wo.shape[1]), lambda i, j: (i, 0)),
            pl.BlockSpec((1, H, P, N), lambda i, j: (i * nb + j, 0, 0, 0)),
        ],
        out_shape=[
            jax.ShapeDtypeStruct((T, wo.shape[1]), F32),
            jax.ShapeDtypeStruct((Bs, H, P, N), F32),
        ],
        scratch_shapes=[
            pltpu.VMEM((Q, CD), F32),
            pltpu.VMEM((Q, inner), F32),
            pltpu.VMEM((Q, inner), F32),
            pltpu.VMEM((Q, inner), F32),
            pltpu.VMEM((G, R * P, Q), BF16),
            pltpu.VMEM((Q, LANES), F32),
            pltpu.VMEM((Q, LANES), F32),
        ],
        compiler_params=_cparams(("parallel", "arbitrary")),
        name="ssd_sample",
    )(proj, cpad, proj, dt, dtT, h0, cw, cb, dtb, dtbT, alog, alogT, dexp, nw, wo)


def _merge_body(x_ref, ya_ref, yb_ref, ga_ref, gb_ref, wout_ref, nf_ref, x1_ref, h2t_ref):
    m = _sigmoid(ga_ref[...]) * ya_ref[...] + _sigmoid(gb_ref[...]) * yb_ref[...]
    x1 = x_ref[...] + _dot(m.astype(BF16), wout_ref[...])
    x1_ref[...] = x1
    h2t_ref[...] = jnp.transpose(_rms(x1, nf_ref[...])).astype(BF16)


def _merge(x2d, ya, yb, proj, col_ga, col_gb, wout, nf):
    T, D = x2d.shape
    TM = _pick(T, 512)
    row = lambda c: pl.BlockSpec((TM, D), lambda i: (i, c))
    return pl.pallas_call(
        _merge_body,
        grid=(T // TM,),
        in_specs=[row(0), row(0), row(0), row(col_ga), row(col_gb),
                  pl.BlockSpec(wout.shape, lambda i: (0, 0)), pl.BlockSpec((1, D), lambda i: (0, 0))],
        out_specs=[row(0), pl.BlockSpec((D, TM), lambda i: (0, i))],
        out_shape=[jax.ShapeDtypeStruct((T, D), F32), jax.ShapeDtypeStruct((D, T), BF16)],
        compiler_params=_cparams(("parallel",)),
        name="merge",
    )(x2d, ya, yb, proj, proj, wout, nf)


def _staircase(k):
    return [(ka, k // (ka + 1)) for ka in range(k)]


def _take_top(cur, k_top, want_rank):
    rank = jnp.full(cur.shape, float(k_top), F32) if want_rank else None
    vals = []
    for k in range(k_top):
        m = jnp.max(cur, axis=0, keepdims=True)
        hit = cur == m
        vals.append(m)
        if want_rank:
            rank = jnp.where(hit, float(k), rank)
        cur = jnp.where(hit, NEG_INF, cur)
    taken = jnp.sum(jnp.where(cur == NEG_INF, 1.0, 0.0), axis=0, keepdims=True)
    return vals, rank, taken


def _take_top_ties(x, k_top, val_ref, rank_ref):
    rows = lax.broadcasted_iota(jnp.int32, x.shape, 0)
    kk = lax.broadcasted_iota(jnp.int32, (k_top, x.shape[1]), 0)

    def body(k, carry):
        cur, rank, vals = carry
        m = jnp.max(cur, axis=0, keepdims=True)
        first = jnp.min(jnp.where(cur == m, rows, x.shape[0]), axis=0, keepdims=True)
        hit = rows == first
        return (jnp.where(hit, NEG_INF, cur), jnp.where(hit, k.astype(F32), rank),
                jnp.where(kk == k, m, vals))

    init = (x, jnp.full(x.shape, float(k_top), F32), jnp.zeros((k_top, x.shape[1]), F32))
    _, rank, vals = lax.fori_loop(0, k_top, body, init)
    val_ref[...] = vals
    rank_ref[...] = rank


def _route_body(h2t_ref, wqt_ref, keys_ref, na_ref, ea_ref, rb_ref, eb_ref, sv_s, rk_s, cand_s, cv_s, cr_s):
    NH = keys_ref.shape[0]
    NK, KH = keys_ref.shape[2], keys_ref.shape[3]
    K = PEER_TOPK
    TB = h2t_ref.shape[1]
    tiles = [slice(tl * LANES, (tl + 1) * LANES) for tl in range(TB // LANES)]
    qt = _dot(wqt_ref[...], h2t_ref[...]).astype(BF16)
    cand_s[...] = jnp.full(cand_s.shape, NEG_INF, F32)

    def candidates():
        off = 0
        for ka, nb in _staircase(K):
            cand_s[off:off + nb, :] = sv_s[0, ka:ka + 1, :] + sv_s[1, 0:nb, :]
            off += nb

    def emit(h, sT, in_top, rb, sel, na_of):
        top = sv_s[0, 0:1, :] + sv_s[1, 0:1, :]
        z = jnp.sum(jnp.where(sel, jnp.exp(cand_s[...] - top), 0.0), axis=0, keepdims=True)
        ex = [jnp.where(in_top[s], jnp.exp(sT[s] - sv_s[s, 0:1, :]), 0.0) for s in range(2)]
        na = jnp.zeros(sT[0].shape, F32)
        off = 0
        for ka, nb in _staircase(K):
            cnt = jnp.sum(jnp.where(sel[off:off + nb, :], 1.0, 0.0), axis=0, keepdims=True)
            na = jnp.where(na_of(ka), cnt, na)
            off += nb
        na_ref[h * NK:(h + 1) * NK, :] = na
        ea_ref[h * NK:(h + 1) * NK, :] = ex[0] / z
        rb_ref[h] = rb.astype(BF16)
        eb_ref[h] = ex[1].astype(BF16)

    def scores(h):
        return [_dot(keys_ref[h, s], qt[(h * 2 + s) * KH:(h * 2 + s + 1) * KH, :]) for s in range(2)]

    most = []
    for h in range(NH):
        sT = scores(h)
        seen = jnp.zeros((1, LANES), F32)
        rbs = []
        for s in range(2):
            for ln in tiles:
                vals, rank, taken = _take_top(sT[s][:, ln], K, s == 1)
                for k in range(K):
                    sv_s[s, k:k + 1, ln] = vals[k]
                seen = jnp.maximum(seen, taken)
                if s == 1:
                    rbs.append(rank)
        candidates()
        cand = cand_s[...]
        tau = _take_top(cand, K, False)[0][K - 1]
        sel = cand >= tau
        most.append(jnp.maximum(seen, jnp.max(jnp.sum(jnp.where(sel, 1.0, 0.0), axis=0, keepdims=True),
                                              axis=1, keepdims=True)))
        emit(h, sT, [sT[s] >= sv_s[s, K - 1:K, :] for s in range(2)], jnp.concatenate(rbs, axis=1), sel,
             lambda ka: sT[0] == sv_s[0, ka:ka + 1, :])

    for h in range(NH):
        @pl.when(jnp.max(most[h]) > K)
        def _(h=h):
            sT = scores(h)
            for s in range(2):
                for ln in tiles:
                    _take_top_ties(sT[s][:, ln], K, sv_s.at[s, :, ln], rk_s.at[s, :, ln])
            candidates()
            for ln in tiles:
                _take_top_ties(cand_s[:, ln], K, cv_s.at[:, ln], cr_s.at[:, ln])
            emit(h, sT, [rk_s[s] < K for s in range(2)], rk_s[1], cr_s[...] < K,
                 lambda ka: rk_s[0] == float(ka))


def _route(h2t, wqt, keys):
    D, T = h2t.shape
    NH, _, NK, KH = keys.shape
    TB = _pick(T, 512)
    assert TB % LANES == 0
    ncand = sum(nb for _, nb in _staircase(PEER_TOPK))
    ncand_pad = -(-ncand // SUBLANES) * SUBLANES
    flat = pl.BlockSpec((NH * NK, TB), lambda i: (0, i))
    tok = pl.BlockSpec((NH, NK, TB), lambda i: (0, 0, i))
    return pl.pallas_call(
        _route_body,
        grid=(T // TB,),
        in_specs=[pl.BlockSpec((D, TB), lambda i: (0, i)),
                  pl.BlockSpec(wqt.shape, lambda i: (0, 0)),
                  pl.BlockSpec(keys.shape, lambda i: (0, 0, 0, 0))],
        out_specs=[flat, flat, tok, tok],
        out_shape=[jax.ShapeDtypeStruct((NH * NK, T), F32), jax.ShapeDtypeStruct((NH * NK, T), F32),
                   jax.ShapeDtypeStruct((NH, NK, T), BF16), jax.ShapeDtypeStruct((NH, NK, T), BF16)],
        scratch_shapes=[pltpu.VMEM((2, PEER_TOPK, TB), F32), pltpu.VMEM((2, NK, TB), F32),
                        pltpu.VMEM((ncand_pad, TB), F32), pltpu.VMEM((PEER_TOPK, TB), F32),
                        pltpu.VMEM((ncand_pad, TB), F32)],
        compiler_params=_cparams(("parallel",)),
        name="peer_route",
    )(h2t, wqt, keys)


def _peer_body(h2t_ref, x1_ref, u_ref, vt_ref, na_ref, ea_ref, rb_ref, eb_ref, nfin_ref, y_ref,
               s_s, a_s, acc_s):
    ec = pl.program_id(1)
    NH, NK, TB = rb_ref.shape
    ni = u_ref.shape[0] // NK

    @pl.when(ec == 0)
    def _():
        acc_s[...] = jnp.zeros(acc_s.shape, F32)

    s_s[...] = _dot(u_ref[...], h2t_ref[...])

    def row_tile(grp, il):
        return jnp.broadcast_to(grp[il:il + 1, :], (BF16_ROWS, LANES)).astype(BF16)

    for tl in range(TB // LANES):
        ln = slice(tl * LANES, (tl + 1) * LANES)
        rows = [pl.ds(pl.multiple_of(h * NK + ec * ni, SUBLANES), ni) for h in range(NH)]
        na = [na_ref[rows[h], ln] for h in range(NH)]
        ea = [ea_ref[rows[h], ln] for h in range(NH)]
        for il in range(ni):
            na_t = [row_tile(na[h], il) for h in range(NH)]
            ea_t = [row_tile(ea[h], il) for h in range(NH)]
            for jt in range(NK // BF16_ROWS):
                js = slice(jt * BF16_ROWS, (jt + 1) * BF16_ROWS)
                w = jnp.zeros((BF16_ROWS, LANES), BF16)
                for h in range(NH):
                    w = w + ea_t[h] * jnp.where(rb_ref[h, js, ln] < na_t[h], eb_ref[h, js, ln], 0.0)
                e = slice(il * NK + jt * BF16_ROWS, il * NK + (jt + 1) * BF16_ROWS)
                a_s[e, ln] = _gelu(s_s[e, ln]).astype(BF16) * w
    acc_s[...] += _dot(vt_ref[...], a_s[...])

    @pl.when(ec == pl.num_programs(1) - 1)
    def _():
        x2 = x1_ref[...] + jnp.transpose(acc_s[...])
        y_ref[...] = _rms(x2, nfin_ref[...])


def _peer(h2t, x1, u, vt, na, ea, rb, eb, nfin):
    D, T = h2t.shape
    E = u.shape[0]
    NH, NK, _ = rb.shape
    TB = _pick(T, 512)
    EC = 2 * SUBLANES * NK
    assert TB % LANES == 0 and E % EC == 0 and NK % BF16_ROWS == 0
    tok = pl.BlockSpec((NH, NK, TB), lambda i, j: (0, 0, i))
    flat = pl.BlockSpec((NH * NK, TB), lambda i, j: (0, i))
    return pl.pallas_call(
        _peer_body,
        grid=(T // TB, E // EC),
        in_specs=[pl.BlockSpec((D, TB), lambda i, j: (0, i)),
                  pl.BlockSpec((TB, D), lambda i, j: (i, 0)),
                  pl.BlockSpec((EC, D), lambda i, j: (j, 0)),
                  pl.BlockSpec((D, EC), lambda i, j: (0, j)),
                  flat, flat, tok, tok,
                  pl.BlockSpec((1, D), lambda i, j: (0, 0))],
        out_specs=pl.BlockSpec((TB, D), lambda i, j: (i, 0)),
        out_shape=jax.ShapeDtypeStruct((T, D), F32),
        scratch_shapes=[pltpu.VMEM((EC, TB), F32), pltpu.VMEM((EC, TB), BF16), pltpu.VMEM((D, TB), F32)],
        compiler_params=_cparams(("parallel", "arbitrary")),
        name="peer_mix",
    )(h2t, x1, u, vt, na, ea, rb, eb, nfin)


def _pad_lanes(row):
    return jnp.pad(row, ((0, 0), (0, LANES - row.shape[1])))


def kernel(x_prompt, x_sample, state_rg_h, state_rg_conv, state_ssd_h, state_ssd_conv, norm_mix, w_in, rg_conv_w, rg_conv_b, rg_wa, rg_ba, rg_wx, rg_bx, rg_lam, w_rg_out, ssd_conv_w, ssd_conv_b, ssd_dt_bias, ssd_a_log, ssd_d, ssd_norm, w_ssd_out, w_out, norm_ffn, peer_wq, peer_keys, peer_u, peer_v, norm_final):
    depth = w_in.shape[0]
    B, L, D = x_prompt.shape
    Bs, Ls, _ = x_sample.shape
    assert depth == 1 and Ls == SAMPLE_ROWS // 2 and rg_conv_w.shape[1] == 4 and L >= 3
    W = rg_conv_w.shape[2]
    H = ssd_a_log.shape[1]
    inner = ssd_norm.shape[1]
    CD = ssd_conv_w.shape[2]
    P = inner // H
    N = state_ssd_h.shape[-1]
    G = (CD - inner) // (2 * N)
    R = H // G
    dims = (G, R, P, N)
    assert W == D and inner == 2 * D and CD == 3 * D and H % SUBLANES == 0 and H <= LANES
    o_x, o_g, o_z, o_xbc, o_dt, o_m = 0, W, 2 * W, 2 * W + inner, 2 * W + inner + CD, 2 * W + inner + CD + H
    col_xbc, col_x, col_z, col_g, col_ga, col_gb = 0, CD // W, (CD + W) // inner, (CD + W + inner) // W, \
        (CD + 2 * W + inner) // D, (CD + 2 * W + inner + D) // D

    yp = x_prompt.reshape(B * L, D)
    ys = x_sample.reshape(Bs * Ls, D)
    outs = {k: [] for k in ("prh", "prc", "psh", "psc", "srh", "src", "ssh", "ssc")}
    lead = SAMPLE_ROWS - Ls
    for l in range(depth):
        wi = w_in[l]
        w_main = jnp.concatenate([wi[:, o_xbc:o_dt], wi[:, o_x:o_g], wi[:, o_z:o_xbc], wi[:, o_g:o_z],
                                  wi[:, o_m:]], axis=1).astype(BF16)
        wdt = _pad_lanes(wi[:, o_dt:o_m])
        g_mix = norm_mix[l][None]
        rg_w = (rg_conv_w[l], rg_conv_b[l][None], rg_wa[l].astype(BF16), rg_wx[l].astype(BF16),
                rg_ba[l][None], rg_bx[l][None], rg_lam[l][None], w_rg_out[l].astype(BF16))
        ssd_w = (ssd_conv_w[l], ssd_conv_b[l][None], _pad_lanes(ssd_dt_bias[l][None]), ssd_dt_bias[l][:, None],
                 _pad_lanes(ssd_a_log[l][None]), ssd_a_log[l][:, None], jnp.repeat(ssd_d[l], P)[None],
                 ssd_norm[l][None], w_ssd_out[l].astype(BF16))
        wout = w_out[l].astype(BF16)
        nf = norm_ffn[l][None]
        wqt = peer_wq[l].astype(BF16).T
        keys = peer_keys[l].astype(BF16)
        u = peer_u[l].astype(BF16)
        vt = peer_v[l].astype(BF16).T
        nfin = norm_final[None]

        proj, dt, dtT = _inproj(yp, g_mix, w_main, wdt, H)
        ya, rgh = _rg_prompt(proj, B, L, col_x, col_g, *rg_w)
        yb, ssh = _ssd_prompt(proj, dt, dtT, B, L, col_xbc, col_z, dims, *ssd_w)
        p3 = proj.reshape(B, L, -1)
        outs["prh"].append(rgh.reshape(B, W))
        outs["prc"].append(p3[:, L - 3:, CD:CD + W])
        outs["psh"].append(ssh)
        outs["psc"].append(p3[:, L - 3:, :CD])
        x1, h2t = _merge(yp, ya, yb, proj, col_ga, col_gb, wout, nf)
        yp = _peer(h2t, x1, u, vt, *_route(h2t, wqt, keys), nfin)

        xe = jnp.pad(ys.reshape(Bs, Ls, D), ((0, 0), (lead, 0), (0, 0))).reshape(Bs * SAMPLE_ROWS, D)
        proj_e, dt_e, dtT_e = _inproj(xe, g_mix, w_main, wdt, H)
        rows = lambda a3: a3.reshape(Bs * SAMPLE_ROWS, a3.shape[-1])
        rg_cpad = rows(jnp.pad(state_rg_conv[l], ((0, 0), (lead - 3, Ls), (0, 0))))
        rg_hpad = rows(jnp.pad(state_rg_h[l][:, None, :], ((0, 0), (lead - 1, Ls), (0, 0))))
        ssd_cpad = rows(jnp.pad(state_ssd_conv[l], ((0, 0), (lead - 3, Ls), (0, 0))))
        ya_e, h_e = _rg_sample(proj_e, col_x, col_g, rg_cpad, rg_hpad, *rg_w)
        yb_e, ssh_s = _ssd_sample(proj_e, dt_e, dtT_e, ssd_cpad, state_ssd_h[l], col_xbc, col_z, dims, *ssd_w)
        toks = lambda a2: a2.reshape(Bs, SAMPLE_ROWS, -1)[:, lead:].reshape(Bs * Ls, -1)
        p3 = proj_e.reshape(Bs, SAMPLE_ROWS, -1)
        outs["srh"].append(h_e.reshape(Bs, SAMPLE_ROWS, W)[:, -1])
        outs["src"].append(p3[:, SAMPLE_ROWS - 3:, CD:CD + W])
        outs["ssh"].append(ssh_s)
        outs["ssc"].append(p3[:, SAMPLE_ROWS - 3:, :CD])
        proj_s = toks(proj_e)
        x1, h2t = _merge(ys, toks(ya_e), toks(yb_e), proj_s, col_ga, col_gb, wout, nf)
        ys = _peer(h2t, x1, u, vt, *_route(h2t, wqt, keys), nfin)

    st = lambda k: jnp.stack(outs[k])
    return (yp.reshape(B, L, D), ys.reshape(Bs, Ls, D), st("prh"), st("prc"), st("psh"), st("psc"),
            st("srh"), st("src"), st("ssh"), st("ssc"))
```

```python
import functools

import jax
import jax.numpy as jnp
from jax import lax
from jax.experimental import pallas as pl
from jax.experimental.pallas import tpu as pltpu

F32 = jnp.float32
BF16 = jnp.bfloat16
EPS = 1e-6
RG_C = 8.0
PEER_TOPK = 16
SSD_CHUNK = 128
LANES = 128
SUBLANES = 8
BF16_ROWS = 16
SAMPLE_ROWS = 8
NEG_INF = float("-inf")
LOG2E = 1.4426950408889634
HIGHEST = lax.Precision.HIGHEST
NT_DIMS = (((1,), (1,)), ((), ()))
VMEM_LIMIT = 56 * 1024 * 1024


def _cparams(sem):
    return pltpu.CompilerParams(dimension_semantics=sem, vmem_limit_bytes=VMEM_LIMIT)


def _pick(n, pref):
    t = min(n, pref)
    while n % t:
        t -= SUBLANES
    return t


def _sigmoid(x):
    return 1.0 / (1.0 + jnp.exp2(x * -LOG2E))


def _silu(x):
    return x * _sigmoid(x)


def _softplus(x):
    return jnp.maximum(x, 0.0) + jnp.log1p(jnp.exp(-jnp.abs(x)))


def _gelu(x):
    k1 = -2.0 * 0.7978845608028654 * LOG2E
    return x / (1.0 + jnp.exp2(x * (k1 + (k1 * 0.044715) * (x * x))))


def _rms(x, g):
    return x * lax.rsqrt(jnp.mean(x * x, axis=-1, keepdims=True) + EPS) * g


def _dot(a, b):
    return jnp.dot(a, b, preferred_element_type=F32)


def _dot_nt(a, b):
    return lax.dot_general(a, b, NT_DIMS, preferred_element_type=F32)


def _inproj_body(x_ref, g_ref, w_ref, wdt_ref, o_ref, odt_ref, odtT_ref, xn_ref):
    @pl.when(pl.program_id(1) == 0)
    def _():
        xn = _rms(x_ref[...], g_ref[...])
        xn_ref[...] = xn.astype(BF16)
        dt = jnp.dot(xn, wdt_ref[...], precision=HIGHEST, preferred_element_type=F32)
        odt_ref[...] = dt
        odtT_ref[...] = jnp.transpose(dt)[:odtT_ref.shape[0], :]

    o_ref[...] = _dot(xn_ref[...], w_ref[...])


def _inproj(x2d, g, w_main, wdt, H):
    T, D = x2d.shape
    N = w_main.shape[1]
    TM = _pick(T, 1024)
    TN = 3072
    assert N % TN == 0 and TM % LANES == 0
    return pl.pallas_call(
        _inproj_body,
        grid=(T // TM, N // TN),
        in_specs=[
            pl.BlockSpec((TM, D), lambda i, j: (i, 0)),
            pl.BlockSpec((1, D), lambda i, j: (0, 0)),
            pl.BlockSpec((D, TN), lambda i, j: (0, j)),
            pl.BlockSpec((D, LANES), lambda i, j: (0, 0)),
        ],
        out_specs=[
            pl.BlockSpec((TM, TN), lambda i, j: (i, j)),
            pl.BlockSpec((TM, LANES), lambda i, j: (i, 0)),
            pl.BlockSpec((H, TM), lambda i, j: (0, i)),
        ],
        out_shape=[
            jax.ShapeDtypeStruct((T, N), F32),
            jax.ShapeDtypeStruct((T, LANES), F32),
            jax.ShapeDtypeStruct((H, T), F32),
        ],
        scratch_shapes=[pltpu.VMEM((TM, D), BF16)],
        compiler_params=_cparams(("parallel", "arbitrary")),
        name="inproj",
    )(x2d, g, w_main, wdt)


def _conv_slab(x, pad, cw_ref, cb_ref, ls):
    L = x.shape[0]
    pad[SUBLANES:SUBLANES + L, :] = x
    out = (cb_ref[:, ls]
           + cw_ref[3:4, ls] * pad[8:8 + L, :]
           + cw_ref[2:3, ls] * pad[7:7 + L, :]
           + cw_ref[1:2, ls] * pad[6:6 + L, :]
           + cw_ref[0:1, ls] * pad[5:5 + L, :])
    pad[0:SUBLANES, :] = pad[L:L + SUBLANES, :]
    return out


def _rg_gates(xc, wa_ref, wx_ref, ba, bx, lam):
    xb = xc.astype(BF16)
    nb, bw = wa_ref.shape[0], wa_ref.shape[1]
    rs, gs = [], []
    for k in range(nb):
        xk = xb[:, k * bw:(k + 1) * bw]
        rs.append(_dot(xk, wa_ref[k]))
        gs.append(_dot(xk, wx_ref[k]))
    r = _sigmoid(jnp.concatenate(rs, axis=1) + ba)
    i = _sigmoid(jnp.concatenate(gs, axis=1) + bx)
    log_a = -RG_C * r * _softplus(-lam)
    a = jnp.exp(log_a)
    em1 = jnp.tanh(log_a) * (a * a + 1.0)
    return a, jnp.sqrt(-em1) * (i * xc)


def _rg_prompt_body(x_ref, gate_ref, cw_ref, cb_ref, wa_ref, wx_ref, ba_ref, bx_ref, lam_ref, wo_ref,
                    ya_ref, hfin_ref, pad_s, a_s, b_s, h_s):
    B, Lc, W = x_ref.shape
    NS = W // LANES
    pitch = a_s.shape[1] // B

    @pl.when(pl.program_id(0) == 0)
    def _():
        pad_s[:, :, 0:SUBLANES, :] = jnp.zeros((B, NS, SUBLANES, LANES), F32)
        h_s[...] = jnp.zeros(h_s.shape, F32)

    for b in range(B):
        xc = jnp.concatenate(
            [_conv_slab(x_ref[b, :, s * LANES:(s + 1) * LANES], pad_s.at[b, s], cw_ref, cb_ref,
                        slice(s * LANES, (s + 1) * LANES)) for s in range(NS)], axis=1)
        a, bt = _rg_gates(xc, wa_ref, wx_ref, ba_ref[...], bx_ref[...], lam_ref[...])
        for s in range(NS):
            a_s[s, b * pitch:b * pitch + Lc, :] = a[:, s * LANES:(s + 1) * LANES]
            b_s[s, b * pitch:b * pitch + Lc, :] = bt[:, s * LANES:(s + 1) * LANES]

    def step(t, hs):
        out = []
        for s in range(NS):
            rows = pl.ds(t, B, stride=pitch)
            h = a_s[s, rows, :] * hs[s] + b_s[s, rows, :]
            b_s[s, rows, :] = h
            out.append(h)
        return tuple(out)

    hs = lax.fori_loop(0, Lc, step, tuple(h_s[:, s * LANES:(s + 1) * LANES] for s in range(NS)), unroll=4)
    for s in range(NS):
        h_s[:, s * LANES:(s + 1) * LANES] = hs[s]
    hfin_ref[...] = h_s[...]
    for b in range(B):
        hb = jnp.concatenate([b_s[s, b * pitch:b * pitch + Lc, :] for s in range(NS)], axis=1)
        ya_ref[b] = _dot((hb * _gelu(gate_ref[b])).astype(BF16), wo_ref[...])


def _rg_prompt(proj, B, L, col_x, col_g, cw, cb, wa, wx, ba, bx, lam, wo):
    W = cw.shape[1]
    Lc = _pick(L, 128)
    pitch = Lc + SUBLANES // 2
    assert B == SUBLANES and W % LANES == 0
    proj3 = proj.reshape(B, L, proj.shape[1])
    full = lambda *s: pl.BlockSpec(s, lambda c: (0,) * len(s))
    ya, hfin = pl.pallas_call(
        _rg_prompt_body,
        grid=(L // Lc,),
        in_specs=[
            pl.BlockSpec((B, Lc, W), lambda c: (0, c, col_x)),
            pl.BlockSpec((B, Lc, W), lambda c: (0, c, col_g)),
            full(4, W), full(1, W), full(*wa.shape), full(*wx.shape), full(1, W), full(1, W), full(1, W),
            full(*wo.shape),
        ],
        out_specs=[
            pl.BlockSpec((B, Lc, wo.shape[1]), lambda c: (0, c, 0)),
            pl.BlockSpec((B, W), lambda c: (0, 0)),
        ],
        out_shape=[
            jax.ShapeDtypeStruct((B, L, wo.shape[1]), F32),
            jax.ShapeDtypeStruct((B, W), F32),
        ],
        scratch_shapes=[
            pltpu.VMEM((B, W // LANES, Lc + SUBLANES, LANES), F32),
            pltpu.VMEM((W // LANES, B * pitch, LANES), F32),
            pltpu.VMEM((W // LANES, B * pitch, LANES), F32),
            pltpu.VMEM((B, W), F32),
        ],
        compiler_params=_cparams(("arbitrary",)),
        name="rg_prompt",
    )(proj3, proj3, cw, cb, wa, wx, ba, bx, lam, wo)
    return ya.reshape(B * L, wo.shape[1]), hfin


def _conv_rolled(u, cw_ref, cb_ref):
    return (cb_ref[...]
            + cw_ref[3:4, :] * u
            + cw_ref[2:3, :] * pltpu.roll(u, 1, axis=0)
            + cw_ref[1:2, :] * pltpu.roll(u, 2, axis=0)
            + cw_ref[0:1, :] * pltpu.roll(u, 3, axis=0))


def _rg_sample_body(x_ref, gate_ref, cpad_ref, hpad_ref, cw_ref, cb_ref, wa_ref, wx_ref, ba_ref, bx_ref,
                    lam_ref, wo_ref, ya_ref, h_ref):
    xc = _conv_rolled(x_ref[...] + cpad_ref[...], cw_ref, cb_ref)
    a, bt = _rg_gates(xc, wa_ref, wx_ref, ba_ref[...], bx_ref[...], lam_ref[...])
    row = lax.broadcasted_iota(jnp.int32, a.shape, 0) & (SAMPLE_ROWS - 1)
    h = hpad_ref[...]
    for k in range(SAMPLE_ROWS // 2, SAMPLE_ROWS):
        h = jnp.where(row == k, a * pltpu.roll(h, 1, axis=0) + bt, h)
    h_ref[...] = h
    ya_ref[...] = _dot((h * _gelu(gate_ref[...])).astype(BF16), wo_ref[...])


def _rg_sample(proj, col_x, col_g, cpad, hpad, cw, cb, wa, wx, ba, bx, lam, wo):
    T = proj.shape[0]
    W = cw.shape[1]
    TM = _pick(T, 256)
    full = lambda *s: pl.BlockSpec(s, lambda i: (0,) * len(s))
    return pl.pallas_call(
        _rg_sample_body,
        grid=(T // TM,),
        in_specs=[
            pl.BlockSpec((TM, W), lambda i: (i, col_x)),
            pl.BlockSpec((TM, W), lambda i: (i, col_g)),
            pl.BlockSpec((TM, W), lambda i: (i, 0)),
            pl.BlockSpec((TM, W), lambda i: (i, 0)),
            full(4, W), full(1, W), full(*wa.shape), full(*wx.shape), full(1, W), full(1, W), full(1, W),
            full(*wo.shape),
        ],
        out_specs=[
            pl.BlockSpec((TM, wo.shape[1]), lambda i: (i, 0)),
            pl.BlockSpec((TM, W), lambda i: (i, 0)),
        ],
        out_shape=[
            jax.ShapeDtypeStruct((T, wo.shape[1]), F32),
            jax.ShapeDtypeStruct((T, W), F32),
        ],
        compiler_params=_cparams(("parallel",)),
        name="rg_sample",
    )(proj, proj, cpad, hpad, cw, cb, wa, wx, ba, bx, lam, wo)


def _ssd_chunk(act_s, dt, dtT, alog_ref, alogT_ref, dexp_ref, seq_shift, G, R, P, N,
               yint_fn, y_s, xw_s):
    Q = act_s.shape[0]
    inner = G * R * P
    a = dt * (-jnp.exp(alog_ref[...]))
    aT = dtT * (-jnp.exp(alogT_ref[...]))
    ri = lax.broadcasted_iota(jnp.int32, (Q, Q), 0)
    ci = lax.broadcasted_iota(jnp.int32, (Q, Q), 1)
    same = (ri >> seq_shift) == (ci >> seq_shift)
    causal = same & (ci <= ri)
    causal_f = jnp.where(causal, 1.0, 0.0).astype(F32)
    same_f = jnp.where(same, 1.0, 0.0).astype(F32)
    acum = jnp.dot(causal_f, a, precision=HIGHEST, preferred_element_type=F32)
    tot = jnp.dot(same_f, a, precision=HIGHEST, preferred_element_type=F32)
    acumT = lax.dot_general(aT, causal_f, NT_DIMS, precision=HIGHEST,
                            preferred_element_type=F32)
    eacum = jnp.exp(acum)
    toend = jnp.exp(tot - acum) * dt
    for g in range(G):
        Bg = act_s[:, inner + g * N:inner + (g + 1) * N].astype(BF16)
        Cg = act_s[:, inner + (G + g) * N:inner + (G + g + 1) * N].astype(BF16)
        cb = _dot_nt(Cg, Bg)
        yint = yint_fn(g, Cg)
        for r in range(R):
            h = g * R + r
            sl = slice(h * P, (h + 1) * P)
            seg = acum[:, h:h + 1] - acumT[h:h + 1, :]
            decay = jnp.exp(jnp.where(causal, seg, NEG_INF))
            wgt = (cb * decay * dtT[h:h + 1, :]).astype(BF16)
            xh = act_s[:, sl]
            yh = _dot(wgt, xh.astype(BF16)) + dexp_ref[:, sl] * xh
            if yint is not None:
                yh = yh + yint[:, r * P:(r + 1) * P] * eacum[:, h:h + 1]
            y_s[:, sl] = yh
            xw_s[:, sl] = xh * toend[:, h:h + 1]
    return tot, eacum


def _expand_heads(x, ex_ref):
    hi = x.astype(BF16)
    r1 = x - hi.astype(F32)
    mid = r1.astype(BF16)
    lo = (r1 - mid.astype(F32)).astype(BF16)
    e = ex_ref[...]
    return _dot(hi, e) + _dot(mid, e) + _dot(lo, e)


def _ssd_finish(y, z, nw_ref, wo_ref):
    y = y * _silu(z)
    return _dot(_rms(y, nw_ref[...]).astype(BF16), wo_ref[...])


def _ssd_prompt_body(G, R, P, N,
                     xbc_ref, z_ref, dt_ref, dtT_ref, cw_ref, cb_ref, dtb_ref, dtbT_ref, alog_ref, alogT_ref,
                     dexp_ref, nw_ref, wo_ref, ex_ref, yb_ref, st_ref, pad_s, act_s, y_s, stT_s):
    Q = xbc_ref.shape[0]
    inner = G * R * P
    GW = R * P

    @pl.when(pl.program_id(1) == 0)
    def _():
        pad_s[:, 0:SUBLANES, :] = jnp.zeros((pad_s.shape[0], SUBLANES, LANES), F32)
        stT_s[...] = jnp.zeros(stT_s.shape, F32)

    for s in range(pad_s.shape[0]):
        ls = slice(s * LANES, (s + 1) * LANES)
        act_s[:, ls] = _silu(_conv_slab(xbc_ref[:, ls], pad_s.at[s], cw_ref, cb_ref, ls))
    dt = _softplus(dt_ref[...] + dtb_ref[...])
    dtT = _softplus(dtT_ref[...] + dtbT_ref[...])
    a = dt * (-jnp.exp(alog_ref[...]))
    aT = dtT * (-jnp.exp(alogT_ref[...]))
    ri = lax.broadcasted_iota(jnp.int32, (Q, Q), 0)
    ci = lax.broadcasted_iota(jnp.int32, (Q, Q), 1)
    causal = ci <= ri
    causal_f = jnp.where(causal, 1.0, 0.0).astype(F32)
    hi = dict(precision=HIGHEST, preferred_element_type=F32)
    acum = jnp.dot(causal_f, a, **hi)
    tot = jnp.dot(jnp.ones((SUBLANES, Q), F32), a, **hi)
    acumT = lax.dot_general(aT, causal_f, NT_DIMS, **hi)
    fx = _expand_heads(jnp.concatenate([jnp.exp(acum), jnp.exp(tot[0:1, :] - acum) * dt, jnp.exp(tot)], axis=0),
                       ex_ref)
    eacum_x, toend_x, dec_x = fx[0:Q], fx[Q:2 * Q], fx[2 * Q:2 * Q + 1]
    col2 = acum * LOG2E
    row2 = (acumT - jnp.log(dtT)) * LOG2E
    lane = lax.broadcasted_iota(jnp.int32, (Q, LANES), 1)
    for g in range(G):
        gs = slice(g * GW, (g + 1) * GW)
        Bf = act_s[:, inner + g * N:inner + (g + 1) * N]
        Bg = Bf.astype(BF16)
        Cg = act_s[:, inner + (G + g) * N:inner + (G + g + 1) * N].astype(BF16)
        cb = jnp.where(causal, _dot_nt(Cg, Bg), 0.0)
        y_s[:, gs] = (_dot(Cg, stT_s[:, gs].astype(BF16)) * eacum_x[:, gs]
                      + dexp_ref[:, gs] * act_s[:, gs])
        for pr in range(R // 2):
            h0 = g * R + 2 * pr
            ps = slice(h0 * P, (h0 + 2) * P)
            wg = []
            for h in (h0, h0 + 1):
                e = jnp.exp2(jnp.where(causal, col2[:, h:h + 1] - row2[h:h + 1, :], NEG_INF))
                wg.append((cb * e).astype(BF16))
            xp = act_s[:, ps]
            xa = jnp.where(lane < P, xp, 0.0).astype(BF16)
            xb = jnp.where(lane < P, 0.0, xp).astype(BF16)
            y_s[:, ps] = y_s[:, ps] + _dot(jnp.concatenate(wg, axis=1), jnp.concatenate([xa, xb], axis=0))
        xw = (act_s[:, gs] * toend_x[:, gs]).astype(BF16)
        ST = _dot(jnp.transpose(Bf).astype(BF16), xw)
        stT_s[:, gs] = dec_x[:, gs] * stT_s[:, gs] + ST
    yb_ref[...] = _ssd_finish(y_s[...], z_ref[...], nw_ref, wo_ref)

    @pl.when(pl.program_id(1) == pl.num_programs(1) - 1)
    def _():
        for g in range(G):
            st_ref[0, g * R:(g + 1) * R] = jnp.transpose(stT_s[:, g * GW:(g + 1) * GW]).reshape(R, P, N)


def _ssd_prompt(proj, dt, dtT, B, L, col_xbc, col_z, dims, cw, cb, dtb, dtbT, alog, alogT, dexp, nw, wo):
    G, R, P, N = dims
    H = G * R
    inner = H * P
    CD = cw.shape[1]
    Q = _pick(L, SSD_CHUNK)
    nC = L // Q
    assert 2 * P == LANES and R % 2 == 0 and N == LANES
    head_of_lane = jnp.arange(inner, dtype=jnp.int32) // P
    ex = (jnp.arange(LANES, dtype=jnp.int32)[:, None] == head_of_lane[None, :]).astype(BF16)
    full = lambda *s: pl.BlockSpec(s, lambda b, c: (0,) * len(s))
    return pl.pallas_call(
        functools.partial(_ssd_prompt_body, G, R, P, N),
        grid=(B, nC),
        in_specs=[
            pl.BlockSpec((Q, CD), lambda b, c: (b * nC + c, col_xbc)),
            pl.BlockSpec((Q, inner), lambda b, c: (b * nC + c, col_z)),
            pl.BlockSpec((Q, LANES), lambda b, c: (b * nC + c, 0)),
            pl.BlockSpec((H, Q), lambda b, c: (0, b * nC + c)),
            full(4, CD), full(1, CD), full(1, LANES), full(H, 1), full(1, LANES), full(H, 1),
            full(1, inner), full(1, inner), full(*wo.shape), full(LANES, inner),
        ],
        out_specs=[
            pl.BlockSpec((Q, wo.shape[1]), lambda b, c: (b * nC + c, 0)),
            pl.BlockSpec((1, H, P, N), lambda b, c: (b, 0, 0, 0)),
        ],
        out_shape=[
            jax.ShapeDtypeStruct((B * L, wo.shape[1]), F32),
            jax.ShapeDtypeStruct((B, H, P, N), F32),
        ],
        scratch_shapes=[
            pltpu.VMEM((CD // LANES, Q + SUBLANES, LANES), F32),
            pltpu.VMEM((Q, CD), F32),
            pltpu.VMEM((Q, inner), F32),
            pltpu.VMEM((N, inner), F32),
        ],
        compiler_params=_cparams(("parallel", "arbitrary")),
        name="ssd_prompt",
    )(proj, proj, dt, dtT, cw, cb, dtb, dtbT, alog, alogT, dexp, nw, wo, ex)


def _ssd_sample_body(G, R, P, N,
                     xbc_ref, cpad_ref, z_ref, dt_ref, dtT_ref, h0_ref, cw_ref, cb_ref, dtb_ref, dtbT_ref,
                     alog_ref, alogT_ref, dexp_ref, nw_ref, wo_ref, yb_ref, st_ref,
                     act_s, y_s, xw_s, yint_s, xwT_s, tot_s, eacum_s):
    Q = xbc_ref.shape[0]
    inner = G * R * P
    bi = pl.program_id(1)
    half = SAMPLE_ROWS // 2

    @pl.when(bi == 0)
    def _():
        act_s[...] = _silu(_conv_rolled(xbc_ref[...] + cpad_ref[...], cw_ref, cb_ref))
        rows = lax.broadcasted_iota(jnp.int32, (Q, LANES), 0) & (SAMPLE_ROWS - 1)
        cols = lax.broadcasted_iota(jnp.int32, dtT_ref.shape, 1) & (SAMPLE_ROWS - 1)
        dt = jnp.where(rows >= half, _softplus(dt_ref[...] + dtb_ref[...]), 0.0)
        dtT = jnp.where(cols >= half, _softplus(dtT_ref[...] + dtbT_ref[...]), 0.0)
        seq_shift = SAMPLE_ROWS.bit_length() - 1
        tot, eacum = _ssd_chunk(act_s, dt, dtT, alog_ref, alogT_ref, dexp_ref, seq_shift, G, R, P, N,
                                lambda g, Cg: None, y_s, xw_s)
        tot_s[...] = tot
        eacum_s[...] = eacum
        for g in range(G):
            xwT_s[g] = jnp.transpose(xw_s[:, g * R * P:(g + 1) * R * P]).astype(BF16)

    r0 = pl.multiple_of(bi * SAMPLE_ROWS, SAMPLE_ROWS)
    rid = lax.broadcasted_iota(jnp.int32, (Q, N), 0)
    mine = (rid >= r0) & (rid < r0 + SAMPLE_ROWS)
    dec = jnp.exp(tot_s[pl.ds(r0, 1), :])
    for g in range(G):
        Cb = act_s[pl.ds(r0, SAMPLE_ROWS), inner + (G + g) * N:inner + (G + g + 1) * N]
        stg = h0_ref[0, g * R:(g + 1) * R].reshape(R * P, N)
        yint_s[pl.ds(r0, SAMPLE_ROWS), g * R * P:(g + 1) * R * P] = _dot_nt(Cb, stg)
        Bg = jnp.where(mine, act_s[:, inner + g * N:inner + (g + 1) * N], 0.0).astype(BF16)
        S = _dot(xwT_s[g], Bg)
        for r in range(R):
            h = g * R + r
            st_ref[0, h] = dec[:, h:h + 1] * h0_ref[0, h] + S[r * P:(r + 1) * P, :]

    @pl.when(bi == pl.num_programs(1) - 1)
    def _():
        for h in range(G * R):
            sl = slice(h * P, (h + 1) * P)
            y_s[:, sl] = y_s[:, sl] + yint_s[:, sl] * eacum_s[:, h:h + 1]
        yb_ref[...] = _ssd_finish(y_s[...], z_ref[...], nw_ref, wo_ref)


def _ssd_sample(proj, dt, dtT, cpad, h0, col_xbc, col_z, dims, cw, cb, dtb, dtbT, alog, alogT, dexp, nw, wo):
    G, R, P, N = dims
    H = G * R
    inner = H * P
    CD = cw.shape[1]
    T = proj.shape[0]
    Bs = h0.shape[0]
    Q = _pick(T, SSD_CHUNK)
    nb = Q // SAMPLE_ROWS
    full = lambda *s: pl.BlockSpec(s, lambda i, j: (0,) * len(s))
    return pl.pallas_call(
        functools.partial(_ssd_sample_body, G, R, P, N),
        grid=(T // Q, nb),
        in_specs=[
            pl.BlockSpec((Q, CD), lambda i, j: (i, col_xbc)),
            pl.BlockSpec((Q, CD), lambda i, j: (i, 0)),
            pl.BlockSpec((Q, inner), lambda i, j: (i, col_z)),
            pl.BlockSpec((Q, LANES), lambda i, j: (i, 0)),
            pl.BlockSpec((H, Q), lambda i, j: (0, i)),
            pl.BlockSpec((1, H, P, N), lambda i, j: (i * nb + j, 0, 0, 0)),
            full(4, CD), full(1, CD), full(1, LANES), full(H, 1), full(1, LANES), full(H, 1),
            full(1, inner), full(1, inner), full(*wo.shape),
        ],
        out_specs=[
            pl.BlockSpec((Q, wo.shape[1]), lambda i, j: (i, 0)),
            pl.BlockSpec((1, H, P, N), lambda i, j: (i * nb + j, 0, 0, 0)),
        ],
        out_shape=[
            jax.ShapeDtypeStruct((T, wo.shape[1]), F32),
            jax.ShapeDtypeStruct((Bs, H, P, N), F32),
        ],
        scratch_shapes=[
            pltpu.VMEM((Q, CD), F32),
            pltpu.VMEM((Q, inner), F32),
            pltpu.VMEM((Q, inner), F32),
            pltpu.VMEM((Q, inner), F32),
            pltpu.VMEM((G, R * P, Q), BF16),
            pltpu.VMEM((Q, LANES), F32),
            pltpu.VMEM((Q, LANES), F32),
        ],
        compiler_params=_cparams(("parallel", "arbitrary")),
        name="ssd_sample",
    )(proj, cpad, proj, dt, dtT, h0, cw, cb, dtb, dtbT, alog, alogT, dexp, nw, wo)


def _merge_body(x_ref, ya_ref, yb_ref, ga_ref, gb_ref, wout_ref, nf_ref, x1_ref, h2t_ref):
    m = _sigmoid(ga_ref[...]) * ya_ref[...] + _sigmoid(gb_ref[...]) * yb_ref[...]
    x1 = x_ref[...] + _dot(m.astype(BF16), wout_ref[...])
    x1_ref[...] = x1
    h2t_ref[...] = jnp.transpose(_rms(x1, nf_ref[...])).astype(BF16)


def _merge(x2d, ya, yb, proj, col_ga, col_gb, wout, nf):
    T, D = x2d.shape
    TM = _pick(T, 512)
    row = lambda c: pl.BlockSpec((TM, D), lambda i: (i, c))
    return pl.pallas_call(
        _merge_body,
        grid=(T // TM,),
        in_specs=[row(0), row(0), row(0), row(col_ga), row(col_gb),
                  pl.BlockSpec(wout.shape, lambda i: (0, 0)), pl.BlockSpec((1, D), lambda i: (0, 0))],
        out_specs=[row(0), pl.BlockSpec((D, TM), lambda i: (0, i))],
        out_shape=[jax.ShapeDtypeStruct((T, D), F32), jax.ShapeDtypeStruct((D, T), BF16)],
        compiler_params=_cparams(("parallel",)),
        name="merge",
    )(x2d, ya, yb, proj, proj, wout, nf)


def _staircase(k):
    return [(ka, k // (ka + 1)) for ka in range(k)]


def _take_top(cur, k_top, want_rank):
    rank = jnp.full(cur.shape, float(k_top), F32) if want_rank else None
    vals = []
    for k in range(k_top):
        m = jnp.max(cur, axis=0, keepdims=True)
        hit = cur == m
        vals.append(m)
        if want_rank:
            rank = jnp.where(hit, float(k), rank)
        cur = jnp.where(hit, NEG_INF, cur)
    taken = jnp.sum(jnp.where(cur == NEG_INF, 1.0, 0.0), axis=0, keepdims=True)
    return vals, rank, taken


def _take_top_ties(x, k_top, val_ref, rank_ref):
    rows = lax.broadcasted_iota(jnp.int32, x.shape, 0)
    kk = lax.broadcasted_iota(jnp.int32, (k_top, x.shape[1]), 0)

    def body(k, carry):
        cur, rank, vals = carry
        m = jnp.max(cur, axis=0, keepdims=True)
        first = jnp.min(jnp.where(cur == m, rows, x.shape[0]), axis=0, keepdims=True)
        hit = rows == first
        return (jnp.where(hit, NEG_INF, cur), jnp.where(hit, k.astype(F32), rank),
                jnp.where(kk == k, m, vals))

    init = (x, jnp.full(x.shape, float(k_top), F32), jnp.zeros((k_top, x.shape[1]), F32))
    _, rank, vals = lax.fori_loop(0, k_top, body, init)
    val_ref[...] = vals
    rank_ref[...] = rank


def _route_body(h2t_ref, wqt_ref, keys_ref, na_ref, ea_ref, rb_ref, eb_ref, sv_s, rk_s, cand_s, cv_s, cr_s):
    NH = keys_ref.shape[0]
    NK, KH = keys_ref.shape[2], keys_ref.shape[3]
    K = PEER_TOPK
    TB = h2t_ref.shape[1]
    tiles = [slice(tl * LANES, (tl + 1) * LANES) for tl in range(TB // LANES)]
    qt = _dot(wqt_ref[...], h2t_ref[...]).astype(BF16)
    cand_s[...] = jnp.full(cand_s.shape, NEG_INF, F32)

    def candidates():
        off = 0
        for ka, nb in _staircase(K):
            cand_s[off:off + nb, :] = sv_s[0, ka:ka + 1, :] + sv_s[1, 0:nb, :]
            off += nb

    def emit(h, sT, in_top, rb, sel, na_of):
        top = sv_s[0, 0:1, :] + sv_s[1, 0:1, :]
        z = jnp.sum(jnp.where(sel, jnp.exp(cand_s[...] - top), 0.0), axis=0, keepdims=True)
        ex = [jnp.where(in_top[s], jnp.exp(sT[s] - sv_s[s, 0:1, :]), 0.0) for s in range(2)]
        na = jnp.zeros(sT[0].shape, F32)
        off = 0
        for ka, nb in _staircase(K):
            cnt = jnp.sum(jnp.where(sel[off:off + nb, :], 1.0, 0.0), axis=0, keepdims=True)
            na = jnp.where(na_of(ka), cnt, na)
            off += nb
        na_ref[h * NK:(h + 1) * NK, :] = na
        ea_ref[h * NK:(h + 1) * NK, :] = ex[0] / z
        rb_ref[h] = rb.astype(BF16)
        eb_ref[h] = ex[1].astype(BF16)

    def scores(h):
        return [_dot(keys_ref[h, s], qt[(h * 2 + s) * KH:(h * 2 + s + 1) * KH, :]) for s in range(2)]

    most = []
    for h in range(NH):
        sT = scores(h)
        seen = jnp.zeros((1, LANES), F32)
        rbs = []
        for s in range(2):
            for ln in tiles:
                vals, rank, taken = _take_top(sT[s][:, ln], K, s == 1)
                for k in range(K):
                    sv_s[s, k:k + 1, ln] = vals[k]
                seen = jnp.maximum(seen, taken)
                if s == 1:
                    rbs.append(rank)
        candidates()
        cand = cand_s[...]
        tau = _take_top(cand, K, False)[0][K - 1]
        sel = cand >= tau
        most.append(jnp.maximum(seen, jnp.max(jnp.sum(jnp.where(sel, 1.0, 0.0), axis=0, keepdims=True),
                                              axis=1, keepdims=True)))
        emit(h, sT, [sT[s] >= sv_s[s, K - 1:K, :] for s in range(2)], jnp.concatenate(rbs, axis=1), sel,
             lambda ka: sT[0] == sv_s[0, ka:ka + 1, :])

    for h in range(NH):
        @pl.when(jnp.max(most[h]) > K)
        def _(h=h):
            sT = scores(h)
            for s in range(2):
                for ln in tiles:
                    _take_top_ties(sT[s][:, ln], K, sv_s.at[s, :, ln], rk_s.at[s, :, ln])
            candidates()
            for ln in tiles:
                _take_top_ties(cand_s[:, ln], K, cv_s.at[:, ln], cr_s.at[:, ln])
            emit(h, sT, [rk_s[s] < K for s in range(2)], rk_s[1], cr_s[...] < K,
                 lambda ka: rk_s[0] == float(ka))


def _route(h2t, wqt, keys):
    D, T = h2t.shape
    NH, _, NK, KH = keys.shape
    TB = _pick(T, 512)
    assert TB % LANES == 0
    ncand = sum(nb for _, nb in _staircase(PEER_TOPK))
    ncand_pad = -(-ncand // SUBLANES) * SUBLANES
    flat = pl.BlockSpec((NH * NK, TB), lambda i: (0, i))
    tok = pl.BlockSpec((NH, NK, TB), lambda i: (0, 0, i))
    return pl.pallas_call(
        _route_body,
        grid=(T // TB,),
        in_specs=[pl.BlockSpec((D, TB), lambda i: (0, i)),
                  pl.BlockSpec(wqt.shape, lambda i: (0, 0)),
                  pl.BlockSpec(keys.shape, lambda i: (0, 0, 0, 0))],
        out_specs=[flat, flat, tok, tok],
        out_shape=[jax.ShapeDtypeStruct((NH * NK, T), F32), jax.ShapeDtypeStruct((NH * NK, T), F32),
                   jax.ShapeDtypeStruct((NH, NK, T), BF16), jax.ShapeDtypeStruct((NH, NK, T), BF16)],
        scratch_shapes=[pltpu.VMEM((2, PEER_TOPK, TB), F32), pltpu.VMEM((2, NK, TB), F32),
                        pltpu.VMEM((ncand_pad, TB), F32), pltpu.VMEM((PEER_TOPK, TB), F32),
                        pltpu.VMEM((ncand_pad, TB), F32)],
        compiler_params=_cparams(("parallel",)),
        name="peer_route",
    )(h2t, wqt, keys)


def _peer_body(h2t_ref, x1_ref, u_ref, vt_ref, na_ref, ea_ref, rb_ref, eb_ref, nfin_ref, y_ref,
               s_s, a_s, acc_s):
    ec = pl.program_id(1)
    NH, NK, TB = rb_ref.shape
    ni = u_ref.shape[0] // NK

    @pl.when(ec == 0)
    def _():
        acc_s[...] = jnp.zeros(acc_s.shape, F32)

    s_s[...] = _dot(u_ref[...], h2t_ref[...])

    def row_tile(grp, il):
        return jnp.broadcast_to(grp[il:il + 1, :], (BF16_ROWS, LANES)).astype(BF16)

    for tl in range(TB // LANES):
        ln = slice(tl * LANES, (tl + 1) * LANES)
        rows = [pl.ds(pl.multiple_of(h * NK + ec * ni, SUBLANES), ni) for h in range(NH)]
        na = [na_ref[rows[h], ln] for h in range(NH)]
        ea = [ea_ref[rows[h], ln] for h in range(NH)]
        for il in range(ni):
            na_t = [row_tile(na[h], il) for h in range(NH)]
            ea_t = [row_tile(ea[h], il) for h in range(NH)]
            for jt in range(NK // BF16_ROWS):
                js = slice(jt * BF16_ROWS, (jt + 1) * BF16_ROWS)
                w = jnp.zeros((BF16_ROWS, LANES), BF16)
                for h in range(NH):
                    w = w + ea_t[h] * jnp.where(rb_ref[h, js, ln] < na_t[h], eb_ref[h, js, ln], 0.0)
                e = slice(il * NK + jt * BF16_ROWS, il * NK + (jt + 1) * BF16_ROWS)
                a_s[e, ln] = _gelu(s_s[e, ln]).astype(BF16) * w
    acc_s[...] += _dot(vt_ref[...], a_s[...])

    @pl.when(ec == pl.num_programs(1) - 1)
    def _():
        x2 = x1_ref[...] + jnp.transpose(acc_s[...])
        y_ref[...] = _rms(x2, nfin_ref[...])


def _peer(h2t, x1, u, vt, na, ea, rb, eb, nfin):
    D, T = h2t.shape
    E = u.shape[0]
    NH, NK, _ = rb.shape
    TB = _pick(T, 512)
    EC = 2 * SUBLANES * NK
    assert TB % LANES == 0 and E % EC == 0 and NK % BF16_ROWS == 0
    tok = pl.BlockSpec((NH, NK, TB), lambda i, j: (0, 0, i))
    flat = pl.BlockSpec((NH * NK, TB), lambda i, j: (0, i))
    return pl.pallas_call(
        _peer_body,
        grid=(T // TB, E // EC),
        in_specs=[pl.BlockSpec((D, TB), lambda i, j: (0, i)),
                  pl.BlockSpec((TB, D), lambda i, j: (i, 0)),
                  pl.BlockSpec((EC, D), lambda i, j: (j, 0)),
                  pl.BlockSpec((D, EC), lambda i, j: (0, j)),
                  flat, flat, tok, tok,
                  pl.BlockSpec((1, D), lambda i, j: (0, 0))],
        out_specs=pl.BlockSpec((TB, D), lambda i, j: (i, 0)),
        out_shape=jax.ShapeDtypeStruct((T, D), F32),
        scratch_shapes=[pltpu.VMEM((EC, TB), F32), pltpu.VMEM((EC, TB), BF16), pltpu.VMEM((D, TB), F32)],
        compiler_params=_cparams(("parallel", "arbitrary")),
        name="peer_mix",
    )(h2t, x1, u, vt, na, ea, rb, eb, nfin)


def _pad_lanes(row):
    return jnp.pad(row, ((0, 0), (0, LANES - row.shape[1])))


def kernel(x_prompt, x_sample, state_rg_h, state_rg_conv, state_ssd_h, state_ssd_conv, norm_mix, w_in, rg_conv_w, rg_conv_b, rg_wa, rg_ba, rg_wx, rg_bx, rg_lam, w_rg_out, ssd_conv_w, ssd_conv_b, ssd_dt_bias, ssd_a_log, ssd_d, ssd_norm, w_ssd_out, w_out, norm_ffn, peer_wq, peer_keys, peer_u, peer_v, norm_final):
    depth = w_in.shape[0]
    B, L, D = x_prompt.shape
    Bs, Ls, _ = x_sample.shape
    assert depth == 1 and Ls == SAMPLE_ROWS // 2 and rg_conv_w.shape[1] == 4 and L >= 3
    W = rg_conv_w.shape[2]
    H = ssd_a_log.shape[1]
    inner = ssd_norm.shape[1]
    CD = ssd_conv_w.shape[2]
    P = inner // H
    N = state_ssd_h.shape[-1]
    G = (CD - inner) // (2 * N)
    R = H // G
    dims = (G, R, P, N)
    assert W == D and inner == 2 * D and CD == 3 * D and H % SUBLANES == 0 and H <= LANES
    o_x, o_g, o_z, o_xbc, o_dt, o_m = 0, W, 2 * W, 2 * W + inner, 2 * W + inner + CD, 2 * W + inner + CD + H
    col_xbc, col_x, col_z, col_g, col_ga, col_gb = 0, CD // W, (CD + W) // inner, (CD + W + inner) // W, \
        (CD + 2 * W + inner) // D, (CD + 2 * W + inner + D) // D

    yp = x_prompt.reshape(B * L, D)
    ys = x_sample.reshape(Bs * Ls, D)
    outs = {k: [] for k in ("prh", "prc", "psh", "psc", "srh", "src", "ssh", "ssc")}
    lead = SAMPLE_ROWS - Ls
    for l in range(depth):
        wi = w_in[l]
        w_main = jnp.concatenate([wi[:, o_xbc:o_dt], wi[:, o_x:o_g], wi[:, o_z:o_xbc], wi[:, o_g:o_z],
                                  wi[:, o_m:]], axis=1).astype(BF16)
        wdt = _pad_lanes(wi[:, o_dt:o_m])
        g_mix = norm_mix[l][None]
        rg_w = (rg_conv_w[l], rg_conv_b[l][None], rg_wa[l].astype(BF16), rg_wx[l].astype(BF16),
                rg_ba[l][None], rg_bx[l][None], rg_lam[l][None], w_rg_out[l].astype(BF16))
        ssd_w = (ssd_conv_w[l], ssd_conv_b[l][None], _pad_lanes(ssd_dt_bias[l][None]), ssd_dt_bias[l][:, None],
                 _pad_lanes(ssd_a_log[l][None]), ssd_a_log[l][:, None], jnp.repeat(ssd_d[l], P)[None],
                 ssd_norm[l][None], w_ssd_out[l].astype(BF16))
        wout = w_out[l].astype(BF16)
        nf = norm_ffn[l][None]
        wqt = peer_wq[l].astype(BF16).T
        keys = peer_keys[l].astype(BF16)
        u = peer_u[l].astype(BF16)
        vt = peer_v[l].astype(BF16).T
        nfin = norm_final[None]

        proj, dt, dtT = _inproj(yp, g_mix, w_main, wdt, H)
        ya, rgh = _rg_prompt(proj, B, L, col_x, col_g, *rg_w)
        yb, ssh = _ssd_prompt(proj, dt, dtT, B, L, col_xbc, col_z, dims, *ssd_w)
        p3 = proj.reshape(B, L, -1)
        outs["prh"].append(rgh.reshape(B, W))
        outs["prc"].append(p3[:, L - 3:, CD:CD + W])
        outs["psh"].append(ssh)
        outs["psc"].append(p3[:, L - 3:, :CD])
        x1, h2t = _merge(yp, ya, yb, proj, col_ga, col_gb, wout, nf)
        yp = _peer(h2t, x1, u, vt, *_route(h2t, wqt, keys), nfin)

        xe = jnp.pad(ys.reshape(Bs, Ls, D), ((0, 0), (lead, 0), (0, 0))).reshape(Bs * SAMPLE_ROWS, D)
        proj_e, dt_e, dtT_e = _inproj(xe, g_mix, w_main, wdt, H)
        rows = lambda a3: a3.reshape(Bs * SAMPLE_ROWS, a3.shape[-1])
        rg_cpad = rows(jnp.pad(state_rg_conv[l], ((0, 0), (lead - 3, Ls), (0, 0))))
        rg_hpad = rows(jnp.pad(state_rg_h[l][:, None, :], ((0, 0), (lead - 1, Ls), (0, 0))))
        ssd_cpad = rows(jnp.pad(state_ssd_conv[l], ((0, 0), (lead - 3, Ls), (0, 0))))
        ya_e, h_e = _rg_sample(proj_e, col_x, col_g, rg_cpad, rg_hpad, *rg_w)
        yb_e, ssh_s = _ssd_sample(proj_e, dt_e, dtT_e, ssd_cpad, state_ssd_h[l], col_xbc, col_z, dims, *ssd_w)
        toks = lambda a2: a2.reshape(Bs, SAMPLE_ROWS, -1)[:, lead:].reshape(Bs * Ls, -1)
        p3 = proj_e.reshape(Bs, SAMPLE_ROWS, -1)
        outs["srh"].append(h_e.reshape(Bs, SAMPLE_ROWS, W)[:, -1])
        outs["src"].append(p3[:, SAMPLE_ROWS - 3:, CD:CD + W])
        outs["ssh"].append(ssh_s)
        outs["ssc"].append(p3[:, SAMPLE_ROWS - 3:, :CD])
        proj_s = toks(proj_e)
        x1, h2t = _merge(ys, toks(ya_e), toks(yb_e), proj_s, col_ga, col_gb, wout, nf)
        ys = _peer(h2t, x1, u, vt, *_route(h2t, wqt, keys), nfin)

    st = lambda k: jnp.stack(outs[k])
    return (yp.reshape(B, L, D), ys.reshape(Bs, Ls, D), st("prh"), st("prc"), st("psh"), st("psc"),
            st("srh"), st("src"), st("ssh"), st("ssc"))
```

```python
import functools

import jax
import jax.numpy as jnp
from jax import lax
from jax.experimental import pallas as pl
from jax.experimental.pallas import tpu as pltpu

F32 = jnp.float32
BF16 = jnp.bfloat16
EPS = 1e-6
RG_C = 8.0
PEER_TOPK = 16
SSD_CHUNK = 128
LANES = 128
SUBLANES = 8
BF16_ROWS = 16
SAMPLE_ROWS = 8
NEG_INF = float("-inf")
LOG2E = 1.4426950408889634
HIGHEST = lax.Precision.HIGHEST
NT_DIMS = (((1,), (1,)), ((), ()))
VMEM_LIMIT = 56 * 1024 * 1024


def _cparams(sem):
    return pltpu.CompilerParams(dimension_semantics=sem, vmem_limit_bytes=VMEM_LIMIT)


def _pick(n, pref):
    t = min(n, pref)
    while n % t:
        t -= SUBLANES
    return t


def _sigmoid(x):
    return 1.0 / (1.0 + jnp.exp2(x * -LOG2E))


def _silu(x):
    return x * _sigmoid(x)


def _softplus(x):
    return jnp.maximum(x, 0.0) + jnp.log1p(jnp.exp(-jnp.abs(x)))


def _gelu(x):
    k1 = -2.0 * 0.7978845608028654 * LOG2E
    return x / (1.0 + jnp.exp2(x * (k1 + (k1 * 0.044715) * (x * x))))


def _rms(x, g):
    return x * lax.rsqrt(jnp.mean(x * x, axis=-1, keepdims=True) + EPS) * g


def _dot(a, b):
    return jnp.dot(a, b, preferred_element_type=F32)


def _dot_nt(a, b):
    return lax.dot_general(a, b, NT_DIMS, preferred_element_type=F32)


def _inproj_body(x_ref, g_ref, w_ref, wdt_ref, o_ref, odt_ref, odtT_ref, xn_ref):
    @pl.when(pl.program_id(1) == 0)
    def _():
        xn = _rms(x_ref[...], g_ref[...])
        xn_ref[...] = xn.astype(BF16)
        dt = jnp.dot(xn, wdt_ref[...], precision=HIGHEST, preferred_element_type=F32)
        odt_ref[...] = dt
        odtT_ref[...] = jnp.transpose(dt)[:odtT_ref.shape[0], :]

    o_ref[...] = _dot(xn_ref[...], w_ref[...])


def _inproj(x2d, g, w_main, wdt, H):
    T, D = x2d.shape
    N = w_main.shape[1]
    TM = _pick(T, 1024)
    TN = 3072
    assert N % TN == 0 and TM % LANES == 0
    return pl.pallas_call(
        _inproj_body,
        grid=(T // TM, N // TN),
        in_specs=[
            pl.BlockSpec((TM, D), lambda i, j: (i, 0)),
            pl.BlockSpec((1, D), lambda i, j: (0, 0)),
            pl.BlockSpec((D, TN), lambda i, j: (0, j)),
            pl.BlockSpec((D, LANES), lambda i, j: (0, 0)),
        ],
        out_specs=[
            pl.BlockSpec((TM, TN), lambda i, j: (i, j)),
            pl.BlockSpec((TM, LANES), lambda i, j: (i, 0)),
            pl.BlockSpec((H, TM), lambda i, j: (0, i)),
        ],
        out_shape=[
            jax.ShapeDtypeStruct((T, N), F32),
            jax.ShapeDtypeStruct((T, LANES), F32),
            jax.ShapeDtypeStruct((H, T), F32),
        ],
        scratch_shapes=[pltpu.VMEM((TM, D), BF16)],
        compiler_params=_cparams(("parallel", "arbitrary")),
        name="inproj",
    )(x2d, g, w_main, wdt)


def _conv_slab(x, pad, cw_ref, cb_ref, ls):
    L = x.shape[0]
    pad[SUBLANES:SUBLANES + L, :] = x
    out = (cb_ref[:, ls]
           + cw_ref[3:4, ls] * pad[8:8 + L, :]
           + cw_ref[2:3, ls] * pad[7:7 + L, :]
           + cw_ref[1:2, ls] * pad[6:6 + L, :]
           + cw_ref[0:1, ls] * pad[5:5 + L, :])
    pad[0:SUBLANES, :] = pad[L:L + SUBLANES, :]
    return out


def _rg_gates(xc, wa_ref, wx_ref, ba, bx, lam):
    xb = xc.astype(BF16)
    nb, bw = wa_ref.shape[0], wa_ref.shape[1]
    rs, gs = [], []
    for k in range(nb):
        xk = xb[:, k * bw:(k + 1) * bw]
        rs.append(_dot(xk, wa_ref[k]))
        gs.append(_dot(xk, wx_ref[k]))
    r = _sigmoid(jnp.concatenate(rs, axis=1) + ba)
    i = _sigmoid(jnp.concatenate(gs, axis=1) + bx)
    log_a = -RG_C * r * _softplus(-lam)
    a = jnp.exp(log_a)
    em1 = jnp.tanh(log_a) * (a * a + 1.0)
    return a, jnp.sqrt(-em1) * (i * xc)


def _rg_prompt_body(x_ref, gate_ref, cw_ref, cb_ref, wa_ref, wx_ref, ba_ref, bx_ref, lam_ref, wo_ref,
                    ya_ref, hfin_ref, pad_s, a_s, b_s, h_s):
    B, Lc, W = x_ref.shape
    NS = W // LANES
    pitch = a_s.shape[1] // B

    @pl.when(pl.program_id(0) == 0)
    def _():
        pad_s[:, :, 0:SUBLANES, :] = jnp.zeros((B, NS, SUBLANES, LANES), F32)
        h_s[...] = jnp.zeros(h_s.shape, F32)

    for b in range(B):
        xc = jnp.concatenate(
            [_conv_slab(x_ref[b, :, s * LANES:(s + 1) * LANES], pad_s.at[b, s], cw_ref, cb_ref,
                        slice(s * LANES, (s + 1) * LANES)) for s in range(NS)], axis=1)
        a, bt = _rg_gates(xc, wa_ref, wx_ref, ba_ref[...], bx_ref[...], lam_ref[...])
        for s in range(NS):
            a_s[s, b * pitch:b * pitch + Lc, :] = a[:, s * LANES:(s + 1) * LANES]
            b_s[s, b * pitch:b * pitch + Lc, :] = bt[:, s * LANES:(s + 1) * LANES]

    def step(t, hs):
        out = []
        for s in range(NS):
            rows = pl.ds(t, B, stride=pitch)
            h = a_s[s, rows, :] * hs[s] + b_s[s, rows, :]
            b_s[s, rows, :] = h
            out.append(h)
        return tuple(out)

    hs = lax.fori_loop(0, Lc, step, tuple(h_s[:, s * LANES:(s + 1) * LANES] for s in range(NS)), unroll=4)
    for s in range(NS):
        h_s[:, s * LANES:(s + 1) * LANES] = hs[s]
    hfin_ref[...] = h_s[...]
    for b in range(B):
        hb = jnp.concatenate([b_s[s, b * pitch:b * pitch + Lc, :] for s in range(NS)], axis=1)
        ya_ref[b] = _dot((hb * _gelu(gate_ref[b])).astype(BF16), wo_ref[...])


def _rg_prompt(proj, B, L, col_x, col_g, cw, cb, wa, wx, ba, bx, lam, wo):
    W = cw.shape[1]
    Lc = _pick(L, 128)
    pitch = Lc + SUBLANES // 2
    assert B == SUBLANES and W % LANES == 0
    proj3 = proj.reshape(B, L, proj.shape[1])
    full = lambda *s: pl.BlockSpec(s, lambda c: (0,) * len(s))
    ya, hfin = pl.pallas_call(
        _rg_prompt_body,
        grid=(L // Lc,),
        in_specs=[
            pl.BlockSpec((B, Lc, W), lambda c: (0, c, col_x)),
            pl.BlockSpec((B, Lc, W), lambda c: (0, c, col_g)),
            full(4, W), full(1, W), full(*wa.shape), full(*wx.shape), full(1, W), full(1, W), full(1, W),
            full(*wo.shape),
        ],
        out_specs=[
            pl.BlockSpec((B, Lc, wo.shape[1]), lambda c: (0, c, 0)),
            pl.BlockSpec((B, W), lambda c: (0, 0)),
        ],
        out_shape=[
            jax.ShapeDtypeStruct((B, L, wo.shape[1]), F32),
            jax.ShapeDtypeStruct((B, W), F32),
        ],
        scratch_shapes=[
            pltpu.VMEM((B, W // LANES, Lc + SUBLANES, LANES), F32),
            pltpu.VMEM((W // LANES, B * pitch, LANES), F32),
            pltpu.VMEM((W // LANES, B * pitch, LANES), F32),
            pltpu.VMEM((B, W), F32),
        ],
        compiler_params=_cparams(("arbitrary",)),
        name="rg_prompt",
    )(proj3, proj3, cw, cb, wa, wx, ba, bx, lam, wo)
    return ya.reshape(B * L, wo.shape[1]), hfin


def _conv_rolled(u, cw_ref, cb_ref):
    return (cb_ref[...]
            + cw_ref[3:4, :] * u
            + cw_ref[2:3, :] * pltpu.roll(u, 1, axis=0)
            + cw_ref[1:2, :] * pltpu.roll(u, 2, axis=0)
            + cw_ref[0:1, :] * pltpu.roll(u, 3, axis=0))


def _rg_sample_body(x_ref, gate_ref, cpad_ref, hpad_ref, cw_ref, cb_ref, wa_ref, wx_ref, ba_ref, bx_ref,
                    lam_ref, wo_ref, ya_ref, h_ref):
    xc = _conv_rolled(x_ref[...] + cpad_ref[...], cw_ref, cb_ref)
    a, bt = _rg_gates(xc, wa_ref, wx_ref, ba_ref[...], bx_ref[...], lam_ref[...])
    row = lax.broadcasted_iota(jnp.int32, a.shape, 0) & (SAMPLE_ROWS - 1)
    h = hpad_ref[...]
    for k in range(SAMPLE_ROWS // 2, SAMPLE_ROWS):
        h = jnp.where(row == k, a * pltpu.roll(h, 1, axis=0) + bt, h)
    h_ref[...] = h
    ya_ref[...] = _dot((h * _gelu(gate_ref[...])).astype(BF16), wo_ref[...])


def _rg_sample(proj, col_x, col_g, cpad, hpad, cw, cb, wa, wx, ba, bx, lam, wo):
    T = proj.shape[0]
    W = cw.shape[1]
    TM = _pick(T, 256)
    full = lambda *s: pl.BlockSpec(s, lambda i: (0,) * len(s))
    return pl.pallas_call(
        _rg_sample_body,
        grid=(T // TM,),
        in_specs=[
            pl.BlockSpec((TM, W), lambda i: (i, col_x)),
            pl.BlockSpec((TM, W), lambda i: (i, col_g)),
            pl.BlockSpec((TM, W), lambda i: (i, 0)),
            pl.BlockSpec((TM, W), lambda i: (i, 0)),
            full(4, W), full(1, W), full(*wa.shape), full(*wx.shape), full(1, W), full(1, W), full(1, W),
            full(*wo.shape),
        ],
        out_specs=[
            pl.BlockSpec((TM, wo.shape[1]), lambda i: (i, 0)),
            pl.BlockSpec((TM, W), lambda i: (i, 0)),
        ],
        out_shape=[
            jax.ShapeDtypeStruct((T, wo.shape[1]), F32),
            jax.ShapeDtypeStruct((T, W), F32),
        ],
        compiler_params=_cparams(("parallel",)),
        name="rg_sample",
    )(proj, proj, cpad, hpad, cw, cb, wa, wx, ba, bx, lam, wo)


def _ssd_chunk(act_s, dt, dtT, alog_ref, alogT_ref, dexp_ref, seq_shift, G, R, P, N,
               yint_fn, y_s, xw_s):
    Q = act_s.shape[0]
    inner = G * R * P
    a = dt * (-jnp.exp(alog_ref[...]))
    aT = dtT * (-jnp.exp(alogT_ref[...]))
    ri = lax.broadcasted_iota(jnp.int32, (Q, Q), 0)
    ci = lax.broadcasted_iota(jnp.int32, (Q, Q), 1)
    same = (ri >> seq_shift) == (ci >> seq_shift)
    causal = same & (ci <= ri)
    causal_f = jnp.where(causal, 1.0, 0.0).astype(F32)
    same_f = jnp.where(same, 1.0, 0.0).astype(F32)
    acum = jnp.dot(causal_f, a, precision=HIGHEST, preferred_element_type=F32)
    tot = jnp.dot(same_f, a, precision=HIGHEST, preferred_element_type=F32)
    acumT = lax.dot_general(aT, causal_f, NT_DIMS, precision=HIGHEST,
                            preferred_element_type=F32)
    eacum = jnp.exp(acum)
    toend = jnp.exp(tot - acum) * dt
    for g in range(G):
        Bg = act_s[:, inner + g * N:inner + (g + 1) * N].astype(BF16)
        Cg = act_s[:, inner + (G + g) * N:inner + (G + g + 1) * N].astype(BF16)
        cb = _dot_nt(Cg, Bg)
        yint = yint_fn(g, Cg)
        for r in range(R):
            h = g * R + r
            sl = slice(h * P, (h + 1) * P)
            seg = acum[:, h:h + 1] - acumT[h:h + 1, :]
            decay = jnp.exp(jnp.where(causal, seg, NEG_INF))
            wgt = (cb * decay * dtT[h:h + 1, :]).astype(BF16)
            xh = act_s[:, sl]
            yh = _dot(wgt, xh.astype(BF16)) + dexp_ref[:, sl] * xh
            if yint is not None:
                yh = yh + yint[:, r * P:(r + 1) * P] * eacum[:, h:h + 1]
            y_s[:, sl] = yh
            xw_s[:, sl] = xh * toend[:, h:h + 1]
    return tot, eacum


def _expand_heads(x, ex_ref):
    hi = x.astype(BF16)
    r1 = x - hi.astype(F32)
    mid = r1.astype(BF16)
    lo = (r1 - mid.astype(F32)).astype(BF16)
    e = ex_ref[...]
    return _dot(hi, e) + _dot(mid, e) + _dot(lo, e)


def _ssd_finish(y, z, nw_ref, wo_ref):
    y = y * _silu(z)
    return _dot(_rms(y, nw_ref[...]).astype(BF16), wo_ref[...])


def _ssd_prompt_body(G, R, P, N,
                     xbc_ref, z_ref, dt_ref, dtT_ref, cw_ref, cb_ref, dtb_ref, dtbT_ref, alog_ref, alogT_ref,
                     dexp_ref, nw_ref, wo_ref, ex_ref, yb_ref, st_ref, pad_s, act_s, y_s, stT_s):
    Q = xbc_ref.shape[0]
    inner = G * R * P
    GW = R * P

    @pl.when(pl.program_id(1) == 0)
    def _():
        pad_s[:, 0:SUBLANES, :] = jnp.zeros((pad_s.shape[0], SUBLANES, LANES), F32)
        stT_s[...] = jnp.zeros(stT_s.shape, F32)

    for s in range(pad_s.shape[0]):
        ls = slice(s * LANES, (s + 1) * LANES)
        act_s[:, ls] = _silu(_conv_slab(xbc_ref[:, ls], pad_s.at[s], cw_ref, cb_ref, ls))
    dt = _softplus(dt_ref[...] + dtb_ref[...])
    dtT = _softplus(dtT_ref[...] + dtbT_ref[...])
    a = dt * (-jnp.exp(alog_ref[...]))
    aT = dtT * (-jnp.exp(alogT_ref[...]))
    ri = lax.broadcasted_iota(jnp.int32, (Q, Q), 0)
    ci = lax.broadcasted_iota(jnp.int32, (Q, Q), 1)
    causal = ci <= ri
    causal_f = jnp.where(causal, 1.0, 0.0).astype(F32)
    hi = dict(precision=HIGHEST, preferred_element_type=F32)
    acum = jnp.dot(causal_f, a, **hi)
    tot = jnp.dot(jnp.ones((SUBLANES, Q), F32), a, **hi)
    acumT = lax.dot_general(aT, causal_f, NT_DIMS, **hi)
    fx = _expand_heads(jnp.concatenate([jnp.exp(acum), jnp.exp(tot[0:1, :] - acum) * dt, jnp.exp(tot)], axis=0),
                       ex_ref)
    eacum_x, toend_x, dec_x = fx[0:Q], fx[Q:2 * Q], fx[2 * Q:2 * Q + 1]
    col2 = acum * LOG2E
    row2 = (acumT - jnp.log(dtT)) * LOG2E
    lane = lax.broadcasted_iota(jnp.int32, (Q, LANES), 1)
    for g in range(G):
        gs = slice(g * GW, (g + 1) * GW)
        Bf = act_s[:, inner + g * N:inner + (g + 1) * N]
        Bg = Bf.astype(BF16)
        Cg = act_s[:, inner + (G + g) * N:inner + (G + g + 1) * N].astype(BF16)
        cb = jnp.where(causal, _dot_nt(Cg, Bg), 0.0)
        y_s[:, gs] = (_dot(Cg, stT_s[:, gs].astype(BF16)) * eacum_x[:, gs]
                      + dexp_ref[:, gs] * act_s[:, gs])
        for pr in range(R // 2):
            h0 = g * R + 2 * pr
            ps = slice(h0 * P, (h0 + 2) * P)
            wg = []
            for h in (h0, h0 + 1):
                e = jnp.exp2(jnp.where(causal, col2[:, h:h + 1] - row2[h:h + 1, :], NEG_INF))
                wg.append((cb * e).astype(BF16))
            xp = act_s[:, ps]
            xa = jnp.where(lane < P, xp, 0.0).astype(BF16)
            xb = jnp.where(lane < P, 0.0, xp).astype(BF16)
            y_s[:, ps] = y_s[:, ps] + _dot(jnp.concatenate(wg, axis=1), jnp.concatenate([xa, xb], axis=0))
        xw = (act_s[:, gs] * toend_x[:, gs]).astype(BF16)
        ST = _dot(jnp.transpose(Bf).astype(BF16), xw)
        stT_s[:, gs] = dec_x[:, gs] * stT_s[:, gs] + ST
    yb_ref[...] = _ssd_finish(y_s[...], z_ref[...], nw_ref, wo_ref)

    @pl.when(pl.program_id(1) == pl.num_programs(1) - 1)
    def _():
        for g in range(G):
            st_ref[0, g * R:(g + 1) * R] = jnp.transpose(stT_s[:, g * GW:(g + 1) * GW]).reshape(R, P, N)


def _ssd_prompt(proj, dt, dtT, B, L, col_xbc, col_z, dims, cw, cb, dtb, dtbT, alog, alogT, dexp, nw, wo):
    G, R, P, N = dims
    H = G * R
    inner = H * P
    CD = cw.shape[1]
    Q = _pick(L, SSD_CHUNK)
    nC = L // Q
    assert 2 * P == LANES and R % 2 == 0 and N == LANES
    head_of_lane = jnp.arange(inner, dtype=jnp.int32) // P
    ex = (jnp.arange(LANES, dtype=jnp.int32)[:, None] == head_of_lane[None, :]).astype(BF16)
    full = lambda *s: pl.BlockSpec(s, lambda b, c: (0,) * len(s))
    return pl.pallas_call(
        functools.partial(_ssd_prompt_body, G, R, P, N),
        grid=(B, nC),
        in_specs=[
            pl.BlockSpec((Q, CD), lambda b, c: (b * nC + c, col_xbc)),
            pl.BlockSpec((Q, inner), lambda b, c: (b * nC + c, col_z)),
            pl.BlockSpec((Q, LANES), lambda b, c: (b * nC + c, 0)),
            pl.BlockSpec((H, Q), lambda b, c: (0, b * nC + c)),
            full(4, CD), full(1, CD), full(1, LANES), full(H, 1), full(1, LANES), full(H, 1),
            full(1, inner), full(1, inner), full(*wo.shape), full(LANES, inner),
        ],
        out_specs=[
            pl.BlockSpec((Q, wo.shape[1]), lambda b, c: (b * nC + c, 0)),
            pl.BlockSpec((1, H, P, N), lambda b, c: (b, 0, 0, 0)),
        ],
        out_shape=[
            jax.ShapeDtypeStruct((B * L, wo.shape[1]), F32),
            jax.ShapeDtypeStruct((B, H, P, N), F32),
        ],
        scratch_shapes=[
            pltpu.VMEM((CD // LANES, Q + SUBLANES, LANES), F32),
            pltpu.VMEM((Q, CD), F32),
            pltpu.VMEM((Q, inner), F32),
            pltpu.VMEM((N, inner), F32),
        ],
        compiler_params=_cparams(("parallel", "arbitrary")),
        name="ssd_prompt",
    )(proj, proj, dt, dtT, cw, cb, dtb, dtbT, alog, alogT, dexp, nw, wo, ex)


def _ssd_sample_body(G, R, P, N,
                     xbc_ref, cpad_ref, z_ref, dt_ref, dtT_ref, h0_ref, cw_ref, cb_ref, dtb_ref, dtbT_ref,
                     alog_ref, alogT_ref, dexp_ref, nw_ref, wo_ref, yb_ref, st_ref,
                     act_s, y_s, xw_s, yint_s, xwT_s, tot_s, eacum_s):
    Q = xbc_ref.shape[0]
    inner = G * R * P
    bi = pl.program_id(1)
    half = SAMPLE_ROWS // 2

    @pl.when(bi == 0)
    def _():
        act_s[...] = _silu(_conv_rolled(xbc_ref[...] + cpad_ref[...], cw_ref, cb_ref))
        rows = lax.broadcasted_iota(jnp.int32, (Q, LANES), 0) & (SAMPLE_ROWS - 1)
        cols = lax.broadcasted_iota(jnp.int32, dtT_ref.shape, 1) & (SAMPLE_ROWS - 1)
        dt = jnp.where(rows >= half, _softplus(dt_ref[...] + dtb_ref[...]), 0.0)
        dtT = jnp.where(cols >= half, _softplus(dtT_ref[...] + dtbT_ref[...]), 0.0)
        seq_shift = SAMPLE_ROWS.bit_length() - 1
        tot, eacum = _ssd_chunk(act_s, dt, dtT, alog_ref, alogT_ref, dexp_ref, seq_shift, G, R, P, N,
                                lambda g, Cg: None, y_s, xw_s)
        tot_s[...] = tot
        eacum_s[...] = eacum
        for g in range(G):
            xwT_s[g] = jnp.transpose(xw_s[:, g * R * P:(g + 1) * R * P]).astype(BF16)

    r0 = pl.multiple_of(bi * SAMPLE_ROWS, SAMPLE_ROWS)
    rid = lax.broadcasted_iota(jnp.int32, (Q, N), 0)
    mine = (rid >= r0) & (rid < r0 + SAMPLE_ROWS)
    dec = jnp.exp(tot_s[pl.ds(r0, 1), :])
    for g in range(G):
        Cb = act_s[pl.ds(r0, SAMPLE_ROWS), inner + (G + g) * N:inner + (G + g + 1) * N]
        stg = h0_ref[0, g * R:(g + 1) * R].reshape(R * P, N)
        yint_s[pl.ds(r0, SAMPLE_ROWS), g * R * P:(g + 1) * R * P] = _dot_nt(Cb, stg)
        Bg = jnp.where(mine, act_s[:, inner + g * N:inner + (g + 1) * N], 0.0).astype(BF16)
        S = _dot(xwT_s[g], Bg)
        for r in range(R):
            h = g * R + r
            st_ref[0, h] = dec[:, h:h + 1] * h0_ref[0, h] + S[r * P:(r + 1) * P, :]

    @pl.when(bi == pl.num_programs(1) - 1)
    def _():
        for h in range(G * R):
            sl = slice(h * P, (h + 1) * P)
            y_s[:, sl] = y_s[:, sl] + yint_s[:, sl] * eacum_s[:, h:h + 1]
        yb_ref[...] = _ssd_finish(y_s[...], z_ref[...], nw_ref, wo_ref)


def _ssd_sample(proj, dt, dtT, cpad, h0, col_xbc, col_z, dims, cw, cb, dtb, dtbT, alog, alogT, dexp, nw, wo):
    G, R, P, N = dims
    H = G * R
    inner = H * P
    CD = cw.shape[1]
    T = proj.shape[0]
    Bs = h0.shape[0]
    Q = _pick(T, SSD_CHUNK)
    nb = Q // SAMPLE_ROWS
    full = lambda *s: pl.BlockSpec(s, lambda i, j: (0,) * len(s))
    return pl.pallas_call(
        functools.partial(_ssd_sample_body, G, R, P, N),
        grid=(T // Q, nb),
        in_specs=[
            pl.BlockSpec((Q, CD), lambda i, j: (i, col_xbc)),
            pl.BlockSpec((Q, CD), lambda i, j: (i, 0)),
            pl.BlockSpec((Q, inner), lambda i, j: (i, col_z)),
            pl.BlockSpec((Q, LANES), lambda i, j: (i, 0)),
            pl.BlockSpec((H, Q), lambda i, j: (0, i)),
            pl.BlockSpec((1, H, P, N), lambda i, j: (i * nb + j, 0, 0, 0)),
            full(4, CD), full(1, CD), full(1, LANES), full(H, 1), full(1, LANES), full(H, 1),
            full(1, inner), full(1, inner), full(*wo.shape),
        ],
        out_specs=[
            pl.BlockSpec((Q, wo.shape[1]), lambda i, j: (i, 0)),
            pl.BlockSpec((1, H, P, N), lambda i, j: (i * nb + j, 0, 0, 0)),
        ],
        out_shape=[
            jax.ShapeDtypeStruct((T, wo.shape[1]), F32),
            jax.ShapeDtypeStruct((Bs, H, P, N), F32),
        ],
        scratch_shapes=[
            pltpu.VMEM((Q, CD), F32),
            pltpu.VMEM((Q, inner), F32),
            pltpu.VMEM((Q, inner), F32),
            pltpu.VMEM((Q, inner), F32),
            pltpu.VMEM((G, R * P, Q), BF16),
            pltpu.VMEM((Q, LANES), F32),
            pltpu.VMEM((Q, LANES), F32),
        ],
        compiler_params=_cparams(("parallel", "arbitrary")),
        name="ssd_sample",
    )(proj, cpad, proj, dt, dtT, h0, cw, cb, dtb, dtbT, alog, alogT, dexp, nw, wo)


def _merge_body(x_ref, ya_ref, yb_ref, ga_ref, gb_ref, wout_ref, nf_ref, x1_ref, h2t_ref):
    m = _sigmoid(ga_ref[...]) * ya_ref[...] + _sigmoid(gb_ref[...]) * yb_ref[...]
    x1 = x_ref[...] + _dot(m.astype(BF16), wout_ref[...])
    x1_ref[...] = x1
    h2t_ref[...] = jnp.transpose(_rms(x1, nf_ref[...])).astype(BF16)


def _merge(x2d, ya, yb, proj, col_ga, col_gb, wout, nf):
    T, D = x2d.shape
    TM = _pick(T, 512)
    row = lambda c: pl.BlockSpec((TM, D), lambda i: (i, c))
    return pl.pallas_call(
        _merge_body,
        grid=(T // TM,),
        in_specs=[row(0), row(0), row(0), row(col_ga), row(col_gb),
                  pl.BlockSpec(wout.shape, lambda i: (0, 0)), pl.BlockSpec((1, D), lambda i: (0, 0))],
        out_specs=[row(0), pl.BlockSpec((D, TM), lambda i: (0, i))],
        out_shape=[jax.ShapeDtypeStruct((T, D), F32), jax.ShapeDtypeStruct((D, T), BF16)],
        compiler_params=_cparams(("parallel",)),
        name="merge",
    )(x2d, ya, yb, proj, proj, wout, nf)


def _staircase(k):
    return [(ka, k // (ka + 1)) for ka in range(k)]


def _take_top(cur, k_top, want_rank):
    rank = jnp.full(cur.shape, float(k_top), F32) if want_rank else None
    vals = []
    for k in range(k_top):
        m = jnp.max(cur, axis=0, keepdims=True)
        hit = cur == m
        vals.append(m)
        if want_rank:
            rank = jnp.where(hit, float(k), rank)
        cur = jnp.where(hit, NEG_INF, cur)
    taken = jnp.sum(jnp.where(cur == NEG_INF, 1.0, 0.0), axis=0, keepdims=True)
    return vals, rank, taken


def _take_top_ties(x, k_top, val_ref, rank_ref):
    rows = lax.broadcasted_iota(jnp.int32, x.shape, 0)
    kk = lax.broadcasted_iota(jnp.int32, (k_top, x.shape[1]), 0)

    def body(k, carry):
        cur, rank, vals = carry
        m = jnp.max(cur, axis=0, keepdims=True)
        first = jnp.min(jnp.where(cur == m, rows, x.shape[0]), axis=0, keepdims=True)
        hit = rows == first
        return (jnp.where(hit, NEG_INF, cur), jnp.where(hit, k.astype(F32), rank),
                jnp.where(kk == k, m, vals))

    init = (x, jnp.full(x.shape, float(k_top), F32), jnp.zeros((k_top, x.shape[1]), F32))
    _, rank, vals = lax.fori_loop(0, k_top, body, init)
    val_ref[...] = vals
    rank_ref[...] = rank


def _route_body(h2t_ref, wqt_ref, keys_ref, na_ref, ea_ref, rb_ref, eb_ref, sv_s, rk_s, cand_s, cv_s, cr_s):
    NH = keys_ref.shape[0]
    NK, KH = keys_ref.shape[2], keys_ref.shape[3]
    K = PEER_TOPK
    TB = h2t_ref.shape[1]
    tiles = [slice(tl * LANES, (tl + 1) * LANES) for tl in range(TB // LANES)]
    qt = _dot(wqt_ref[...], h2t_ref[...]).astype(BF16)
    cand_s[...] = jnp.full(cand_s.shape, NEG_INF, F32)

    def candidates():
        off = 0
        for ka, nb in _staircase(K):
            cand_s[off:off + nb, :] = sv_s[0, ka:ka + 1, :] + sv_s[1, 0:nb, :]
            off += nb

    def emit(h, sT, in_top, rb, sel, na_of):
        top = sv_s[0, 0:1, :] + sv_s[1, 0:1, :]
        z = jnp.sum(jnp.where(sel, jnp.exp(cand_s[...] - top), 0.0), axis=0, keepdims=True)
        ex = [jnp.where(in_top[s], jnp.exp(sT[s] - sv_s[s, 0:1, :]), 0.0) for s in range(2)]
        na = jnp.zeros(sT[0].shape, F32)
        off = 0
        for ka, nb in _staircase(K):
            cnt = jnp.sum(jnp.where(sel[off:off + nb, :], 1.0, 0.0), axis=0, keepdims=True)
            na = jnp.where(na_of(ka), cnt, na)
            off += nb
        na_ref[h * NK:(h + 1) * NK, :] = na
        ea_ref[h * NK:(h + 1) * NK, :] = ex[0] / z
        rb_ref[h] = rb.astype(BF16)
        eb_ref[h] = ex[1].astype(BF16)

    def scores(h):
        return [_dot(keys_ref[h, s], qt[(h * 2 + s) * KH:(h * 2 + s + 1) * KH, :]) for s in range(2)]

    most = []
    for h in range(NH):
        sT = scores(h)
        seen = jnp.zeros((1, LANES), F32)
        rbs = []
        for s in range(2):
            for ln in tiles:
                vals, rank, taken = _take_top(sT[s][:, ln], K, s == 1)
                for k in range(K):
                    sv_s[s, k:k + 1, ln] = vals[k]
                seen = jnp.maximum(seen, taken)
                if s == 1:
                    rbs.append(rank)
        candidates()
        cand = cand_s[...]
        tau = _take_top(cand, K, False)[0][K - 1]
        sel = cand >= tau
        most.append(jnp.maximum(seen, jnp.max(jnp.sum(jnp.where(sel, 1.0, 0.0), axis=0, keepdims=True),
                                              axis=1, keepdims=True)))
        emit(h, sT, [sT[s] >= sv_s[s, K - 1:K, :] for s in range(2)], jnp.concatenate(rbs, axis=1), sel,
             lambda ka: sT[0] == sv_s[0, ka:ka + 1, :])

    for h in range(NH):
        @pl.when(jnp.max(most[h]) > K)
        def _(h=h):
            sT = scores(h)
            for s in range(2):
                for ln in tiles:
                    _take_top_ties(sT[s][:, ln], K, sv_s.at[s, :, ln], rk_s.at[s, :, ln])
            candidates()
            for ln in tiles:
                _take_top_ties(cand_s[:, ln], K, cv_s.at[:, ln], cr_s.at[:, ln])
            emit(h, sT, [rk_s[s] < K for s in range(2)], rk_s[1], cr_s[...] < K,
                 lambda ka: rk_s[0] == float(ka))


def _route(h2t, wqt, keys):
    D, T = h2t.shape
    NH, _, NK, KH = keys.shape
    TB = _pick(T, 256)
    assert TB % LANES == 0
    ncand =sum(nb for _, nb in _staircase(PEER_TOPK))
    ncand_pad = -(-ncand // SUBLANES) * SUBLANES
    flat = pl.BlockSpec((NH * NK, TB), lambda i: (0, i))
    tok = pl.BlockSpec((NH, NK, TB), lambda i: (0, 0, i))
    return pl.pallas_call(
        _route_body,
        grid=(T // TB,),
        in_specs=[pl.BlockSpec((D, TB), lambda i: (0, i)),
                  pl.BlockSpec(wqt.shape, lambda i: (0, 0)),
                  pl.BlockSpec(keys.shape, lambda i: (0, 0, 0, 0))],
        out_specs=[flat, flat, tok, tok],
        out_shape=[jax.ShapeDtypeStruct((NH * NK, T), F32), jax.ShapeDtypeStruct((NH * NK, T), F32),
                   jax.ShapeDtypeStruct((NH, NK, T), BF16), jax.ShapeDtypeStruct((NH, NK, T), BF16)],
        scratch_shapes=[pltpu.VMEM((2, PEER_TOPK, TB), F32), pltpu.VMEM((2, NK, TB), F32),
                        pltpu.VMEM((ncand_pad, TB), F32), pltpu.VMEM((PEER_TOPK, TB), F32),
                        pltpu.VMEM((ncand_pad, TB), F32)],
        compiler_params=_cparams(("parallel",)),
        name="peer_route",
    )(h2t, wqt, keys)


def _peer_body(h2t_ref, x1_ref, u_ref, vt_ref, na_ref, ea_ref, rb_ref, eb_ref, nfin_ref, y_ref,
               s_s, a_s, acc_s):
    ec = pl.program_id(1)
    NH, NK, TB = rb_ref.shape
    ni = u_ref.shape[0] // NK

    @pl.when(ec == 0)
    def _():
        acc_s[...] = jnp.zeros(acc_s.shape, F32)

    s_s[...] = _dot(u_ref[...], h2t_ref[...])

    def row_tile(grp, il):
        return jnp.broadcast_to(grp[il:il + 1, :], (BF16_ROWS, LANES)).astype(BF16)

    for tl in range(TB // LANES):
        ln = slice(tl * LANES, (tl + 1) * LANES)
        rows = [pl.ds(pl.multiple_of(h * NK + ec * ni, SUBLANES), ni) for h in range(NH)]
        na = [na_ref[rows[h], ln] for h in range(NH)]
        ea = [ea_ref[rows[h], ln] for h in range(NH)]
        for il in range(ni):
            na_t = [row_tile(na[h], il) for h in range(NH)]
            ea_t = [row_tile(ea[h], il) for h in range(NH)]
            for jt in range(NK // BF16_ROWS):
                js = slice(jt * BF16_ROWS, (jt + 1) * BF16_ROWS)
                w = jnp.zeros((BF16_ROWS, LANES), BF16)
                for h in range(NH):
                    w = w + ea_t[h] * jnp.where(rb_ref[h, js, ln] < na_t[h], eb_ref[h, js, ln], 0.0)
                e = slice(il * NK + jt * BF16_ROWS, il * NK + (jt + 1) * BF16_ROWS)
                a_s[e, ln] = _gelu(s_s[e, ln]).astype(BF16) * w
    acc_s[...] += _dot(vt_ref[...], a_s[...])

    @pl.when(ec == pl.num_programs(1) - 1)
    def _():
        x2 = x1_ref[...] + jnp.transpose(acc_s[...])
        y_ref[...] = _rms(x2, nfin_ref[...])


def _peer(h2t, x1, u, vt, na, ea, rb, eb, nfin):
    D, T = h2t.shape
    E = u.shape[0]
    NH, NK, _ = rb.shape
    TB = _pick(T, 512)
    EC = 2 * SUBLANES * NK
    assert TB % LANES == 0 and E % EC == 0 and NK % BF16_ROWS == 0
    tok = pl.BlockSpec((NH, NK, TB), lambda i, j: (0, 0, i))
    flat = pl.BlockSpec((NH * NK, TB), lambda i, j: (0, i))
    return pl.pallas_call(
        _peer_body,
        grid=(T // TB, E // EC),
        in_specs=[pl.BlockSpec((D, TB), lambda i, j: (0, i)),
                  pl.BlockSpec((TB, D), lambda i, j: (i, 0)),
                  pl.BlockSpec((EC, D), lambda i, j: (j, 0)),
                  pl.BlockSpec((D, EC), lambda i, j: (0, j)),
                  flat, flat, tok, tok,
                  pl.BlockSpec((1, D), lambda i, j: (0, 0))],
        out_specs=pl.BlockSpec((TB, D), lambda i, j: (i, 0)),
        out_shape=jax.ShapeDtypeStruct((T, D), F32),
        scratch_shapes=[pltpu.VMEM((EC, TB), F32), pltpu.VMEM((EC, TB), BF16), pltpu.VMEM((D, TB), F32)],
        compiler_params=_cparams(("parallel", "arbitrary")),
        name="peer_mix",
    )(h2t, x1, u, vt, na, ea, rb, eb, nfin)


def _pad_lanes(row):
    return jnp.pad(row, ((0, 0), (0, LANES - row.shape[1])))


def kernel(x_prompt, x_sample, state_rg_h, state_rg_conv, state_ssd_h, state_ssd_conv, norm_mix, w_in, rg_conv_w, rg_conv_b, rg_wa, rg_ba, rg_wx, rg_bx, rg_lam, w_rg_out, ssd_conv_w, ssd_conv_b, ssd_dt_bias, ssd_a_log, ssd_d, ssd_norm, w_ssd_out, w_out, norm_ffn, peer_wq, peer_keys, peer_u, peer_v, norm_final):
    depth = w_in.shape[0]
    B, L, D = x_prompt.shape
    Bs, Ls, _ = x_sample.shape
    assert depth == 1 and Ls == SAMPLE_ROWS // 2 and rg_conv_w.shape[1] == 4 and L >= 3
    W = rg_conv_w.shape[2]
    H = ssd_a_log.shape[1]
    inner = ssd_norm.shape[1]
    CD = ssd_conv_w.shape[2]
    P = inner // H
    N = state_ssd_h.shape[-1]
    G = (CD - inner) // (2 * N)
    R = H // G
    dims = (G, R, P, N)
    assert W == D and inner == 2 * D and CD == 3 * D and H % SUBLANES == 0 and H <= LANES
    o_x, o_g, o_z, o_xbc, o_dt, o_m = 0, W, 2 * W, 2 * W + inner, 2 * W + inner + CD, 2 * W + inner + CD + H
    col_xbc, col_x, col_z, col_g, col_ga, col_gb = 0, CD // W, (CD + W) // inner, (CD + W + inner) // W, \
        (CD + 2 * W + inner) // D, (CD + 2 * W + inner + D) // D

    yp = x_prompt.reshape(B * L, D)
    ys = x_sample.reshape(Bs * Ls, D)
    outs = {k: [] for k in ("prh", "prc", "psh", "psc", "srh", "src", "ssh", "ssc")}
    lead = SAMPLE_ROWS - Ls
    for l in range(depth):
        wi = w_in[l]
        w_main = jnp.concatenate([wi[:, o_xbc:o_dt], wi[:, o_x:o_g], wi[:, o_z:o_xbc], wi[:, o_g:o_z],
                                  wi[:, o_m:]], axis=1).astype(BF16)
        wdt = _pad_lanes(wi[:, o_dt:o_m])
        g_mix = norm_mix[l][None]
        rg_w = (rg_conv_w[l], rg_conv_b[l][None], rg_wa[l].astype(BF16), rg_wx[l].astype(BF16),
                rg_ba[l][None], rg_bx[l][None], rg_lam[l][None], w_rg_out[l].astype(BF16))
        ssd_w = (ssd_conv_w[l], ssd_conv_b[l][None], _pad_lanes(ssd_dt_bias[l][None]), ssd_dt_bias[l][:, None],
                 _pad_lanes(ssd_a_log[l][None]), ssd_a_log[l][:, None], jnp.repeat(ssd_d[l], P)[None],
                 ssd_norm[l][None], w_ssd_out[l].astype(BF16))
        wout = w_out[l].astype(BF16)
        nf = norm_ffn[l][None]
        wqt = peer_wq[l].astype(BF16).T
        keys = peer_keys[l].astype(BF16)
        u = peer_u[l].astype(BF16)
        vt = peer_v[l].astype(BF16).T
        nfin = norm_final[None]

        proj, dt, dtT = _inproj(yp, g_mix, w_main, wdt, H)
        ya, rgh = _rg_prompt(proj, B, L, col_x, col_g, *rg_w)
        yb, ssh = _ssd_prompt(proj, dt, dtT, B, L, col_xbc, col_z, dims, *ssd_w)
        p3 = proj.reshape(B, L, -1)
        outs["prh"].append(rgh.reshape(B, W))
        outs["prc"].append(p3[:, L - 3:, CD:CD + W])
        outs["psh"].append(ssh)
        outs["psc"].append(p3[:, L - 3:, :CD])
        x1, h2t = _merge(yp, ya, yb, proj, col_ga, col_gb, wout, nf)
        yp = _peer(h2t, x1, u, vt, *_route(h2t, wqt, keys), nfin)

        xe = jnp.pad(ys.reshape(Bs, Ls, D), ((0, 0), (lead, 0), (0, 0))).reshape(Bs * SAMPLE_ROWS, D)
        proj_e, dt_e, dtT_e = _inproj(xe, g_mix, w_main, wdt, H)
        rows = lambda a3: a3.reshape(Bs * SAMPLE_ROWS, a3.shape[-1])
        rg_cpad = rows(jnp.pad(state_rg_conv[l], ((0, 0), (lead - 3, Ls), (0, 0))))
        rg_hpad = rows(jnp.pad(state_rg_h[l][:, None, :], ((0, 0), (lead - 1, Ls), (0, 0))))
        ssd_cpad = rows(jnp.pad(state_ssd_conv[l], ((0, 0), (lead - 3, Ls), (0, 0))))
        ya_e, h_e = _rg_sample(proj_e, col_x, col_g, rg_cpad, rg_hpad, *rg_w)
        yb_e, ssh_s = _ssd_sample(proj_e, dt_e, dtT_e, ssd_cpad, state_ssd_h[l], col_xbc, col_z, dims, *ssd_w)
        toks = lambda a2: a2.reshape(Bs, SAMPLE_ROWS, -1)[:, lead:].reshape(Bs * Ls, -1)
        p3 = proj_e.reshape(Bs, SAMPLE_ROWS, -1)
        outs["srh"].append(h_e.reshape(Bs, SAMPLE_ROWS, W)[:, -1])
        outs["src"].append(p3[:, SAMPLE_ROWS - 3:, CD:CD + W])
        outs["ssh"].append(ssh_s)
        outs["ssc"].append(p3[:, SAMPLE_ROWS - 3:, :CD])
        proj_s = toks(proj_e)
        x1, h2t = _merge(ys, toks(ya_e), toks(yb_e), proj_s, col_ga, col_gb, wout, nf)
        ys = _peer(h2t, x1, u, vt, *_route(h2t, wqt, keys), nfin)

    st = lambda k: jnp.stack(outs[k])
    return (yp.reshape(B, L, D), ys.reshape(Bs, Ls, D), st("prh"), st("prc"), st("psh"), st("psc"),
            st("srh"), st("src"), st("ssh"), st("ssc"))
```

```python
import functools

import jax
import jax.numpy as jnp
from jax import lax
from jax.experimental import pallas as pl
from jax.experimental.pallas import tpu as pltpu

F32 = jnp.float32
BF16 = jnp.bfloat16
EPS = 1e-6
RG_C = 8.0
PEER_TOPK = 16
SSD_CHUNK = 128
LANES = 128
SUBLANES = 8
BF16_ROWS = 16
SAMPLE_ROWS = 8
NEG_INF = float("-inf")
LOG2E = 1.4426950408889634
HIGHEST = lax.Precision.HIGHEST
NT_DIMS = (((1,), (1,)), ((), ()))
VMEM_LIMIT = 56 * 1024 * 1024


def _cparams(sem):
    return pltpu.CompilerParams(dimension_semantics=sem, vmem_limit_bytes=VMEM_LIMIT)


def _pick(n, pref):
    t = min(n, pref)
    while n % t:
        t -= SUBLANES
    return t


def _sigmoid(x):
    return 1.0 / (1.0 + jnp.exp2(x * -LOG2E))


def _silu(x):
    return x * _sigmoid(x)


def _softplus(x):
    return jnp.maximum(x, 0.0) + jnp.log1p(jnp.exp(-jnp.abs(x)))


def _gelu(x):
    k1 = -2.0 * 0.7978845608028654 * LOG2E
    return x / (1.0 + jnp.exp2(x * (k1 + (k1 * 0.044715) * (x * x))))


def _rms(x, g):
    return x * lax.rsqrt(jnp.mean(x * x, axis=-1, keepdims=True) + EPS) * g


def _dot(a, b):
    return jnp.dot(a, b, preferred_element_type=F32)


def _dot_nt(a, b):
    return lax.dot_general(a, b, NT_DIMS, preferred_element_type=F32)


def _inproj_body(x_ref, g_ref, w_ref, wdt_ref, o_ref, odt_ref, odtT_ref, xn_ref):
    @pl.when(pl.program_id(1) == 0)
    def _():
        xn = _rms(x_ref[...], g_ref[...])
        xn_ref[...] = xn.astype(BF16)
        dt = jnp.dot(xn, wdt_ref[...], precision=HIGHEST, preferred_element_type=F32)
        odt_ref[...] = dt
        odtT_ref[...] = jnp.transpose(dt)[:odtT_ref.shape[0], :]

    o_ref[...] = _dot(xn_ref[...], w_ref[...])


def _inproj(x2d, g, w_main, wdt, H):
    T, D = x2d.shape
    N = w_main.shape[1]
    TM = _pick(T, 1024)
    TN = 3072
    assert N % TN == 0 and TM % LANES == 0
    return pl.pallas_call(
        _inproj_body,
        grid=(T // TM, N // TN),
        in_specs=[
            pl.BlockSpec((TM, D), lambda i, j: (i, 0)),
            pl.BlockSpec((1, D), lambda i, j: (0, 0)),
            pl.BlockSpec((D, TN), lambda i, j: (0, j)),
            pl.BlockSpec((D, LANES), lambda i, j: (0, 0)),
        ],
        out_specs=[
            pl.BlockSpec((TM, TN), lambda i, j: (i, j)),
            pl.BlockSpec((TM, LANES), lambda i, j: (i, 0)),
            pl.BlockSpec((H, TM), lambda i, j: (0, i)),
        ],
        out_shape=[
            jax.ShapeDtypeStruct((T, N), F32),
            jax.ShapeDtypeStruct((T, LANES), F32),
            jax.ShapeDtypeStruct((H, T), F32),
        ],
        scratch_shapes=[pltpu.VMEM((TM, D), BF16)],
        compiler_params=_cparams(("parallel", "arbitrary")),
        name="inproj",
    )(x2d, g, w_main, wdt)


def _conv_slab(x, pad, cw_ref, cb_ref, ls):
    L = x.shape[0]
    pad[SUBLANES:SUBLANES + L, :] = x
    out = (cb_ref[:, ls]
           + cw_ref[3:4, ls] * pad[8:8 + L, :]
           + cw_ref[2:3, ls] * pad[7:7 + L, :]
           + cw_ref[1:2, ls] * pad[6:6 + L, :]
           + cw_ref[0:1, ls] * pad[5:5 + L, :])
    pad[0:SUBLANES, :] = pad[L:L + SUBLANES, :]
    return out


def _rg_gates(xc, wa_ref, wx_ref, ba, bx, lam):
    xb = xc.astype(BF16)
    nb, bw = wa_ref.shape[0], wa_ref.shape[1]
    rs, gs = [], []
    for k in range(nb):
        xk = xb[:, k * bw:(k + 1) * bw]
        rs.append(_dot(xk, wa_ref[k]))
        gs.append(_dot(xk, wx_ref[k]))
    r = _sigmoid(jnp.concatenate(rs, axis=1) + ba)
    i = _sigmoid(jnp.concatenate(gs, axis=1) + bx)
    log_a = -RG_C * r * _softplus(-lam)
    a = jnp.exp(log_a)
    em1 = jnp.tanh(log_a) * (a * a + 1.0)
    return a, jnp.sqrt(-em1) * (i * xc)


def _rg_prompt_body(x_ref, gate_ref, cw_ref, cb_ref, wa_ref, wx_ref, ba_ref, bx_ref, lam_ref, wo_ref,
                    ya_ref, hfin_ref, pad_s, a_s, b_s, h_s):
    B, Lc, W = x_ref.shape
    NS = W // LANES
    pitch = a_s.shape[1] // B

    @pl.when(pl.program_id(0) == 0)
    def _():
        pad_s[:, :, 0:SUBLANES, :] = jnp.zeros((B, NS, SUBLANES, LANES), F32)
        h_s[...] = jnp.zeros(h_s.shape, F32)

    for b in range(B):
        xc = jnp.concatenate(
            [_conv_slab(x_ref[b, :, s * LANES:(s + 1) * LANES], pad_s.at[b, s], cw_ref, cb_ref,
                        slice(s * LANES, (s + 1) * LANES)) for s in range(NS)], axis=1)
        a, bt = _rg_gates(xc, wa_ref, wx_ref, ba_ref[...], bx_ref[...], lam_ref[...])
        for s in range(NS):
            a_s[s, b * pitch:b * pitch + Lc, :] = a[:, s * LANES:(s + 1) * LANES]
            b_s[s, b * pitch:b * pitch + Lc, :] = bt[:, s * LANES:(s + 1) * LANES]

    def step(t, hs):
        out = []
        for s in range(NS):
            rows = pl.ds(t, B, stride=pitch)
            h = a_s[s, rows, :] * hs[s] + b_s[s, rows, :]
            b_s[s, rows, :] = h
            out.append(h)
        return tuple(out)

    hs = lax.fori_loop(0, Lc, step, tuple(h_s[:, s * LANES:(s + 1) * LANES] for s in range(NS)), unroll=4)
    for s in range(NS):
        h_s[:, s * LANES:(s + 1) * LANES] = hs[s]
    hfin_ref[...] = h_s[...]
    for b in range(B):
        hb = jnp.concatenate([b_s[s, b * pitch:b * pitch + Lc, :] for s in range(NS)], axis=1)
        ya_ref[b] = _dot((hb * _gelu(gate_ref[b])).astype(BF16), wo_ref[...])


def _rg_prompt(proj, B, L, col_x, col_g, cw, cb, wa, wx, ba, bx, lam, wo):
    W = cw.shape[1]
    Lc = _pick(L, 128)
    pitch = Lc + SUBLANES // 2
    assert B == SUBLANES and W % LANES == 0
    proj3 = proj.reshape(B, L, proj.shape[1])
    full = lambda *s: pl.BlockSpec(s, lambda c: (0,) * len(s))
    ya, hfin = pl.pallas_call(
        _rg_prompt_body,
        grid=(L // Lc,),
        in_specs=[
            pl.BlockSpec((B, Lc, W), lambda c: (0, c, col_x)),
            pl.BlockSpec((B, Lc, W), lambda c: (0, c, col_g)),
            full(4, W), full(1, W), full(*wa.shape), full(*wx.shape), full(1, W), full(1, W), full(1, W),
            full(*wo.shape),
        ],
        out_specs=[
            pl.BlockSpec((B, Lc, wo.shape[1]), lambda c: (0, c, 0)),
            pl.BlockSpec((B, W), lambda c: (0, 0)),
        ],
        out_shape=[
            jax.ShapeDtypeStruct((B, L, wo.shape[1]), F32),
            jax.ShapeDtypeStruct((B, W), F32),
        ],
        scratch_shapes=[
            pltpu.VMEM((B, W // LANES, Lc + SUBLANES, LANES), F32),
            pltpu.VMEM((W // LANES, B * pitch, LANES), F32),
            pltpu.VMEM((W // LANES, B * pitch, LANES), F32),
            pltpu.VMEM((B, W), F32),
        ],
        compiler_params=_cparams(("arbitrary",)),
        name="rg_prompt",
    )(proj3, proj3, cw, cb, wa, wx, ba, bx, lam, wo)
    return ya.reshape(B * L, wo.shape[1]), hfin


def _conv_rolled(u, cw_ref, cb_ref):
    return (cb_ref[...]
            + cw_ref[3:4, :] * u
            + cw_ref[2:3, :] * pltpu.roll(u, 1, axis=0)
            + cw_ref[1:2, :] * pltpu.roll(u, 2, axis=0)
            + cw_ref[0:1, :] * pltpu.roll(u, 3, axis=0))


def _rg_sample_body(x_ref, gate_ref, cpad_ref, hpad_ref, cw_ref, cb_ref, wa_ref, wx_ref, ba_ref, bx_ref,
                    lam_ref, wo_ref, ya_ref, h_ref):
    xc = _conv_rolled(x_ref[...] + cpad_ref[...], cw_ref, cb_ref)
    a, bt = _rg_gates(xc, wa_ref, wx_ref, ba_ref[...], bx_ref[...], lam_ref[...])
    row = lax.broadcasted_iota(jnp.int32, a.shape, 0) & (SAMPLE_ROWS - 1)
    h = hpad_ref[...]
    for k in range(SAMPLE_ROWS // 2, SAMPLE_ROWS):
        h = jnp.where(row == k, a * pltpu.roll(h, 1, axis=0) + bt, h)
    h_ref[...] = h
    ya_ref[...] = _dot((h * _gelu(gate_ref[...])).astype(BF16), wo_ref[...])


def _rg_sample(proj, col_x, col_g, cpad, hpad, cw, cb, wa, wx, ba, bx, lam, wo):
    T = proj.shape[0]
    W = cw.shape[1]
    TM = _pick(T, 256)
    full = lambda *s: pl.BlockSpec(s, lambda i: (0,) * len(s))
    return pl.pallas_call(
        _rg_sample_body,
        grid=(T // TM,),
        in_specs=[
            pl.BlockSpec((TM, W), lambda i: (i, col_x)),
            pl.BlockSpec((TM, W), lambda i: (i, col_g)),
            pl.BlockSpec((TM, W), lambda i: (i, 0)),
            pl.BlockSpec((TM, W), lambda i: (i, 0)),
            full(4, W), full(1, W), full(*wa.shape), full(*wx.shape), full(1, W), full(1, W), full(1, W),
            full(*wo.shape),
        ],
        out_specs=[
            pl.BlockSpec((TM, wo.shape[1]), lambda i: (i, 0)),
            pl.BlockSpec((TM, W), lambda i: (i, 0)),
        ],
        out_shape=[
            jax.ShapeDtypeStruct((T, wo.shape[1]), F32),
            jax.ShapeDtypeStruct((T, W), F32),
        ],
        compiler_params=_cparams(("parallel",)),
        name="rg_sample",
    )(proj, proj, cpad, hpad, cw, cb, wa, wx, ba, bx, lam, wo)


def _ssd_chunk(act_s, dt, dtT, alog_ref, alogT_ref, dexp_ref, seq_shift, G, R, P, N,
               yint_fn, y_s, xw_s):
    Q = act_s.shape[0]
    inner = G * R * P
    a = dt * (-jnp.exp(alog_ref[...]))
    aT = dtT * (-jnp.exp(alogT_ref[...]))
    ri = lax.broadcasted_iota(jnp.int32, (Q, Q), 0)
    ci = lax.broadcasted_iota(jnp.int32, (Q, Q), 1)
    same = (ri >> seq_shift) == (ci >> seq_shift)
    causal = same & (ci <= ri)
    causal_f = jnp.where(causal, 1.0, 0.0).astype(F32)
    same_f = jnp.where(same, 1.0, 0.0).astype(F32)
    acum = jnp.dot(causal_f, a, precision=HIGHEST, preferred_element_type=F32)
    tot = jnp.dot(same_f, a, precision=HIGHEST, preferred_element_type=F32)
    acumT = lax.dot_general(aT, causal_f, NT_DIMS, precision=HIGHEST,
                            preferred_element_type=F32)
    eacum = jnp.exp(acum)
    toend = jnp.exp(tot - acum) * dt
    for g in range(G):
        Bg = act_s[:, inner + g * N:inner + (g + 1) * N].astype(BF16)
        Cg = act_s[:, inner + (G + g) * N:inner + (G + g + 1) * N].astype(BF16)
        cb = _dot_nt(Cg, Bg)
        yint = yint_fn(g, Cg)
        for r in range(R):
            h = g * R + r
            sl = slice(h * P, (h + 1) * P)
            seg = acum[:, h:h + 1] - acumT[h:h + 1, :]
            decay = jnp.exp(jnp.where(causal, seg, NEG_INF))
            wgt = (cb * decay * dtT[h:h + 1, :]).astype(BF16)
            xh = act_s[:, sl]
            yh = _dot(wgt, xh.astype(BF16)) + dexp_ref[:, sl] * xh
            if yint is not None:
                yh = yh + yint[:, r * P:(r + 1) * P] * eacum[:, h:h + 1]
            y_s[:, sl] = yh
            xw_s[:, sl] = xh * toend[:, h:h + 1]
    return tot, eacum


def _expand_heads(x, ex_ref):
    hi = x.astype(BF16)
    r1 = x - hi.astype(F32)
    mid = r1.astype(BF16)
    lo = (r1 - mid.astype(F32)).astype(BF16)
    e = ex_ref[...]
    return _dot(hi, e) + _dot(mid, e) + _dot(lo, e)


def _ssd_finish(y, z, nw_ref, wo_ref):
    y = y * _silu(z)
    return _dot(_rms(y, nw_ref[...]).astype(BF16), wo_ref[...])


def _ssd_prompt_body(G, R, P, N,
                     xbc_ref, z_ref, dt_ref, dtT_ref, cw_ref, cb_ref, dtb_ref, dtbT_ref, alog_ref, alogT_ref,
                     dexp_ref, nw_ref, wo_ref, ex_ref, yb_ref, st_ref, pad_s, act_s, y_s, stT_s):
    Q = xbc_ref.shape[0]
    inner = G * R * P
    GW = R * P

    @pl.when(pl.program_id(1) == 0)
    def _():
        pad_s[:, 0:SUBLANES, :] = jnp.zeros((pad_s.shape[0], SUBLANES, LANES), F32)
        stT_s[...] = jnp.zeros(stT_s.shape, F32)

    for s in range(pad_s.shape[0]):
        ls = slice(s * LANES, (s + 1) * LANES)
        act_s[:, ls] = _silu(_conv_slab(xbc_ref[:, ls], pad_s.at[s], cw_ref, cb_ref, ls))
    dt = _softplus(dt_ref[...] + dtb_ref[...])
    dtT = _softplus(dtT_ref[...] + dtbT_ref[...])
    a = dt * (-jnp.exp(alog_ref[...]))
    aT = dtT * (-jnp.exp(alogT_ref[...]))
    ri = lax.broadcasted_iota(jnp.int32, (Q, Q), 0)
    ci = lax.broadcasted_iota(jnp.int32, (Q, Q), 1)
    causal = ci <= ri
    causal_f = jnp.where(causal, 1.0, 0.0).astype(F32)
    hi = dict(precision=HIGHEST, preferred_element_type=F32)
    acum = jnp.dot(causal_f, a, **hi)
    tot = jnp.dot(jnp.ones((SUBLANES, Q), F32), a, **hi)
    acumT = lax.dot_general(aT, causal_f, NT_DIMS, **hi)
    fx = _expand_heads(jnp.concatenate([jnp.exp(acum), jnp.exp(tot[0:1, :] - acum) * dt, jnp.exp(tot)], axis=0),
                       ex_ref)
    eacum_x, toend_x, dec_x = fx[0:Q], fx[Q:2 * Q], fx[2 * Q:2 * Q + 1]
    col2 = acum * LOG2E
    row2 = (acumT - jnp.log(dtT)) * LOG2E
    lane = lax.broadcasted_iota(jnp.int32, (Q, LANES), 1)
    for g in range(G):
        gs = slice(g * GW, (g + 1) * GW)
        Bf = act_s[:, inner + g * N:inner + (g + 1) * N]
        Bg = Bf.astype(BF16)
        Cg = act_s[:, inner + (G + g) * N:inner + (G + g + 1) * N].astype(BF16)
        cb = jnp.where(causal, _dot_nt(Cg, Bg), 0.0)
        y_s[:, gs] = (_dot(Cg, stT_s[:, gs].astype(BF16)) * eacum_x[:, gs]
                      + dexp_ref[:, gs] * act_s[:, gs])
        for pr in range(R // 2):
            h0 = g * R + 2 * pr
            ps = slice(h0 * P, (h0 + 2) * P)
            wg = []
            for h in (h0, h0 + 1):
                e = jnp.exp2(jnp.where(causal, col2[:, h:h + 1] - row2[h:h + 1, :], NEG_INF))
                wg.append((cb * e).astype(BF16))
            xp = act_s[:, ps]
            xa = jnp.where(lane < P, xp, 0.0).astype(BF16)
            xb = jnp.where(lane < P, 0.0, xp).astype(BF16)
            y_s[:, ps] = y_s[:, ps] + _dot(jnp.concatenate(wg, axis=1), jnp.concatenate([xa, xb], axis=0))
        xw = (act_s[:, gs] * toend_x[:, gs]).astype(BF16)
        ST = _dot(jnp.transpose(Bf).astype(BF16), xw)
        stT_s[:, gs] = dec_x[:, gs] * stT_s[:, gs] + ST
    yb_ref[...] = _ssd_finish(y_s[...], z_ref[...], nw_ref, wo_ref)

    @pl.when(pl.program_id(1) == pl.num_programs(1) - 1)
    def _():
        for g in range(G):
            st_ref[0, g * R:(g + 1) * R] = jnp.transpose(stT_s[:, g * GW:(g + 1) * GW]).reshape(R, P, N)


def _ssd_prompt(proj, dt, dtT, B, L, col_xbc, col_z, dims, cw, cb, dtb, dtbT, alog, alogT, dexp, nw, wo):
    G, R, P, N = dims
    H = G * R
    inner = H * P
    CD = cw.shape[1]
    Q = _pick(L, SSD_CHUNK)
    nC = L // Q
    assert 2 * P == LANES and R % 2 == 0 and N == LANES
    head_of_lane = jnp.arange(inner, dtype=jnp.int32) // P
    ex = (jnp.arange(LANES, dtype=jnp.int32)[:, None] == head_of_lane[None, :]).astype(BF16)
    full = lambda *s: pl.BlockSpec(s, lambda b, c: (0,) * len(s))
    return pl.pallas_call(
        functools.partial(_ssd_prompt_body, G, R, P, N),
        grid=(B, nC),
        in_specs=[
            pl.BlockSpec((Q, CD), lambda b, c: (b * nC + c, col_xbc)),
            pl.BlockSpec((Q, inner), lambda b, c: (b * nC + c, col_z)),
            pl.BlockSpec((Q, LANES), lambda b, c: (b * nC + c, 0)),
            pl.BlockSpec((H, Q), lambda b, c: (0, b * nC + c)),
            full(4, CD), full(1, CD), full(1, LANES), full(H, 1), full(1, LANES), full(H, 1),
            full(1, inner), full(1, inner), full(*wo.shape), full(LANES, inner),
        ],
        out_specs=[
            pl.BlockSpec((Q, wo.shape[1]), lambda b, c: (b * nC + c, 0)),
            pl.BlockSpec((1, H, P, N), lambda b, c: (b, 0, 0, 0)),
        ],
        out_shape=[
            jax.ShapeDtypeStruct((B * L, wo.shape[1]), F32),
            jax.ShapeDtypeStruct((B, H, P, N), F32),
        ],
        scratch_shapes=[
            pltpu.VMEM((CD // LANES, Q + SUBLANES, LANES), F32),
            pltpu.VMEM((Q, CD), F32),
            pltpu.VMEM((Q, inner), F32),
            pltpu.VMEM((N, inner), F32),
        ],
        compiler_params=_cparams(("parallel", "arbitrary")),
        name="ssd_prompt",
    )(proj, proj, dt, dtT, cw, cb, dtb, dtbT, alog, alogT, dexp, nw, wo, ex)


def _ssd_sample_body(G, R, P, N,
                     xbc_ref, cpad_ref, z_ref, dt_ref, dtT_ref, h0_ref, cw_ref, cb_ref, dtb_ref, dtbT_ref,
                     alog_ref, alogT_ref, dexp_ref, nw_ref, wo_ref, yb_ref, st_ref,
                     act_s, y_s, xw_s, yint_s, xwT_s, tot_s, eacum_s):
    Q = xbc_ref.shape[0]
    inner = G * R * P
    bi = pl.program_id(1)
    half = SAMPLE_ROWS // 2

    @pl.when(bi == 0)
    def _():
        act_s[...] = _silu(_conv_rolled(xbc_ref[...] + cpad_ref[...], cw_ref, cb_ref))
        rows = lax.broadcasted_iota(jnp.int32, (Q, LANES), 0) & (SAMPLE_ROWS - 1)
        cols = lax.broadcasted_iota(jnp.int32, dtT_ref.shape, 1) & (SAMPLE_ROWS - 1)
        dt = jnp.where(rows >= half, _softplus(dt_ref[...] + dtb_ref[...]), 0.0)
        dtT = jnp.where(cols >= half, _softplus(dtT_ref[...] + dtbT_ref[...]), 0.0)
        seq_shift = SAMPLE_ROWS.bit_length() - 1
        tot, eacum = _ssd_chunk(act_s, dt, dtT, alog_ref, alogT_ref, dexp_ref, seq_shift, G, R, P, N,
                                lambda g, Cg: None, y_s, xw_s)
        tot_s[...] = tot
        eacum_s[...] = eacum
        for g in range(G):
            xwT_s[g] = jnp.transpose(xw_s[:, g * R * P:(g + 1) * R * P]).astype(BF16)

    rid = lax.broadcasted_iota(jnp.int32, (Q, N), 0)
    for q in range(h0_ref.shape[0]):
        r0 = pl.multiple_of((bi * h0_ref.shape[0] + q) * SAMPLE_ROWS, SAMPLE_ROWS)
        mine = (rid >= r0) & (rid < r0 + SAMPLE_ROWS)
        dec = jnp.exp(tot_s[pl.ds(r0, 1), :])
        for g in range(G):
            Cb = act_s[pl.ds(r0, SAMPLE_ROWS), inner + (G + g) * N:inner + (G + g + 1) * N]
            stg = h0_ref[q, g * R:(g + 1) * R].reshape(R * P, N)
            yint_s[pl.ds(r0, SAMPLE_ROWS), g * R * P:(g + 1) * R * P] = _dot_nt(Cb, stg)
            Bg = jnp.where(mine, act_s[:, inner + g * N:inner + (g + 1) * N], 0.0).astype(BF16)
            S = _dot(xwT_s[g], Bg)
            for r in range(R):
                h = g * R + r
                st_ref[q, h] = dec[:, h:h + 1] * h0_ref[q, h] + S[r * P:(r + 1) * P, :]

    @pl.when(bi == pl.num_programs(1) - 1)
    def _():
        for h in range(G * R):
            sl = slice(h * P, (h + 1) * P)
            y_s[:, sl] = y_s[:, sl] + yint_s[:, sl] * eacum_s[:, h:h + 1]
        yb_ref[...] = _ssd_finish(y_s[...], z_ref[...], nw_ref, wo_ref)


def _ssd_sample(proj, dt, dtT, cpad, h0, col_xbc, col_z, dims, cw, cb, dtb, dtbT, alog, alogT, dexp, nw, wo):
    G, R, P, N = dims
    H = G * R
    inner = H * P
    CD = cw.shape[1]
    T = proj.shape[0]
    Bs = h0.shape[0]
    Q = _pick(T, SSD_CHUNK)
    SB = 2
    nb = Q // SAMPLE_ROWS // SB
    assert Q % (SAMPLE_ROWS * SB) == 0
    full = lambda *s: pl.BlockSpec(s, lambda i, j: (0,) * len(s))
    return pl.pallas_call(
        functools.partial(_ssd_sample_body, G, R, P, N),
        grid=(T // Q, nb),
        in_specs=[
            pl.BlockSpec((Q, CD), lambda i, j: (i, col_xbc)),
            pl.BlockSpec((Q, CD), lambda i, j: (i, 0)),
            pl.BlockSpec((Q, inner), lambda i, j: (i, col_z)),
            pl.BlockSpec((Q, LANES), lambda i, j: (i, 0)),
            pl.BlockSpec((H, Q), lambda i, j: (0, i)),
            pl.BlockSpec((SB, H, P, N), lambda i, j: (i * nb + j, 0, 0, 0)),
            full(4, CD), full(1, CD), full(1, LANES), full(H, 1), full(1, LANES), full(H, 1),
            full(1, inner), full(1, inner), full(*wo.shape),
        ],
        out_specs=[
            pl.BlockSpec((Q, wo.shape[1]), lambda i, j: (i, 0)),
            pl.BlockSpec((SB, H, P, N), lambda i, j: (i * nb + j, 0, 0, 0)),
        ],
        out_shape=[
            jax.ShapeDtypeStruct((T, wo.shape[1]), F32),
            jax.ShapeDtypeStruct((Bs, H, P, N), F32),
        ],
        scratch_shapes=[
            pltpu.VMEM((Q, CD), F32),
            pltpu.VMEM((Q, inner), F32),
            pltpu.VMEM((Q, inner), F32),
            pltpu.VMEM((Q, inner), F32),
            pltpu.VMEM((G, R * P, Q), BF16),
            pltpu.VMEM((Q, LANES), F32),
            pltpu.VMEM((Q, LANES), F32),
        ],
        compiler_params=_cparams(("parallel", "arbitrary")),
        name="ssd_sample",
    )(proj, cpad, proj, dt, dtT, h0, cw, cb, dtb, dtbT, alog, alogT, dexp, nw, wo)


def _merge_body(x_ref, ya_ref, yb_ref, ga_ref, gb_ref, wout_ref, nf_ref, x1_ref, h2t_ref):
    m = _sigmoid(ga_ref[...]) * ya_ref[...] + _sigmoid(gb_ref[...]) * yb_ref[...]
    x1 = x_ref[...] + _dot(m.astype(BF16), wout_ref[...])
    x1_ref[...] = x1
    h2t_ref[...] = jnp.transpose(_rms(x1, nf_ref[...])).astype(BF16)


def _merge(x2d, ya, yb, proj, col_ga, col_gb, wout, nf):
    T, D = x2d.shape
    TM = _pick(T, 512)
    row = lambda c: pl.BlockSpec((TM, D), lambda i: (i, c))
    return pl.pallas_call(
        _merge_body,
        grid=(T // TM,),
        in_specs=[row(0), row(0), row(0), row(col_ga), row(col_gb),
                  pl.BlockSpec(wout.shape, lambda i: (0, 0)), pl.BlockSpec((1, D), lambda i: (0, 0))],
        out_specs=[row(0), pl.BlockSpec((D, TM), lambda i: (0, i))],
        out_shape=[jax.ShapeDtypeStruct((T, D), F32), jax.ShapeDtypeStruct((D, T), BF16)],
        compiler_params=_cparams(("parallel",)),
        name="merge",
    )(x2d, ya, yb, proj, proj, wout, nf)


def _staircase(k):
    return [(ka, k // (ka + 1)) for ka in range(k)]


def _take_top(cur, k_top, want_rank):
    rank = jnp.full(cur.shape, float(k_top), F32) if want_rank else None
    vals = []
    for k in range(k_top):
        m = jnp.max(cur, axis=0, keepdims=True)
        hit = cur == m
        vals.append(m)
        if want_rank:
            rank = jnp.where(hit, float(k), rank)
        cur = jnp.where(hit, NEG_INF, cur)
    taken = jnp.sum(jnp.where(cur == NEG_INF, 1.0, 0.0), axis=0, keepdims=True)
    return vals, rank, taken


def _take_top_ties(x, k_top, val_ref, rank_ref):
    rows = lax.broadcasted_iota(jnp.int32, x.shape, 0)
    kk = lax.broadcasted_iota(jnp.int32, (k_top, x.shape[1]), 0)

    def body(k, carry):
        cur, rank, vals = carry
        m = jnp.max(cur, axis=0, keepdims=True)
        first = jnp.min(jnp.where(cur == m, rows, x.shape[0]), axis=0, keepdims=True)
        hit = rows == first
        return (jnp.where(hit, NEG_INF, cur), jnp.where(hit, k.astype(F32), rank),
                jnp.where(kk == k, m, vals))

    init = (x, jnp.full(x.shape, float(k_top), F32), jnp.zeros((k_top, x.shape[1]), F32))
    _, rank, vals = lax.fori_loop(0, k_top, body, init)
    val_ref[...] = vals
    rank_ref[...] = rank


def _route_body(h2t_ref, wqt_ref, keys_ref, na_ref, ea_ref, rb_ref, eb_ref, sv_s, rk_s, cand_s, cv_s, cr_s):
    NH = keys_ref.shape[0]
    NK, KH = keys_ref.shape[2], keys_ref.shape[3]
    K = PEER_TOPK
    TB = h2t_ref.shape[1]
    tiles = [slice(tl * LANES, (tl + 1) * LANES) for tl in range(TB // LANES)]
    qt = _dot(wqt_ref[...], h2t_ref[...]).astype(BF16)
    cand_s[...] = jnp.full(cand_s.shape, NEG_INF, F32)

    def candidates():
        off = 0
        for ka, nb in _staircase(K):
            cand_s[off:off + nb, :] = sv_s[0, ka:ka + 1, :] + sv_s[1, 0:nb, :]
            off += nb

    def emit(h, sT, in_top, rb, sel, na_of):
        top = sv_s[0, 0:1, :] + sv_s[1, 0:1, :]
        z = jnp.sum(jnp.where(sel, jnp.exp(cand_s[...] - top), 0.0), axis=0, keepdims=True)
        ex = [jnp.where(in_top[s], jnp.exp(sT[s] - sv_s[s, 0:1, :]), 0.0) for s in range(2)]
        na = jnp.zeros(sT[0].shape, F32)
        off = 0
        for ka, nb in _staircase(K):
            cnt = jnp.sum(jnp.where(sel[off:off + nb, :], 1.0, 0.0), axis=0, keepdims=True)
            na = jnp.where(na_of(ka), cnt, na)
            off += nb
        na_ref[h * NK:(h + 1) * NK, :] = na
        ea_ref[h * NK:(h + 1) * NK, :] = ex[0] / z
        rb_ref[h] = rb.astype(BF16)
        eb_ref[h] = ex[1].astype(BF16)

    def scores(h):
        return [_dot(keys_ref[h, s], qt[(h * 2 + s) * KH:(h * 2 + s + 1) * KH, :]) for s in range(2)]

    most = []
    for h in range(NH):
        sT = scores(h)
        seen = jnp.zeros((1, LANES), F32)
        rbs = []
        for s in range(2):
            for ln in tiles:
                vals, rank, taken = _take_top(sT[s][:, ln], K, s == 1)
                for k in range(K):
                    sv_s[s, k:k + 1, ln] = vals[k]
                seen = jnp.maximum(seen, taken)
                if s == 1:
                    rbs.append(rank)
        candidates()
        cand = cand_s[...]
        tau = _take_top(cand, K, False)[0][K - 1]
        sel = cand >= tau
        most.append(jnp.maximum(seen, jnp.max(jnp.sum(jnp.where(sel, 1.0, 0.0), axis=0, keepdims=True),
                                              axis=1, keepdims=True)))
        emit(h, sT, [sT[s] >= sv_s[s, K - 1:K, :] for s in range(2)], jnp.concatenate(rbs, axis=1), sel,
             lambda ka: sT[0] == sv_s[0, ka:ka + 1, :])

    for h in range(NH):
        @pl.when(jnp.max(most[h]) > K)
        def _(h=h):
            sT = scores(h)
            for s in range(2):
                for ln in tiles:
                    _take_top_ties(sT[s][:, ln], K, sv_s.at[s, :, ln], rk_s.at[s, :, ln])
            candidates()
            for ln in tiles:
                _take_top_ties(cand_s[:, ln], K, cv_s.at[:, ln], cr_s.at[:, ln])
            emit(h, sT, [rk_s[s] < K for s in range(2)], rk_s[1], cr_s[...] < K,
                 lambda ka: rk_s[0] == float(ka))


def _route(h2t, wqt, keys):
    D, T = h2t.shape
    NH, _, NK, KH = keys.shape
    TB = _pick(T, 256)
    assert TB % LANES == 0
    ncand = sum(nb for _, nb in _staircase(PEER_TOPK))
    ncand_pad = -(-ncand // SUBLANES) * SUBLANES
    flat = pl.BlockSpec((NH * NK, TB), lambda i: (0, i))
    tok = pl.BlockSpec((NH, NK, TB), lambda i: (0, 0, i))
    return pl.pallas_call(
        _route_body,
        grid=(T // TB,),
        in_specs=[pl.BlockSpec((D, TB), lambda i: (0, i)),
                  pl.BlockSpec(wqt.shape, lambda i: (0, 0)),
                  pl.BlockSpec(keys.shape, lambda i: (0, 0, 0, 0))],
        out_specs=[flat, flat, tok, tok],
        out_shape=[jax.ShapeDtypeStruct((NH * NK, T), F32), jax.ShapeDtypeStruct((NH * NK, T), F32),
                   jax.ShapeDtypeStruct((NH, NK, T), BF16), jax.ShapeDtypeStruct((NH, NK, T), BF16)],
        scratch_shapes=[pltpu.VMEM((2, PEER_TOPK, TB), F32), pltpu.VMEM((2, NK, TB), F32),
                        pltpu.VMEM((ncand_pad, TB), F32), pltpu.VMEM((PEER_TOPK, TB), F32),
                        pltpu.VMEM((ncand_pad, TB), F32)],
        compiler_params=_cparams(("parallel",)),
        name="peer_route",
    )(h2t, wqt, keys)


def _peer_body(h2t_ref, x1_ref, u_ref, vt_ref, na_ref, ea_ref, rb_ref, eb_ref, nfin_ref, y_ref,
               s_s, a_s, acc_s):
    ec = pl.program_id(1)
    NH, NK, TB = rb_ref.shape
    ni = u_ref.shape[0] // NK

    @pl.when(ec == 0)
    def _():
        acc_s[...] = jnp.zeros(acc_s.shape, F32)

    s_s[...] = _dot(u_ref[...], h2t_ref[...])

    def row_tile(grp, il):
        return jnp.broadcast_to(grp[il:il + 1, :], (BF16_ROWS, LANES)).astype(BF16)

    for tl in range(TB // LANES):
        ln = slice(tl * LANES, (tl + 1) * LANES)
        rows = [pl.ds(pl.multiple_of(h * NK + ec * ni, SUBLANES), ni) for h in range(NH)]
        na = [na_ref[rows[h], ln] for h in range(NH)]
        ea = [ea_ref[rows[h], ln] for h in range(NH)]
        for il in range(ni):
            na_t = [row_tile(na[h], il) for h in range(NH)]
            ea_t = [row_tile(ea[h], il) for h in range(NH)]
            for jt in range(NK // BF16_ROWS):
                js = slice(jt * BF16_ROWS, (jt + 1) * BF16_ROWS)
                w = jnp.zeros((BF16_ROWS, LANES), BF16)
                for h in range(NH):
                    w = w + ea_t[h] * jnp.where(rb_ref[h, js, ln] < na_t[h], eb_ref[h, js, ln], 0.0)
                e = slice(il * NK + jt * BF16_ROWS, il * NK + (jt + 1) * BF16_ROWS)
                a_s[e, ln] = _gelu(s_s[e, ln]).astype(BF16) * w
    acc_s[...] += _dot(vt_ref[...], a_s[...])

    @pl.when(ec == pl.num_programs(1) - 1)
    def _():
        x2 = x1_ref[...] + jnp.transpose(acc_s[...])
        y_ref[...] = _rms(x2, nfin_ref[...])


def _peer(h2t, x1, u, vt, na, ea, rb, eb, nfin):
    D, T = h2t.shape
    E = u.shape[0]
    NH, NK, _ = rb.shape
    TB = _pick(T, 512)
    EC = 2 * SUBLANES * NK
    assert TB % LANES == 0 and E % EC == 0 and NK % BF16_ROWS == 0
    tok = pl.BlockSpec((NH, NK, TB), lambda i, j: (0, 0, i))
    flat = pl.BlockSpec((NH * NK, TB), lambda i, j: (0, i))
    return pl.pallas_call(
        _peer_body,
        grid=(T // TB, E // EC),
        in_specs=[pl.BlockSpec((D, TB), lambda i, j: (0, i)),
                  pl.BlockSpec((TB, D), lambda i, j: (i, 0)),
                  pl.BlockSpec((EC, D), lambda i, j: (j, 0)),
                  pl.BlockSpec((D, EC), lambda i, j: (0, j)),
                  flat, flat, tok, tok,
                  pl.BlockSpec((1, D), lambda i, j: (0, 0))],
        out_specs=pl.BlockSpec((TB, D), lambda i, j: (i, 0)),
        out_shape=jax.ShapeDtypeStruct((T, D), F32),
        scratch_shapes=[pltpu.VMEM((EC, TB), F32), pltpu.VMEM((EC, TB), BF16), pltpu.VMEM((D, TB), F32)],
        compiler_params=_cparams(("parallel", "arbitrary")),
        name="peer_mix",
    )(h2t, x1, u, vt, na, ea, rb, eb, nfin)


def _pad_lanes(row):
    return jnp.pad(row, ((0, 0), (0, LANES - row.shape[1])))


def kernel(x_prompt, x_sample, state_rg_h, state_rg_conv, state_ssd_h, state_ssd_conv, norm_mix, w_in, rg_conv_w, rg_conv_b, rg_wa, rg_ba, rg_wx, rg_bx, rg_lam, w_rg_out, ssd_conv_w, ssd_conv_b, ssd_dt_bias, ssd_a_log, ssd_d, ssd_norm, w_ssd_out, w_out, norm_ffn, peer_wq, peer_keys, peer_u, peer_v, norm_final):
    depth = w_in.shape[0]
    B, L, D = x_prompt.shape
    Bs, Ls, _ = x_sample.shape
    assert depth == 1 and Ls == SAMPLE_ROWS // 2 and rg_conv_w.shape[1] == 4 and L >= 3
    W = rg_conv_w.shape[2]
    H = ssd_a_log.shape[1]
    inner = ssd_norm.shape[1]
    CD = ssd_conv_w.shape[2]
    P = inner // H
    N = state_ssd_h.shape[-1]
    G = (CD - inner) // (2 * N)
    R = H // G
    dims = (G, R, P, N)
    assert W == D and inner == 2 * D and CD == 3 * D and H % SUBLANES == 0 and H <= LANES
    o_x, o_g, o_z, o_xbc, o_dt, o_m = 0, W, 2 * W, 2 * W + inner, 2 * W + inner + CD, 2 * W + inner + CD + H
    col_xbc, col_x, col_z, col_g, col_ga, col_gb = 0, CD // W, (CD + W) // inner, (CD + W + inner) // W, \
        (CD + 2 * W + inner) // D, (CD + 2 * W + inner + D) // D

    yp = x_prompt.reshape(B * L, D)
    ys = x_sample.reshape(Bs * Ls, D)
    outs = {k: [] for k in ("prh", "prc", "psh", "psc", "srh", "src", "ssh", "ssc")}
    lead = SAMPLE_ROWS - Ls
    for l in range(depth):
        wi = w_in[l]
        w_main = jnp.concatenate([wi[:, o_xbc:o_dt], wi[:, o_x:o_g], wi[:, o_z:o_xbc], wi[:, o_g:o_z],
                                  wi[:, o_m:]], axis=1).astype(BF16)
        wdt = _pad_lanes(wi[:, o_dt:o_m])
        g_mix = norm_mix[l][None]
        rg_w = (rg_conv_w[l], rg_conv_b[l][None], rg_wa[l].astype(BF16), rg_wx[l].astype(BF16),
                rg_ba[l][None], rg_bx[l][None], rg_lam[l][None], w_rg_out[l].astype(BF16))
        ssd_w = (ssd_conv_w[l], ssd_conv_b[l][None], _pad_lanes(ssd_dt_bias[l][None]), ssd_dt_bias[l][:, None],
                 _pad_lanes(ssd_a_log[l][None]), ssd_a_log[l][:, None], jnp.repeat(ssd_d[l], P)[None],
                 ssd_norm[l][None], w_ssd_out[l].astype(BF16))
        wout = w_out[l].astype(BF16)
        nf = norm_ffn[l][None]
        wqt = peer_wq[l].astype(BF16).T
        keys = peer_keys[l].astype(BF16)
        u = peer_u[l].astype(BF16)
        vt = peer_v[l].astype(BF16).T
        nfin = norm_final[None]

        proj, dt, dtT = _inproj(yp, g_mix, w_main, wdt, H)
        ya, rgh = _rg_prompt(proj, B, L, col_x, col_g, *rg_w)
        yb, ssh = _ssd_prompt(proj, dt, dtT, B, L, col_xbc, col_z, dims, *ssd_w)
        p3 = proj.reshape(B, L, -1)
        outs["prh"].append(rgh.reshape(B, W))
        outs["prc"].append(p3[:, L - 3:, CD:CD + W])
        outs["psh"].append(ssh)
        outs["psc"].append(p3[:, L - 3:, :CD])
        x1, h2t = _merge(yp, ya, yb, proj, col_ga, col_gb, wout, nf)
        yp = _peer(h2t, x1, u, vt, *_route(h2t, wqt, keys), nfin)

        xe = jnp.pad(ys.reshape(Bs, Ls, D), ((0, 0), (lead, 0), (0, 0))).reshape(Bs * SAMPLE_ROWS, D)
        proj_e, dt_e, dtT_e = _inproj(xe, g_mix, w_main, wdt, H)
        rows = lambda a3: a3.reshape(Bs * SAMPLE_ROWS, a3.shape[-1])
        rg_cpad = rows(jnp.pad(state_rg_conv[l], ((0, 0), (lead - 3, Ls), (0, 0))))
        rg_hpad = rows(jnp.pad(state_rg_h[l][:, None, :], ((0, 0), (lead - 1, Ls), (0, 0))))
        ssd_cpad = rows(jnp.pad(state_ssd_conv[l], ((0, 0), (lead - 3, Ls), (0, 0))))
        ya_e, h_e = _rg_sample(proj_e, col_x, col_g, rg_cpad, rg_hpad, *rg_w)
        yb_e, ssh_s = _ssd_sample(proj_e, dt_e, dtT_e, ssd_cpad, state_ssd_h[l], col_xbc, col_z, dims, *ssd_w)
        toks = lambda a2: a2.reshape(Bs, SAMPLE_ROWS, -1)[:, lead:].reshape(Bs * Ls, -1)
        p3 = proj_e.reshape(Bs, SAMPLE_ROWS, -1)
        outs["srh"].append(h_e.reshape(Bs, SAMPLE_ROWS, W)[:, -1])
        outs["src"].append(p3[:, SAMPLE_ROWS - 3:, CD:CD + W])
        outs["ssh"].append(ssh_s)
        outs["ssc"].append(p3[:, SAMPLE_ROWS - 3:, :CD])
        proj_s = toks(proj_e)
        x1, h2t = _merge(ys, toks(ya_e), toks(yb_e), proj_s, col_ga, col_gb, wout, nf)
        ys = _peer(h2t, x1, u, vt, *_route(h2t, wqt, keys), nfin)

    st = lambda k: jnp.stack(outs[k])
    return (yp.reshape(B, L, D), ys.reshape(Bs, Ls, D), st("prh"), st("prc"), st("psh"), st("psc"),
            st("srh"), st("src"), st("ssh"), st("ssc"))
```

```python
import functools

import jax
import jax.numpy as jnp
from jax import lax
from jax.experimental import pallas as pl
from jax.experimental.pallas import tpu as pltpu

F32 = jnp.float32
BF16 = jnp.bfloat16
EPS = 1e-6
RG_C = 8.0
PEER_TOPK = 16
SSD_CHUNK = 128
LANES = 128
SUBLANES = 8
BF16_ROWS = 16
SAMPLE_ROWS = 8
NEG_INF = float("-inf")
LOG2E = 1.4426950408889634
HIGHEST = lax.Precision.HIGHEST
NT_DIMS = (((1,), (1,)), ((), ()))
VMEM_LIMIT = 56 * 1024 * 1024


def _cparams(sem):
    return pltpu.CompilerParams(dimension_semantics=sem, vmem_limit_bytes=VMEM_LIMIT)


def _pick(n, pref):
    t = min(n, pref)
    while n % t:
        t -= SUBLANES
    return t


def _sigmoid(x):
    return 1.0 / (1.0 + jnp.exp2(x * -LOG2E))


def _silu(x):
    return x * _sigmoid(x)


def _softplus(x):
    return jnp.maximum(x, 0.0) + jnp.log1p(jnp.exp(-jnp.abs(x)))


def _gelu(x):
    k1 = -2.0 * 0.7978845608028654 * LOG2E
    return x / (1.0 + jnp.exp2(x * (k1 + (k1 * 0.044715) * (x * x))))


def _rms(x, g):
    return x * lax.rsqrt(jnp.mean(x * x, axis=-1, keepdims=True) + EPS) * g


def _dot(a, b):
    return jnp.dot(a, b, preferred_element_type=F32)


def _dot_nt(a, b):
    return lax.dot_general(a, b, NT_DIMS, preferred_element_type=F32)


def _inproj_body(x_ref, g_ref, w_ref, wdt_ref, o_ref, odt_ref, odtT_ref, xn_ref):
    @pl.when(pl.program_id(1) == 0)
    def _():
        xn = _rms(x_ref[...], g_ref[...])
        xn_ref[...] = xn.astype(BF16)
        dt = jnp.dot(xn, wdt_ref[...], precision=HIGHEST, preferred_element_type=F32)
        odt_ref[...] = dt
        odtT_ref[...] = jnp.transpose(dt)[:odtT_ref.shape[0], :]

    o_ref[...] = _dot(xn_ref[...], w_ref[...])


def _inproj(x2d, g, w_main, wdt, H):
    T, D = x2d.shape
    N = w_main.shape[1]
    TM = _pick(T, 1024)
    TN = 3072
    assert N % TN == 0 and TM % LANES == 0
    return pl.pallas_call(
        _inproj_body,
        grid=(T // TM, N // TN),
        in_specs=[
            pl.BlockSpec((TM, D), lambda i, j: (i, 0)),
            pl.BlockSpec((1, D), lambda i, j: (0, 0)),
            pl.BlockSpec((D, TN), lambda i, j: (0, j)),
            pl.BlockSpec((D, LANES), lambda i, j: (0, 0)),
        ],
        out_specs=[
            pl.BlockSpec((TM, TN), lambda i, j: (i, j)),
            pl.BlockSpec((TM, LANES), lambda i, j: (i, 0)),
            pl.BlockSpec((H, TM), lambda i, j: (0, i)),
        ],
        out_shape=[
            jax.ShapeDtypeStruct((T, N), F32),
            jax.ShapeDtypeStruct((T, LANES), F32),
            jax.ShapeDtypeStruct((H, T), F32),
        ],
        scratch_shapes=[pltpu.VMEM((TM, D), BF16)],
        compiler_params=_cparams(("parallel", "arbitrary")),
        name="inproj",
    )(x2d, g, w_main, wdt)


def _conv_slab(x, pad, cw_ref, cb_ref, ls):
    L = x.shape[0]
    pad[SUBLANES:SUBLANES + L, :] = x
    out = (cb_ref[:, ls]
           + cw_ref[3:4, ls] * pad[8:8 + L, :]
           + cw_ref[2:3, ls] * pad[7:7 + L, :]
           + cw_ref[1:2, ls] * pad[6:6 + L, :]
           + cw_ref[0:1, ls] * pad[5:5 + L, :])
    pad[0:SUBLANES, :] = pad[L:L + SUBLANES, :]
    return out


def _rg_gates(xc, wa_ref, wx_ref, ba, bx, lam):
    xb = xc.astype(BF16)
    nb, bw = wa_ref.shape[0], wa_ref.shape[1]
    rs, gs = [], []
    for k in range(nb):
        xk = xb[:, k * bw:(k + 1) * bw]
        rs.append(_dot(xk, wa_ref[k]))
        gs.append(_dot(xk, wx_ref[k]))
    r = _sigmoid(jnp.concatenate(rs, axis=1) + ba)
    i = _sigmoid(jnp.concatenate(gs, axis=1) + bx)
    log_a = -RG_C * r * _softplus(-lam)
    a = jnp.exp(log_a)
    em1 = jnp.tanh(log_a) * (a * a + 1.0)
    return a, jnp.sqrt(-em1) * (i * xc)


def _rg_prompt_body(x_ref, gate_ref, cw_ref, cb_ref, wa_ref, wx_ref, ba_ref, bx_ref, lam_ref, wo_ref,
                    ya_ref, hfin_ref, pad_s, a_s, b_s, h_s):
    B, Lc, W = x_ref.shape
    NS = W // LANES
    pitch = a_s.shape[1] // B

    @pl.when(pl.program_id(0) == 0)
    def _():
        pad_s[:, :, 0:SUBLANES, :] = jnp.zeros((B, NS, SUBLANES, LANES), F32)
        h_s[...] = jnp.zeros(h_s.shape, F32)

    for b in range(B):
        xc = jnp.concatenate(
            [_conv_slab(x_ref[b, :, s * LANES:(s + 1) * LANES], pad_s.at[b, s], cw_ref, cb_ref,
                        slice(s * LANES, (s + 1) * LANES)) for s in range(NS)], axis=1)
        a, bt = _rg_gates(xc, wa_ref, wx_ref, ba_ref[...], bx_ref[...], lam_ref[...])
        for s in range(NS):
            a_s[s, b * pitch:b * pitch + Lc, :] = a[:, s * LANES:(s + 1) * LANES]
            b_s[s, b * pitch:b * pitch + Lc, :] = bt[:, s * LANES:(s + 1) * LANES]

    def step(t, hs):
        out = []
        for s in range(NS):
            rows = pl.ds(t, B, stride=pitch)
            h = a_s[s, rows, :] * hs[s] + b_s[s, rows, :]
            b_s[s, rows, :] = h
            out.append(h)
        return tuple(out)

    hs = lax.fori_loop(0, Lc, step, tuple(h_s[:, s * LANES:(s + 1) * LANES] for s in range(NS)), unroll=4)
    for s in range(NS):
        h_s[:, s * LANES:(s + 1) * LANES] = hs[s]
    hfin_ref[...] = h_s[...]
    for b in range(B):
        hb = jnp.concatenate([b_s[s, b * pitch:b * pitch + Lc, :] for s in range(NS)], axis=1)
        ya_ref[b] = _dot((hb * _gelu(gate_ref[b])).astype(BF16), wo_ref[...])


def _rg_prompt(proj, B, L, col_x, col_g, cw, cb, wa, wx, ba, bx, lam, wo):
    W = cw.shape[1]
    Lc = _pick(L, 128)
    pitch = Lc + SUBLANES // 2
    assert B == SUBLANES and W % LANES == 0
    proj3 = proj.reshape(B, L, proj.shape[1])
    full = lambda *s: pl.BlockSpec(s, lambda c: (0,) * len(s))
    ya, hfin = pl.pallas_call(
        _rg_prompt_body,
        grid=(L // Lc,),
        in_specs=[
            pl.BlockSpec((B, Lc, W), lambda c: (0, c, col_x)),
            pl.BlockSpec((B, Lc, W), lambda c: (0, c, col_g)),
            full(4, W), full(1, W), full(*wa.shape), full(*wx.shape), full(1, W), full(1, W), full(1, W),
            full(*wo.shape),
        ],
        out_specs=[
            pl.BlockSpec((B, Lc, wo.shape[1]), lambda c: (0, c, 0)),
            pl.BlockSpec((B, W), lambda c: (0, 0)),
        ],
        out_shape=[
            jax.ShapeDtypeStruct((B, L, wo.shape[1]), F32),
            jax.ShapeDtypeStruct((B, W), F32),
        ],
        scratch_shapes=[
            pltpu.VMEM((B, W // LANES, Lc + SUBLANES, LANES), F32),
            pltpu.VMEM((W // LANES, B * pitch, LANES), F32),
            pltpu.VMEM((W // LANES, B * pitch, LANES), F32),
            pltpu.VMEM((B, W), F32),
        ],
        compiler_params=_cparams(("arbitrary",)),
        name="rg_prompt",
    )(proj3, proj3, cw, cb, wa, wx, ba, bx, lam, wo)
    return ya.reshape(B * L, wo.shape[1]), hfin


def _conv_rolled(u, cw_ref, cb_ref):
    return (cb_ref[...]
            + cw_ref[3:4, :] * u
            + cw_ref[2:3, :] * pltpu.roll(u, 1, axis=0)
            + cw_ref[1:2, :] * pltpu.roll(u, 2, axis=0)
            + cw_ref[0:1, :] * pltpu.roll(u, 3, axis=0))


def _rg_sample_body(x_ref, gate_ref, cpad_ref, hpad_ref, cw_ref, cb_ref, wa_ref, wx_ref, ba_ref, bx_ref,
                    lam_ref, wo_ref, ya_ref, h_ref):
    xc = _conv_rolled(x_ref[...] + cpad_ref[...], cw_ref, cb_ref)
    a, bt = _rg_gates(xc, wa_ref, wx_ref, ba_ref[...], bx_ref[...], lam_ref[...])
    row = lax.broadcasted_iota(jnp.int32, a.shape, 0) & (SAMPLE_ROWS - 1)
    h = hpad_ref[...]
    for k in range(SAMPLE_ROWS // 2, SAMPLE_ROWS):
        h = jnp.where(row == k, a * pltpu.roll(h, 1, axis=0) + bt, h)
    h_ref[...] = h
    ya_ref[...] = _dot((h * _gelu(gate_ref[...])).astype(BF16), wo_ref[...])


def _rg_sample(proj, col_x, col_g, cpad, hpad, cw, cb, wa, wx, ba, bx, lam, wo):
    T = proj.shape[0]
    W = cw.shape[1]
    TM = _pick(T, 256)
    full = lambda *s: pl.BlockSpec(s, lambda i: (0,) * len(s))
    return pl.pallas_call(
        _rg_sample_body,
        grid=(T // TM,),
        in_specs=[
            pl.BlockSpec((TM, W), lambda i: (i, col_x)),
            pl.BlockSpec((TM, W), lambda i: (i, col_g)),
            pl.BlockSpec((TM, W), lambda i: (i, 0)),
            pl.BlockSpec((TM, W), lambda i: (i, 0)),
            full(4, W), full(1, W), full(*wa.shape), full(*wx.shape), full(1, W), full(1, W), full(1, W),
            full(*wo.shape),
        ],
        out_specs=[
            pl.BlockSpec((TM, wo.shape[1]), lambda i: (i, 0)),
            pl.BlockSpec((TM, W), lambda i: (i, 0)),
        ],
        out_shape=[
            jax.ShapeDtypeStruct((T, wo.shape[1]), F32),
            jax.ShapeDtypeStruct((T, W), F32),
        ],
        compiler_params=_cparams(("parallel",)),
        name="rg_sample",
    )(proj, proj, cpad, hpad, cw, cb, wa, wx, ba, bx, lam, wo)


def _ssd_chunk(act_s, dt, dtT, alog_ref, alogT_ref, dexp_ref, seq_shift, G, R, P, N,
               yint_fn, y_s, xw_s):
    Q = act_s.shape[0]
    inner = G * R * P
    a = dt * (-jnp.exp(alog_ref[...]))
    aT = dtT * (-jnp.exp(alogT_ref[...]))
    ri = lax.broadcasted_iota(jnp.int32, (Q, Q), 0)
    ci = lax.broadcasted_iota(jnp.int32, (Q, Q), 1)
    same = (ri >> seq_shift) == (ci >> seq_shift)
    causal = same & (ci <= ri)
    causal_f = jnp.where(causal, 1.0, 0.0).astype(F32)
    same_f = jnp.where(same, 1.0, 0.0).astype(F32)
    acum = jnp.dot(causal_f, a, precision=HIGHEST, preferred_element_type=F32)
    tot = jnp.dot(same_f, a, precision=HIGHEST, preferred_element_type=F32)
    acumT = lax.dot_general(aT, causal_f, NT_DIMS, precision=HIGHEST,
                            preferred_element_type=F32)
    eacum = jnp.exp(acum)
    toend = jnp.exp(tot - acum) * dt
    for g in range(G):
        Bg = act_s[:, inner + g * N:inner + (g + 1) * N].astype(BF16)
        Cg = act_s[:, inner + (G + g) * N:inner + (G + g + 1) * N].astype(BF16)
        cb = _dot_nt(Cg, Bg)
        yint = yint_fn(g, Cg)
        for r in range(R):
            h = g * R + r
            sl = slice(h * P, (h + 1) * P)
            seg = acum[:, h:h + 1] - acumT[h:h + 1, :]
            decay = jnp.exp(jnp.where(causal, seg, NEG_INF))
            wgt = (cb * decay * dtT[h:h + 1, :]).astype(BF16)
            xh = act_s[:, sl]
            yh = _dot(wgt, xh.astype(BF16)) + dexp_ref[:, sl] * xh
            if yint is not None:
                yh = yh + yint[:, r * P:(r + 1) * P] * eacum[:, h:h + 1]
            y_s[:, sl] = yh
            xw_s[:, sl] = xh * toend[:, h:h + 1]
    return tot, eacum


def _expand_heads(x, ex_ref):
    hi = x.astype(BF16)
    r1 = x - hi.astype(F32)
    mid = r1.astype(BF16)
    lo = (r1 - mid.astype(F32)).astype(BF16)
    e = ex_ref[...]
    return _dot(hi, e) + _dot(mid, e) + _dot(lo, e)


def _ssd_finish(y, z, nw_ref, wo_ref):
    y = y * _silu(z)
    return _dot(_rms(y, nw_ref[...]).astype(BF16), wo_ref[...])


def _ssd_prompt_body(G, R, P, N,
                     xbc_ref, z_ref, dt_ref, dtT_ref, cw_ref, cb_ref, dtb_ref, dtbT_ref, alog_ref, alogT_ref,
                     dexp_ref, nw_ref, wo_ref, ex_ref, yb_ref, st_ref, pad_s, act_s, y_s, stT_s):
    Q = xbc_ref.shape[0]
    inner = G * R * P
    GW = R * P

    @pl.when(pl.program_id(1) == 0)
    def _():
        pad_s[:, 0:SUBLANES, :] = jnp.zeros((pad_s.shape[0], SUBLANES, LANES), F32)
        stT_s[...] = jnp.zeros(stT_s.shape, F32)

    for s in range(pad_s.shape[0]):
        ls = slice(s * LANES, (s + 1) * LANES)
        act_s[:, ls] = _silu(_conv_slab(xbc_ref[:, ls], pad_s.at[s], cw_ref, cb_ref, ls))
    dt = _softplus(dt_ref[...] + dtb_ref[...])
    dtT = _softplus(dtT_ref[...] + dtbT_ref[...])
    a = dt * (-jnp.exp(alog_ref[...]))
    aT = dtT * (-jnp.exp(alogT_ref[...]))
    ri = lax.broadcasted_iota(jnp.int32, (Q, Q), 0)
    ci = lax.broadcasted_iota(jnp.int32, (Q, Q), 1)
    causal = ci <= ri
    causal_f = jnp.where(causal, 1.0, 0.0).astype(F32)
    hi = dict(precision=HIGHEST, preferred_element_type=F32)
    acum = jnp.dot(causal_f, a, **hi)
    tot = jnp.dot(jnp.ones((SUBLANES, Q), F32), a, **hi)
    acumT = lax.dot_general(aT, causal_f, NT_DIMS, **hi)
    fx = _expand_heads(jnp.concatenate([jnp.exp(acum), jnp.exp(tot[0:1, :] - acum) * dt, jnp.exp(tot)], axis=0),
                       ex_ref)
    eacum_x, toend_x, dec_x = fx[0:Q], fx[Q:2 * Q], fx[2 * Q:2 * Q + 1]
    col2 = acum * LOG2E
    row2 = (acumT - jnp.log(dtT)) * LOG2E
    lane = lax.broadcasted_iota(jnp.int32, (Q, LANES), 1)
    for g in range(G):
        gs = slice(g * GW, (g + 1) * GW)
        Bf = act_s[:, inner + g * N:inner + (g + 1) * N]
        Bg = Bf.astype(BF16)
        Cg = act_s[:, inner + (G + g) * N:inner + (G + g + 1) * N].astype(BF16)
        cb = jnp.where(causal, _dot_nt(Cg, Bg), 0.0)
        y_s[:, gs] = (_dot(Cg, stT_s[:, gs].astype(BF16)) * eacum_x[:, gs]
                      + dexp_ref[:, gs] * act_s[:, gs])
        for pr in range(R // 2):
            h0 = g * R + 2 * pr
            ps = slice(h0 * P, (h0 + 2) * P)
            wg = []
            for h in (h0, h0 + 1):
                e = jnp.exp2(jnp.where(causal, col2[:, h:h + 1] - row2[h:h + 1, :], NEG_INF))
                wg.append((cb * e).astype(BF16))
            xp = act_s[:, ps]
            xa = jnp.where(lane < P, xp, 0.0).astype(BF16)
            xb = jnp.where(lane < P, 0.0, xp).astype(BF16)
            y_s[:, ps] = y_s[:, ps] + _dot(jnp.concatenate(wg, axis=1), jnp.concatenate([xa, xb], axis=0))
        xw = (act_s[:, gs] * toend_x[:, gs]).astype(BF16)
        ST = _dot(jnp.transpose(Bf).astype(BF16), xw)
        stT_s[:, gs] = dec_x[:, gs] * stT_s[:, gs] + ST
    yb_ref[...] = _ssd_finish(y_s[...], z_ref[...], nw_ref, wo_ref)

    @pl.when(pl.program_id(1) == pl.num_programs(1) - 1)
    def _():
        for g in range(G):
            st_ref[0, g * R:(g + 1) * R] = jnp.transpose(stT_s[:, g * GW:(g + 1) * GW]).reshape(R, P, N)


def _ssd_prompt(proj, dt, dtT, B, L, col_xbc, col_z, dims, cw, cb, dtb, dtbT, alog, alogT, dexp, nw, wo):
    G, R, P, N = dims
    H = G * R
    inner = H * P
    CD = cw.shape[1]
    Q = _pick(L, SSD_CHUNK)
    nC = L // Q
    assert 2 * P == LANES and R % 2 == 0 and N == LANES
    head_of_lane = jnp.arange(inner, dtype=jnp.int32) // P
    ex = (jnp.arange(LANES, dtype=jnp.int32)[:, None] == head_of_lane[None, :]).astype(BF16)
    full = lambda *s: pl.BlockSpec(s, lambda b, c: (0,) * len(s))
    return pl.pallas_call(
        functools.partial(_ssd_prompt_body, G, R, P, N),
        grid=(B, nC),
        in_specs=[
            pl.BlockSpec((Q, CD), lambda b, c: (b * nC + c, col_xbc)),
            pl.BlockSpec((Q, inner), lambda b, c: (b * nC + c, col_z)),
            pl.BlockSpec((Q, LANES), lambda b, c: (b * nC + c, 0)),
            pl.BlockSpec((H, Q), lambda b, c: (0, b * nC + c)),
            full(4, CD), full(1, CD), full(1, LANES), full(H, 1), full(1, LANES), full(H, 1),
            full(1, inner), full(1, inner), full(*wo.shape), full(LANES, inner),
        ],
        out_specs=[
            pl.BlockSpec((Q, wo.shape[1]), lambda b, c: (b * nC + c, 0)),
            pl.BlockSpec((1, H, P, N), lambda b, c: (b, 0, 0, 0)),
        ],
        out_shape=[
            jax.ShapeDtypeStruct((B * L, wo.shape[1]), F32),
            jax.ShapeDtypeStruct((B, H, P, N), F32),
        ],
        scratch_shapes=[
            pltpu.VMEM((CD // LANES, Q + SUBLANES, LANES), F32),
            pltpu.VMEM((Q, CD), F32),
            pltpu.VMEM((Q, inner), F32),
            pltpu.VMEM((N, inner), F32),
        ],
        compiler_params=_cparams(("parallel", "arbitrary")),
        name="ssd_prompt",
    )(proj, proj, dt, dtT, cw, cb, dtb, dtbT, alog, alogT, dexp, nw, wo, ex)


def _ssd_sample_body(G, R, P, N,
                     xbc_ref, cpad_ref, z_ref, dt_ref, dtT_ref, h0_ref, cw_ref, cb_ref, dtb_ref, dtbT_ref,
                     alog_ref, alogT_ref, dexp_ref, nw_ref, wo_ref, yb_ref, st_ref,
                     act_s, y_s, xw_s, yint_s, xwT_s, tot_s, eacum_s):
    Q = xbc_ref.shape[0]
    inner = G * R * P
    bi = pl.program_id(1)
    half = SAMPLE_ROWS // 2

    @pl.when(bi == 0)
    def _():
        act_s[...] = _silu(_conv_rolled(xbc_ref[...] + cpad_ref[...], cw_ref, cb_ref))
        rows = lax.broadcasted_iota(jnp.int32, (Q, LANES), 0) & (SAMPLE_ROWS - 1)
        cols = lax.broadcasted_iota(jnp.int32, dtT_ref.shape, 1) & (SAMPLE_ROWS - 1)
        dt = jnp.where(rows >= half, _softplus(dt_ref[...] + dtb_ref[...]), 0.0)
        dtT = jnp.where(cols >= half, _softplus(dtT_ref[...] + dtbT_ref[...]), 0.0)
        seq_shift = SAMPLE_ROWS.bit_length() - 1
        tot, eacum = _ssd_chunk(act_s, dt, dtT, alog_ref, alogT_ref, dexp_ref, seq_shift, G, R, P, N,
                                lambda g, Cg: None, y_s, xw_s)
        tot_s[...] = tot
        eacum_s[...] = eacum
        for g in range(G):
            xwT_s[g] = jnp.transpose(xw_s[:, g * R * P:(g + 1) * R * P]).astype(BF16)

    rid = lax.broadcasted_iota(jnp.int32, (Q, N), 0)
    for q in range(h0_ref.shape[0]):
        r0 = pl.multiple_of((bi * h0_ref.shape[0] + q) * SAMPLE_ROWS, SAMPLE_ROWS)
        mine = (rid >= r0) & (rid < r0 + SAMPLE_ROWS)
        dec = jnp.exp(tot_s[pl.ds(r0, 1), :])
        for g in range(G):
            Cb = act_s[pl.ds(r0, SAMPLE_ROWS), inner + (G + g) * N:inner + (G + g + 1) * N]
            stg = h0_ref[q, g * R:(g + 1) * R].reshape(R * P, N)
            yint_s[pl.ds(r0, SAMPLE_ROWS), g * R * P:(g + 1) * R * P] = _dot_nt(Cb, stg)
            Bg = jnp.where(mine, act_s[:, inner + g * N:inner + (g + 1) * N], 0.0).astype(BF16)
            S = _dot(xwT_s[g], Bg)
            for r in range(R):
                h = g * R + r
                st_ref[q, h] = dec[:, h:h + 1] * h0_ref[q, h] + S[r * P:(r + 1) * P, :]

    @pl.when(bi == pl.num_programs(1) - 1)
    def _():
        for h in range(G * R):
            sl = slice(h * P, (h + 1) * P)
            y_s[:, sl] = y_s[:, sl] + yint_s[:, sl] * eacum_s[:, h:h + 1]
        yb_ref[...] = _ssd_finish(y_s[...], z_ref[...], nw_ref, wo_ref)


def _ssd_sample(proj, dt, dtT, cpad, h0, col_xbc, col_z, dims, cw, cb, dtb, dtbT, alog, alogT, dexp, nw, wo):
    G, R, P, N = dims
    H = G * R
    inner = H * P
    CD = cw.shape[1]
    T = proj.shape[0]
    Bs = h0.shape[0]
    Q = _pick(T, SSD_CHUNK)
    SB = 4
    nb = Q // SAMPLE_ROWS // SB
    assert Q % (SAMPLE_ROWS * SB) == 0
    full = lambda *s: pl.BlockSpec(s, lambda i, j: (0,) * len(s))
    return pl.pallas_call(
        functools.partial(_ssd_sample_body, G, R, P, N),
        grid=(T // Q, nb),
        in_specs=[
            pl.BlockSpec((Q, CD), lambda i, j: (i, col_xbc)),
            pl.BlockSpec((Q, CD), lambda i, j: (i, 0)),
            pl.BlockSpec((Q, inner), lambda i, j: (i, col_z)),
            pl.BlockSpec((Q, LANES), lambda i, j: (i, 0)),
            pl.BlockSpec((H, Q), lambda i, j: (0, i)),
            pl.BlockSpec((SB, H, P, N), lambda i, j: (i * nb + j, 0, 0, 0)),
            full(4, CD), full(1, CD), full(1, LANES), full(H, 1), full(1, LANES), full(H, 1),
            full(1, inner), full(1, inner), full(*wo.shape),
        ],
        out_specs=[
            pl.BlockSpec((Q, wo.shape[1]), lambda i, j: (i, 0)),
            pl.BlockSpec((SB, H, P, N), lambda i, j: (i * nb + j, 0, 0, 0)),
        ],
        out_shape=[
            jax.ShapeDtypeStruct((T, wo.shape[1]), F32),
            jax.ShapeDtypeStruct((Bs, H, P, N), F32),
        ],
        scratch_shapes=[
            pltpu.VMEM((Q, CD), F32),
            pltpu.VMEM((Q, inner), F32),
            pltpu.VMEM((Q, inner), F32),
            pltpu.VMEM((Q, inner), F32),
            pltpu.VMEM((G, R * P, Q), BF16),
            pltpu.VMEM((Q, LANES), F32),
            pltpu.VMEM((Q, LANES), F32),
        ],
        compiler_params=_cparams(("parallel", "arbitrary")),
        name="ssd_sample",
    )(proj, cpad, proj, dt, dtT, h0, cw, cb, dtb, dtbT, alog, alogT, dexp, nw, wo)


def _merge_body(x_ref, ya_ref, yb_ref, ga_ref, gb_ref, wout_ref, nf_ref, x1_ref, h2t_ref):
    m = _sigmoid(ga_ref[...]) * ya_ref[...] + _sigmoid(gb_ref[...]) * yb_ref[...]
    x1 = x_ref[...] + _dot(m.astype(BF16), wout_ref[...])
    x1_ref[...] = x1
    h2t_ref[...] = jnp.transpose(_rms(x1, nf_ref[...])).astype(BF16)


def _merge(x2d, ya, yb, proj, col_ga, col_gb, wout, nf):
    T, D = x2d.shape
    TM = _pick(T, 512)
    row = lambda c: pl.BlockSpec((TM, D), lambda i: (i, c))
    return pl.pallas_call(
        _merge_body,
        grid=(T // TM,),
        in_specs=[row(0), row(0), row(0), row(col_ga), row(col_gb),
                  pl.BlockSpec(wout.shape, lambda i: (0, 0)), pl.BlockSpec((1, D), lambda i: (0, 0))],
        out_specs=[row(0), pl.BlockSpec((D, TM), lambda i: (0, i))],
        out_shape=[jax.ShapeDtypeStruct((T, D), F32), jax.ShapeDtypeStruct((D, T), BF16)],
        compiler_params=_cparams(("parallel",)),
        name="merge",
    )(x2d, ya, yb, proj, proj, wout, nf)


def _staircase(k):
    return [(ka, k // (ka + 1)) for ka in range(k)]


def _take_top(cur, k_top, want_rank):
    rank = jnp.full(cur.shape, float(k_top), F32) if want_rank else None
    vals = []
    for k in range(k_top):
        m = jnp.max(cur, axis=0, keepdims=True)
        hit = cur == m
        vals.append(m)
        if want_rank:
            rank = jnp.where(hit, float(k), rank)
        cur = jnp.where(hit, NEG_INF, cur)
    taken = jnp.sum(jnp.where(cur == NEG_INF, 1.0, 0.0), axis=0, keepdims=True)
    return vals, rank, taken


def _take_top_ties(x, k_top, val_ref, rank_ref):
    rows = lax.broadcasted_iota(jnp.int32, x.shape, 0)
    kk = lax.broadcasted_iota(jnp.int32, (k_top, x.shape[1]), 0)

    def body(k, carry):
        cur, rank, vals = carry
        m = jnp.max(cur, axis=0, keepdims=True)
        first = jnp.min(jnp.where(cur == m, rows, x.shape[0]), axis=0, keepdims=True)
        hit = rows == first
        return (jnp.where(hit, NEG_INF, cur), jnp.where(hit, k.astype(F32), rank),
                jnp.where(kk == k, m, vals))

    init = (x, jnp.full(x.shape, float(k_top), F32), jnp.zeros((k_top, x.shape[1]), F32))
    _, rank, vals = lax.fori_loop(0, k_top, body, init)
    val_ref[...] = vals
    rank_ref[...] = rank


def _route_body(h2t_ref, wqt_ref, keys_ref, na_ref, ea_ref, rb_ref, eb_ref, sv_s, rk_s, cand_s, cv_s, cr_s):
    NH = keys_ref.shape[0]
    NK, KH = keys_ref.shape[2], keys_ref.shape[3]
    K = PEER_TOPK
    TB = h2t_ref.shape[1]
    tiles = [slice(tl * LANES, (tl + 1) * LANES) for tl in range(TB // LANES)]
    qt = _dot(wqt_ref[...], h2t_ref[...]).astype(BF16)
    cand_s[...] = jnp.full(cand_s.shape, NEG_INF, F32)

    def candidates():
        off = 0
        for ka, nb in _staircase(K):
            cand_s[off:off + nb, :] = sv_s[0, ka:ka + 1, :] + sv_s[1, 0:nb, :]
            off += nb

    def emit(h, sT, in_top, rb, sel, na_of):
        top = sv_s[0, 0:1, :] + sv_s[1, 0:1, :]
        z = jnp.sum(jnp.where(sel, jnp.exp(cand_s[...] - top), 0.0), axis=0, keepdims=True)
        ex = [jnp.where(in_top[s], jnp.exp(sT[s] - sv_s[s, 0:1, :]), 0.0) for s in range(2)]
        na = jnp.zeros(sT[0].shape, F32)
        off = 0
        for ka, nb in _staircase(K):
            cnt = jnp.sum(jnp.where(sel[off:off + nb, :], 1.0, 0.0), axis=0, keepdims=True)
            na = jnp.where(na_of(ka), cnt, na)
            off += nb
        na_ref[h * NK:(h + 1) * NK, :] = na
        ea_ref[h * NK:(h + 1) * NK, :] = ex[0] / z
        rb_ref[h] = rb.astype(BF16)
        eb_ref[h] = ex[1].astype(BF16)

    def scores(h):
        return [_dot(keys_ref[h, s], qt[(h * 2 + s) * KH:(h * 2 + s + 1) * KH, :]) for s in range(2)]

    most = []
    for h in range(NH):
        sT = scores(h)
        seen = jnp.zeros((1, LANES), F32)
        rbs = []
        for s in range(2):
            for ln in tiles:
                vals, rank, taken = _take_top(sT[s][:, ln], K, s == 1)
                for k in range(K):
                    sv_s[s, k:k + 1, ln] = vals[k]
                seen = jnp.maximum(seen, taken)
                if s == 1:
                    rbs.append(rank)
        candidates()
        cand = cand_s[...]
        tau = _take_top(cand, K, False)[0][K - 1]
        sel = cand >= tau
        most.append(jnp.maximum(seen, jnp.max(jnp.sum(jnp.where(sel, 1.0, 0.0), axis=0, keepdims=True),
                                              axis=1, keepdims=True)))
        emit(h, sT, [sT[s] >= sv_s[s, K - 1:K, :] for s in range(2)], jnp.concatenate(rbs, axis=1), sel,
             lambda ka: sT[0] == sv_s[0, ka:ka + 1, :])

    for h in range(NH):
        @pl.when(jnp.max(most[h]) > K)
        def _(h=h):
            sT = scores(h)
            for s in range(2):
                for ln in tiles:
                    _take_top_ties(sT[s][:, ln], K, sv_s.at[s, :, ln], rk_s.at[s, :, ln])
            candidates()
            for ln in tiles:
                _take_top_ties(cand_s[:, ln], K, cv_s.at[:, ln], cr_s.at[:, ln])
            emit(h, sT, [rk_s[s] < K for s in range(2)], rk_s[1], cr_s[...] < K,
                 lambda ka: rk_s[0] == float(ka))


def _route(h2t, wqt, keys):
    D, T = h2t.shape
    NH, _, NK, KH = keys.shape
    TB = _pick(T, 256)
    assert TB % LANES == 0
    ncand = sum(nb for _, nb in _staircase(PEER_TOPK))
    ncand_pad = -(-ncand // SUBLANES) * SUBLANES
    flat = pl.BlockSpec((NH * NK, TB), lambda i: (0, i))
    tok = pl.BlockSpec((NH, NK, TB), lambda i: (0, 0, i))
    return pl.pallas_call(
        _route_body,
        grid=(T // TB,),
        in_specs=[pl.BlockSpec((D, TB), lambda i: (0, i)),
                  pl.BlockSpec(wqt.shape, lambda i: (0, 0)),
                  pl.BlockSpec(keys.shape, lambda i: (0, 0, 0, 0))],
        out_specs=[flat, flat, tok, tok],
        out_shape=[jax.ShapeDtypeStruct((NH * NK, T), F32), jax.ShapeDtypeStruct((NH * NK, T), F32),
                   jax.ShapeDtypeStruct((NH, NK, T), BF16), jax.ShapeDtypeStruct((NH, NK, T), BF16)],
        scratch_shapes=[pltpu.VMEM((2, PEER_TOPK, TB), F32), pltpu.VMEM((2, NK, TB), F32),
                        pltpu.VMEM((ncand_pad, TB), F32), pltpu.VMEM((PEER_TOPK, TB), F32),
                        pltpu.VMEM((ncand_pad, TB), F32)],
        compiler_params=_cparams(("parallel",)),
        name="peer_route",
    )(h2t, wqt, keys)


def _peer_body(h2t_ref, x1_ref, u_ref, vt_ref, na_ref, ea_ref, rb_ref, eb_ref, nfin_ref, y_ref,
               s_s, a_s, acc_s):
    ec = pl.program_id(1)
    NH, NK, TB = rb_ref.shape
    ni = u_ref.shape[0] // NK

    @pl.when(ec == 0)
    def _():
        acc_s[...] = jnp.zeros(acc_s.shape, F32)

    s_s[...] = _dot(u_ref[...], h2t_ref[...])

    def row_tile(grp, il):
        return jnp.broadcast_to(grp[il:il + 1, :], (BF16_ROWS, LANES)).astype(BF16)

    for tl in range(TB // LANES):
        ln = slice(tl * LANES, (tl + 1) * LANES)
        rows = [pl.ds(pl.multiple_of(h * NK + ec * ni, SUBLANES), ni) for h in range(NH)]
        na = [na_ref[rows[h], ln] for h in range(NH)]
        ea = [ea_ref[rows[h], ln] for h in range(NH)]
        for il in range(ni):
            na_t = [row_tile(na[h], il) for h in range(NH)]
            ea_t = [row_tile(ea[h], il) for h in range(NH)]
            for jt in range(NK // BF16_ROWS):
                js = slice(jt * BF16_ROWS, (jt + 1) * BF16_ROWS)
                w = jnp.zeros((BF16_ROWS, LANES), BF16)
                for h in range(NH):
                    w = w + ea_t[h] * jnp.where(rb_ref[h, js, ln] < na_t[h], eb_ref[h, js, ln], 0.0)
                e = slice(il * NK + jt * BF16_ROWS, il * NK + (jt + 1) * BF16_ROWS)
                a_s[e, ln] = _gelu(s_s[e, ln]).astype(BF16) * w
    acc_s[...] += _dot(vt_ref[...], a_s[...])

    @pl.when(ec == pl.num_programs(1) - 1)
    def _():
        x2 = x1_ref[...] + jnp.transpose(acc_s[...])
        y_ref[...] = _rms(x2, nfin_ref[...])


def _peer(h2t, x1, u, vt, na, ea, rb, eb, nfin):
    D, T = h2t.shape
    E = u.shape[0]
    NH, NK, _ = rb.shape
    TB = _pick(T, 512)
    EC = 2 * SUBLANES * NK
    assert TB % LANES == 0 and E % EC == 0 and NK % BF16_ROWS == 0
    tok = pl.BlockSpec((NH, NK, TB), lambda i, j: (0, 0, i))
    flat = pl.BlockSpec((NH * NK, TB), lambda i, j: (0, i))
    return pl.pallas_call(
        _peer_body,
        grid=(T // TB, E // EC),
        in_specs=[pl.BlockSpec((D, TB), lambda i, j: (0, i)),
                  pl.BlockSpec((TB, D), lambda i, j: (i, 0)),
                  pl.BlockSpec((EC, D), lambda i, j: (j, 0)),
                  pl.BlockSpec((D, EC), lambda i, j: (0, j)),
                  flat, flat, tok, tok,
                  pl.BlockSpec((1, D), lambda i, j: (0, 0))],
        out_specs=pl.BlockSpec((TB, D), lambda i, j: (i, 0)),
        out_shape=jax.ShapeDtypeStruct((T, D), F32),
        scratch_shapes=[pltpu.VMEM((EC, TB), F32), pltpu.VMEM((EC, TB), BF16), pltpu.VMEM((D, TB), F32)],
        compiler_params=_cparams(("parallel", "arbitrary")),
        name="peer_mix",
    )(h2t, x1, u, vt, na, ea, rb, eb, nfin)


def _pad_lanes(row):
    return jnp.pad(row, ((0, 0), (0, LANES - row.shape[1])))


def kernel(x_prompt, x_sample, state_rg_h, state_rg_conv, state_ssd_h, state_ssd_conv, norm_mix, w_in, rg_conv_w, rg_conv_b, rg_wa, rg_ba, rg_wx, rg_bx, rg_lam, w_rg_out, ssd_conv_w, ssd_conv_b, ssd_dt_bias, ssd_a_log, ssd_d, ssd_norm, w_ssd_out, w_out, norm_ffn, peer_wq, peer_keys, peer_u, peer_v, norm_final):
    depth = w_in.shape[0]
    B, L, D = x_prompt.shape
    Bs, Ls, _ = x_sample.shape
    assert depth == 1 and Ls == SAMPLE_ROWS // 2 and rg_conv_w.shape[1] == 4 and L >= 3
    W = rg_conv_w.shape[2]
    H = ssd_a_log.shape[1]
    inner = ssd_norm.shape[1]
    CD = ssd_conv_w.shape[2]
    P = inner // H
    N = state_ssd_h.shape[-1]
    G = (CD - inner) // (2 * N)
    R = H // G
    dims = (G, R, P, N)
    assert W == D and inner == 2 * D and CD == 3 * D and H % SUBLANES == 0 and H <= LANES
    o_x, o_g, o_z, o_xbc, o_dt, o_m = 0, W, 2 * W, 2 * W + inner, 2 * W + inner + CD, 2 * W + inner + CD + H
    col_xbc, col_x, col_z, col_g, col_ga, col_gb = 0, CD // W, (CD + W) // inner, (CD + W + inner) // W, \
        (CD + 2 * W + inner) // D, (CD + 2 * W + inner + D) // D

    yp = x_prompt.reshape(B * L, D)
    ys = x_sample.reshape(Bs * Ls, D)
    outs = {k: [] for k in ("prh", "prc", "psh", "psc", "srh", "src", "ssh", "ssc")}
    lead = SAMPLE_ROWS - Ls
    for l in range(depth):
        wi = w_in[l]
        w_main = jnp.concatenate([wi[:, o_xbc:o_dt], wi[:, o_x:o_g], wi[:, o_z:o_xbc], wi[:, o_g:o_z],
                                  wi[:, o_m:]], axis=1).astype(BF16)
        wdt = _pad_lanes(wi[:, o_dt:o_m])
        g_mix = norm_mix[l][None]
        rg_w = (rg_conv_w[l], rg_conv_b[l][None], rg_wa[l].astype(BF16), rg_wx[l].astype(BF16),
                rg_ba[l][None], rg_bx[l][None], rg_lam[l][None], w_rg_out[l].astype(BF16))
        ssd_w = (ssd_conv_w[l], ssd_conv_b[l][None], _pad_lanes(ssd_dt_bias[l][None]), ssd_dt_bias[l][:, None],
                 _pad_lanes(ssd_a_log[l][None]), ssd_a_log[l][:, None], jnp.repeat(ssd_d[l], P)[None],
                 ssd_norm[l][None], w_ssd_out[l].astype(BF16))
        wout = w_out[l].astype(BF16)
        nf = norm_ffn[l][None]
        wqt = peer_wq[l].astype(BF16).T
        keys = peer_keys[l].astype(BF16)
        u = peer_u[l].astype(BF16)
        vt = peer_v[l].astype(BF16).T
        nfin = norm_final[None]

        proj, dt, dtT = _inproj(yp, g_mix, w_main, wdt, H)
        ya, rgh = _rg_prompt(proj, B, L, col_x, col_g, *rg_w)
        yb, ssh = _ssd_prompt(proj, dt, dtT, B, L, col_xbc, col_z, dims, *ssd_w)
        p3 = proj.reshape(B, L, -1)
        outs["prh"].append(rgh.reshape(B, W))
        outs["prc"].append(p3[:, L - 3:, CD:CD + W])
        outs["psh"].append(ssh)
        outs["psc"].append(p3[:, L - 3:, :CD])
        x1, h2t = _merge(yp, ya, yb, proj, col_ga, col_gb, wout, nf)
        yp = _peer(h2t, x1, u, vt, *_route(h2t, wqt, keys), nfin)

        xe = jnp.pad(ys.reshape(Bs, Ls, D), ((0, 0), (lead, 0), (0, 0))).reshape(Bs * SAMPLE_ROWS, D)
        proj_e, dt_e, dtT_e = _inproj(xe, g_mix, w_main, wdt, H)
        rows = lambda a3: a3.reshape(Bs * SAMPLE_ROWS, a3.shape[-1])
        rg_cpad = rows(jnp.pad(state_rg_conv[l], ((0, 0), (lead - 3, Ls), (0, 0))))
        rg_hpad = rows(jnp.pad(state_rg_h[l][:, None, :], ((0, 0), (lead - 1, Ls), (0, 0))))
        ssd_cpad = rows(jnp.pad(state_ssd_conv[l], ((0, 0), (lead - 3, Ls), (0, 0))))
        ya_e, h_e = _rg_sample(proj_e, col_x, col_g, rg_cpad, rg_hpad, *rg_w)
        yb_e, ssh_s = _ssd_sample(proj_e, dt_e, dtT_e, ssd_cpad, state_ssd_h[l], col_xbc, col_z, dims, *ssd_w)
        toks = lambda a2: a2.reshape(Bs, SAMPLE_ROWS, -1)[:, lead:].reshape(Bs * Ls, -1)
        p3 = proj_e.reshape(Bs, SAMPLE_ROWS, -1)
        outs["srh"].append(h_e.reshape(Bs, SAMPLE_ROWS, W)[:, -1])
        outs["src"].append(p3[:, SAMPLE_ROWS - 3:, CD:CD + W])
        outs["ssh"].append(ssh_s)
        outs["ssc"].append(p3[:, SAMPLE_ROWS - 3:, :CD])
        proj_s = toks(proj_e)
        x1, h2t = _merge(ys, toks(ya_e), toks(yb_e), proj_s, col_ga, col_gb, wout, nf)
        ys = _peer(h2t, x1, u, vt, *_route(h2t, wqt, keys), nfin)

    st = lambda k: jnp.stack(outs[k])
    return (yp.reshape(B, L, D), ys.reshape(Bs, Ls, D), st("prh"), st("prc"), st("psh"), st("psc"),
            st("srh"), st("src"), st("ssh"), st("ssc"))
```

```python
import functools

import jax
import jax.numpy as jnp
from jax import lax
from jax.experimental import pallas as pl
from jax.experimental.pallas import tpu as pltpu

F32 = jnp.float32
BF16 = jnp.bfloat16
EPS = 1e-6
RG_C = 8.0
PEER_TOPK = 16
SSD_CHUNK = 128
LANES = 128
SUBLANES = 8
BF16_ROWS = 16
SAMPLE_ROWS = 8
NEG_INF = float("-inf")
LOG2E = 1.4426950408889634
HIGHEST = lax.Precision.HIGHEST
NT_DIMS = (((1,), (1,)), ((), ()))
VMEM_LIMIT = 56 * 1024 * 1024


def _cparams(sem):
    return pltpu.CompilerParams(dimension_semantics=sem, vmem_limit_bytes=VMEM_LIMIT)


def _pick(n, pref):
    t = min(n, pref)
    while n % t:
        t -= SUBLANES
    return t


def _sigmoid(x):
    return 1.0 / (1.0 + jnp.exp2(x * -LOG2E))


def _silu(x):
    return x * _sigmoid(x)


def _softplus(x):
    return jnp.maximum(x, 0.0) + jnp.log1p(jnp.exp(-jnp.abs(x)))


def _gelu(x):
    k1 = -2.0 * 0.7978845608028654 * LOG2E
    return x / (1.0 + jnp.exp2(x * (k1 + (k1 * 0.044715) * (x * x))))


def _rms(x, g):
    return x * lax.rsqrt(jnp.mean(x * x, axis=-1, keepdims=True) + EPS) * g


def _dot(a, b):
    return jnp.dot(a, b, preferred_element_type=F32)


def _dot_nt(a, b):
    return lax.dot_general(a, b, NT_DIMS, preferred_element_type=F32)


def _inproj_body(x_ref, g_ref, w_ref, wdt_ref, o_ref, odt_ref, odtT_ref, xn_ref):
    @pl.when(pl.program_id(1) == 0)
    def _():
        xn = _rms(x_ref[...], g_ref[...])
        xn_ref[...] = xn.astype(BF16)
        dt = jnp.dot(xn, wdt_ref[...], precision=HIGHEST, preferred_element_type=F32)
        odt_ref[...] = dt
        odtT_ref[...] = jnp.transpose(dt)[:odtT_ref.shape[0], :]

    o_ref[...] = _dot(xn_ref[...], w_ref[...])


def _inproj(x2d, g, w_main, wdt, H):
    T, D = x2d.shape
    N = w_main.shape[1]
    TM = _pick(T, 1024)
    TN = 3072
    assert N % TN == 0 and TM % LANES == 0
    return pl.pallas_call(
        _inproj_body,
        grid=(T // TM, N // TN),
        in_specs=[
            pl.BlockSpec((TM, D), lambda i, j: (i, 0)),
            pl.BlockSpec((1, D), lambda i, j: (0, 0)),
            pl.BlockSpec((D, TN), lambda i, j: (0, j)),
            pl.BlockSpec((D, LANES), lambda i, j: (0, 0)),
        ],
        out_specs=[
            pl.BlockSpec((TM, TN), lambda i, j: (i, j)),
            pl.BlockSpec((TM, LANES), lambda i, j: (i, 0)),
            pl.BlockSpec((H, TM), lambda i, j: (0, i)),
        ],
        out_shape=[
            jax.ShapeDtypeStruct((T, N), F32),
            jax.ShapeDtypeStruct((T, LANES), F32),
            jax.ShapeDtypeStruct((H, T), F32),
        ],
        scratch_shapes=[pltpu.VMEM((TM, D), BF16)],
        compiler_params=_cparams(("parallel", "arbitrary")),
        name="inproj",
    )(x2d, g, w_main, wdt)


def _conv_slab(x, pad, cw_ref, cb_ref, ls):
    L = x.shape[0]
    pad[SUBLANES:SUBLANES + L, :] = x
    out = (cb_ref[:, ls]
           + cw_ref[3:4, ls] * pad[8:8 + L, :]
           + cw_ref[2:3, ls] * pad[7:7 + L, :]
           + cw_ref[1:2, ls] * pad[6:6 + L, :]
           + cw_ref[0:1, ls] * pad[5:5 + L, :])
    pad[0:SUBLANES, :] = pad[L:L + SUBLANES, :]
    return out


def _rg_gates(xc, wa_ref, wx_ref, ba, bx, lam):
    xb = xc.astype(BF16)
    nb, bw = wa_ref.shape[0], wa_ref.shape[1]
    rs, gs = [], []
    for k in range(nb):
        xk = xb[:, k * bw:(k + 1) * bw]
        rs.append(_dot(xk, wa_ref[k]))
        gs.append(_dot(xk, wx_ref[k]))
    r = _sigmoid(jnp.concatenate(rs, axis=1) + ba)
    i = _sigmoid(jnp.concatenate(gs, axis=1) + bx)
    log_a = -RG_C * r * _softplus(-lam)
    a = jnp.exp(log_a)
    em1 = jnp.tanh(log_a) * (a * a + 1.0)
    return a, jnp.sqrt(-em1) * (i * xc)


def _rg_prompt_body(x_ref, gate_ref, cw_ref, cb_ref, wa_ref, wx_ref, ba_ref, bx_ref, lam_ref, wo_ref,
                    ya_ref, hfin_ref, pad_s, a_s, b_s, h_s):
    B, Lc, W = x_ref.shape
    NS = W // LANES
    pitch = a_s.shape[1] // B

    @pl.when(pl.program_id(0) == 0)
    def _():
        pad_s[:, :, 0:SUBLANES, :] = jnp.zeros((B, NS, SUBLANES, LANES), F32)
        h_s[...] = jnp.zeros(h_s.shape, F32)

    for b in range(B):
        xc = jnp.concatenate(
            [_conv_slab(x_ref[b, :, s * LANES:(s + 1) * LANES], pad_s.at[b, s], cw_ref, cb_ref,
                        slice(s * LANES, (s + 1) * LANES)) for s in range(NS)], axis=1)
        a, bt = _rg_gates(xc, wa_ref, wx_ref, ba_ref[...], bx_ref[...], lam_ref[...])
        for s in range(NS):
            a_s[s, b * pitch:b * pitch + Lc, :] = a[:, s * LANES:(s + 1) * LANES]
            b_s[s, b * pitch:b * pitch + Lc, :] = bt[:, s * LANES:(s + 1) * LANES]

    def step(t, hs):
        out = []
        for s in range(NS):
            rows = pl.ds(t, B, stride=pitch)
            h = a_s[s, rows, :] * hs[s] + b_s[s, rows, :]
            b_s[s, rows, :] = h
            out.append(h)
        return tuple(out)

    hs = lax.fori_loop(0, Lc, step, tuple(h_s[:, s * LANES:(s + 1) * LANES] for s in range(NS)), unroll=4)
    for s in range(NS):
        h_s[:, s * LANES:(s + 1) * LANES] = hs[s]
    hfin_ref[...] = h_s[...]
    for b in range(B):
        hb = jnp.concatenate([b_s[s, b * pitch:b * pitch + Lc, :] for s in range(NS)], axis=1)
        ya_ref[b] = _dot((hb * _gelu(gate_ref[b])).astype(BF16), wo_ref[...])


def _rg_prompt(proj, B, L, col_x, col_g, cw, cb, wa, wx, ba, bx, lam, wo):
    W = cw.shape[1]
    Lc = _pick(L, 128)
    pitch = Lc + SUBLANES // 2
    assert B == SUBLANES and W % LANES == 0
    proj3 = proj.reshape(B, L, proj.shape[1])
    full = lambda *s: pl.BlockSpec(s, lambda c: (0,) * len(s))
    ya, hfin = pl.pallas_call(
        _rg_prompt_body,
        grid=(L // Lc,),
        in_specs=[
            pl.BlockSpec((B, Lc, W), lambda c: (0, c, col_x)),
            pl.BlockSpec((B, Lc, W), lambda c: (0, c, col_g)),
            full(4, W), full(1, W), full(*wa.shape), full(*wx.shape), full(1, W), full(1, W), full(1, W),
            full(*wo.shape),
        ],
        out_specs=[
            pl.BlockSpec((B, Lc, wo.shape[1]), lambda c: (0, c, 0)),
            pl.BlockSpec((B, W), lambda c: (0, 0)),
        ],
        out_shape=[
            jax.ShapeDtypeStruct((B, L, wo.shape[1]), F32),
            jax.ShapeDtypeStruct((B, W), F32),
        ],
        scratch_shapes=[
            pltpu.VMEM((B, W // LANES, Lc + SUBLANES, LANES), F32),
            pltpu.VMEM((W // LANES, B * pitch, LANES), F32),
            pltpu.VMEM((W // LANES, B * pitch, LANES), F32),
            pltpu.VMEM((B, W), F32),
        ],
        compiler_params=_cparams(("arbitrary",)),
        name="rg_prompt",
    )(proj3, proj3, cw, cb, wa, wx, ba, bx, lam, wo)
    return ya.reshape(B * L, wo.shape[1]), hfin


def _conv_rolled(u, cw_ref, cb_ref):
    return (cb_ref[...]
            + cw_ref[3:4, :] * u
            + cw_ref[2:3, :] * pltpu.roll(u, 1, axis=0)
            + cw_ref[1:2, :] * pltpu.roll(u, 2, axis=0)
            + cw_ref[0:1, :] * pltpu.roll(u, 3, axis=0))


def _rg_sample_body(x_ref, gate_ref, cpad_ref, hpad_ref, cw_ref, cb_ref, wa_ref, wx_ref, ba_ref, bx_ref,
                    lam_ref, wo_ref, ya_ref, h_ref):
    xc = _conv_rolled(x_ref[...] + cpad_ref[...], cw_ref, cb_ref)
    a, bt = _rg_gates(xc, wa_ref, wx_ref, ba_ref[...], bx_ref[...], lam_ref[...])
    row = lax.broadcasted_iota(jnp.int32, a.shape, 0) & (SAMPLE_ROWS - 1)
    h = hpad_ref[...]
    for k in range(SAMPLE_ROWS // 2, SAMPLE_ROWS):
        h = jnp.where(row == k, a * pltpu.roll(h, 1, axis=0) + bt, h)
    h_ref[...] = h
    ya_ref[...] = _dot((h * _gelu(gate_ref[...])).astype(BF16), wo_ref[...])


def _rg_sample(proj, col_x, col_g, cpad, hpad, cw, cb, wa, wx, ba, bx, lam, wo):
    T = proj.shape[0]
    W = cw.shape[1]
    TM = _pick(T, 256)
    full = lambda *s: pl.BlockSpec(s, lambda i: (0,) * len(s))
    return pl.pallas_call(
        _rg_sample_body,
        grid=(T // TM,),
        in_specs=[
            pl.BlockSpec((TM, W), lambda i: (i, col_x)),
            pl.BlockSpec((TM, W), lambda i: (i, col_g)),
            pl.BlockSpec((TM, W), lambda i: (i, 0)),
            pl.BlockSpec((TM, W), lambda i: (i, 0)),
            full(4, W), full(1, W), full(*wa.shape), full(*wx.shape), full(1, W), full(1, W), full(1, W),
            full(*wo.shape),
        ],
        out_specs=[
            pl.BlockSpec((TM, wo.shape[1]), lambda i: (i, 0)),
            pl.BlockSpec((TM, W), lambda i: (i, 0)),
        ],
        out_shape=[
            jax.ShapeDtypeStruct((T, wo.shape[1]), F32),
            jax.ShapeDtypeStruct((T, W), F32),
        ],
        compiler_params=_cparams(("parallel",)),
        name="rg_sample",
    )(proj, proj, cpad, hpad, cw, cb, wa, wx, ba, bx, lam, wo)


def _ssd_chunk(act_s, dt, dtT, alog_ref, alogT_ref, dexp_ref, seq_shift, G, R, P, N,
               yint_fn, y_s, xw_s):
    Q = act_s.shape[0]
    inner = G * R * P
    a = dt * (-jnp.exp(alog_ref[...]))
    aT = dtT * (-jnp.exp(alogT_ref[...]))
    ri = lax.broadcasted_iota(jnp.int32, (Q, Q), 0)
    ci = lax.broadcasted_iota(jnp.int32, (Q, Q), 1)
    same = (ri >> seq_shift) == (ci >> seq_shift)
    causal = same & (ci <= ri)
    causal_f = jnp.where(causal, 1.0, 0.0).astype(F32)
    same_f = jnp.where(same, 1.0, 0.0).astype(F32)
    acum = jnp.dot(causal_f, a, precision=HIGHEST, preferred_element_type=F32)
    tot = jnp.dot(same_f, a, precision=HIGHEST, preferred_element_type=F32)
    acumT = lax.dot_general(aT, causal_f, NT_DIMS, precision=HIGHEST,
                            preferred_element_type=F32)
    eacum = jnp.exp(acum)
    toend = jnp.exp(tot - acum) * dt
    for g in range(G):
        Bg = act_s[:, inner + g * N:inner + (g + 1) * N].astype(BF16)
        Cg = act_s[:, inner + (G + g) * N:inner + (G + g + 1) * N].astype(BF16)
        cb = _dot_nt(Cg, Bg)
        yint = yint_fn(g, Cg)
        for r in range(R):
            h = g * R + r
            sl = slice(h * P, (h + 1) * P)
            seg = acum[:, h:h + 1] - acumT[h:h + 1, :]
            decay = jnp.exp(jnp.where(causal, seg, NEG_INF))
            wgt = (cb * decay * dtT[h:h + 1, :]).astype(BF16)
            xh = act_s[:, sl]
            yh = _dot(wgt, xh.astype(BF16)) + dexp_ref[:, sl] * xh
            if yint is not None:
                yh = yh + yint[:, r * P:(r + 1) * P] * eacum[:, h:h + 1]
            y_s[:, sl] = yh
            xw_s[:, sl] = xh * toend[:, h:h + 1]
    return tot, eacum


def _expand_heads(x, ex_ref):
    hi = x.astype(BF16)
    r1 = x - hi.astype(F32)
    mid = r1.astype(BF16)
    lo = (r1 - mid.astype(F32)).astype(BF16)
    e = ex_ref[...]
    return _dot(hi, e) + _dot(mid, e) + _dot(lo, e)


def _ssd_finish(y, z, nw_ref, wo_ref):
    y = y * _silu(z)
    return _dot(_rms(y, nw_ref[...]).astype(BF16), wo_ref[...])


def _ssd_prompt_body(G, R, P, N,
                     xbc_ref, z_ref, dt_ref, dtT_ref, cw_ref, cb_ref, dtb_ref, dtbT_ref, alog_ref, alogT_ref,
                     dexp_ref, nw_ref, wo_ref, ex_ref, yb_ref, st_ref, pad_s, act_s, y_s, stT_s):
    Q = xbc_ref.shape[0]
    inner = G * R * P
    GW = R * P

    @pl.when(pl.program_id(1) == 0)
    def _():
        pad_s[:, 0:SUBLANES, :] = jnp.zeros((pad_s.shape[0], SUBLANES, LANES), F32)
        stT_s[...] = jnp.zeros(stT_s.shape, F32)

    for s in range(pad_s.shape[0]):
        ls = slice(s * LANES, (s + 1) * LANES)
        act_s[:, ls] = _silu(_conv_slab(xbc_ref[:, ls], pad_s.at[s], cw_ref, cb_ref, ls))
    dt = _softplus(dt_ref[...] + dtb_ref[...])
    dtT = _softplus(dtT_ref[...] + dtbT_ref[...])
    a = dt * (-jnp.exp(alog_ref[...]))
    aT = dtT * (-jnp.exp(alogT_ref[...]))
    ri = lax.broadcasted_iota(jnp.int32, (Q, Q), 0)
    ci = lax.broadcasted_iota(jnp.int32, (Q, Q), 1)
    causal = ci <= ri
    causal_f = jnp.where(causal, 1.0, 0.0).astype(F32)
    hi = dict(precision=HIGHEST, preferred_element_type=F32)
    acum = jnp.dot(causal_f, a, **hi)
    tot = jnp.dot(jnp.ones((SUBLANES, Q), F32), a, **hi)
    acumT = lax.dot_general(aT, causal_f, NT_DIMS, **hi)
    fx = _expand_heads(jnp.concatenate([jnp.exp(acum), jnp.exp(tot[0:1, :] - acum) * dt, jnp.exp(tot)], axis=0),
                       ex_ref)
    eacum_x, toend_x, dec_x = fx[0:Q], fx[Q:2 * Q], fx[2 * Q:2 * Q + 1]
    col2 = acum * LOG2E
    row2 = (acumT - jnp.log(dtT)) * LOG2E
    lane = lax.broadcasted_iota(jnp.int32, (Q, LANES), 1)
    for g in range(G):
        gs = slice(g * GW, (g + 1) * GW)
        Bf = act_s[:, inner + g * N:inner + (g + 1) * N]
        Bg = Bf.astype(BF16)
        Cg = act_s[:, inner + (G + g) * N:inner + (G + g + 1) * N].astype(BF16)
        cb = jnp.where(causal, _dot_nt(Cg, Bg), 0.0)
        y_s[:, gs] = (_dot(Cg, stT_s[:, gs].astype(BF16)) * eacum_x[:, gs]
                      + dexp_ref[:, gs] * act_s[:, gs])
        for pr in range(R // 2):
            h0 = g * R + 2 * pr
            ps = slice(h0 * P, (h0 + 2) * P)
            wg = []
            for h in (h0, h0 + 1):
                e = jnp.exp2(jnp.where(causal, col2[:, h:h + 1] - row2[h:h + 1, :], NEG_INF))
                wg.append((cb * e).astype(BF16))
            xp = act_s[:, ps]
            xa = jnp.where(lane < P, xp, 0.0).astype(BF16)
            xb = jnp.where(lane < P, 0.0, xp).astype(BF16)
            y_s[:, ps] = y_s[:, ps] + _dot(jnp.concatenate(wg, axis=1), jnp.concatenate([xa, xb], axis=0))
        xw = (act_s[:, gs] * toend_x[:, gs]).astype(BF16)
        ST = _dot(jnp.transpose(Bf).astype(BF16), xw)
        stT_s[:, gs] = dec_x[:, gs] * stT_s[:, gs] + ST
    yb_ref[...] = _ssd_finish(y_s[...], z_ref[...], nw_ref, wo_ref)

    @pl.when(pl.program_id(1) == pl.num_programs(1) - 1)
    def _():
        for g in range(G):
            st_ref[0, g * R:(g + 1) * R] = jnp.transpose(stT_s[:, g * GW:(g + 1) * GW]).reshape(R, P, N)


def _ssd_prompt(proj, dt, dtT, B, L, col_xbc, col_z, dims, cw, cb, dtb, dtbT, alog, alogT, dexp, nw, wo):
    G, R, P, N = dims
    H = G * R
    inner = H * P
    CD = cw.shape[1]
    Q = _pick(L, SSD_CHUNK)
    nC = L // Q
    assert 2 * P == LANES and R % 2 == 0 and N == LANES
    head_of_lane = jnp.arange(inner, dtype=jnp.int32) // P
    ex = (jnp.arange(LANES, dtype=jnp.int32)[:, None] == head_of_lane[None, :]).astype(BF16)
    full = lambda *s: pl.BlockSpec(s, lambda b, c: (0,) * len(s))
    return pl.pallas_call(
        functools.partial(_ssd_prompt_body, G, R, P, N),
        grid=(B, nC),
        in_specs=[
            pl.BlockSpec((Q, CD), lambda b, c: (b * nC + c, col_xbc)),
            pl.BlockSpec((Q, inner), lambda b, c: (b * nC + c, col_z)),
            pl.BlockSpec((Q, LANES), lambda b, c: (b * nC + c, 0)),
            pl.BlockSpec((H, Q), lambda b, c: (0, b * nC + c)),
            full(4, CD), full(1, CD), full(1, LANES), full(H, 1), full(1, LANES), full(H, 1),
            full(1, inner), full(1, inner), full(*wo.shape), full(LANES, inner),
        ],
        out_specs=[
            pl.BlockSpec((Q, wo.shape[1]), lambda b, c: (b * nC + c, 0)),
            pl.BlockSpec((1, H, P, N), lambda b, c: (b, 0, 0, 0)),
        ],
        out_shape=[
            jax.ShapeDtypeStruct((B * L, wo.shape[1]), F32),
            jax.ShapeDtypeStruct((B, H, P, N), F32),
        ],
        scratch_shapes=[
            pltpu.VMEM((CD // LANES, Q + SUBLANES, LANES), F32),
            pltpu.VMEM((Q, CD), F32),
            pltpu.VMEM((Q, inner), F32),
            pltpu.VMEM((N, inner), F32),
        ],
        compiler_params=_cparams(("parallel", "arbitrary")),
        name="ssd_prompt",
    )(proj, proj, dt, dtT, cw, cb, dtb, dtbT, alog, alogT, dexp, nw, wo, ex)


def _ssd_sample_body(G, R, P, N,
                     xbc_ref, cpad_ref, z_ref, dt_ref, dtT_ref, h0_ref, cw_ref, cb_ref, dtb_ref, dtbT_ref,
                     alog_ref, alogT_ref, dexp_ref, nw_ref, wo_ref, yb_ref, st_ref,
                     act_s, y_s, xw_s, yint_s, xwT_s, tot_s, eacum_s):
    Q = xbc_ref.shape[0]
    inner = G * R * P
    bi = pl.program_id(1)
    half = SAMPLE_ROWS // 2

    @pl.when(bi == 0)
    def _():
        act_s[...] = _silu(_conv_rolled(xbc_ref[...] + cpad_ref[...], cw_ref, cb_ref))
        rows = lax.broadcasted_iota(jnp.int32, (Q, LANES), 0) & (SAMPLE_ROWS - 1)
        cols = lax.broadcasted_iota(jnp.int32, dtT_ref.shape, 1) & (SAMPLE_ROWS - 1)
        dt = jnp.where(rows >= half, _softplus(dt_ref[...] + dtb_ref[...]), 0.0)
        dtT = jnp.where(cols >= half, _softplus(dtT_ref[...] + dtbT_ref[...]), 0.0)
        seq_shift = SAMPLE_ROWS.bit_length() - 1
        tot, eacum = _ssd_chunk(act_s, dt, dtT, alog_ref, alogT_ref, dexp_ref, seq_shift, G, R, P, N,
                                lambda g, Cg: None, y_s, xw_s)
        tot_s[...] = tot
        eacum_s[...] = eacum
        for g in range(G):
            xwT_s[g] = jnp.transpose(xw_s[:, g * R * P:(g + 1) * R * P]).astype(BF16)

    rid = lax.broadcasted_iota(jnp.int32, (Q, N), 0)
    for q in range(h0_ref.shape[0]):
        r0 = pl.multiple_of((bi * h0_ref.shape[0] + q) * SAMPLE_ROWS, SAMPLE_ROWS)
        mine = (rid >= r0) & (rid < r0 + SAMPLE_ROWS)
        dec = jnp.exp(tot_s[pl.ds(r0, 1), :])
        for g in range(G):
            Cb = act_s[pl.ds(r0, SAMPLE_ROWS), inner + (G + g) * N:inner + (G + g + 1) * N]
            stg = h0_ref[q, g * R:(g + 1) * R].reshape(R * P, N)
            yint_s[pl.ds(r0, SAMPLE_ROWS), g * R * P:(g + 1) * R * P] = _dot_nt(Cb, stg)
            Bg = jnp.where(mine, act_s[:, inner + g * N:inner + (g + 1) * N], 0.0).astype(BF16)
            S = _dot(xwT_s[g], Bg)
            for r in range(R):
                h = g * R + r
                st_ref[q, h] = dec[:, h:h + 1] * h0_ref[q, h] + S[r * P:(r + 1) * P, :]

    @pl.when(bi == pl.num_programs(1) - 1)
    def _():
        for h in range(G * R):
            sl = slice(h * P, (h + 1) * P)
            y_s[:, sl] = y_s[:, sl] + yint_s[:, sl] * eacum_s[:, h:h + 1]
        yb_ref[...] = _ssd_finish(y_s[...], z_ref[...], nw_ref, wo_ref)


def _ssd_sample(proj, dt, dtT, cpad, h0, col_xbc, col_z, dims, cw, cb, dtb, dtbT, alog, alogT, dexp, nw, wo):
    G, R, P, N = dims
    H = G * R
    inner = H * P
    CD = cw.shape[1]
    T = proj.shape[0]
    Bs = h0.shape[0]
    Q = _pick(T, SSD_CHUNK)
    SB = 4
    nb = Q // SAMPLE_ROWS // SB
    assert Q % (SAMPLE_ROWS * SB) == 0
    full = lambda *s: pl.BlockSpec(s, lambda i, j: (0,) * len(s))
    return pl.pallas_call(
        functools.partial(_ssd_sample_body, G, R, P, N),
        grid=(T // Q, nb),
        in_specs=[
            pl.BlockSpec((Q, CD), lambda i, j: (i, col_xbc)),
            pl.BlockSpec((Q, CD), lambda i, j: (i, 0)),
            pl.BlockSpec((Q, inner), lambda i, j: (i, col_z)),
            pl.BlockSpec((Q, LANES), lambda i, j: (i, 0)),
            pl.BlockSpec((H, Q), lambda i, j: (0, i)),
            pl.BlockSpec((SB, H, P, N), lambda i, j: (i * nb + j, 0, 0, 0)),
            full(4, CD), full(1, CD), full(1, LANES), full(H, 1), full(1, LANES), full(H, 1),
            full(1, inner), full(1, inner), full(*wo.shape),
        ],
        out_specs=[
            pl.BlockSpec((Q, wo.shape[1]), lambda i, j: (i, 0)),
            pl.BlockSpec((SB, H, P, N), lambda i, j: (i * nb + j, 0, 0, 0)),
        ],
        out_shape=[
            jax.ShapeDtypeStruct((T, wo.shape[1]), F32),
            jax.ShapeDtypeStruct((Bs, H, P, N), F32),
        ],
        scratch_shapes=[
            pltpu.VMEM((Q, CD), F32),
            pltpu.VMEM((Q, inner), F32),
            pltpu.VMEM((Q, inner), F32),
            pltpu.VMEM((Q, inner), F32),
            pltpu.VMEM((G, R * P, Q), BF16),
            pltpu.VMEM((Q, LANES), F32),
            pltpu.VMEM((Q, LANES), F32),
        ],
        compiler_params=_cparams(("parallel", "arbitrary")),
        name="ssd_sample",
    )(proj, cpad, proj, dt, dtT, h0, cw, cb, dtb, dtbT, alog, alogT, dexp, nw, wo)


def _merge_body(x_ref, ya_ref, yb_ref, ga_ref, gb_ref, wout_ref, nf_ref, x1_ref, h2t_ref):
    m = _sigmoid(ga_ref[...]) * ya_ref[...] + _sigmoid(gb_ref[...]) * yb_ref[...]
    x1 = x_ref[...] + _dot(m.astype(BF16), wout_ref[...])
    x1_ref[...] = x1
    h2t_ref[...] = jnp.transpose(_rms(x1, nf_ref[...])).astype(BF16)


def _merge(x2d, ya, yb, proj, col_ga, col_gb, wout, nf):
    T, D = x2d.shape
    TM = _pick(T, 512)
    row = lambda c: pl.BlockSpec((TM, D), lambda i: (i, c))
    return pl.pallas_call(
        _merge_body,
        grid=(T // TM,),
        in_specs=[row(0), row(0), row(0), row(col_ga), row(col_gb),
                  pl.BlockSpec(wout.shape, lambda i: (0, 0)), pl.BlockSpec((1, D), lambda i: (0, 0))],
        out_specs=[row(0), pl.BlockSpec((D, TM), lambda i: (0, i))],
        out_shape=[jax.ShapeDtypeStruct((T, D), F32), jax.ShapeDtypeStruct((D, T), BF16)],
        compiler_params=_cparams(("parallel",)),
        name="merge",
    )(x2d, ya, yb, proj, proj, wout, nf)


def _staircase(k):
    return [(ka, k // (ka + 1)) for ka in range(k)]


def _take_top(cur, k_top, want_rank):
    rank = jnp.full(cur.shape, float(k_top), F32) if want_rank else None
    vals = []
    for k in range(k_top):
        m = jnp.max(cur, axis=0, keepdims=True)
        hit = cur == m
        vals.append(m)
        if want_rank:
            rank = jnp.where(hit, float(k), rank)
        cur = jnp.where(hit, NEG_INF, cur)
    taken = jnp.sum(jnp.where(cur == NEG_INF, 1.0, 0.0), axis=0, keepdims=True)
    return vals, rank, taken


def _take_top_ties(x, k_top, val_ref, rank_ref):
    rows = lax.broadcasted_iota(jnp.int32, x.shape, 0)
    kk = lax.broadcasted_iota(jnp.int32, (k_top, x.shape[1]), 0)

    def body(k, carry):
        cur, rank, vals = carry
        m = jnp.max(cur, axis=0, keepdims=True)
        first = jnp.min(jnp.where(cur == m, rows, x.shape[0]), axis=0, keepdims=True)
        hit = rows == first
        return (jnp.where(hit, NEG_INF, cur), jnp.where(hit, lax.convert_element_type(k, F32), rank),
                jnp.where(kk == k, m, vals))

    init = (x, jnp.full(x.shape, float(k_top), F32), jnp.zeros((k_top, x.shape[1]), F32))
    _, rank, vals = lax.fori_loop(0, k_top, body, init)
    val_ref[...] = vals
    rank_ref[...] = rank


def _route_body(h2t_ref, wqt_ref, keys_ref, na_ref, ea_ref, rb_ref, eb_ref, sv_s, rk_s, cand_s, cv_s, cr_s):
    NH = keys_ref.shape[0]
    NK, KH = keys_ref.shape[2], keys_ref.shape[3]
    K = PEER_TOPK
    TB = h2t_ref.shape[1]
    tiles = [slice(tl * LANES, (tl + 1) * LANES) for tl in range(TB // LANES)]
    qt = _dot(wqt_ref[...], h2t_ref[...]).astype(BF16)
    cand_s[...] = jnp.full(cand_s.shape, NEG_INF, F32)

    def candidates():
        off = 0
        for ka, nb in _staircase(K):
            cand_s[off:off + nb, :] = sv_s[0, ka:ka + 1, :] + sv_s[1, 0:nb, :]
            off += nb

    def emit(h, sT, in_top, rb, sel, na_of):
        top = sv_s[0, 0:1, :] + sv_s[1, 0:1, :]
        z = jnp.sum(jnp.where(sel, jnp.exp(cand_s[...] - top), 0.0), axis=0, keepdims=True)
        ex = [jnp.where(in_top[s], jnp.exp(sT[s] - sv_s[s, 0:1, :]), 0.0) for s in range(2)]
        na = jnp.zeros(sT[0].shape, F32)
        off = 0
        for ka, nb in _staircase(K):
            cnt = jnp.sum(jnp.where(sel[off:off + nb, :], 1.0, 0.0), axis=0, keepdims=True)
            na = jnp.where(na_of(ka), cnt, na)
            off += nb
        na_ref[h * NK:(h + 1) * NK, :] = na
        ea_ref[h * NK:(h + 1) * NK, :] = ex[0] / z
        rb_ref[h] = rb.astype(BF16)
        eb_ref[h] = ex[1].astype(BF16)

    def scores(h):
        return [_dot(keys_ref[h, s], qt[(h * 2 + s) * KH:(h * 2 + s + 1) * KH, :]) for s in range(2)]

    most = []
    for h in range(NH):
        sT = scores(h)
        seen = jnp.zeros((1, LANES), F32)
        rbs = []
        for s in range(2):
            for ln in tiles:
                vals, rank, taken = _take_top(sT[s][:, ln], K, s == 1)
                for k in range(K):
                    sv_s[s, k:k + 1, ln] = vals[k]
                seen = jnp.maximum(seen, taken)
                if s == 1:
                    rbs.append(rank)
        candidates()
        cand = cand_s[...]
        tau = _take_top(cand, K, False)[0][K - 1]
        sel = cand >= tau
        most.append(jnp.maximum(seen, jnp.max(jnp.sum(jnp.where(sel, 1.0, 0.0), axis=0, keepdims=True),
                                              axis=1, keepdims=True)))
        emit(h, sT, [sT[s] >= sv_s[s, K - 1:K, :] for s in range(2)], jnp.concatenate(rbs, axis=1), sel,
             lambda ka: sT[0] == sv_s[0, ka:ka + 1, :])

    for h in range(NH):
        @pl.when(jnp.max(most[h]) > K)
        def _(h=h):
            sT = scores(h)
            for s in range(2):
                for ln in tiles:
                    _take_top_ties(sT[s][:, ln], K, sv_s.at[s, :, ln], rk_s.at[s, :, ln])
            candidates()
            for ln in tiles:
                _take_top_ties(cand_s[:, ln], K, cv_s.at[:, ln], cr_s.at[:, ln])
            emit(h, sT, [rk_s[s] < K for s in range(2)], rk_s[1], cr_s[...] < K,
                 lambda ka: rk_s[0] == float(ka))


def _route(h2t, wqt, keys):
    D, T = h2t.shape
    NH, _, NK, KH = keys.shape
    TB = _pick(T, 256)
    assert TB % LANES == 0
    ncand = sum(nb for _, nb in _staircase(PEER_TOPK))
    ncand_pad = -(-ncand // SUBLANES) * SUBLANES
    flat = pl.BlockSpec((NH * NK, TB), lambda i: (0, i))
    tok = pl.BlockSpec((NH, NK, TB), lambda i: (0, 0, i))
    return pl.pallas_call(
        _route_body,
        grid=(T // TB,),
        in_specs=[pl.BlockSpec((D, TB), lambda i: (0, i)),
                  pl.BlockSpec(wqt.shape, lambda i: (0, 0)),
                  pl.BlockSpec(keys.shape, lambda i: (0, 0, 0, 0))],
        out_specs=[flat, flat, tok, tok],
        out_shape=[jax.ShapeDtypeStruct((NH * NK, T), F32), jax.ShapeDtypeStruct((NH * NK, T), F32),
                   jax.ShapeDtypeStruct((NH, NK, T), BF16), jax.ShapeDtypeStruct((NH, NK, T), BF16)],
        scratch_shapes=[pltpu.VMEM((2, PEER_TOPK, TB), F32), pltpu.VMEM((2, NK, TB), F32),
                        pltpu.VMEM((ncand_pad, TB), F32), pltpu.VMEM((PEER_TOPK, TB), F32),
                        pltpu.VMEM((ncand_pad, TB), F32)],
        compiler_params=_cparams(("parallel",)),
        name="peer_route",
    )(h2t, wqt, keys)


def _peer_body(h2t_ref, x1_ref, u_ref, vt_ref, na_ref, ea_ref, rb_ref, eb_ref, nfin_ref, y_ref,
               s_s, a_s, acc_s):
    ec = pl.program_id(1)
    NH, NK, TB = rb_ref.shape
    ni = u_ref.shape[0] // NK

    @pl.when(ec == 0)
    def _():
        acc_s[...] = jnp.zeros(acc_s.shape, F32)

    s_s[...] = _dot(u_ref[...], h2t_ref[...])

    def row_tile(grp, il):
        return jnp.broadcast_to(grp[il:il + 1, :], (BF16_ROWS, LANES)).astype(BF16)

    for tl in range(TB // LANES):
        ln = slice(tl * LANES, (tl + 1) * LANES)
        rows = [pl.ds(pl.multiple_of(h * NK + ec * ni, SUBLANES), ni) for h in range(NH)]
        na = [na_ref[rows[h], ln] for h in range(NH)]
        ea = [ea_ref[rows[h], ln] for h in range(NH)]
        for il in range(ni):
            na_t = [row_tile(na[h], il) for h in range(NH)]
            ea_t = [row_tile(ea[h], il) for h in range(NH)]
            for jt in range(NK // BF16_ROWS):
                js = slice(jt * BF16_ROWS, (jt + 1) * BF16_ROWS)
                w = jnp.zeros((BF16_ROWS, LANES), BF16)
                for h in range(NH):
                    w = w + ea_t[h] * jnp.where(rb_ref[h, js, ln] < na_t[h], eb_ref[h, js, ln], 0.0)
                e = slice(il * NK + jt * BF16_ROWS, il * NK + (jt + 1) * BF16_ROWS)
                a_s[e, ln] = _gelu(s_s[e, ln]).astype(BF16) * w
    acc_s[...] += _dot(vt_ref[...], a_s[...])

    @pl.when(ec == pl.num_programs(1) - 1)
    def _():
        x2 = x1_ref[...] + jnp.transpose(acc_s[...])
        y_ref[...] = _rms(x2, nfin_ref[...])


def _peer(h2t, x1, u, vt, na, ea, rb, eb, nfin):
    D, T = h2t.shape
    E = u.shape[0]
    NH, NK, _ = rb.shape
    TB = _pick(T, 512)
    EC = 2 * SUBLANES * NK
    assert TB % LANES == 0 and E % EC == 0 and NK % BF16_ROWS == 0
    tok = pl.BlockSpec((NH, NK, TB), lambda i, j: (0, 0, i))
    flat = pl.BlockSpec((NH * NK, TB), lambda i, j: (0, i))
    return pl.pallas_call(
        _peer_body,
        grid=(T // TB, E // EC),
        in_specs=[pl.BlockSpec((D, TB), lambda i, j: (0, i)),
                  pl.BlockSpec((TB, D), lambda i, j: (i, 0)),
                  pl.BlockSpec((EC, D), lambda i, j: (j, 0)),
                  pl.BlockSpec((D, EC), lambda i, j: (0, j)),
                  flat, flat, tok, tok,
                  pl.BlockSpec((1, D), lambda i, j: (0, 0))],
        out_specs=pl.BlockSpec((TB, D), lambda i, j: (i, 0)),
        out_shape=jax.ShapeDtypeStruct((T, D), F32),
        scratch_shapes=[pltpu.VMEM((EC, TB), F32), pltpu.VMEM((EC, TB), BF16), pltpu.VMEM((D, TB), F32)],
        compiler_params=_cparams(("parallel", "arbitrary")),
        name="peer_mix",
    )(h2t, x1, u, vt, na, ea, rb, eb, nfin)


def _pad_lanes(row):
    return jnp.pad(row, ((0, 0), (0, LANES - row.shape[1])))


def kernel(x_prompt, x_sample, state_rg_h, state_rg_conv, state_ssd_h, state_ssd_conv, norm_mix, w_in, rg_conv_w, rg_conv_b, rg_wa, rg_ba, rg_wx, rg_bx, rg_lam, w_rg_out, ssd_conv_w, ssd_conv_b, ssd_dt_bias, ssd_a_log, ssd_d, ssd_norm, w_ssd_out, w_out, norm_ffn, peer_wq, peer_keys, peer_u, peer_v, norm_final):
    depth = w_in.shape[0]
    B, L, D = x_prompt.shape
    Bs, Ls, _ = x_sample.shape
    assert depth == 1 and Ls == SAMPLE_ROWS // 2 and rg_conv_w.shape[1] == 4 and L >= 3
    W = rg_conv_w.shape[2]
    H = ssd_a_log.shape[1]
    inner = ssd_norm.shape[1]
    CD = ssd_conv_w.shape[2]
    P = inner // H
    N = state_ssd_h.shape[-1]
    G = (CD - inner) // (2 * N)
    R = H // G
    dims = (G, R, P, N)
    assert W == D and inner == 2 * D and CD == 3 * D and H % SUBLANES == 0 and H <= LANES
    o_x, o_g, o_z, o_xbc, o_dt, o_m = 0, W, 2 * W, 2 * W + inner, 2 * W + inner + CD, 2 * W + inner + CD + H
    col_xbc, col_x, col_z, col_g, col_ga, col_gb = 0, CD // W, (CD + W) // inner, (CD + W + inner) // W, \
        (CD + 2 * W + inner) // D, (CD + 2 * W + inner + D) // D

    yp = x_prompt.reshape(B * L, D)
    ys = x_sample.reshape(Bs * Ls, D)
    outs = {k: [] for k in ("prh", "prc", "psh", "psc", "srh", "src", "ssh", "ssc")}
    lead = SAMPLE_ROWS - Ls
    for l in range(depth):
        wi = w_in[l]
        w_main = jnp.concatenate([wi[:, o_xbc:o_dt], wi[:, o_x:o_g], wi[:, o_z:o_xbc], wi[:, o_g:o_z],
                                  wi[:, o_m:]], axis=1).astype(BF16)
        wdt = _pad_lanes(wi[:, o_dt:o_m])
        g_mix = norm_mix[l][None]
        rg_w = (rg_conv_w[l], rg_conv_b[l][None], rg_wa[l].astype(BF16), rg_wx[l].astype(BF16),
                rg_ba[l][None], rg_bx[l][None], rg_lam[l][None], w_rg_out[l].astype(BF16))
        ssd_w = (ssd_conv_w[l], ssd_conv_b[l][None], _pad_lanes(ssd_dt_bias[l][None]), ssd_dt_bias[l][:, None],
                 _pad_lanes(ssd_a_log[l][None]), ssd_a_log[l][:, None], jnp.repeat(ssd_d[l], P)[None],
                 ssd_norm[l][None], w_ssd_out[l].astype(BF16))
        wout = w_out[l].astype(BF16)
        nf = norm_ffn[l][None]
        wqt = peer_wq[l].astype(BF16).T
        keys = peer_keys[l].astype(BF16)
        u = peer_u[l].astype(BF16)
        vt = peer_v[l].astype(BF16).T
        nfin = norm_final[None]

        proj, dt, dtT = _inproj(yp, g_mix, w_main, wdt, H)
        ya, rgh = _rg_prompt(proj, B, L, col_x, col_g, *rg_w)
        yb, ssh = _ssd_prompt(proj, dt, dtT, B, L, col_xbc, col_z, dims, *ssd_w)
        p3 = proj.reshape(B, L, -1)
        outs["prh"].append(rgh.reshape(B, W))
        outs["prc"].append(p3[:, L - 3:, CD:CD + W])
        outs["psh"].append(ssh)
        outs["psc"].append(p3[:, L - 3:, :CD])
        x1, h2t = _merge(yp, ya, yb, proj, col_ga, col_gb, wout, nf)
        yp = _peer(h2t, x1, u, vt, *_route(h2t, wqt, keys), nfin)

        xe = jnp.pad(ys.reshape(Bs, Ls, D), ((0, 0), (lead, 0), (0, 0))).reshape(Bs * SAMPLE_ROWS, D)
        proj_e, dt_e, dtT_e = _inproj(xe, g_mix, w_main, wdt, H)
        rows = lambda a3: a3.reshape(Bs * SAMPLE_ROWS, a3.shape[-1])
        rg_cpad = rows(jnp.pad(state_rg_conv[l], ((0, 0), (lead - 3, Ls), (0, 0))))
        rg_hpad = rows(jnp.pad(state_rg_h[l][:, None, :], ((0, 0), (lead - 1, Ls), (0, 0))))
        ssd_cpad = rows(jnp.pad(state_ssd_conv[l], ((0, 0), (lead - 3, Ls), (0, 0))))
        ya_e, h_e = _rg_sample(proj_e, col_x, col_g, rg_cpad, rg_hpad, *rg_w)
        yb_e, ssh_s = _ssd_sample(proj_e, dt_e, dtT_e, ssd_cpad, state_ssd_h[l], col_xbc, col_z, dims, *ssd_w)
        toks = lambda a2: a2.reshape(Bs, SAMPLE_ROWS, -1)[:, lead:].reshape(Bs * Ls, -1)
        p3 = proj_e.reshape(Bs, SAMPLE_ROWS, -1)
        outs["srh"].append(h_e.reshape(Bs, SAMPLE_ROWS, W)[:, -1])
        outs["src"].append(p3[:, SAMPLE_ROWS - 3:, CD:CD + W])
        outs["ssh"].append(ssh_s)
        outs["ssc"].append(p3[:, SAMPLE_ROWS - 3:, :CD])
        x1_e, h2t_e = _merge(xe, ya_e, yb_e, proj_e, col_ga, col_gb, wout, nf)
        x1 = toks(x1_e)
        h2t = h2t_e.reshape(D, Bs, SAMPLE_ROWS)[:, :, lead:].reshape(D, Bs * Ls)
        ys = _peer(h2t, x1, u, vt, *_route(h2t, wqt, keys), nfin)

    st = lambda k: jnp.stack(outs[k])
    return (yp.reshape(B, L, D), ys.reshape(Bs, Ls, D), st("prh"), st("prc"), st("psh"), st("psc"),
            st("srh"), st("src"), st("ssh"), st("ssc"))
```

```python
import functools

import jax
import jax.numpy as jnp
from jax import lax
from jax.experimental import pallas as pl
from jax.experimental.pallas import tpu as pltpu

F32 = jnp.float32
BF16 = jnp.bfloat16
EPS = 1e-6
RG_C = 8.0
PEER_TOPK = 16
SSD_CHUNK = 128
LANES = 128
SUBLANES = 8
BF16_ROWS = 16
SAMPLE_ROWS = 8
NEG_INF = float("-inf")
LOG2E = 1.4426950408889634
HIGHEST = lax.Precision.HIGHEST
NT_DIMS = (((1,), (1,)), ((), ()))
VMEM_LIMIT = 56 * 1024 * 1024


def _cparams(sem):
    return pltpu.CompilerParams(dimension_semantics=sem, vmem_limit_bytes=VMEM_LIMIT)


def _pick(n, pref):
    t = min(n, pref)
    while n % t:
        t -= SUBLANES
    return t


def _sigmoid(x):
    return 1.0 / (1.0 + jnp.exp2(x * -LOG2E))


def _silu(x):
    return x * _sigmoid(x)


def _softplus(x):
    return jnp.maximum(x, 0.0) + jnp.log1p(jnp.exp(-jnp.abs(x)))


def _gelu(x):
    k1 = -2.0 * 0.7978845608028654 * LOG2E
    return x / (1.0 + jnp.exp2(x * (k1 + (k1 * 0.044715) * (x * x))))


def _rms(x, g):
    return x * lax.rsqrt(jnp.mean(x * x, axis=-1, keepdims=True) + EPS) * g


def _dot(a, b):
    return jnp.dot(a, b, preferred_element_type=F32)


def _dot_nt(a, b):
    return lax.dot_general(a, b, NT_DIMS, preferred_element_type=F32)


def _inproj_body(x_ref, g_ref, w_ref, wdt_ref, o_ref, odt_ref, odtT_ref, xn_ref):
    @pl.when(pl.program_id(1) == 0)
    def _():
        xn = _rms(x_ref[...], g_ref[...])
        xn_ref[...] = xn.astype(BF16)
        dt = jnp.dot(xn, wdt_ref[...], precision=HIGHEST, preferred_element_type=F32)
        odt_ref[...] = dt
        odtT_ref[...] = jnp.transpose(dt)[:odtT_ref.shape[0], :]

    o_ref[...] = _dot(xn_ref[...], w_ref[...])


def _inproj(x2d, g, w_main, wdt, H):
    T, D = x2d.shape
    N = w_main.shape[1]
    TM = _pick(T, 1024)
    TN = 3072
    assert N % TN == 0 and TM % LANES == 0
    return pl.pallas_call(
        _inproj_body,
        grid=(T // TM, N // TN),
        in_specs=[
            pl.BlockSpec((TM, D), lambda i, j: (i, 0)),
            pl.BlockSpec((1, D), lambda i, j: (0, 0)),
            pl.BlockSpec((D, TN), lambda i, j: (0, j)),
            pl.BlockSpec((D, LANES), lambda i, j: (0, 0)),
        ],
        out_specs=[
            pl.BlockSpec((TM, TN), lambda i, j: (i, j)),
            pl.BlockSpec((TM, LANES), lambda i, j: (i, 0)),
            pl.BlockSpec((H, TM), lambda i, j: (0, i)),
        ],
        out_shape=[
            jax.ShapeDtypeStruct((T, N), F32),
            jax.ShapeDtypeStruct((T, LANES), F32),
            jax.ShapeDtypeStruct((H, T), F32),
        ],
        scratch_shapes=[pltpu.VMEM((TM, D), BF16)],
        compiler_params=_cparams(("parallel", "arbitrary")),
        name="inproj",
    )(x2d, g, w_main, wdt)


def _conv_slab(x, pad, cw_ref, cb_ref, ls):
    L = x.shape[0]
    pad[SUBLANES:SUBLANES + L, :] = x
    out = (cb_ref[:, ls]
           + cw_ref[3:4, ls] * pad[8:8 + L, :]
           + cw_ref[2:3, ls] * pad[7:7 + L, :]
           + cw_ref[1:2, ls] * pad[6:6 + L, :]
           + cw_ref[0:1, ls] * pad[5:5 + L, :])
    pad[0:SUBLANES, :] = pad[L:L + SUBLANES, :]
    return out


def _rg_gates(xc, wa_ref, wx_ref, ba, bx, lam):
    xb = xc.astype(BF16)
    nb, bw = wa_ref.shape[0], wa_ref.shape[1]
    rs, gs = [], []
    for k in range(nb):
        xk = xb[:, k * bw:(k + 1) * bw]
        rs.append(_dot(xk, wa_ref[k]))
        gs.append(_dot(xk, wx_ref[k]))
    r = _sigmoid(jnp.concatenate(rs, axis=1) + ba)
    i = _sigmoid(jnp.concatenate(gs, axis=1) + bx)
    log_a = -RG_C * r * _softplus(-lam)
    a = jnp.exp(log_a)
    em1 = jnp.tanh(log_a) * (a * a + 1.0)
    return a, jnp.sqrt(-em1) * (i * xc)


def _rg_prompt_body(x_ref, gate_ref, cw_ref, cb_ref, wa_ref, wx_ref, ba_ref, bx_ref, lam_ref, wo_ref,
                    ya_ref, hfin_ref, pad_s, a_s, b_s, h_s):
    B, Lc, W = x_ref.shape
    NS = W // LANES
    pitch = a_s.shape[1] // B

    @pl.when(pl.program_id(0) == 0)
    def _():
        pad_s[:, :, 0:SUBLANES, :] = jnp.zeros((B, NS, SUBLANES, LANES), F32)
        h_s[...] = jnp.zeros(h_s.shape, F32)

    for b in range(B):
        xc = jnp.concatenate(
            [_conv_slab(x_ref[b, :, s * LANES:(s + 1) * LANES], pad_s.at[b, s], cw_ref, cb_ref,
                        slice(s * LANES, (s + 1) * LANES)) for s in range(NS)], axis=1)
        a, bt = _rg_gates(xc, wa_ref, wx_ref, ba_ref[...], bx_ref[...], lam_ref[...])
        for s in range(NS):
            a_s[s, b * pitch:b * pitch + Lc, :] = a[:, s * LANES:(s + 1) * LANES]
            b_s[s, b * pitch:b * pitch + Lc, :] = bt[:, s * LANES:(s + 1) * LANES]

    def step(t, hs):
        out = []
        for s in range(NS):
            rows = pl.ds(t, B, stride=pitch)
            h = a_s[s, rows, :] * hs[s] + b_s[s, rows, :]
            b_s[s, rows, :] = h
            out.append(h)
        return tuple(out)

    hs = lax.fori_loop(0, Lc, step, tuple(h_s[:, s * LANES:(s + 1) * LANES] for s in range(NS)), unroll=4)
    for s in range(NS):
        h_s[:, s * LANES:(s + 1) * LANES] = hs[s]
    hfin_ref[...] = h_s[...]
    for b in range(B):
        hb = jnp.concatenate([b_s[s, b * pitch:b * pitch + Lc, :] for s in range(NS)], axis=1)
        ya_ref[b] = _dot((hb * _gelu(gate_ref[b])).astype(BF16), wo_ref[...])


def _rg_prompt(proj, B, L, col_x, col_g, cw, cb, wa, wx, ba, bx, lam, wo):
    W = cw.shape[1]
    Lc = _pick(L, 128)
    pitch = Lc + SUBLANES // 2
    assert B == SUBLANES and W % LANES == 0
    proj3 = proj.reshape(B, L, proj.shape[1])
    full = lambda *s: pl.BlockSpec(s, lambda c: (0,) * len(s))
    ya, hfin = pl.pallas_call(
        _rg_prompt_body,
        grid=(L // Lc,),
        in_specs=[
            pl.BlockSpec((B, Lc, W), lambda c: (0, c, col_x)),
            pl.BlockSpec((B, Lc, W), lambda c: (0, c, col_g)),
            full(4, W), full(1, W), full(*wa.shape), full(*wx.shape), full(1, W), full(1, W), full(1, W),
            full(*wo.shape),
        ],
        out_specs=[
            pl.BlockSpec((B, Lc, wo.shape[1]), lambda c: (0, c, 0)),
            pl.BlockSpec((B, W), lambda c: (0, 0)),
        ],
        out_shape=[
            jax.ShapeDtypeStruct((B, L, wo.shape[1]), F32),
            jax.ShapeDtypeStruct((B, W), F32),
        ],
        scratch_shapes=[
            pltpu.VMEM((B, W // LANES, Lc + SUBLANES, LANES), F32),
            pltpu.VMEM((W // LANES, B * pitch, LANES), F32),
            pltpu.VMEM((W // LANES, B * pitch, LANES), F32),
            pltpu.VMEM((B, W), F32),
        ],
        compiler_params=_cparams(("arbitrary",)),
        name="rg_prompt",
    )(proj3, proj3, cw, cb, wa, wx, ba, bx, lam, wo)
    return ya.reshape(B * L, wo.shape[1]), hfin


def _conv_rolled(u, cw_ref, cb_ref):
    return (cb_ref[...]
            + cw_ref[3:4, :] * u
            + cw_ref[2:3, :] * pltpu.roll(u, 1, axis=0)
            + cw_ref[1:2, :] * pltpu.roll(u, 2, axis=0)
            + cw_ref[0:1, :] * pltpu.roll(u, 3, axis=0))


def _rg_sample_body(x_ref, gate_ref, cpad_ref, hpad_ref, cw_ref, cb_ref, wa_ref, wx_ref, ba_ref, bx_ref,
                    lam_ref, wo_ref, ya_ref, h_ref):
    xc = _conv_rolled(x_ref[...] + cpad_ref[...], cw_ref, cb_ref)
    a, bt = _rg_gates(xc, wa_ref, wx_ref, ba_ref[...], bx_ref[...], lam_ref[...])
    row = lax.broadcasted_iota(jnp.int32, a.shape, 0) & (SAMPLE_ROWS - 1)
    h = hpad_ref[...]
    for k in range(SAMPLE_ROWS // 2, SAMPLE_ROWS):
        h = jnp.where(row == k, a * pltpu.roll(h, 1, axis=0) + bt, h)
    h_ref[...] = h
    ya_ref[...] = _dot((h * _gelu(gate_ref[...])).astype(BF16), wo_ref[...])


def _rg_sample(proj, col_x, col_g, cpad, hpad, cw, cb, wa, wx, ba, bx, lam, wo):
    T = proj.shape[0]
    W = cw.shape[1]
    TM = _pick(T, 256)
    full = lambda *s: pl.BlockSpec(s, lambda i: (0,) * len(s))
    return pl.pallas_call(
        _rg_sample_body,
        grid=(T // TM,),
        in_specs=[
            pl.BlockSpec((TM, W), lambda i: (i, col_x)),
            pl.BlockSpec((TM, W), lambda i: (i, col_g)),
            pl.BlockSpec((TM, W), lambda i: (i, 0)),
            pl.BlockSpec((TM, W), lambda i: (i, 0)),
            full(4, W), full(1, W), full(*wa.shape), full(*wx.shape), full(1, W), full(1, W), full(1, W),
            full(*wo.shape),
        ],
        out_specs=[
            pl.BlockSpec((TM, wo.shape[1]), lambda i: (i, 0)),
            pl.BlockSpec((TM, W), lambda i: (i, 0)),
        ],
        out_shape=[
            jax.ShapeDtypeStruct((T, wo.shape[1]), F32),
            jax.ShapeDtypeStruct((T, W), F32),
        ],
        compiler_params=_cparams(("parallel",)),
        name="rg_sample",
    )(proj, proj, cpad, hpad, cw, cb, wa, wx, ba, bx, lam, wo)


def _ssd_chunk(act_s, dt, dtT, alog_ref, alogT_ref, dexp_ref, seq_shift, G, R, P, N,
               yint_fn, y_s, xw_s):
    Q = act_s.shape[0]
    inner = G * R * P
    a = dt * (-jnp.exp(alog_ref[...]))
    aT = dtT * (-jnp.exp(alogT_ref[...]))
    ri = lax.broadcasted_iota(jnp.int32, (Q, Q), 0)
    ci = lax.broadcasted_iota(jnp.int32, (Q, Q), 1)
    same = (ri >> seq_shift) == (ci >> seq_shift)
    causal = same & (ci <= ri)
    causal_f = jnp.where(causal, 1.0, 0.0).astype(F32)
    same_f = jnp.where(same, 1.0, 0.0).astype(F32)
    acum = jnp.dot(causal_f, a, precision=HIGHEST, preferred_element_type=F32)
    tot = jnp.dot(same_f, a, precision=HIGHEST, preferred_element_type=F32)
    acumT = lax.dot_general(aT, causal_f, NT_DIMS, precision=HIGHEST,
                            preferred_element_type=F32)
    eacum = jnp.exp(acum)
    toend = jnp.exp(tot - acum) * dt
    for g in range(G):
        Bg = act_s[:, inner + g * N:inner + (g + 1) * N].astype(BF16)
        Cg = act_s[:, inner + (G + g) * N:inner + (G + g + 1) * N].astype(BF16)
        cb = _dot_nt(Cg, Bg)
        yint = yint_fn(g, Cg)
        for r in range(R):
            h = g * R + r
            sl = slice(h * P, (h + 1) * P)
            seg = acum[:, h:h + 1] - acumT[h:h + 1, :]
            decay = jnp.exp(jnp.where(causal, seg, NEG_INF))
            wgt = (cb * decay * dtT[h:h + 1, :]).astype(BF16)
            xh = act_s[:, sl]
            yh = _dot(wgt, xh.astype(BF16)) + dexp_ref[:, sl] * xh
            if yint is not None:
                yh = yh + yint[:, r * P:(r + 1) * P] * eacum[:, h:h + 1]
            y_s[:, sl] = yh
            xw_s[:, sl] = xh * toend[:, h:h + 1]
    return tot, eacum


def _expand_heads(x, ex_ref):
    hi = x.astype(BF16)
    r1 = x - hi.astype(F32)
    mid = r1.astype(BF16)
    lo = (r1 - mid.astype(F32)).astype(BF16)
    e = ex_ref[...]
    return _dot(hi, e) + _dot(mid, e) + _dot(lo, e)


def _ssd_finish(y, z, nw_ref, wo_ref):
    y = y * _silu(z)
    return _dot(_rms(y, nw_ref[...]).astype(BF16), wo_ref[...])


def _ssd_prompt_body(G, R, P, N,
                     xbc_ref, z_ref, dt_ref, dtT_ref, cw_ref, cb_ref, dtb_ref, dtbT_ref, alog_ref, alogT_ref,
                     dexp_ref, nw_ref, wo_ref, ex_ref, yb_ref, st_ref, pad_s, act_s, y_s, stT_s):
    Q = xbc_ref.shape[0]
    inner = G * R * P
    GW = R * P

    @pl.when(pl.program_id(1) == 0)
    def _():
        pad_s[:, 0:SUBLANES, :] = jnp.zeros((pad_s.shape[0], SUBLANES, LANES), F32)
        stT_s[...] = jnp.zeros(stT_s.shape, F32)

    for s in range(pad_s.shape[0]):
        ls = slice(s * LANES, (s + 1) * LANES)
        act_s[:, ls] = _silu(_conv_slab(xbc_ref[:, ls], pad_s.at[s], cw_ref, cb_ref, ls))
    dt = _softplus(dt_ref[...] + dtb_ref[...])
    dtT = _softplus(dtT_ref[...] + dtbT_ref[...])
    a = dt * (-jnp.exp(alog_ref[...]))
    aT = dtT * (-jnp.exp(alogT_ref[...]))
    ri = lax.broadcasted_iota(jnp.int32, (Q, Q), 0)
    ci = lax.broadcasted_iota(jnp.int32, (Q, Q), 1)
    causal = ci <= ri
    causal_f = jnp.where(causal, 1.0, 0.0).astype(F32)
    hi = dict(precision=HIGHEST, preferred_element_type=F32)
    acum = jnp.dot(causal_f, a, **hi)
    tot = jnp.dot(jnp.ones((SUBLANES, Q), F32), a, **hi)
    acumT = lax.dot_general(aT, causal_f, NT_DIMS, **hi)
    fx = _expand_heads(jnp.concatenate([jnp.exp(acum), jnp.exp(tot[0:1, :] - acum) * dt, jnp.exp(tot)], axis=0),
                       ex_ref)
    eacum_x, toend_x, dec_x = fx[0:Q], fx[Q:2 * Q], fx[2 * Q:2 * Q + 1]
    col2 = acum * LOG2E
    row2 = (acumT - jnp.log(dtT)) * LOG2E
    lane = lax.broadcasted_iota(jnp.int32, (Q, LANES), 1)
    for g in range(G):
        gs = slice(g * GW, (g + 1) * GW)
        Bf = act_s[:, inner + g * N:inner + (g + 1) * N]
        Bg = Bf.astype(BF16)
        Cg = act_s[:, inner + (G + g) * N:inner + (G + g + 1) * N].astype(BF16)
        cb = jnp.where(causal, _dot_nt(Cg, Bg), 0.0)
        y_s[:, gs] = (_dot(Cg, stT_s[:, gs].astype(BF16)) * eacum_x[:, gs]
                      + dexp_ref[:, gs] * act_s[:, gs])
        for pr in range(R // 2):
            h0 = g * R + 2 * pr
            ps = slice(h0 * P, (h0 + 2) * P)
            wg = []
            for h in (h0, h0 + 1):
                e = jnp.exp2(jnp.where(causal, col2[:, h:h + 1] - row2[h:h + 1, :], NEG_INF))
                wg.append((cb * e).astype(BF16))
            xp = act_s[:, ps]
            xa = jnp.where(lane < P, xp, 0.0).astype(BF16)
            xb = jnp.where(lane < P, 0.0, xp).astype(BF16)
            y_s[:, ps] = y_s[:, ps] + _dot(jnp.concatenate(wg, axis=1), jnp.concatenate([xa, xb], axis=0))
        xw = (act_s[:, gs] * toend_x[:, gs]).astype(BF16)
        ST = _dot(jnp.transpose(Bf).astype(BF16), xw)
        stT_s[:, gs] = dec_x[:, gs] * stT_s[:, gs] + ST
    yb_ref[...] = _ssd_finish(y_s[...], z_ref[...], nw_ref, wo_ref)

    @pl.when(pl.program_id(1) == pl.num_programs(1) - 1)
    def _():
        for g in range(G):
            st_ref[0, g * R:(g + 1) * R] = jnp.transpose(stT_s[:, g * GW:(g + 1) * GW]).reshape(R, P, N)


def _ssd_prompt(proj, dt, dtT, B, L, col_xbc, col_z, dims, cw, cb, dtb, dtbT, alog, alogT, dexp, nw, wo):
    G, R, P, N = dims
    H = G * R
    inner = H * P
    CD = cw.shape[1]
    Q = _pick(L, SSD_CHUNK)
    nC = L // Q
    assert 2 * P == LANES and R % 2 == 0 and N == LANES
    head_of_lane = jnp.arange(inner, dtype=jnp.int32) // P
    ex = (jnp.arange(LANES, dtype=jnp.int32)[:, None] == head_of_lane[None, :]).astype(BF16)
    full = lambda *s: pl.BlockSpec(s, lambda b, c: (0,) * len(s))
    return pl.pallas_call(
        functools.partial(_ssd_prompt_body, G, R, P, N),
        grid=(B, nC),
        in_specs=[
            pl.BlockSpec((Q, CD), lambda b, c: (b * nC + c, col_xbc)),
            pl.BlockSpec((Q, inner), lambda b, c: (b * nC + c, col_z)),
            pl.BlockSpec((Q, LANES), lambda b, c: (b * nC + c, 0)),
            pl.BlockSpec((H, Q), lambda b, c: (0, b * nC + c)),
            full(4, CD), full(1, CD), full(1, LANES), full(H, 1), full(1, LANES), full(H, 1),
            full(1, inner), full(1, inner), full(*wo.shape), full(LANES, inner),
        ],
        out_specs=[
            pl.BlockSpec((Q, wo.shape[1]), lambda b, c: (b * nC + c, 0)),
            pl.BlockSpec((1, H, P, N), lambda b, c: (b, 0, 0, 0)),
        ],
        out_shape=[
            jax.ShapeDtypeStruct((B * L, wo.shape[1]), F32),
            jax.ShapeDtypeStruct((B, H, P, N), F32),
        ],
        scratch_shapes=[
            pltpu.VMEM((CD // LANES, Q + SUBLANES, LANES), F32),
            pltpu.VMEM((Q, CD), F32),
            pltpu.VMEM((Q, inner), F32),
            pltpu.VMEM((N, inner), F32),
        ],
        compiler_params=_cparams(("parallel", "arbitrary")),
        name="ssd_prompt",
    )(proj, proj, dt, dtT, cw, cb, dtb, dtbT, alog, alogT, dexp, nw, wo, ex)


def _ssd_sample_body(G, R, P, N,
                     xbc_ref, cpad_ref, z_ref, dt_ref, dtT_ref, h0_ref, cw_ref, cb_ref, dtb_ref, dtbT_ref,
                     alog_ref, alogT_ref, dexp_ref, nw_ref, wo_ref, yb_ref, st_ref,
                     act_s, y_s, xw_s, yint_s, xwT_s, tot_s, eacum_s):
    Q = xbc_ref.shape[0]
    inner = G * R * P
    bi = pl.program_id(1)
    half = SAMPLE_ROWS // 2

    @pl.when(bi == 0)
    def _():
        act_s[...] = _silu(_conv_rolled(xbc_ref[...] + cpad_ref[...], cw_ref, cb_ref))
        rows = lax.broadcasted_iota(jnp.int32, (Q, LANES), 0) & (SAMPLE_ROWS - 1)
        cols = lax.broadcasted_iota(jnp.int32, dtT_ref.shape, 1) & (SAMPLE_ROWS - 1)
        dt = jnp.where(rows >= half, _softplus(dt_ref[...] + dtb_ref[...]), 0.0)
        dtT = jnp.where(cols >= half, _softplus(dtT_ref[...] + dtbT_ref[...]), 0.0)
        seq_shift = SAMPLE_ROWS.bit_length() - 1
        tot, eacum = _ssd_chunk(act_s, dt, dtT, alog_ref, alogT_ref, dexp_ref, seq_shift, G, R, P, N,
                                lambda g, Cg: None, y_s, xw_s)
        tot_s[...] = tot
        eacum_s[...] = eacum
        for g in range(G):
            xwT_s[g] = jnp.transpose(xw_s[:, g * R * P:(g + 1) * R * P]).astype(BF16)

    rid = lax.broadcasted_iota(jnp.int32, (Q, N), 0)
    for q in range(h0_ref.shape[0]):
        r0 = pl.multiple_of((bi * h0_ref.shape[0] + q) * SAMPLE_ROWS, SAMPLE_ROWS)
        mine = (rid >= r0) & (rid < r0 + SAMPLE_ROWS)
        dec = jnp.exp(tot_s[pl.ds(r0, 1), :])
        for g in range(G):
            Cb = act_s[pl.ds(r0, SAMPLE_ROWS), inner + (G + g) * N:inner + (G + g + 1) * N]
            stg = h0_ref[q, g * R:(g + 1) * R].reshape(R * P, N)
            yint_s[pl.ds(r0, SAMPLE_ROWS), g * R * P:(g + 1) * R * P] = _dot_nt(Cb, stg)
            Bg = jnp.where(mine, act_s[:, inner + g * N:inner + (g + 1) * N], 0.0).astype(BF16)
            S = _dot(xwT_s[g], Bg)
            for r in range(R):
                h = g * R + r
                st_ref[q, h] = dec[:, h:h + 1] * h0_ref[q, h] + S[r * P:(r + 1) * P, :]

    @pl.when(bi == pl.num_programs(1) - 1)
    def _():
        for h in range(G * R):
            sl = slice(h * P, (h + 1) * P)
            y_s[:, sl] = y_s[:, sl] + yint_s[:, sl] * eacum_s[:, h:h + 1]
        yb_ref[...] = _ssd_finish(y_s[...], z_ref[...], nw_ref, wo_ref)


def _ssd_sample(proj, dt, dtT, cpad, h0, col_xbc, col_z, dims, cw, cb, dtb, dtbT, alog, alogT, dexp, nw, wo):
    G, R, P, N = dims
    H = G * R
    inner = H * P
    CD = cw.shape[1]
    T = proj.shape[0]
    Bs = h0.shape[0]
    Q = _pick(T, SSD_CHUNK)
    SB = 4
    nb = Q // SAMPLE_ROWS // SB
    assert Q % (SAMPLE_ROWS * SB) == 0
    full = lambda *s: pl.BlockSpec(s, lambda i, j: (0,) * len(s))
    return pl.pallas_call(
        functools.partial(_ssd_sample_body, G, R, P, N),
        grid=(T // Q, nb),
        in_specs=[
            pl.BlockSpec((Q, CD), lambda i, j: (i, col_xbc)),
            pl.BlockSpec((Q, CD), lambda i, j: (i, 0)),
            pl.BlockSpec((Q, inner), lambda i, j: (i, col_z)),
            pl.BlockSpec((Q, LANES), lambda i, j: (i, 0)),
            pl.BlockSpec((H, Q), lambda i, j: (0, i)),
            pl.BlockSpec((SB, H, P, N), lambda i, j: (i * nb + j, 0, 0, 0)),
            full(4, CD), full(1, CD), full(1, LANES), full(H, 1), full(1, LANES), full(H, 1),
            full(1, inner), full(1, inner), full(*wo.shape),
        ],
        out_specs=[
            pl.BlockSpec((Q, wo.shape[1]), lambda i, j: (i, 0)),
            pl.BlockSpec((SB, H, P, N), lambda i, j: (i * nb + j, 0, 0, 0)),
        ],
        out_shape=[
            jax.ShapeDtypeStruct((T, wo.shape[1]), F32),
            jax.ShapeDtypeStruct((Bs, H, P, N), F32),
        ],
        scratch_shapes=[
            pltpu.VMEM((Q, CD), F32),
            pltpu.VMEM((Q, inner), F32),
            pltpu.VMEM((Q, inner), F32),
            pltpu.VMEM((Q, inner), F32),
            pltpu.VMEM((G, R * P, Q), BF16),
            pltpu.VMEM((Q, LANES), F32),
            pltpu.VMEM((Q, LANES), F32),
        ],
        compiler_params=_cparams(("parallel", "arbitrary")),
        name="ssd_sample",
    )(proj, cpad, proj, dt, dtT, h0, cw, cb, dtb, dtbT, alog, alogT, dexp, nw, wo)


def _staircase(k):
    return [(ka, k // (ka + 1)) for ka in range(k)]


def _take_top(cur, k_top, want_rank):
    rank = jnp.full(cur.shape, float(k_top), F32) if want_rank else None
    vals = []
    for k in range(k_top):
        m = jnp.max(cur, axis=0, keepdims=True)
        hit = cur == m
        vals.append(m)
        if want_rank:
            rank = jnp.where(hit, float(k), rank)
        cur = jnp.where(hit, NEG_INF, cur)
    taken = jnp.sum(jnp.where(cur == NEG_INF, 1.0, 0.0), axis=0, keepdims=True)
    return vals, rank, taken


def _take_top_ties(x, k_top, val_ref, rank_ref):
    rows = lax.broadcasted_iota(jnp.int32, x.shape, 0)
    kk = lax.broadcasted_iota(jnp.int32, (k_top, x.shape[1]), 0)

    def body(k, carry):
        cur, rank, vals = carry
        m = jnp.max(cur, axis=0, keepdims=True)
        first = jnp.min(jnp.where(cur == m, rows, x.shape[0]), axis=0, keepdims=True)
        hit = rows == first
        return (jnp.where(hit, NEG_INF, cur), jnp.where(hit, lax.convert_element_type(k, F32), rank),
                jnp.where(kk == k, m, vals))

    init = (x, jnp.full(x.shape, float(k_top), F32), jnp.zeros((k_top, x.shape[1]), F32))
    _, rank, vals = lax.fori_loop(0, k_top, body, init)
    val_ref[...] = vals
    rank_ref[...] = rank


def _route_body(x_ref, ya_ref, yb_ref, ga_ref, gb_ref, wout_ref, nf_ref, wqt_ref, keys_ref,
                x1_ref, h2t_ref, na_ref, ea_ref, rb_ref, eb_ref, sv_s, rk_s, cand_s, cv_s, cr_s):
    NH = keys_ref.shape[0]
    NK, KH = keys_ref.shape[2], keys_ref.shape[3]
    K = PEER_TOPK
    TB = x_ref.shape[0]
    tiles = [slice(tl * LANES, (tl + 1) * LANES) for tl in range(TB // LANES)]
    m = _sigmoid(ga_ref[...]) * ya_ref[...] + _sigmoid(gb_ref[...]) * yb_ref[...]
    x1 = x_ref[...] + _dot(m.astype(BF16), wout_ref[...])
    x1_ref[...] = x1
    h2t = jnp.transpose(_rms(x1, nf_ref[...])).astype(BF16)
    h2t_ref[...] = h2t
    qt = _dot(wqt_ref[...], h2t).astype(BF16)
    cand_s[...] = jnp.full(cand_s.shape, NEG_INF, F32)

    def candidates():
        off = 0
        for ka, nb in _staircase(K):
            cand_s[off:off + nb, :] = sv_s[0, ka:ka + 1, :] + sv_s[1, 0:nb, :]
            off += nb

    def emit(h, sT, in_top, rb, sel, na_of):
        top = sv_s[0, 0:1, :] + sv_s[1, 0:1, :]
        z = jnp.sum(jnp.where(sel, jnp.exp(cand_s[...] - top), 0.0), axis=0, keepdims=True)
        ex = [jnp.where(in_top[s], jnp.exp(sT[s] - sv_s[s, 0:1, :]), 0.0) for s in range(2)]
        na = jnp.zeros(sT[0].shape, F32)
        off = 0
        for ka, nb in _staircase(K):
            cnt = jnp.sum(jnp.where(sel[off:off + nb, :], 1.0, 0.0), axis=0, keepdims=True)
            na = jnp.where(na_of(ka), cnt, na)
            off += nb
        na_ref[h * NK:(h + 1) * NK, :] = na
        ea_ref[h * NK:(h + 1) * NK, :] = ex[0] / z
        rb_ref[h] = rb.astype(BF16)
        eb_ref[h] = ex[1].astype(BF16)

    def scores(h):
        return [_dot(keys_ref[h, s], qt[(h * 2 + s) * KH:(h * 2 + s + 1) * KH, :]) for s in range(2)]

    most = []
    for h in range(NH):
        sT = scores(h)
        seen = jnp.zeros((1, LANES), F32)
        rbs = []
        for s in range(2):
            for ln in tiles:
                vals, rank, taken = _take_top(sT[s][:, ln], K, s == 1)
                for k in range(K):
                    sv_s[s, k:k + 1, ln] = vals[k]
                seen = jnp.maximum(seen, taken)
                if s == 1:
                    rbs.append(rank)
        candidates()
        cand = cand_s[...]
        tau = _take_top(cand, K, False)[0][K - 1]
        sel = cand >= tau
        most.append(jnp.maximum(seen, jnp.max(jnp.sum(jnp.where(sel, 1.0, 0.0), axis=0, keepdims=True),
                                              axis=1, keepdims=True)))
        emit(h, sT, [sT[s] >= sv_s[s, K - 1:K, :] for s in range(2)], jnp.concatenate(rbs, axis=1), sel,
             lambda ka: sT[0] == sv_s[0, ka:ka + 1, :])

    for h in range(NH):
        @pl.when(jnp.max(most[h]) > K)
        def _(h=h):
            sT = scores(h)
            for s in range(2):
                for ln in tiles:
                    _take_top_ties(sT[s][:, ln], K, sv_s.at[s, :, ln], rk_s.at[s, :, ln])
            candidates()
            for ln in tiles:
                _take_top_ties(cand_s[:, ln], K, cv_s.at[:, ln], cr_s.at[:, ln])
            emit(h, sT, [rk_s[s] < K for s in range(2)], rk_s[1], cr_s[...] < K,
                 lambda ka: rk_s[0] == float(ka))


def _route(x2d, ya, yb, gates, col_ga, col_gb, wout, nf, wqt, keys):
    T, D = x2d.shape
    NH, _, NK, KH = keys.shape
    TB = _pick(T, 256)
    assert TB % LANES == 0
    ncand = sum(nb for _, nb in _staircase(PEER_TOPK))
    ncand_pad = -(-ncand // SUBLANES) * SUBLANES
    flat = pl.BlockSpec((NH * NK, TB), lambda i: (0, i))
    tok = pl.BlockSpec((NH, NK, TB), lambda i: (0, 0, i))
    row = lambda c: pl.BlockSpec((TB, D), lambda i: (i, c))
    return pl.pallas_call(
        _route_body,
        grid=(T // TB,),
        in_specs=[row(0), row(0), row(0), row(col_ga), row(col_gb),
                  pl.BlockSpec(wout.shape, lambda i: (0, 0)), pl.BlockSpec((1, D), lambda i: (0, 0)),
                  pl.BlockSpec(wqt.shape, lambda i: (0, 0)),
                  pl.BlockSpec(keys.shape, lambda i: (0, 0, 0, 0))],
        out_specs=[row(0), pl.BlockSpec((D, TB), lambda i: (0, i)), flat, flat, tok, tok],
        out_shape=[jax.ShapeDtypeStruct((T, D), F32), jax.ShapeDtypeStruct((D, T), BF16),
                   jax.ShapeDtypeStruct((NH * NK, T), F32), jax.ShapeDtypeStruct((NH * NK, T), F32),
                   jax.ShapeDtypeStruct((NH, NK, T), BF16), jax.ShapeDtypeStruct((NH, NK, T), BF16)],
        scratch_shapes=[pltpu.VMEM((2, PEER_TOPK, TB), F32), pltpu.VMEM((2, NK, TB), F32),
                        pltpu.VMEM((ncand_pad, TB), F32), pltpu.VMEM((PEER_TOPK, TB), F32),
                        pltpu.VMEM((ncand_pad, TB), F32)],
        compiler_params=_cparams(("parallel",)),
        name="peer_route",
    )(x2d, ya, yb, gates, gates, wout, nf, wqt, keys)


def _peer_body(h2t_ref, x1_ref, u_ref, vt_ref, na_ref, ea_ref, rb_ref, eb_ref, nfin_ref, y_ref,
               s_s, a_s, acc_s):
    ec = pl.program_id(1)
    NH, NK, TB = rb_ref.shape
    ni = u_ref.shape[0] // NK

    @pl.when(ec == 0)
    def _():
        acc_s[...] = jnp.zeros(acc_s.shape, F32)

    s_s[...] = _dot(u_ref[...], h2t_ref[...])

    def row_tile(grp, il):
        return jnp.broadcast_to(grp[il:il + 1, :], (BF16_ROWS, LANES)).astype(BF16)

    for tl in range(TB // LANES):
        ln = slice(tl * LANES, (tl + 1) * LANES)
        rows = [pl.ds(pl.multiple_of(h * NK + ec * ni, SUBLANES), ni) for h in range(NH)]
        na = [na_ref[rows[h], ln] for h in range(NH)]
        ea = [ea_ref[rows[h], ln] for h in range(NH)]
        for il in range(ni):
            na_t = [row_tile(na[h], il) for h in range(NH)]
            ea_t = [row_tile(ea[h], il) for h in range(NH)]
            for jt in range(NK // BF16_ROWS):
                js = slice(jt * BF16_ROWS, (jt + 1) * BF16_ROWS)
                w = jnp.zeros((BF16_ROWS, LANES), BF16)
                for h in range(NH):
                    w = w + ea_t[h] * jnp.where(rb_ref[h, js, ln] < na_t[h], eb_ref[h, js, ln], 0.0)
                e = slice(il * NK + jt * BF16_ROWS, il * NK + (jt + 1) * BF16_ROWS)
                a_s[e, ln] = _gelu(s_s[e, ln].astype(BF16)) * w
    acc_s[...] += _dot(vt_ref[...], a_s[...])

    @pl.when(ec == pl.num_programs(1) - 1)
    def _():
        x2 = x1_ref[...] + jnp.transpose(acc_s[...])
        y_ref[...] = _rms(x2, nfin_ref[...])


def _peer(h2t, x1, u, vt, na, ea, rb, eb, nfin):
    D, T = h2t.shape
    E = u.shape[0]
    NH, NK, _ = rb.shape
    TB = _pick(T, 512)
    EC = 2 * SUBLANES * NK
    assert TB % LANES == 0 and E % EC == 0 and NK % BF16_ROWS == 0
    tok = pl.BlockSpec((NH, NK, TB), lambda i, j: (0, 0, i))
    flat = pl.BlockSpec((NH * NK, TB), lambda i, j: (0, i))
    return pl.pallas_call(
        _peer_body,
        grid=(T // TB, E // EC),
        in_specs=[pl.BlockSpec((D, TB), lambda i, j: (0, i)),
                  pl.BlockSpec((TB, D), lambda i, j: (i, 0)),
                  pl.BlockSpec((EC, D), lambda i, j: (j, 0)),
                  pl.BlockSpec((D, EC), lambda i, j: (0, j)),
                  flat, flat, tok, tok,
                  pl.BlockSpec((1, D), lambda i, j: (0, 0))],
        out_specs=pl.BlockSpec((TB, D), lambda i, j: (i, 0)),
        out_shape=jax.ShapeDtypeStruct((T, D), F32),
        scratch_shapes=[pltpu.VMEM((EC, TB), F32), pltpu.VMEM((EC, TB), BF16), pltpu.VMEM((D, TB), F32)],
        compiler_params=_cparams(("parallel", "arbitrary")),
        name="peer_mix",
    )(h2t, x1, u, vt, na, ea, rb, eb, nfin)


def _pad_lanes(row):
    return jnp.pad(row, ((0, 0), (0, LANES - row.shape[1])))


def kernel(x_prompt, x_sample, state_rg_h, state_rg_conv, state_ssd_h, state_ssd_conv, norm_mix, w_in, rg_conv_w, rg_conv_b, rg_wa, rg_ba, rg_wx, rg_bx, rg_lam, w_rg_out, ssd_conv_w, ssd_conv_b, ssd_dt_bias, ssd_a_log, ssd_d, ssd_norm, w_ssd_out, w_out, norm_ffn, peer_wq, peer_keys, peer_u, peer_v, norm_final):
    depth = w_in.shape[0]
    B, L, D = x_prompt.shape
    Bs, Ls, _ = x_sample.shape
    assert depth == 1 and Ls == SAMPLE_ROWS // 2 and rg_conv_w.shape[1] == 4 and L >= 3
    W = rg_conv_w.shape[2]
    H = ssd_a_log.shape[1]
    inner = ssd_norm.shape[1]
    CD = ssd_conv_w.shape[2]
    P = inner // H
    N = state_ssd_h.shape[-1]
    G = (CD - inner) // (2 * N)
    R = H // G
    dims = (G, R, P, N)
    assert W == D and inner == 2 * D and CD == 3 * D and H % SUBLANES == 0 and H <= LANES
    o_x, o_g, o_z, o_xbc, o_dt, o_m = 0, W, 2 * W, 2 * W + inner, 2 * W + inner + CD, 2 * W + inner + CD + H
    col_xbc, col_x, col_z, col_g, col_ga, col_gb = 0, CD // W, (CD + W) // inner, (CD + W + inner) // W, \
        (CD + 2 * W + inner) // D, (CD + 2 * W + inner + D) // D

    yp = x_prompt.reshape(B * L, D)
    ys = x_sample.reshape(Bs * Ls, D)
    outs = {k: [] for k in ("prh", "prc", "psh", "psc", "srh", "src", "ssh", "ssc")}
    lead = SAMPLE_ROWS - Ls
    for l in range(depth):
        wi = w_in[l]
        w_main = jnp.concatenate([wi[:, o_xbc:o_dt], wi[:, o_x:o_g], wi[:, o_z:o_xbc], wi[:, o_g:o_z],
                                  wi[:, o_m:]], axis=1).astype(BF16)
        wdt = _pad_lanes(wi[:, o_dt:o_m])
        g_mix = norm_mix[l][None]
        rg_w = (rg_conv_w[l], rg_conv_b[l][None], rg_wa[l].astype(BF16), rg_wx[l].astype(BF16),
                rg_ba[l][None], rg_bx[l][None], rg_lam[l][None], w_rg_out[l].astype(BF16))
        ssd_w = (ssd_conv_w[l], ssd_conv_b[l][None], _pad_lanes(ssd_dt_bias[l][None]), ssd_dt_bias[l][:, None],
                 _pad_lanes(ssd_a_log[l][None]), ssd_a_log[l][:, None], jnp.repeat(ssd_d[l], P)[None],
                 ssd_norm[l][None], w_ssd_out[l].astype(BF16))
        wout = w_out[l].astype(BF16)
        nf = norm_ffn[l][None]
        wqt = peer_wq[l].astype(BF16).T
        keys = peer_keys[l].astype(BF16)
        u = peer_u[l].astype(BF16)
        vt = peer_v[l].astype(BF16).T
        nfin = norm_final[None]

        proj, dt, dtT = _inproj(yp, g_mix, w_main, wdt, H)
        ya, rgh = _rg_prompt(proj, B, L, col_x, col_g, *rg_w)
        yb, ssh = _ssd_prompt(proj, dt, dtT, B, L, col_xbc, col_z, dims, *ssd_w)
        p3 = proj.reshape(B, L, -1)
        outs["prh"].append(rgh.reshape(B, W))
        outs["prc"].append(p3[:, L - 3:, CD:CD + W])
        outs["psh"].append(ssh)
        outs["psc"].append(p3[:, L - 3:, :CD])
        x1, h2t, *routing = _route(yp, ya, yb, proj, col_ga, col_gb, wout, nf, wqt, keys)
        yp = _peer(h2t, x1, u, vt, *routing, nfin)

        xe = jnp.pad(ys.reshape(Bs, Ls, D), ((0, 0), (lead, 0), (0, 0))).reshape(Bs * SAMPLE_ROWS, D)
        proj_e, dt_e, dtT_e = _inproj(xe, g_mix, w_main, wdt, H)
        rows = lambda a3: a3.reshape(Bs * SAMPLE_ROWS, a3.shape[-1])
        rg_cpad = rows(jnp.pad(state_rg_conv[l], ((0, 0), (lead - 3, Ls), (0, 0))))
        rg_hpad = rows(jnp.pad(state_rg_h[l][:, None, :], ((0, 0), (lead - 1, Ls), (0, 0))))
        ssd_cpad = rows(jnp.pad(state_ssd_conv[l], ((0, 0), (lead - 3, Ls), (0, 0))))
        ya_e, h_e = _rg_sample(proj_e, col_x, col_g, rg_cpad, rg_hpad, *rg_w)
        yb_e, ssh_s = _ssd_sample(proj_e, dt_e, dtT_e, ssd_cpad, state_ssd_h[l], col_xbc, col_z, dims, *ssd_w)
        toks = lambda a2: a2.reshape(Bs, SAMPLE_ROWS, -1)[:, lead:].reshape(Bs * Ls, -1)
        p3 = proj_e.reshape(Bs, SAMPLE_ROWS, -1)
        outs["srh"].append(h_e.reshape(Bs, SAMPLE_ROWS, W)[:, -1])
        outs["src"].append(p3[:, SAMPLE_ROWS - 3:, CD:CD + W])
        outs["ssh"].append(ssh_s)
        outs["ssc"].append(p3[:, SAMPLE_ROWS - 3:, :CD])
        gates_s = toks(proj_e[:, col_ga * D:(col_gb + 1) * D])
        x1, h2t, *routing = _route(ys, toks(ya_e), toks(yb_e), gates_s, 0, 1, wout, nf, wqt, keys)
        ys = _peer(h2t, x1, u, vt, *routing, nfin)

    st = lambda k: jnp.stack(outs[k])
    return (yp.reshape(B, L, D), ys.reshape(Bs, Ls, D), st("prh"), st("prc"), st("psh"), st("psc"),
            st("srh"), st("src"), st("ssh"), st("ssc"))
```

```python
import functools

import jax
import jax.numpy as jnp
from jax import lax
from jax.experimental import pallas as pl
from jax.experimental.pallas import tpu as pltpu

F32 = jnp.float32
BF16 = jnp.bfloat16
EPS = 1e-6
RG_C = 8.0
PEER_TOPK = 16
SSD_CHUNK = 128
LANES = 128
SUBLANES = 8
BF16_ROWS = 16
SAMPLE_ROWS = 8
NEG_INF = float("-inf")
LOG2E = 1.4426950408889634
HIGHEST = lax.Precision.HIGHEST
NT_DIMS = (((1,), (1,)), ((), ()))
VMEM_LIMIT = 56 * 1024 * 1024


def _cparams(sem):
    return pltpu.CompilerParams(dimension_semantics=sem, vmem_limit_bytes=VMEM_LIMIT)


def _pick(n, pref):
    t = min(n, pref)
    while n % t:
        t -= SUBLANES
    return t


def _sigmoid(x):
    return 1.0 / (1.0 + jnp.exp2(x * -LOG2E))


def _silu(x):
    return x * _sigmoid(x)


def _softplus(x):
    return jnp.maximum(x, 0.0) + jnp.log1p(jnp.exp(-jnp.abs(x)))


def _gelu(x):
    k1 = -2.0 * 0.7978845608028654 * LOG2E
    return x / (1.0 + jnp.exp2(x * (k1 + (k1 * 0.044715) * (x * x))))


def _rms(x, g):
    return x * lax.rsqrt(jnp.mean(x * x, axis=-1, keepdims=True) + EPS) * g


def _dot(a, b):
    return jnp.dot(a, b, preferred_element_type=F32)


def _dot_nt(a, b):
    return lax.dot_general(a, b, NT_DIMS, preferred_element_type=F32)


def _inproj_body(x_ref, g_ref, w_ref, wdt_ref, o_ref, odt_ref, odtT_ref, xn_ref):
    @pl.when(pl.program_id(1) == 0)
    def _():
        xn = _rms(x_ref[...], g_ref[...])
        xn_ref[...] = xn.astype(BF16)
        dt = jnp.dot(xn, wdt_ref[...], precision=HIGHEST, preferred_element_type=F32)
        odt_ref[...] = dt
        odtT_ref[...] = jnp.transpose(dt)[:odtT_ref.shape[0], :]

    o_ref[...] = _dot(xn_ref[...], w_ref[...])


def _inproj(x2d, g, w_main, wdt, H):
    T, D = x2d.shape
    N = w_main.shape[1]
    TM = _pick(T, 1024)
    TN = 3072
    assert N % TN == 0 and TM % LANES == 0
    return pl.pallas_call(
        _inproj_body,
        grid=(T // TM, N // TN),
        in_specs=[
            pl.BlockSpec((TM, D), lambda i, j: (i, 0)),
            pl.BlockSpec((1, D), lambda i, j: (0, 0)),
            pl.BlockSpec((D, TN), lambda i, j: (0, j)),
            pl.BlockSpec((D, LANES), lambda i, j: (0, 0)),
        ],
        out_specs=[
            pl.BlockSpec((TM, TN), lambda i, j: (i, j)),
            pl.BlockSpec((TM, LANES), lambda i, j: (i, 0)),
            pl.BlockSpec((H, TM), lambda i, j: (0, i)),
        ],
        out_shape=[
            jax.ShapeDtypeStruct((T, N), F32),
            jax.ShapeDtypeStruct((T, LANES), F32),
            jax.ShapeDtypeStruct((H, T), F32),
        ],
        scratch_shapes=[pltpu.VMEM((TM, D), BF16)],
        compiler_params=_cparams(("parallel", "arbitrary")),
        name="inproj",
    )(x2d, g, w_main, wdt)


def _conv_slab(x, pad, cw_ref, cb_ref, ls):
    L = x.shape[0]
    pad[SUBLANES:SUBLANES + L, :] = x
    out = (cb_ref[:, ls]
           + cw_ref[3:4, ls] * pad[8:8 + L, :]
           + cw_ref[2:3, ls] * pad[7:7 + L, :]
           + cw_ref[1:2, ls] * pad[6:6 + L, :]
           + cw_ref[0:1, ls] * pad[5:5 + L, :])
    pad[0:SUBLANES, :] = pad[L:L + SUBLANES, :]
    return out


def _rg_gates(xc, wa_ref, wx_ref, ba, bx, lam):
    xb = xc.astype(BF16)
    nb, bw = wa_ref.shape[0], wa_ref.shape[1]
    rs, gs = [], []
    for k in range(nb):
        xk = xb[:, k * bw:(k + 1) * bw]
        rs.append(_dot(xk, wa_ref[k]))
        gs.append(_dot(xk, wx_ref[k]))
    r = _sigmoid(jnp.concatenate(rs, axis=1) + ba)
    i = _sigmoid(jnp.concatenate(gs, axis=1) + bx)
    log_a = -RG_C * r * _softplus(-lam)
    a = jnp.exp(log_a)
    em1 = jnp.tanh(log_a) * (a * a + 1.0)
    return a, jnp.sqrt(-em1) * (i * xc)


def _rg_prompt_body(x_ref, gate_ref, cw_ref, cb_ref, wa_ref, wx_ref, ba_ref, bx_ref, lam_ref, wo_ref,
                    ya_ref, hfin_ref, pad_s, a_s, b_s, h_s):
    B, Lc, W = x_ref.shape
    NS = W // LANES
    pitch = a_s.shape[1] // B

    @pl.when(pl.program_id(0) == 0)
    def _():
        pad_s[:, :, 0:SUBLANES, :] = jnp.zeros((B, NS, SUBLANES, LANES), F32)
        h_s[...] = jnp.zeros(h_s.shape, F32)

    for b in range(B):
        xc = jnp.concatenate(
            [_conv_slab(x_ref[b, :, s * LANES:(s + 1) * LANES], pad_s.at[b, s], cw_ref, cb_ref,
                        slice(s * LANES, (s + 1) * LANES)) for s in range(NS)], axis=1)
        a, bt = _rg_gates(xc, wa_ref, wx_ref, ba_ref[...], bx_ref[...], lam_ref[...])
        for s in range(NS):
            a_s[s, b * pitch:b * pitch + Lc, :] = a[:, s * LANES:(s + 1) * LANES]
            b_s[s, b * pitch:b * pitch + Lc, :] = bt[:, s * LANES:(s + 1) * LANES]

    def step(t, hs):
        out = []
        for s in range(NS):
            rows = pl.ds(t, B, stride=pitch)
            h = a_s[s, rows, :] * hs[s] + b_s[s, rows, :]
            b_s[s, rows, :] = h
            out.append(h)
        return tuple(out)

    hs = lax.fori_loop(0, Lc, step, tuple(h_s[:, s * LANES:(s + 1) * LANES] for s in range(NS)), unroll=4)
    for s in range(NS):
        h_s[:, s * LANES:(s + 1) * LANES] = hs[s]
    hfin_ref[...] = h_s[...]
    for b in range(B):
        hb = jnp.concatenate([b_s[s, b * pitch:b * pitch + Lc, :] for s in range(NS)], axis=1)
        ya_ref[b] = _dot((hb * _gelu(gate_ref[b])).astype(BF16), wo_ref[...])


def _rg_prompt(proj, B, L, col_x, col_g, cw, cb, wa, wx, ba, bx, lam, wo):
    W = cw.shape[1]
    Lc = _pick(L, 128)
    pitch = Lc + SUBLANES // 2
    assert B == SUBLANES and W % LANES == 0
    proj3 = proj.reshape(B, L, proj.shape[1])
    full = lambda *s: pl.BlockSpec(s, lambda c: (0,) * len(s))
    ya, hfin = pl.pallas_call(
        _rg_prompt_body,
        grid=(L // Lc,),
        in_specs=[
            pl.BlockSpec((B, Lc, W), lambda c: (0, c, col_x)),
            pl.BlockSpec((B, Lc, W), lambda c: (0, c, col_g)),
            full(4, W), full(1, W), full(*wa.shape), full(*wx.shape), full(1, W), full(1, W), full(1, W),
            full(*wo.shape),
        ],
        out_specs=[
            pl.BlockSpec((B, Lc, wo.shape[1]), lambda c: (0, c, 0)),
            pl.BlockSpec((B, W), lambda c: (0, 0)),
        ],
        out_shape=[
            jax.ShapeDtypeStruct((B, L, wo.shape[1]), F32),
            jax.ShapeDtypeStruct((B, W), F32),
        ],
        scratch_shapes=[
            pltpu.VMEM((B, W // LANES, Lc + SUBLANES, LANES), F32),
            pltpu.VMEM((W // LANES, B * pitch, LANES), F32),
            pltpu.VMEM((W // LANES, B * pitch, LANES), F32),
            pltpu.VMEM((B, W), F32),
        ],
        compiler_params=_cparams(("arbitrary",)),
        name="rg_prompt",
    )(proj3, proj3, cw, cb, wa, wx, ba, bx, lam, wo)
    return ya.reshape(B * L, wo.shape[1]), hfin


def _conv_rolled(u, cw_ref, cb_ref):
    return (cb_ref[...]
            + cw_ref[3:4, :] * u
            + cw_ref[2:3, :] * pltpu.roll(u, 1, axis=0)
            + cw_ref[1:2, :] * pltpu.roll(u, 2, axis=0)
            + cw_ref[0:1, :] * pltpu.roll(u, 3, axis=0))


def _rg_sample_body(x_ref, gate_ref, cpad_ref, hpad_ref, cw_ref, cb_ref, wa_ref, wx_ref, ba_ref, bx_ref,
                    lam_ref, wo_ref, ya_ref, h_ref):
    xc = _conv_rolled(x_ref[...] + cpad_ref[...], cw_ref, cb_ref)
    a, bt = _rg_gates(xc, wa_ref, wx_ref, ba_ref[...], bx_ref[...], lam_ref[...])
    row = lax.broadcasted_iota(jnp.int32, a.shape, 0) & (SAMPLE_ROWS - 1)
    h = hpad_ref[...]
    for k in range(SAMPLE_ROWS // 2, SAMPLE_ROWS):
        h = jnp.where(row == k, a * pltpu.roll(h, 1, axis=0) + bt, h)
    h_ref[...] = h
    ya_ref[...] = _dot((h * _gelu(gate_ref[...])).astype(BF16), wo_ref[...])


def _rg_sample(proj, col_x, col_g, cpad, hpad, cw, cb, wa, wx, ba, bx, lam, wo):
    T = proj.shape[0]
    W = cw.shape[1]
    TM = _pick(T, 256)
    full = lambda *s: pl.BlockSpec(s, lambda i: (0,) * len(s))
    return pl.pallas_call(
        _rg_sample_body,
        grid=(T // TM,),
        in_specs=[
            pl.BlockSpec((TM, W), lambda i: (i, col_x)),
            pl.BlockSpec((TM, W), lambda i: (i, col_g)),
            pl.BlockSpec((TM, W), lambda i: (i, 0)),
            pl.BlockSpec((TM, W), lambda i: (i, 0)),
            full(4, W), full(1, W), full(*wa.shape), full(*wx.shape), full(1, W), full(1, W), full(1, W),
            full(*wo.shape),
        ],
        out_specs=[
            pl.BlockSpec((TM, wo.shape[1]), lambda i: (i, 0)),
            pl.BlockSpec((TM, W), lambda i: (i, 0)),
        ],
        out_shape=[
            jax.ShapeDtypeStruct((T, wo.shape[1]), F32),
            jax.ShapeDtypeStruct((T, W), F32),
        ],
        compiler_params=_cparams(("parallel",)),
        name="rg_sample",
    )(proj, proj, cpad, hpad, cw, cb, wa, wx, ba, bx, lam, wo)


def _ssd_chunk(act_s, dt, dtT, alog_ref, alogT_ref, dexp_ref, seq_shift, G, R, P, N,
               yint_fn, y_s, xw_s):
    Q = act_s.shape[0]
    inner = G * R * P
    a = dt * (-jnp.exp(alog_ref[...]))
    aT = dtT * (-jnp.exp(alogT_ref[...]))
    ri = lax.broadcasted_iota(jnp.int32, (Q, Q), 0)
    ci = lax.broadcasted_iota(jnp.int32, (Q, Q), 1)
    same = (ri >> seq_shift) == (ci >> seq_shift)
    causal = same & (ci <= ri)
    causal_f = jnp.where(causal, 1.0, 0.0).astype(F32)
    same_f = jnp.where(same, 1.0, 0.0).astype(F32)
    acum = jnp.dot(causal_f, a, precision=HIGHEST, preferred_element_type=F32)
    tot = jnp.dot(same_f, a, precision=HIGHEST, preferred_element_type=F32)
    acumT = lax.dot_general(aT, causal_f, NT_DIMS, precision=HIGHEST,
                            preferred_element_type=F32)
    eacum = jnp.exp(acum)
    toend = jnp.exp(tot - acum) * dt
    for g in range(G):
        Bg = act_s[:, inner + g * N:inner + (g + 1) * N].astype(BF16)
        Cg = act_s[:, inner + (G + g) * N:inner + (G + g + 1) * N].astype(BF16)
        cb = _dot_nt(Cg, Bg)
        yint = yint_fn(g, Cg)
        for r in range(R):
            h = g * R + r
            sl = slice(h * P, (h + 1) * P)
            seg = acum[:, h:h + 1] - acumT[h:h + 1, :]
            decay = jnp.exp(jnp.where(causal, seg, NEG_INF))
            wgt = (cb * decay * dtT[h:h + 1, :]).astype(BF16)
            xh = act_s[:, sl]
            yh = _dot(wgt, xh.astype(BF16)) + dexp_ref[:, sl] * xh
            if yint is not None:
                yh = yh + yint[:, r * P:(r + 1) * P] * eacum[:, h:h + 1]
            y_s[:, sl] = yh
            xw_s[:, sl] = xh * toend[:, h:h + 1]
    return tot, eacum


def _expand_heads(x, ex_ref):
    hi = x.astype(BF16)
    r1 = x - hi.astype(F32)
    mid = r1.astype(BF16)
    lo = (r1 - mid.astype(F32)).astype(BF16)
    e = ex_ref[...]
    return _dot(hi, e) + _dot(mid, e) + _dot(lo, e)


def _ssd_finish(y, z, nw_ref, wo_ref):
    y = y * _silu(z)
    return _dot(_rms(y, nw_ref[...]).astype(BF16), wo_ref[...])


def _ssd_prompt_body(G, R, P, N,
                     xbc_ref, z_ref, dt_ref, dtT_ref, cw_ref, cb_ref, dtb_ref, dtbT_ref, alog_ref, alogT_ref,
                     dexp_ref, nw_ref, wo_ref, ex_ref, yb_ref, st_ref, pad_s, act_s, y_s, stT_s):
    Q = xbc_ref.shape[0]
    inner = G * R * P
    GW = R * P

    @pl.when(pl.program_id(1) == 0)
    def _():
        pad_s[:, 0:SUBLANES, :] = jnp.zeros((pad_s.shape[0], SUBLANES, LANES), F32)
        stT_s[...] = jnp.zeros(stT_s.shape, F32)

    for s in range(pad_s.shape[0]):
        ls = slice(s * LANES, (s + 1) * LANES)
        act_s[:, ls] = _silu(_conv_slab(xbc_ref[:, ls], pad_s.at[s], cw_ref, cb_ref, ls))
    dt = _softplus(dt_ref[...] + dtb_ref[...])
    dtT = _softplus(dtT_ref[...] + dtbT_ref[...])
    a = dt * (-jnp.exp(alog_ref[...]))
    aT = dtT * (-jnp.exp(alogT_ref[...]))
    ri = lax.broadcasted_iota(jnp.int32, (Q, Q), 0)
    ci = lax.broadcasted_iota(jnp.int32, (Q, Q), 1)
    causal = ci <= ri
    causal_f = jnp.where(causal, 1.0, 0.0).astype(F32)
    hi = dict(precision=HIGHEST, preferred_element_type=F32)
    acum = jnp.dot(causal_f, a, **hi)
    tot = jnp.dot(jnp.ones((SUBLANES, Q), F32), a, **hi)
    acumT = lax.dot_general(aT, causal_f, NT_DIMS, **hi)
    fx = _expand_heads(jnp.concatenate([jnp.exp(acum), jnp.exp(tot[0:1, :] - acum) * dt, jnp.exp(tot)], axis=0),
                       ex_ref)
    eacum_x, toend_x, dec_x = fx[0:Q], fx[Q:2 * Q], fx[2 * Q:2 * Q + 1]
    col2 = acum * LOG2E
    row2 = (acumT - jnp.log(dtT)) * LOG2E
    lane = lax.broadcasted_iota(jnp.int32, (Q, LANES), 1)
    for g in range(G):
        gs = slice(g * GW, (g + 1) * GW)
        Bf = act_s[:, inner + g * N:inner + (g + 1) * N]
        Bg = Bf.astype(BF16)
        Cg = act_s[:, inner + (G + g) * N:inner + (G + g + 1) * N].astype(BF16)
        cb = jnp.where(causal, _dot_nt(Cg, Bg), 0.0)
        y_s[:, gs] = (_dot(Cg, stT_s[:, gs].astype(BF16)) * eacum_x[:, gs]
                      + dexp_ref[:, gs] * act_s[:, gs])
        for pr in range(R // 2):
            h0 = g * R + 2 * pr
            ps = slice(h0 * P, (h0 + 2) * P)
            wg = []
            for h in (h0, h0 + 1):
                e = jnp.exp2(jnp.where(causal, col2[:, h:h + 1] - row2[h:h + 1, :], NEG_INF))
                wg.append((cb * e).astype(BF16))
            xp = act_s[:, ps]
            xa = jnp.where(lane < P, xp, 0.0).astype(BF16)
            xb = jnp.where(lane < P, 0.0, xp).astype(BF16)
            y_s[:, ps] = y_s[:, ps] + _dot(jnp.concatenate(wg, axis=1), jnp.concatenate([xa, xb], axis=0))
        xw = (act_s[:, gs] * toend_x[:, gs]).astype(BF16)
        ST = _dot(jnp.transpose(Bf).astype(BF16), xw)
        stT_s[:, gs] = dec_x[:, gs] * stT_s[:, gs] + ST
    yb_ref[...] = _ssd_finish(y_s[...], z_ref[...], nw_ref, wo_ref)

    @pl.when(pl.program_id(1) == pl.num_programs(1) - 1)
    def _():
        for g in range(G):
            st_ref[0, g * R:(g + 1) * R] = jnp.transpose(stT_s[:, g * GW:(g + 1) * GW]).reshape(R, P, N)


def _ssd_prompt(proj, dt, dtT, B, L, col_xbc, col_z, dims, cw, cb, dtb, dtbT, alog, alogT, dexp, nw, wo):
    G, R, P, N = dims
    H = G * R
    inner = H * P
    CD = cw.shape[1]
    Q = _pick(L, SSD_CHUNK)
    nC = L // Q
    assert 2 * P == LANES and R % 2 == 0 and N == LANES
    head_of_lane = jnp.arange(inner, dtype=jnp.int32) // P
    ex = (jnp.arange(LANES, dtype=jnp.int32)[:, None] == head_of_lane[None, :]).astype(BF16)
    full = lambda *s: pl.BlockSpec(s, lambda b, c: (0,) * len(s))
    return pl.pallas_call(
        functools.partial(_ssd_prompt_body, G, R, P, N),
        grid=(B, nC),
        in_specs=[
            pl.BlockSpec((Q, CD), lambda b, c: (b * nC + c, col_xbc)),
            pl.BlockSpec((Q, inner), lambda b, c: (b * nC + c, col_z)),
            pl.BlockSpec((Q, LANES), lambda b, c: (b * nC + c, 0)),
            pl.BlockSpec((H, Q), lambda b, c: (0, b * nC + c)),
            full(4, CD), full(1, CD), full(1, LANES), full(H, 1), full(1, LANES), full(H, 1),
            full(1, inner), full(1, inner), full(*wo.shape), full(LANES, inner),
        ],
        out_specs=[
            pl.BlockSpec((Q, wo.shape[1]), lambda b, c: (b * nC + c, 0)),
            pl.BlockSpec((1, H, P, N), lambda b, c: (b, 0, 0, 0)),
        ],
        out_shape=[
            jax.ShapeDtypeStruct((B * L, wo.shape[1]), F32),
            jax.ShapeDtypeStruct((B, H, P, N), F32),
        ],
        scratch_shapes=[
            pltpu.VMEM((CD // LANES, Q + SUBLANES, LANES), F32),
            pltpu.VMEM((Q, CD), F32),
            pltpu.VMEM((Q, inner), F32),
            pltpu.VMEM((N, inner), F32),
        ],
        compiler_params=_cparams(("parallel", "arbitrary")),
        name="ssd_prompt",
    )(proj, proj, dt, dtT, cw, cb, dtb, dtbT, alog, alogT, dexp, nw, wo, ex)


def _ssd_sample_body(G, R, P, N,
                     xbc_ref, cpad_ref, z_ref, dt_ref, dtT_ref, h0_ref, cw_ref, cb_ref, dtb_ref, dtbT_ref,
                     alog_ref, alogT_ref, dexp_ref, nw_ref, wo_ref, yb_ref, st_ref,
                     act_s, y_s, xw_s, yint_s, xwT_s, tot_s, eacum_s):
    Q = xbc_ref.shape[0]
    inner = G * R * P
    bi = pl.program_id(1)
    half = SAMPLE_ROWS // 2

    @pl.when(bi == 0)
    def _():
        act_s[...] = _silu(_conv_rolled(xbc_ref[...] + cpad_ref[...], cw_ref, cb_ref))
        rows = lax.broadcasted_iota(jnp.int32, (Q, LANES), 0) & (SAMPLE_ROWS - 1)
        cols = lax.broadcasted_iota(jnp.int32, dtT_ref.shape, 1) & (SAMPLE_ROWS - 1)
        dt = jnp.where(rows >= half, _softplus(dt_ref[...] + dtb_ref[...]), 0.0)
        dtT = jnp.where(cols >= half, _softplus(dtT_ref[...] + dtbT_ref[...]), 0.0)
        seq_shift = SAMPLE_ROWS.bit_length() - 1
        tot, eacum = _ssd_chunk(act_s, dt, dtT, alog_ref, alogT_ref, dexp_ref, seq_shift, G, R, P, N,
                                lambda g, Cg: None, y_s, xw_s)
        tot_s[...] = tot
        eacum_s[...] = eacum
        for g in range(G):
            xwT_s[g] = jnp.transpose(xw_s[:, g * R * P:(g + 1) * R * P]).astype(BF16)

    rid = lax.broadcasted_iota(jnp.int32, (Q, N), 0)
    for q in range(h0_ref.shape[0]):
        r0 = pl.multiple_of((bi * h0_ref.shape[0] + q) * SAMPLE_ROWS, SAMPLE_ROWS)
        mine = (rid >= r0) & (rid < r0 + SAMPLE_ROWS)
        dec = jnp.exp(tot_s[pl.ds(r0, 1), :])
        for g in range(G):
            Cb = act_s[pl.ds(r0, SAMPLE_ROWS), inner + (G + g) * N:inner + (G + g + 1) * N]
            stg = h0_ref[q, g * R:(g + 1) * R].reshape(R * P, N)
            yint_s[pl.ds(r0, SAMPLE_ROWS), g * R * P:(g + 1) * R * P] = _dot_nt(Cb, stg)
            Bg = jnp.where(mine, act_s[:, inner + g * N:inner + (g + 1) * N], 0.0).astype(BF16)
            S = _dot(xwT_s[g], Bg)
            for r in range(R):
                h = g * R + r
                st_ref[q, h] = dec[:, h:h + 1] * h0_ref[q, h] + S[r * P:(r + 1) * P, :]

    @pl.when(bi == pl.num_programs(1) - 1)
    def _():
        for h in range(G * R):
            sl = slice(h * P, (h + 1) * P)
            y_s[:, sl] = y_s[:, sl] + yint_s[:, sl] * eacum_s[:, h:h + 1]
        yb_ref[...] = _ssd_finish(y_s[...], z_ref[...], nw_ref, wo_ref)


def _ssd_sample(proj, dt, dtT, cpad, h0, col_xbc, col_z, dims, cw, cb, dtb, dtbT, alog, alogT, dexp, nw, wo):
    G, R, P, N = dims
    H = G * R
    inner = H * P
    CD = cw.shape[1]
    T = proj.shape[0]
    Bs = h0.shape[0]
    Q = _pick(T, SSD_CHUNK)
    SB = 4
    nb = Q // SAMPLE_ROWS // SB
    assert Q % (SAMPLE_ROWS * SB) == 0
    full = lambda *s: pl.BlockSpec(s, lambda i, j: (0,) * len(s))
    return pl.pallas_call(
        functools.partial(_ssd_sample_body, G, R, P, N),
        grid=(T // Q, nb),
        in_specs=[
            pl.BlockSpec((Q, CD), lambda i, j: (i, col_xbc)),
            pl.BlockSpec((Q, CD), lambda i, j: (i, 0)),
            pl.BlockSpec((Q, inner), lambda i, j: (i, col_z)),
            pl.BlockSpec((Q, LANES), lambda i, j: (i, 0)),
            pl.BlockSpec((H, Q), lambda i, j: (0, i)),
            pl.BlockSpec((SB, H, P, N), lambda i, j: (i * nb + j, 0, 0, 0)),
            full(4, CD), full(1, CD), full(1, LANES), full(H, 1), full(1, LANES), full(H, 1),
            full(1, inner), full(1, inner), full(*wo.shape),
        ],
        out_specs=[
            pl.BlockSpec((Q, wo.shape[1]), lambda i, j: (i, 0)),
            pl.BlockSpec((SB, H, P, N), lambda i, j: (i * nb + j, 0, 0, 0)),
        ],
        out_shape=[
            jax.ShapeDtypeStruct((T, wo.shape[1]), F32),
            jax.ShapeDtypeStruct((Bs, H, P, N), F32),
        ],
        scratch_shapes=[
            pltpu.VMEM((Q, CD), F32),
            pltpu.VMEM((Q, inner), F32),
            pltpu.VMEM((Q, inner), F32),
            pltpu.VMEM((Q, inner), F32),
            pltpu.VMEM((G, R * P, Q), BF16),
            pltpu.VMEM((Q, LANES), F32),
            pltpu.VMEM((Q, LANES), F32),
        ],
        compiler_params=_cparams(("parallel", "arbitrary")),
        name="ssd_sample",
    )(proj, cpad, proj, dt, dtT, h0, cw, cb, dtb, dtbT, alog, alogT, dexp, nw, wo)


def _staircase(k):
    return [(ka, k // (ka + 1)) for ka in range(k)]


def _take_top(cur, k_top, want_rank):
    rank = jnp.full(cur.shape, float(k_top), F32) if want_rank else None
    vals = []
    for k in range(k_top):
        m = jnp.max(cur, axis=0, keepdims=True)
        hit = cur == m
        vals.append(m)
        if want_rank:
            rank = jnp.where(hit, float(k), rank)
        cur = jnp.where(hit, NEG_INF, cur)
    taken = jnp.sum(jnp.where(cur == NEG_INF, 1.0, 0.0), axis=0, keepdims=True)
    return vals, rank, taken


def _take_top_ties(x, k_top, val_ref, rank_ref):
    rows = lax.broadcasted_iota(jnp.int32, x.shape, 0)
    kk = lax.broadcasted_iota(jnp.int32, (k_top, x.shape[1]), 0)

    def body(k, carry):
        cur, rank, vals = carry
        m = jnp.max(cur, axis=0, keepdims=True)
        first = jnp.min(jnp.where(cur == m, rows, x.shape[0]), axis=0, keepdims=True)
        hit = rows == first
        return (jnp.where(hit, NEG_INF, cur), jnp.where(hit, lax.convert_element_type(k, F32), rank),
                jnp.where(kk == k, m, vals))

    init = (x, jnp.full(x.shape, float(k_top), F32), jnp.zeros((k_top, x.shape[1]), F32))
    _, rank, vals = lax.fori_loop(0, k_top, body, init)
    val_ref[...] = vals
    rank_ref[...] = rank


def _route_body(x_ref, ya_ref, yb_ref, ga_ref, gb_ref, wout_ref, nf_ref, wqt_ref, keys_ref,
                x1_ref, h2t_ref, na_ref, ea_ref, rb_ref, eb_ref, sv_s, rk_s, cand_s, cv_s, cr_s):
    NH = keys_ref.shape[0]
    NK, KH = keys_ref.shape[2], keys_ref.shape[3]
    K = PEER_TOPK
    TB = x_ref.shape[0]
    tiles = [slice(tl * LANES, (tl + 1) * LANES) for tl in range(TB // LANES)]
    m = _sigmoid(ga_ref[...]) * ya_ref[...] + _sigmoid(gb_ref[...]) * yb_ref[...]
    x1 = x_ref[...] + _dot(m.astype(BF16), wout_ref[...])
    x1_ref[...] = x1
    h2t = jnp.transpose(_rms(x1, nf_ref[...])).astype(BF16)
    h2t_ref[...] = h2t
    qt = _dot(wqt_ref[...], h2t).astype(BF16)
    cand_s[...] = jnp.full(cand_s.shape, NEG_INF, F32)

    def candidates():
        off = 0
        for ka, nb in _staircase(K):
            cand_s[off:off + nb, :] = sv_s[0, ka:ka + 1, :] + sv_s[1, 0:nb, :]
            off += nb

    def emit(h, sT, in_top, rb, sel, na_of):
        top = sv_s[0, 0:1, :] + sv_s[1, 0:1, :]
        z = jnp.sum(jnp.where(sel, jnp.exp(cand_s[...] - top), 0.0), axis=0, keepdims=True)
        ex = [jnp.where(in_top[s], jnp.exp(sT[s] - sv_s[s, 0:1, :]), 0.0) for s in range(2)]
        na = jnp.zeros(sT[0].shape, F32)
        off = 0
        for ka, nb in _staircase(K):
            cnt = jnp.sum(jnp.where(sel[off:off + nb, :], 1.0, 0.0), axis=0, keepdims=True)
            na = jnp.where(na_of(ka), cnt, na)
            off += nb
        na_ref[h * NK:(h + 1) * NK, :] = na
        ea_ref[h * NK:(h + 1) * NK, :] = ex[0] / z
        rb_ref[h] = rb.astype(BF16)
        eb_ref[h] = ex[1].astype(BF16)

    def scores(h):
        return [_dot(keys_ref[h, s], qt[(h * 2 + s) * KH:(h * 2 + s + 1) * KH, :]) for s in range(2)]

    most = []
    for h in range(NH):
        sT = scores(h)
        seen = jnp.zeros((1, LANES), F32)
        rbs = []
        for s in range(2):
            for ln in tiles:
                vals, rank, taken = _take_top(sT[s][:, ln], K, s == 1)
                for k in range(K):
                    sv_s[s, k:k + 1, ln] = vals[k]
                seen = jnp.maximum(seen, taken)
                if s == 1:
                    rbs.append(rank)
        candidates()
        cand = cand_s[...]
        tau = _take_top(cand, K, False)[0][K - 1]
        sel = cand >= tau
        most.append(jnp.maximum(seen, jnp.max(jnp.sum(jnp.where(sel, 1.0, 0.0), axis=0, keepdims=True),
                                              axis=1, keepdims=True)))
        emit(h, sT, [sT[s] >= sv_s[s, K - 1:K, :] for s in range(2)], jnp.concatenate(rbs, axis=1), sel,
             lambda ka: sT[0] == sv_s[0, ka:ka + 1, :])

    for h in range(NH):
        @pl.when(jnp.max(most[h]) > K)
        def _(h=h):
            sT = scores(h)
            for s in range(2):
                for ln in tiles:
                    _take_top_ties(sT[s][:, ln], K, sv_s.at[s, :, ln], rk_s.at[s, :, ln])
            candidates()
            for ln in tiles:
                _take_top_ties(cand_s[:, ln], K, cv_s.at[:, ln], cr_s.at[:, ln])
            emit(h, sT, [rk_s[s] < K for s in range(2)], rk_s[1], cr_s[...] < K,
                 lambda ka: rk_s[0] == float(ka))


def _route(x2d, ya, yb, gates, col_ga, col_gb, wout, nf, wqt, keys):
    T, D = x2d.shape
    NH, _, NK, KH = keys.shape
    TB = _pick(T, 256)
    assert TB % LANES == 0
    ncand = sum(nb for _, nb in _staircase(PEER_TOPK))
    ncand_pad = -(-ncand // SUBLANES) * SUBLANES
    flat = pl.BlockSpec((NH * NK, TB), lambda i: (0, i))
    tok = pl.BlockSpec((NH, NK, TB), lambda i: (0, 0, i))
    row = lambda c: pl.BlockSpec((TB, D), lambda i: (i, c))
    return pl.pallas_call(
        _route_body,
        grid=(T // TB,),
        in_specs=[row(0), row(0), row(0), row(col_ga), row(col_gb),
                  pl.BlockSpec(wout.shape, lambda i: (0, 0)), pl.BlockSpec((1, D), lambda i: (0, 0)),
                  pl.BlockSpec(wqt.shape, lambda i: (0, 0)),
                  pl.BlockSpec(keys.shape, lambda i: (0, 0, 0, 0))],
        out_specs=[row(0), pl.BlockSpec((D, TB), lambda i: (0, i)), flat, flat, tok, tok],
        out_shape=[jax.ShapeDtypeStruct((T, D), F32), jax.ShapeDtypeStruct((D, T), BF16),
                   jax.ShapeDtypeStruct((NH * NK, T), F32), jax.ShapeDtypeStruct((NH * NK, T), F32),
                   jax.ShapeDtypeStruct((NH, NK, T), BF16), jax.ShapeDtypeStruct((NH, NK, T), BF16)],
        scratch_shapes=[pltpu.VMEM((2, PEER_TOPK, TB), F32), pltpu.VMEM((2, NK, TB), F32),
                        pltpu.VMEM((ncand_pad, TB), F32), pltpu.VMEM((PEER_TOPK, TB), F32),
                        pltpu.VMEM((ncand_pad, TB), F32)],
        compiler_params=_cparams(("parallel",)),
        name="peer_route",
    )(x2d, ya, yb, gates, gates, wout, nf, wqt, keys)


def _peer_body(h2t_ref, x1_ref, u_ref, vt_ref, na_ref, ea_ref, rb_ref, eb_ref, nfin_ref, y_ref,
               s_s, a_s, acc_s):
    ec = pl.program_id(1)
    NH, NK, TB = rb_ref.shape
    ni = u_ref.shape[0] // NK

    @pl.when(ec == 0)
    def _():
        acc_s[...] = jnp.zeros(acc_s.shape, F32)

    s_s[...] = _dot(u_ref[...], h2t_ref[...])

    def row_tile(grp, il):
        return jnp.broadcast_to(grp[il:il + 1, :], (BF16_ROWS, LANES)).astype(BF16)

    for tl in range(TB // LANES):
        ln = slice(tl * LANES, (tl + 1) * LANES)
        rows = [pl.ds(pl.multiple_of(h * NK + ec * ni, SUBLANES), ni) for h in range(NH)]
        na = [na_ref[rows[h], ln] for h in range(NH)]
        ea = [ea_ref[rows[h], ln] for h in range(NH)]
        for il in range(ni):
            na_t = [row_tile(na[h], il) for h in range(NH)]
            ea_t = [row_tile(ea[h], il) for h in range(NH)]
            for jt in range(NK // BF16_ROWS):
                js = slice(jt * BF16_ROWS, (jt + 1) * BF16_ROWS)
                w = None
                for h in range(NH):
                    term = ea_t[h] * jnp.where(rb_ref[h, js, ln] < na_t[h], eb_ref[h, js, ln], 0.0)
                    w = term if w is None else w + term
                e = slice(il * NK + jt * BF16_ROWS, il * NK + (jt + 1) * BF16_ROWS)
                a_s[e, ln] = _gelu(s_s[e, ln].astype(BF16)) * w
    acc_s[...] += _dot(vt_ref[...], a_s[...])

    @pl.when(ec == pl.num_programs(1) - 1)
    def _():
        x2 = x1_ref[...] + jnp.transpose(acc_s[...])
        y_ref[...] = _rms(x2, nfin_ref[...])


def _peer(h2t, x1, u, vt, na, ea, rb, eb, nfin):
    D, T = h2t.shape
    E = u.shape[0]
    NH, NK, _ = rb.shape
    TB = _pick(T, 512)
    EC = 2 * SUBLANES * NK
    assert TB % LANES == 0 and E % EC == 0 and NK % BF16_ROWS == 0
    tok = pl.BlockSpec((NH, NK, TB), lambda i, j: (0, 0, i))
    flat = pl.BlockSpec((NH * NK, TB), lambda i, j: (0, i))
    return pl.pallas_call(
        _peer_body,
        grid=(T // TB, E // EC),
        in_specs=[pl.BlockSpec((D, TB), lambda i, j: (0, i)),
                  pl.BlockSpec((TB, D), lambda i, j: (i, 0)),
                  pl.BlockSpec((EC, D), lambda i, j: (j, 0)),
                  pl.BlockSpec((D, EC), lambda i, j: (0, j)),
                  flat, flat, tok, tok,
                  pl.BlockSpec((1, D), lambda i, j: (0, 0))],
        out_specs=pl.BlockSpec((TB, D), lambda i, j: (i, 0)),
        out_shape=jax.ShapeDtypeStruct((T, D), F32),
        scratch_shapes=[pltpu.VMEM((EC, TB), F32), pltpu.VMEM((EC, TB), BF16), pltpu.VMEM((D, TB), F32)],
        compiler_params=_cparams(("parallel", "arbitrary")),
        name="peer_mix",
    )(h2t, x1, u, vt, na, ea, rb, eb, nfin)


def _pad_lanes(row):
    return jnp.pad(row, ((0, 0), (0, LANES - row.shape[1])))


def kernel(x_prompt, x_sample, state_rg_h, state_rg_conv, state_ssd_h, state_ssd_conv, norm_mix, w_in, rg_conv_w, rg_conv_b, rg_wa, rg_ba, rg_wx, rg_bx, rg_lam, w_rg_out, ssd_conv_w, ssd_conv_b, ssd_dt_bias, ssd_a_log, ssd_d, ssd_norm, w_ssd_out, w_out, norm_ffn, peer_wq, peer_keys, peer_u, peer_v, norm_final):
    depth = w_in.shape[0]
    B, L, D = x_prompt.shape
    Bs, Ls, _ = x_sample.shape
    assert depth == 1 and Ls == SAMPLE_ROWS // 2 and rg_conv_w.shape[1] == 4 and L >= 3
    W = rg_conv_w.shape[2]
    H = ssd_a_log.shape[1]
    inner = ssd_norm.shape[1]
    CD = ssd_conv_w.shape[2]
    P = inner // H
    N = state_ssd_h.shape[-1]
    G = (CD - inner) // (2 * N)
    R = H // G
    dims = (G, R, P, N)
    assert W == D and inner == 2 * D and CD == 3 * D and H % SUBLANES == 0 and H <= LANES
    o_x, o_g, o_z, o_xbc, o_dt, o_m = 0, W, 2 * W, 2 * W + inner, 2 * W + inner + CD, 2 * W + inner + CD + H
    col_xbc, col_x, col_z, col_g, col_ga, col_gb = 0, CD // W, (CD + W) // inner, (CD + W + inner) // W, \
        (CD + 2 * W + inner) // D, (CD + 2 * W + inner + D) // D

    yp = x_prompt.reshape(B * L, D)
    ys = x_sample.reshape(Bs * Ls, D)
    outs = {k: [] for k in ("prh", "prc", "psh", "psc", "srh", "src", "ssh", "ssc")}
    lead = SAMPLE_ROWS - Ls
    for l in range(depth):
        wi = w_in[l]
        w_main = jnp.concatenate([wi[:, o_xbc:o_dt], wi[:, o_x:o_g], wi[:, o_z:o_xbc], wi[:, o_g:o_z],
                                  wi[:, o_m:]], axis=1).astype(BF16)
        wdt = _pad_lanes(wi[:, o_dt:o_m])
        g_mix = norm_mix[l][None]
        rg_w = (rg_conv_w[l], rg_conv_b[l][None], rg_wa[l].astype(BF16), rg_wx[l].astype(BF16),
                rg_ba[l][None], rg_bx[l][None], rg_lam[l][None], w_rg_out[l].astype(BF16))
        ssd_w = (ssd_conv_w[l], ssd_conv_b[l][None], _pad_lanes(ssd_dt_bias[l][None]), ssd_dt_bias[l][:, None],
                 _pad_lanes(ssd_a_log[l][None]), ssd_a_log[l][:, None], jnp.repeat(ssd_d[l], P)[None],
                 ssd_norm[l][None], w_ssd_out[l].astype(BF16))
        wout = w_out[l].astype(BF16)
        nf = norm_ffn[l][None]
        wqt = peer_wq[l].astype(BF16).T
        keys = peer_keys[l].astype(BF16)
        u = peer_u[l].astype(BF16)
        vt = peer_v[l].astype(BF16).T
        nfin = norm_final[None]

        proj, dt, dtT = _inproj(yp, g_mix, w_main, wdt, H)
        ya, rgh = _rg_prompt(proj, B, L, col_x, col_g, *rg_w)
        yb, ssh = _ssd_prompt(proj, dt, dtT, B, L, col_xbc, col_z, dims, *ssd_w)
        p3 = proj.reshape(B, L, -1)
        outs["prh"].append(rgh.reshape(B, W))
        outs["prc"].append(p3[:, L - 3:, CD:CD + W])
        outs["psh"].append(ssh)
        outs["psc"].append(p3[:, L - 3:, :CD])
        x1, h2t, *routing = _route(yp, ya, yb, proj, col_ga, col_gb, wout, nf, wqt, keys)
        yp = _peer(h2t, x1, u, vt, *routing, nfin)

        xe = jnp.pad(ys.reshape(Bs, Ls, D), ((0, 0), (lead, 0), (0, 0))).reshape(Bs * SAMPLE_ROWS, D)
        proj_e, dt_e, dtT_e = _inproj(xe, g_mix, w_main, wdt, H)
        rows = lambda a3: a3.reshape(Bs * SAMPLE_ROWS, a3.shape[-1])
        rg_cpad = rows(jnp.pad(state_rg_conv[l], ((0, 0), (lead - 3, Ls), (0, 0))))
        rg_hpad = rows(jnp.pad(state_rg_h[l][:, None, :], ((0, 0), (lead - 1, Ls), (0, 0))))
        ssd_cpad = rows(jnp.pad(state_ssd_conv[l], ((0, 0), (lead - 3, Ls), (0, 0))))
        ya_e, h_e = _rg_sample(proj_e, col_x, col_g, rg_cpad, rg_hpad, *rg_w)
        yb_e, ssh_s = _ssd_sample(proj_e, dt_e, dtT_e, ssd_cpad, state_ssd_h[l], col_xbc, col_z, dims, *ssd_w)
        toks = lambda a2: a2.reshape(Bs, SAMPLE_ROWS, -1)[:, lead:].reshape(Bs * Ls, -1)
        p3 = proj_e.reshape(Bs, SAMPLE_ROWS, -1)
        outs["srh"].append(h_e.reshape(Bs, SAMPLE_ROWS, W)[:, -1])
        outs["src"].append(p3[:, SAMPLE_ROWS - 3:, CD:CD + W])
        outs["ssh"].append(ssh_s)
        outs["ssc"].append(p3[:, SAMPLE_ROWS - 3:, :CD])
        gates_s = toks(proj_e[:, col_ga * D:(col_gb + 1) * D])
        x1, h2t, *routing = _route(ys, toks(ya_e), toks(yb_e), gates_s, 0, 1, wout, nf, wqt, keys)
        ys = _peer(h2t, x1, u, vt, *routing, nfin)

    st = lambda k: jnp.stack(outs[k])
    return (yp.reshape(B, L, D), ys.reshape(Bs, Ls, D), st("prh"), st("prc"), st("psh"), st("psc"),
            st("srh"), st("src"), st("ssh"), st("ssc"))
```

```python
import functools

import jax
import jax.numpy as jnp
from jax import lax
from jax.experimental import pallas as pl
from jax.experimental.pallas import tpu as pltpu

F32 = jnp.float32
BF16 = jnp.bfloat16
EPS = 1e-6
RG_C = 8.0
PEER_TOPK = 16
SSD_CHUNK = 128
LANES = 128
SUBLANES = 8
BF16_ROWS = 16
SAMPLE_ROWS = 8
NEG_INF = float("-inf")
LOG2E = 1.4426950408889634
HIGHEST = lax.Precision.HIGHEST
NT_DIMS = (((1,), (1,)), ((), ()))
VMEM_LIMIT = 56 * 1024 * 1024


def _cparams(sem):
    return pltpu.CompilerParams(dimension_semantics=sem, vmem_limit_bytes=VMEM_LIMIT)


def _pick(n, pref):
    t = min(n, pref)
    while n % t:
        t -= SUBLANES
    return t


def _sigmoid(x):
    return 1.0 / (1.0 + jnp.exp2(x * -LOG2E))


def _silu(x):
    return x * _sigmoid(x)


def _softplus(x):
    return jnp.maximum(x, 0.0) + jnp.log1p(jnp.exp(-jnp.abs(x)))


def _gelu(x):
    k1 = -2.0 * 0.7978845608028654 * LOG2E
    return x / (1.0 + jnp.exp2(x * (k1 + (k1 * 0.044715) * (x * x))))


def _rms(x, g):
    return x * lax.rsqrt(jnp.mean(x * x, axis=-1, keepdims=True) + EPS) * g


def _dot(a, b):
    return jnp.dot(a, b, preferred_element_type=F32)


def _dot_nt(a, b):
    return lax.dot_general(a, b, NT_DIMS, preferred_element_type=F32)


def _inproj_body(x_ref, g_ref, w_ref, wdt_ref, o_ref, odt_ref, odtT_ref, xn_ref):
    @pl.when(pl.program_id(1) == 0)
    def _():
        xn = _rms(x_ref[...], g_ref[...])
        xn_ref[...] = xn.astype(BF16)
        dt = jnp.dot(xn, wdt_ref[...], precision=HIGHEST, preferred_element_type=F32)
        odt_ref[...] = dt
        odtT_ref[...] = jnp.transpose(dt)[:odtT_ref.shape[0], :]

    o_ref[...] = _dot(xn_ref[...], w_ref[...])


def _inproj(x2d, g, w_main, wdt, H):
    T, D = x2d.shape
    N = w_main.shape[1]
    TM = _pick(T, 1024)
    TN = 3072
    assert N % TN == 0 and TM % LANES == 0
    return pl.pallas_call(
        _inproj_body,
        grid=(T // TM, N // TN),
        in_specs=[
            pl.BlockSpec((TM, D), lambda i, j: (i, 0)),
            pl.BlockSpec((1, D), lambda i, j: (0, 0)),
            pl.BlockSpec((D, TN), lambda i, j: (0, j)),
            pl.BlockSpec((D, LANES), lambda i, j: (0, 0)),
        ],
        out_specs=[
            pl.BlockSpec((TM, TN), lambda i, j: (i, j)),
            pl.BlockSpec((TM, LANES), lambda i, j: (i, 0)),
            pl.BlockSpec((H, TM), lambda i, j: (0, i)),
        ],
        out_shape=[
            jax.ShapeDtypeStruct((T, N), F32),
            jax.ShapeDtypeStruct((T, LANES), F32),
            jax.ShapeDtypeStruct((H, T), F32),
        ],
        scratch_shapes=[pltpu.VMEM((TM, D), BF16)],
        compiler_params=_cparams(("parallel", "arbitrary")),
        name="inproj",
    )(x2d, g, w_main, wdt)


def _conv_slab(x, pad, cw_ref, cb_ref, ls):
    L = x.shape[0]
    pad[SUBLANES:SUBLANES + L, :] = x
    out = (cb_ref[:, ls]
           + cw_ref[3:4, ls] * pad[8:8 + L, :]
           + cw_ref[2:3, ls] * pad[7:7 + L, :]
           + cw_ref[1:2, ls] * pad[6:6 + L, :]
           + cw_ref[0:1, ls] * pad[5:5 + L, :])
    pad[0:SUBLANES, :] = pad[L:L + SUBLANES, :]
    return out


def _rg_gates(xc, wa_ref, wx_ref, ba, bx, lam):
    xb = xc.astype(BF16)
    nb, bw = wa_ref.shape[0], wa_ref.shape[1]
    rs, gs = [], []
    for k in range(nb):
        xk = xb[:, k * bw:(k + 1) * bw]
        rs.append(_dot(xk, wa_ref[k]))
        gs.append(_dot(xk, wx_ref[k]))
    r = _sigmoid(jnp.concatenate(rs, axis=1) + ba)
    i = _sigmoid(jnp.concatenate(gs, axis=1) + bx)
    log_a = -RG_C * r * _softplus(-lam)
    a = jnp.exp(log_a)
    em1 = jnp.tanh(log_a) * (a * a + 1.0)
    return a, jnp.sqrt(-em1) * (i * xc)


def _rg_prompt_body(x_ref, gate_ref, cw_ref, cb_ref, wa_ref, wx_ref, ba_ref, bx_ref, lam_ref, wo_ref,
                    ya_ref, hfin_ref, pad_s, a_s, b_s, h_s):
    B, Lc, W = x_ref.shape
    NS = W // LANES
    pitch = a_s.shape[1] // B

    @pl.when(pl.program_id(0) == 0)
    def _():
        pad_s[:, :, 0:SUBLANES, :] = jnp.zeros((B, NS, SUBLANES, LANES), F32)
        h_s[...] = jnp.zeros(h_s.shape, F32)

    for b in range(B):
        xc = jnp.concatenate(
            [_conv_slab(x_ref[b, :, s * LANES:(s + 1) * LANES], pad_s.at[b, s], cw_ref, cb_ref,
                        slice(s * LANES, (s + 1) * LANES)) for s in range(NS)], axis=1)
        a, bt = _rg_gates(xc, wa_ref, wx_ref, ba_ref[...], bx_ref[...], lam_ref[...])
        for s in range(NS):
            a_s[s, b * pitch:b * pitch + Lc, :] = a[:, s * LANES:(s + 1) * LANES]
            b_s[s, b * pitch:b * pitch + Lc, :] = bt[:, s * LANES:(s + 1) * LANES]

    def step(t, hs):
        out = []
        for s in range(NS):
            rows = pl.ds(t, B, stride=pitch)
            h = a_s[s, rows, :] * hs[s] + b_s[s, rows, :]
            b_s[s, rows, :] = h
            out.append(h)
        return tuple(out)

    hs = lax.fori_loop(0, Lc, step, tuple(h_s[:, s * LANES:(s + 1) * LANES] for s in range(NS)), unroll=4)
    for s in range(NS):
        h_s[:, s * LANES:(s + 1) * LANES] = hs[s]
    hfin_ref[...] = h_s[...]
    for b in range(B):
        hb = jnp.concatenate([b_s[s, b * pitch:b * pitch + Lc, :] for s in range(NS)], axis=1)
        ya_ref[b] = _dot((hb * _gelu(gate_ref[b])).astype(BF16), wo_ref[...])


def _rg_prompt(proj, B, L, col_x, col_g, cw, cb, wa, wx, ba, bx, lam, wo):
    W = cw.shape[1]
    Lc = _pick(L, 128)
    pitch = Lc + SUBLANES // 2
    assert B == SUBLANES and W % LANES == 0
    proj3 = proj.reshape(B, L, proj.shape[1])
    full = lambda *s: pl.BlockSpec(s, lambda c: (0,) * len(s))
    ya, hfin = pl.pallas_call(
        _rg_prompt_body,
        grid=(L // Lc,),
        in_specs=[
            pl.BlockSpec((B, Lc, W), lambda c: (0, c, col_x)),
            pl.BlockSpec((B, Lc, W), lambda c: (0, c, col_g)),
            full(4, W), full(1, W), full(*wa.shape), full(*wx.shape), full(1, W), full(1, W), full(1, W),
            full(*wo.shape),
        ],
        out_specs=[
            pl.BlockSpec((B, Lc, wo.shape[1]), lambda c: (0, c, 0)),
            pl.BlockSpec((B, W), lambda c: (0, 0)),
        ],
        out_shape=[
            jax.ShapeDtypeStruct((B, L, wo.shape[1]), F32),
            jax.ShapeDtypeStruct((B, W), F32),
        ],
        scratch_shapes=[
            pltpu.VMEM((B, W // LANES, Lc + SUBLANES, LANES), F32),
            pltpu.VMEM((W // LANES, B * pitch, LANES), F32),
            pltpu.VMEM((W // LANES, B * pitch, LANES), F32),
            pltpu.VMEM((B, W), F32),
        ],
        compiler_params=_cparams(("arbitrary",)),
        name="rg_prompt",
    )(proj3, proj3, cw, cb, wa, wx, ba, bx, lam, wo)
    return ya.reshape(B * L, wo.shape[1]), hfin


def _conv_rolled(u, cw_ref, cb_ref):
    return (cb_ref[...]
            + cw_ref[3:4, :] * u
            + cw_ref[2:3, :] * pltpu.roll(u, 1, axis=0)
            + cw_ref[1:2, :] * pltpu.roll(u, 2, axis=0)
            + cw_ref[0:1, :] * pltpu.roll(u, 3, axis=0))


def _rg_sample_body(x_ref, gate_ref, cpad_ref, hpad_ref, cw_ref, cb_ref, wa_ref, wx_ref, ba_ref, bx_ref,
                    lam_ref, wo_ref, ya_ref, h_ref):
    xc = _conv_rolled(x_ref[...] + cpad_ref[...], cw_ref, cb_ref)
    a, bt = _rg_gates(xc, wa_ref, wx_ref, ba_ref[...], bx_ref[...], lam_ref[...])
    row = lax.broadcasted_iota(jnp.int32, a.shape, 0) & (SAMPLE_ROWS - 1)
    h = hpad_ref[...]
    for k in range(SAMPLE_ROWS // 2, SAMPLE_ROWS):
        h = jnp.where(row == k, a * pltpu.roll(h, 1, axis=0) + bt, h)
    h_ref[...] = h
    ya_ref[...] = _dot((h * _gelu(gate_ref[...])).astype(BF16), wo_ref[...])


def _rg_sample(proj, col_x, col_g, cpad, hpad, cw, cb, wa, wx, ba, bx, lam, wo):
    T = proj.shape[0]
    W = cw.shape[1]
    TM = _pick(T, 256)
    full = lambda *s: pl.BlockSpec(s, lambda i: (0,) * len(s))
    return pl.pallas_call(
        _rg_sample_body,
        grid=(T // TM,),
        in_specs=[
            pl.BlockSpec((TM, W), lambda i: (i, col_x)),
            pl.BlockSpec((TM, W), lambda i: (i, col_g)),
            pl.BlockSpec((TM, W), lambda i: (i, 0)),
            pl.BlockSpec((TM, W), lambda i: (i, 0)),
            full(4, W), full(1, W), full(*wa.shape), full(*wx.shape), full(1, W), full(1, W), full(1, W),
            full(*wo.shape),
        ],
        out_specs=[
            pl.BlockSpec((TM, wo.shape[1]), lambda i: (i, 0)),
            pl.BlockSpec((TM, W), lambda i: (i, 0)),
        ],
        out_shape=[
            jax.ShapeDtypeStruct((T, wo.shape[1]), F32),
            jax.ShapeDtypeStruct((T, W), F32),
        ],
        compiler_params=_cparams(("parallel",)),
        name="rg_sample",
    )(proj, proj, cpad, hpad, cw, cb, wa, wx, ba, bx, lam, wo)


def _ssd_chunk(act_s, dt, dtT, alog_ref, alogT_ref, dexp_ref, seq_shift, G, R, P, N,
               yint_fn, y_s, xw_s):
    Q = act_s.shape[0]
    inner = G * R * P
    a = dt * (-jnp.exp(alog_ref[...]))
    aT = dtT * (-jnp.exp(alogT_ref[...]))
    ri = lax.broadcasted_iota(jnp.int32, (Q, Q), 0)
    ci = lax.broadcasted_iota(jnp.int32, (Q, Q), 1)
    same = (ri >> seq_shift) == (ci >> seq_shift)
    causal = same & (ci <= ri)
    causal_f = jnp.where(causal, 1.0, 0.0).astype(F32)
    same_f = jnp.where(same, 1.0, 0.0).astype(F32)
    acum = jnp.dot(causal_f, a, precision=HIGHEST, preferred_element_type=F32)
    tot = jnp.dot(same_f, a, precision=HIGHEST, preferred_element_type=F32)
    acumT = lax.dot_general(aT, causal_f, NT_DIMS, precision=HIGHEST,
                            preferred_element_type=F32)
    eacum = jnp.exp(acum)
    toend = jnp.exp(tot - acum) * dt
    for g in range(G):
        Bg = act_s[:, inner + g * N:inner + (g + 1) * N].astype(BF16)
        Cg = act_s[:, inner + (G + g) * N:inner + (G + g + 1) * N].astype(BF16)
        cb = _dot_nt(Cg, Bg)
        yint = yint_fn(g, Cg)
        for r in range(R):
            h = g * R + r
            sl = slice(h * P, (h + 1) * P)
            seg = acum[:, h:h + 1] - acumT[h:h + 1, :]
            decay = jnp.exp(jnp.where(causal, seg, NEG_INF))
            wgt = (cb * decay * dtT[h:h + 1, :]).astype(BF16)
            xh = act_s[:, sl]
            yh = _dot(wgt, xh.astype(BF16)) + dexp_ref[:, sl] * xh
            if yint is not None:
                yh = yh + yint[:, r * P:(r + 1) * P] * eacum[:, h:h + 1]
            y_s[:, sl] = yh
            xw_s[:, sl] = xh * toend[:, h:h + 1]
    return tot, eacum


def _expand_heads(x, ex_ref):
    hi = x.astype(BF16)
    r1 = x - hi.astype(F32)
    mid = r1.astype(BF16)
    lo = (r1 - mid.astype(F32)).astype(BF16)
    e = ex_ref[...]
    return _dot(hi, e) + _dot(mid, e) + _dot(lo, e)


def _ssd_finish(y, z, nw_ref, wo_ref):
    y = y * _silu(z)
    return _dot(_rms(y, nw_ref[...]).astype(BF16), wo_ref[...])


def _ssd_prompt_body(G, R, P, N,
                     xbc_ref, z_ref, dt_ref, dtT_ref, cw_ref, cb_ref, dtb_ref, dtbT_ref, alog_ref, alogT_ref,
                     dexp_ref, nw_ref, wo_ref, ex_ref, yb_ref, st_ref, pad_s, act_s, y_s, stT_s):
    Q = xbc_ref.shape[0]
    inner = G * R * P
    GW = R * P

    @pl.when(pl.program_id(1) == 0)
    def _():
        pad_s[:, 0:SUBLANES, :] = jnp.zeros((pad_s.shape[0], SUBLANES, LANES), F32)
        stT_s[...] = jnp.zeros(stT_s.shape, F32)

    for s in range(pad_s.shape[0]):
        ls = slice(s * LANES, (s + 1) * LANES)
        act_s[:, ls] = _silu(_conv_slab(xbc_ref[:, ls], pad_s.at[s], cw_ref, cb_ref, ls))
    dt = _softplus(dt_ref[...] + dtb_ref[...])
    dtT = _softplus(dtT_ref[...] + dtbT_ref[...])
    a = dt * (-jnp.exp(alog_ref[...]))
    aT = dtT * (-jnp.exp(alogT_ref[...]))
    ri = lax.broadcasted_iota(jnp.int32, (Q, Q), 0)
    ci = lax.broadcasted_iota(jnp.int32, (Q, Q), 1)
    causal = ci <= ri
    causal_f = jnp.where(causal, 1.0, 0.0).astype(F32)
    hi = dict(precision=HIGHEST, preferred_element_type=F32)
    acum = jnp.dot(causal_f, a, **hi)
    tot = jnp.dot(jnp.ones((SUBLANES, Q), F32), a, **hi)
    acumT = lax.dot_general(aT, causal_f, NT_DIMS, **hi)
    fx = _expand_heads(jnp.concatenate([jnp.exp(acum), jnp.exp(tot[0:1, :] - acum) * dt, jnp.exp(tot)], axis=0),
                       ex_ref)
    eacum_x, toend_x, dec_x = fx[0:Q], fx[Q:2 * Q], fx[2 * Q:2 * Q + 1]
    col2 = acum * LOG2E
    row2 = (acumT - jnp.log(dtT)) * LOG2E
    lane = lax.broadcasted_iota(jnp.int32, (Q, LANES), 1)
    for g in range(G):
        gs = slice(g * GW, (g + 1) * GW)
        Bf = act_s[:, inner + g * N:inner + (g + 1) * N]
        Bg = Bf.astype(BF16)
        Cg = act_s[:, inner + (G + g) * N:inner + (G + g + 1) * N].astype(BF16)
        cb = jnp.where(causal, _dot_nt(Cg, Bg), 0.0)
        y_s[:, gs] = (_dot(Cg, stT_s[:, gs].astype(BF16)) * eacum_x[:, gs]
                      + dexp_ref[:, gs] * act_s[:, gs])
        for pr in range(R // 2):
            h0 = g * R + 2 * pr
            ps = slice(h0 * P, (h0 + 2) * P)
            wg = []
            for h in (h0, h0 + 1):
                e = jnp.exp2(jnp.where(causal, col2[:, h:h + 1] - row2[h:h + 1, :], NEG_INF))
                wg.append((cb * e).astype(BF16))
            xp = act_s[:, ps]
            xa = jnp.where(lane < P, xp, 0.0).astype(BF16)
            xb = jnp.where(lane < P, 0.0, xp).astype(BF16)
            y_s[:, ps] = y_s[:, ps] + _dot(jnp.concatenate(wg, axis=1), jnp.concatenate([xa, xb], axis=0))
        xw = (act_s[:, gs] * toend_x[:, gs]).astype(BF16)
        ST = _dot(jnp.transpose(Bf).astype(BF16), xw)
        stT_s[:, gs] = dec_x[:, gs] * stT_s[:, gs] + ST
    yb_ref[...] = _ssd_finish(y_s[...], z_ref[...], nw_ref, wo_ref)

    @pl.when(pl.program_id(1) == pl.num_programs(1) - 1)
    def _():
        for g in range(G):
            st_ref[0, g * R:(g + 1) * R] = jnp.transpose(stT_s[:, g * GW:(g + 1) * GW]).reshape(R, P, N)


def _ssd_prompt(proj, dt, dtT, B, L, col_xbc, col_z, dims, cw, cb, dtb, dtbT, alog, alogT, dexp, nw, wo):
    G, R, P, N = dims
    H = G * R
    inner = H * P
    CD = cw.shape[1]
    Q = _pick(L, SSD_CHUNK)
    nC = L // Q
    assert 2 * P == LANES and R % 2 == 0 and N == LANES
    head_of_lane = jnp.arange(inner, dtype=jnp.int32) // P
    ex = (jnp.arange(LANES, dtype=jnp.int32)[:, None] == head_of_lane[None, :]).astype(BF16)
    full = lambda *s: pl.BlockSpec(s, lambda b, c: (0,) * len(s))
    return pl.pallas_call(
        functools.partial(_ssd_prompt_body, G, R, P, N),
        grid=(B, nC),
        in_specs=[
            pl.BlockSpec((Q, CD), lambda b, c: (b * nC + c, col_xbc)),
            pl.BlockSpec((Q, inner), lambda b, c: (b * nC + c, col_z)),
            pl.BlockSpec((Q, LANES), lambda b, c: (b * nC + c, 0)),
            pl.BlockSpec((H, Q), lambda b, c: (0, b * nC + c)),
            full(4, CD), full(1, CD), full(1, LANES), full(H, 1), full(1, LANES), full(H, 1),
            full(1, inner), full(1, inner), full(*wo.shape), full(LANES, inner),
        ],
        out_specs=[
            pl.BlockSpec((Q, wo.shape[1]), lambda b, c: (b * nC + c, 0)),
            pl.BlockSpec((1, H, P, N), lambda b, c: (b, 0, 0, 0)),
        ],
        out_shape=[
            jax.ShapeDtypeStruct((B * L, wo.shape[1]), F32),
            jax.ShapeDtypeStruct((B, H, P, N), F32),
        ],
        scratch_shapes=[
            pltpu.VMEM((CD // LANES, Q + SUBLANES, LANES), F32),
            pltpu.VMEM((Q, CD), F32),
            pltpu.VMEM((Q, inner), F32),
            pltpu.VMEM((N, inner), F32),
        ],
        compiler_params=_cparams(("parallel", "arbitrary")),
        name="ssd_prompt",
    )(proj, proj, dt, dtT, cw, cb, dtb, dtbT, alog, alogT, dexp, nw, wo, ex)


def _ssd_sample_body(G, R, P, N,
                     xbc_ref, cpad_ref, z_ref, dt_ref, dtT_ref, h0_ref, cw_ref, cb_ref, dtb_ref, dtbT_ref,
                     alog_ref, alogT_ref, dexp_ref, nw_ref, wo_ref, yb_ref, st_ref,
                     act_s, y_s, xw_s, yint_s, xwT_s, tot_s, eacum_s):
    Q = xbc_ref.shape[0]
    inner = G * R * P
    bi = pl.program_id(1)
    half = SAMPLE_ROWS // 2

    @pl.when(bi == 0)
    def _():
        act_s[...] = _silu(_conv_rolled(xbc_ref[...] + cpad_ref[...], cw_ref, cb_ref))
        rows = lax.broadcasted_iota(jnp.int32, (Q, LANES), 0) & (SAMPLE_ROWS - 1)
        cols = lax.broadcasted_iota(jnp.int32, dtT_ref.shape, 1) & (SAMPLE_ROWS - 1)
        dt = jnp.where(rows >= half, _softplus(dt_ref[...] + dtb_ref[...]), 0.0)
        dtT = jnp.where(cols >= half, _softplus(dtT_ref[...] + dtbT_ref[...]), 0.0)
        seq_shift = SAMPLE_ROWS.bit_length() - 1
        tot, eacum = _ssd_chunk(act_s, dt, dtT, alog_ref, alogT_ref, dexp_ref, seq_shift, G, R, P, N,
                                lambda g, Cg: None, y_s, xw_s)
        tot_s[...] = tot
        eacum_s[...] = eacum
        for g in range(G):
            xwT_s[g] = jnp.transpose(xw_s[:, g * R * P:(g + 1) * R * P]).astype(BF16)

    rid = lax.broadcasted_iota(jnp.int32, (Q, N), 0)
    for q in range(h0_ref.shape[0]):
        r0 = pl.multiple_of((bi * h0_ref.shape[0] + q) * SAMPLE_ROWS, SAMPLE_ROWS)
        mine = (rid >= r0) & (rid < r0 + SAMPLE_ROWS)
        dec = jnp.exp(tot_s[pl.ds(r0, 1), :])
        for g in range(G):
            Cb = act_s[pl.ds(r0, SAMPLE_ROWS), inner + (G + g) * N:inner + (G + g + 1) * N]
            stg = h0_ref[q, g * R:(g + 1) * R].reshape(R * P, N)
            yint_s[pl.ds(r0, SAMPLE_ROWS), g * R * P:(g + 1) * R * P] = _dot_nt(Cb, stg)
            Bg = jnp.where(mine, act_s[:, inner + g * N:inner + (g + 1) * N], 0.0).astype(BF16)
            S = _dot(xwT_s[g], Bg)
            for r in range(R):
                h = g * R + r
                st_ref[q, h] = dec[:, h:h + 1] * h0_ref[q, h] + S[r * P:(r + 1) * P, :]

    @pl.when(bi == pl.num_programs(1) - 1)
    def _():
        for h in range(G * R):
            sl = slice(h * P, (h + 1) * P)
            y_s[:, sl] = y_s[:, sl] + yint_s[:, sl] * eacum_s[:, h:h + 1]
        yb_ref[...] = _ssd_finish(y_s[...], z_ref[...], nw_ref, wo_ref)


def _ssd_sample(proj, dt, dtT, cpad, h0, col_xbc, col_z, dims, cw, cb, dtb, dtbT, alog, alogT, dexp, nw, wo):
    G, R, P, N = dims
    H = G * R
    inner = H * P
    CD = cw.shape[1]
    T = proj.shape[0]
    Bs = h0.shape[0]
    Q = _pick(T, SSD_CHUNK)
    SB = 4
    nb = Q // SAMPLE_ROWS // SB
    assert Q % (SAMPLE_ROWS * SB) == 0
    full = lambda *s: pl.BlockSpec(s, lambda i, j: (0,) * len(s))
    return pl.pallas_call(
        functools.partial(_ssd_sample_body, G, R, P, N),
        grid=(T // Q, nb),
        in_specs=[
            pl.BlockSpec((Q, CD), lambda i, j: (i, col_xbc)),
            pl.BlockSpec((Q, CD), lambda i, j: (i, 0)),
            pl.BlockSpec((Q, inner), lambda i, j: (i, col_z)),
            pl.BlockSpec((Q, LANES), lambda i, j: (i, 0)),
            pl.BlockSpec((H, Q), lambda i, j: (0, i)),
            pl.BlockSpec((SB, H, P, N), lambda i, j: (i * nb + j, 0, 0, 0)),
            full(4, CD), full(1, CD), full(1, LANES), full(H, 1), full(1, LANES), full(H, 1),
            full(1, inner), full(1, inner), full(*wo.shape),
        ],
        out_specs=[
            pl.BlockSpec((Q, wo.shape[1]), lambda i, j: (i, 0)),
            pl.BlockSpec((SB, H, P, N), lambda i, j: (i * nb + j, 0, 0, 0)),
        ],
        out_shape=[
            jax.ShapeDtypeStruct((T, wo.shape[1]), F32),
            jax.ShapeDtypeStruct((Bs, H, P, N), F32),
        ],
        scratch_shapes=[
            pltpu.VMEM((Q, CD), F32),
            pltpu.VMEM((Q, inner), F32),
            pltpu.VMEM((Q, inner), F32),
            pltpu.VMEM((Q, inner), F32),
            pltpu.VMEM((G, R * P, Q), BF16),
            pltpu.VMEM((Q, LANES), F32),
            pltpu.VMEM((Q, LANES), F32),
        ],
        compiler_params=_cparams(("parallel", "arbitrary")),
        name="ssd_sample",
    )(proj, cpad, proj, dt, dtT, h0, cw, cb, dtb, dtbT, alog, alogT, dexp, nw, wo)


def _staircase(k):
    return [(ka, k // (ka + 1)) for ka in range(k)]


def _take_top(cur, k_top, want_rank):
    rank = jnp.full(cur.shape, float(k_top), F32) if want_rank else None
    vals = []
    for k in range(k_top):
        m = jnp.max(cur, axis=0, keepdims=True)
        hit = cur == m
        vals.append(m)
        if want_rank:
            rank = jnp.where(hit, float(k), rank)
        cur = jnp.where(hit, NEG_INF, cur)
    taken = jnp.sum(jnp.where(cur == NEG_INF, 1.0, 0.0), axis=0, keepdims=True)
    return vals, rank, taken


def _take_top_ties(x, k_top, val_ref, rank_ref):
    rows = lax.broadcasted_iota(jnp.int32, x.shape, 0)
    kk = lax.broadcasted_iota(jnp.int32, (k_top, x.shape[1]), 0)

    def body(k, carry):
        cur, rank, vals = carry
        m = jnp.max(cur, axis=0, keepdims=True)
        first = jnp.min(jnp.where(cur == m, rows, x.shape[0]), axis=0, keepdims=True)
        hit = rows == first
        return (jnp.where(hit, NEG_INF, cur), jnp.where(hit, lax.convert_element_type(k, F32), rank),
                jnp.where(kk == k, m, vals))

    init = (x, jnp.full(x.shape, float(k_top), F32), jnp.zeros((k_top, x.shape[1]), F32))
    _, rank, vals = lax.fori_loop(0, k_top, body, init)
    val_ref[...] = vals
    rank_ref[...] = rank


def _route_body(x_ref, ya_ref, yb_ref, ga_ref, gb_ref, wout_ref, nf_ref, wqt_ref, keys_ref,
                x1_ref, h2t_ref, na_ref, ea_ref, rb_ref, eb_ref, sv_s, rk_s, cand_s, cv_s, cr_s):
    NH = keys_ref.shape[0]
    NK, KH = keys_ref.shape[2], keys_ref.shape[3]
    K = PEER_TOPK
    TB = x_ref.shape[0]
    tiles = [slice(tl * LANES, (tl + 1) * LANES) for tl in range(TB // LANES)]
    m = _sigmoid(ga_ref[...]) * ya_ref[...] + _sigmoid(gb_ref[...]) * yb_ref[...]
    x1 = x_ref[...] + _dot(m.astype(BF16), wout_ref[...])
    x1_ref[...] = x1
    h2t = jnp.transpose(_rms(x1, nf_ref[...])).astype(BF16)
    h2t_ref[...] = h2t
    qt = _dot(wqt_ref[...], h2t).astype(BF16)
    cand_s[...] = jnp.full(cand_s.shape, NEG_INF, F32)

    def candidates():
        off = 0
        for ka, nb in _staircase(K):
            cand_s[off:off + nb, :] = sv_s[0, ka:ka + 1, :] + sv_s[1, 0:nb, :]
            off += nb

    def emit(h, sT, in_top, rb, sel, na_of):
        top = sv_s[0, 0:1, :] + sv_s[1, 0:1, :]
        z = jnp.sum(jnp.where(sel, jnp.exp(cand_s[...] - top), 0.0), axis=0, keepdims=True)
        ex = [jnp.where(in_top[s], jnp.exp(sT[s] - sv_s[s, 0:1, :]), 0.0) for s in range(2)]
        na = jnp.zeros(sT[0].shape, F32)
        off = 0
        for ka, nb in _staircase(K):
            cnt = jnp.sum(jnp.where(sel[off:off + nb, :], 1.0, 0.0), axis=0, keepdims=True)
            na = jnp.where(na_of(ka), cnt, na)
            off += nb
        na_ref[h * NK:(h + 1) * NK, :] = na
        ea_ref[h * NK:(h + 1) * NK, :] = ex[0] / z
        rb_ref[h] = rb.astype(BF16)
        eb_ref[h] = ex[1].astype(BF16)

    def scores(h):
        return [_dot(keys_ref[h, s], qt[(h * 2 + s) * KH:(h * 2 + s + 1) * KH, :]) for s in range(2)]

    most = []
    for h in range(NH):
        sT = scores(h)
        seen = jnp.zeros((1, LANES), F32)
        rbs = []
        for s in range(2):
            for ln in tiles:
                vals, rank, taken = _take_top(sT[s][:, ln], K, s == 1)
                for k in range(K):
                    sv_s[s, k:k + 1, ln] = vals[k]
                seen = jnp.maximum(seen, taken)
                if s == 1:
                    rbs.append(rank)
        candidates()
        cand = cand_s[...]
        tau = _take_top(cand, K, False)[0][K - 1]
        sel = cand >= tau
        most.append(jnp.maximum(seen, jnp.max(jnp.sum(jnp.where(sel, 1.0, 0.0), axis=0, keepdims=True),
                                              axis=1, keepdims=True)))
        emit(h, sT, [sT[s] >= sv_s[s, K - 1:K, :] for s in range(2)], jnp.concatenate(rbs, axis=1), sel,
             lambda ka: sT[0] == sv_s[0, ka:ka + 1, :])

    for h in range(NH):
        @pl.when(jnp.max(most[h]) > K)
        def _(h=h):
            sT = scores(h)
            for s in range(2):
                for ln in tiles:
                    _take_top_ties(sT[s][:, ln], K, sv_s.at[s, :, ln], rk_s.at[s, :, ln])
            candidates()
            for ln in tiles:
                _take_top_ties(cand_s[:, ln], K, cv_s.at[:, ln], cr_s.at[:, ln])
            emit(h, sT, [rk_s[s] < K for s in range(2)], rk_s[1], cr_s[...] < K,
                 lambda ka: rk_s[0] == float(ka))


def _route(x2d, ya, yb, gates, col_ga, col_gb, wout, nf, wqt, keys):
    T, D = x2d.shape
    NH, _, NK, KH = keys.shape
    TB = _pick(T, 256)
    assert TB % LANES == 0
    ncand = sum(nb for _, nb in _staircase(PEER_TOPK))
    ncand_pad = -(-ncand // SUBLANES) * SUBLANES
    flat = pl.BlockSpec((NH * NK, TB), lambda i: (0, i))
    tok = pl.BlockSpec((NH, NK, TB), lambda i: (0, 0, i))
    row = lambda c: pl.BlockSpec((TB, D), lambda i: (i, c))
    return pl.pallas_call(
        _route_body,
        grid=(T // TB,),
        in_specs=[row(0), row(0), row(0), row(col_ga), row(col_gb),
                  pl.BlockSpec(wout.shape, lambda i: (0, 0)), pl.BlockSpec((1, D), lambda i: (0, 0)),
                  pl.BlockSpec(wqt.shape, lambda i: (0, 0)),
                  pl.BlockSpec(keys.shape, lambda i: (0, 0, 0, 0))],
        out_specs=[row(0), pl.BlockSpec((D, TB), lambda i: (0, i)), flat, flat, tok, tok],
        out_shape=[jax.ShapeDtypeStruct((T, D), F32), jax.ShapeDtypeStruct((D, T), BF16),
                   jax.ShapeDtypeStruct((NH * NK, T), F32), jax.ShapeDtypeStruct((NH * NK, T), F32),
                   jax.ShapeDtypeStruct((NH, NK, T), BF16), jax.ShapeDtypeStruct((NH, NK, T), BF16)],
        scratch_shapes=[pltpu.VMEM((2, PEER_TOPK, TB), F32), pltpu.VMEM((2, NK, TB), F32),
                        pltpu.VMEM((ncand_pad, TB), F32), pltpu.VMEM((PEER_TOPK, TB), F32),
                        pltpu.VMEM((ncand_pad, TB), F32)],
        compiler_params=_cparams(("parallel",)),
        name="peer_route",
    )(x2d, ya, yb, gates, gates, wout, nf, wqt, keys)


def _peer_body(h2t_ref, x1_ref, u_ref, vt_ref, na_ref, ea_ref, rb_ref, eb_ref, nfin_ref, y_ref,
               s_s, a_s, acc_s):
    ec = pl.program_id(1)
    NH, NK, TB = rb_ref.shape
    ni = u_ref.shape[0] // NK

    @pl.when(ec == 0)
    def _():
        acc_s[...] = jnp.zeros(acc_s.shape, F32)

    s_s[...] = _dot(u_ref[...], h2t_ref[...])

    def row_tile(ref, h, il, ln):
        base = pl.multiple_of(h * NK + ec * ni + (il // SUBLANES) * SUBLANES, SUBLANES)
        grp = ref[pl.ds(base, SUBLANES), ln]
        r = il % SUBLANES
        return jnp.broadcast_to(grp[r:r + 1, :], (BF16_ROWS, LANES)).astype(BF16)

    for tl in range(TB // LANES):
        ln = slice(tl * LANES, (tl + 1) * LANES)
        for il in range(ni):
            na_t = [row_tile(na_ref, h, il, ln) for h in range(NH)]
            ea_t = [row_tile(ea_ref, h, il, ln) for h in range(NH)]
            for jt in range(NK // BF16_ROWS):
                js = slice(jt * BF16_ROWS, (jt + 1) * BF16_ROWS)
                w = jnp.zeros((BF16_ROWS, LANES), BF16)
                for h in range(NH):
                    w = w + ea_t[h] * jnp.where(rb_ref[h, js, ln] < na_t[h], eb_ref[h, js, ln], 0.0)
                e = slice(il * NK + jt * BF16_ROWS, il * NK + (jt + 1) * BF16_ROWS)
                a_s[e, ln] = _gelu(s_s[e, ln].astype(BF16)) * w
    acc_s[...] += _dot(vt_ref[...], a_s[...])

    @pl.when(ec == pl.num_programs(1) - 1)
    def _():
        x2 = x1_ref[...] + jnp.transpose(acc_s[...])
        y_ref[...] = _rms(x2, nfin_ref[...])


def _peer(h2t, x1, u, vt, na, ea, rb, eb, nfin):
    D, T = h2t.shape
    E = u.shape[0]
    NH, NK, _ = rb.shape
    TB = _pick(T, 512)
    EC = 2 * SUBLANES * NK
    assert TB % LANES == 0 and E % EC == 0 and NK % BF16_ROWS == 0
    tok = pl.BlockSpec((NH, NK, TB), lambda i, j: (0, 0, i))
    flat = pl.BlockSpec((NH * NK, TB), lambda i, j: (0, i))
    return pl.pallas_call(
        _peer_body,
        grid=(T // TB, E // EC),
        in_specs=[pl.BlockSpec((D, TB), lambda i, j: (0, i)),
                  pl.BlockSpec((TB, D), lambda i, j: (i, 0)),
                  pl.BlockSpec((EC, D), lambda i, j: (j, 0)),
                  pl.BlockSpec((D, EC), lambda i, j: (0, j)),
                  flat, flat, tok, tok,
                  pl.BlockSpec((1, D), lambda i, j: (0, 0))],
        out_specs=pl.BlockSpec((TB, D), lambda i, j: (i, 0)),
        out_shape=jax.ShapeDtypeStruct((T, D), F32),
        scratch_shapes=[pltpu.VMEM((EC, TB), F32), pltpu.VMEM((EC, TB), BF16), pltpu.VMEM((D, TB), F32)],
        compiler_params=_cparams(("parallel", "arbitrary")),
        name="peer_mix",
    )(h2t, x1, u, vt, na, ea, rb, eb, nfin)


def _pad_lanes(row):
    return jnp.pad(row, ((0, 0), (0, LANES - row.shape[1])))


def kernel(x_prompt, x_sample, state_rg_h, state_rg_conv, state_ssd_h, state_ssd_conv, norm_mix, w_in, rg_conv_w, rg_conv_b, rg_wa, rg_ba, rg_wx, rg_bx, rg_lam, w_rg_out, ssd_conv_w, ssd_conv_b, ssd_dt_bias, ssd_a_log, ssd_d, ssd_norm, w_ssd_out, w_out, norm_ffn, peer_wq, peer_keys, peer_u, peer_v, norm_final):
    depth = w_in.shape[0]
    B, L, D = x_prompt.shape
    Bs, Ls, _ = x_sample.shape
    assert depth == 1 and Ls == SAMPLE_ROWS // 2 and rg_conv_w.shape[1] == 4 and L >= 3
    W = rg_conv_w.shape[2]
    H = ssd_a_log.shape[1]
    inner = ssd_norm.shape[1]
    CD = ssd_conv_w.shape[2]
    P = inner // H
    N = state_ssd_h.shape[-1]
    G = (CD - inner) // (2 * N)
    R = H // G
    dims = (G, R, P, N)
    assert W == D and inner == 2 * D and CD == 3 * D and H % SUBLANES == 0 and H <= LANES
    o_x, o_g, o_z, o_xbc, o_dt, o_m = 0, W, 2 * W, 2 * W + inner, 2 * W + inner + CD, 2 * W + inner + CD + H
    col_xbc, col_x, col_z, col_g, col_ga, col_gb = 0, CD // W, (CD + W) // inner, (CD + W + inner) // W, \
        (CD + 2 * W + inner) // D, (CD + 2 * W + inner + D) // D

    yp = x_prompt.reshape(B * L, D)
    ys = x_sample.reshape(Bs * Ls, D)
    outs = {k: [] for k in ("prh", "prc", "psh", "psc", "srh", "src", "ssh", "ssc")}
    lead = SAMPLE_ROWS - Ls
    for l in range(depth):
        wi = w_in[l]
        w_main = jnp.concatenate([wi[:, o_xbc:o_dt], wi[:, o_x:o_g], wi[:, o_z:o_xbc], wi[:, o_g:o_z],
                                  wi[:, o_m:]], axis=1).astype(BF16)
        wdt = _pad_lanes(wi[:, o_dt:o_m])
        g_mix = norm_mix[l][None]
        rg_w = (rg_conv_w[l], rg_conv_b[l][None], rg_wa[l].astype(BF16), rg_wx[l].astype(BF16),
                rg_ba[l][None], rg_bx[l][None], rg_lam[l][None], w_rg_out[l].astype(BF16))
        ssd_w = (ssd_conv_w[l], ssd_conv_b[l][None], _pad_lanes(ssd_dt_bias[l][None]), ssd_dt_bias[l][:, None],
                 _pad_lanes(ssd_a_log[l][None]), ssd_a_log[l][:, None], jnp.repeat(ssd_d[l], P)[None],
                 ssd_norm[l][None], w_ssd_out[l].astype(BF16))
        wout = w_out[l].astype(BF16)
        nf = norm_ffn[l][None]
        wqt = peer_wq[l].astype(BF16).T
        keys = peer_keys[l].astype(BF16)
        u = peer_u[l].astype(BF16)
        vt = peer_v[l].astype(BF16).T
        nfin = norm_final[None]

        proj, dt, dtT = _inproj(yp, g_mix, w_main, wdt, H)
        ya, rgh = _rg_prompt(proj, B, L, col_x, col_g, *rg_w)
        yb, ssh = _ssd_prompt(proj, dt, dtT, B, L, col_xbc, col_z, dims, *ssd_w)
        p3 = proj.reshape(B, L, -1)
        outs["prh"].append(rgh.reshape(B, W))
        outs["prc"].append(p3[:, L - 3:, CD:CD + W])
        outs["psh"].append(ssh)
        outs["psc"].append(p3[:, L - 3:, :CD])
        x1, h2t, *routing = _route(yp, ya, yb, proj, col_ga, col_gb, wout, nf, wqt, keys)
        yp = _peer(h2t, x1, u, vt, *routing, nfin)

        xe = jnp.pad(ys.reshape(Bs, Ls, D), ((0, 0), (lead, 0), (0, 0))).reshape(Bs * SAMPLE_ROWS, D)
        proj_e, dt_e, dtT_e = _inproj(xe, g_mix, w_main, wdt, H)
        rows = lambda a3: a3.reshape(Bs * SAMPLE_ROWS, a3.shape[-1])
        rg_cpad = rows(jnp.pad(state_rg_conv[l], ((0, 0), (lead - 3, Ls), (0, 0))))
        rg_hpad = rows(jnp.pad(state_rg_h[l][:, None, :], ((0, 0), (lead - 1, Ls), (0, 0))))
        ssd_cpad = rows(jnp.pad(state_ssd_conv[l], ((0, 0), (lead - 3, Ls), (0, 0))))
        ya_e, h_e = _rg_sample(proj_e, col_x, col_g, rg_cpad, rg_hpad, *rg_w)
        yb_e, ssh_s = _ssd_sample(proj_e, dt_e, dtT_e, ssd_cpad, state_ssd_h[l], col_xbc, col_z, dims, *ssd_w)
        toks = lambda a2: a2.reshape(Bs, SAMPLE_ROWS, -1)[:, lead:].reshape(Bs * Ls, -1)
        p3 = proj_e.reshape(Bs, SAMPLE_ROWS, -1)
        outs["srh"].append(h_e.reshape(Bs, SAMPLE_ROWS, W)[:, -1])
        outs["src"].append(p3[:, SAMPLE_ROWS - 3:, CD:CD + W])
        outs["ssh"].append(ssh_s)
        outs["ssc"].append(p3[:, SAMPLE_ROWS - 3:, :CD])
        gates_s = toks(proj_e[:, col_ga * D:(col_gb + 1) * D])
        x1, h2t, *routing = _route(ys, toks(ya_e), toks(yb_e), gates_s, 0, 1, wout, nf, wqt, keys)
        ys = _peer(h2t, x1, u, vt, *routing, nfin)

    st = lambda k: jnp.stack(outs[k])
    return (yp.reshape(B, L, D), ys.reshape(Bs, Ls, D), st("prh"), st("prc"), st("psh"), st("psc"),
            st("srh"), st("src"), st("ssh"), st("ssc"))
```
